```python
import math
import jax
import jax.numpy as jnp
from jax import lax

D_MODEL = 1024
BATCH = 8
SEQ = 8192
DEPTH = 2

GRID_W = 64
CTX_LEN = 256
GDN_HEADS = 8
GDN_DK = 128
GDN_DV = 128
GDN_QK = GDN_HEADS * GDN_DK
GDN_V = GDN_HEADS * GDN_DV
GDN_QKV = 2 * GDN_QK + GDN_V
SSM_INNER = 2 * D_MODEL
SSM_HEAD_DIM = 64
SSM_HEADS = SSM_INNER // SSM_HEAD_DIM
SSM_GROUPS = 8
SSM_HPG = SSM_HEADS // SSM_GROUPS
SSM_STATE = 128
SSM_GN = SSM_GROUPS * SSM_STATE
SSM_XBC = SSM_INNER + 2 * SSM_GN
CONV_K = 5
CHUNK = 64
D_FF = 4 * D_MODEL
DEEPNORM_ALPHA = (2 * DEPTH) ** 0.25
DEEPNORM_BETA = (8 * DEPTH) ** -0.25
LN_EPS = 1e-5
NORM_EPS = 1e-6
IN_SPLITS = (GDN_QKV, GDN_V, 2 * GDN_HEADS, 2 * GDN_HEADS, SSM_INNER, SSM_XBC, 2 * SSM_HEADS, D_MODEL, D_MODEL)
IN_DIM = sum(IN_SPLITS)

kernel_name = 'hybrid_gdn_ssd_deepnorm_dit'


def split_cols(t):
    pts, acc = [], 0
    for s in IN_SPLITS[:-1]:
        acc += s
        pts.append(acc)
    return jnp.split(t, pts, axis=-1)


def layer_norm(t, g, b):
    tf = t.astype(jnp.float32)
    mu = jnp.mean(tf, -1, keepdims=True)
    var = jnp.mean(jnp.square(tf - mu), -1, keepdims=True)
    return ((tf - mu) * lax.rsqrt(var + LN_EPS)).astype(t.dtype) * g + b


def l2norm(t):
    return t * lax.rsqrt(jnp.sum(t * t, -1, keepdims=True) + NORM_EPS)


def dwconv(t, w):
    pad = w.shape[0] // 2
    L = t.shape[-2]
    tp = jnp.pad(t, [(0, 0)] * (t.ndim - 2) + [(pad, pad), (0, 0)])
    out = tp[..., 0:L, :] * w[0]
    for j in range(1, w.shape[0]):
        out = out + tp[..., j:j + L, :] * w[j]
    return out


def chunk_front(t, axis):
    n = t.shape[axis] // CHUNK
    t = t.reshape(t.shape[:axis] + (n, CHUNK) + t.shape[axis + 1:])
    return jnp.moveaxis(t, axis, 0)


def unchunk(t, axis):
    t = jnp.moveaxis(t, 0, axis)
    return t.reshape(t.shape[:axis] + (-1,) + t.shape[axis + 2:])


def mlp(h, w1, b1, w2, b2):
    return jnp.square(jax.nn.relu(h @ w1 + b1)) @ w2 + b2


def gdn_prep(qkv, a_raw, b_raw, A_log, dt_bias):
    Bsz, L, _ = qkv.shape
    qkv = qkv.astype(jnp.float32)

    def heads(t, d):
        return t.reshape(Bsz, L, GDN_HEADS, d).transpose(0, 2, 1, 3)

    q = l2norm(heads(qkv[..., :GDN_QK], GDN_DK)) * GDN_DK ** -0.5
    k = l2norm(heads(qkv[..., GDN_QK:2 * GDN_QK], GDN_DK))
    v = heads(qkv[..., 2 * GDN_QK:], GDN_DV)
    a = a_raw.astype(jnp.float32).reshape(Bsz, L, 2, GDN_HEADS)
    bt = b_raw.astype(jnp.float32).reshape(Bsz, L, 2, GDN_HEADS)
    g = (-jnp.exp(A_log.astype(jnp.float32)) * jax.nn.softplus(a + dt_bias.astype(jnp.float32))).transpose(2, 0, 3, 1)
    beta = jax.nn.sigmoid(bt).transpose(2, 0, 3, 1)
    return q, k, v, g, beta


def gdn_chunk_scan(q, k, v, g, beta, S0, with_out):
    strict = jnp.tril(jnp.ones((CHUNK, CHUNK), bool), -1)
    incl = jnp.tril(jnp.ones((CHUNK, CHUNK), bool))
    eye = jnp.eye(CHUNK, dtype=jnp.float32)
    xs = (chunk_front(q, 2), chunk_front(k, 2), chunk_front(v, 2), chunk_front(g, 2), chunk_front(beta, 2))

    def step(S, inp):
        qc, kc, vc, gc, bc = inp
        gcum = jnp.cumsum(gc, axis=-1)
        glast = gcum[..., -1:]
        diff = gcum[..., :, None] - gcum[..., None, :]
        kk = jnp.einsum('bhid,bhjd->bhij', kc, kc)
        a_mat = jnp.where(strict, bc[..., :, None] * kk * jnp.exp(jnp.where(strict, diff, 0.0)), 0.0)
        rhs = jnp.concatenate([(bc * jnp.exp(gcum))[..., None] * kc, bc[..., None] * vc], axis=-1)
        wu = lax.linalg.triangular_solve(eye + a_mat, rhs, left_side=True, lower=True, unit_diagonal=True)
        w_blk, u_blk = wu[..., :GDN_DK], wu[..., GDN_DK:]
        v_new = u_blk - jnp.einsum('bhck,bhkv->bhcv', w_blk, S)
        S_next = jnp.exp(glast)[..., None] * S + jnp.einsum('bhck,bhcv->bhkv', kc * jnp.exp(glast - gcum)[..., None], v_new)
        if not with_out:
            return S_next, None
        qk = jnp.where(incl, jnp.einsum('bhid,bhjd->bhij', qc, kc) * jnp.exp(jnp.where(incl, diff, 0.0)), 0.0)
        o = jnp.einsum('bhck,bhkv->bhcv', qc * jnp.exp(gcum)[..., None], S) + jnp.einsum('bhij,bhjv->bhiv', qk, v_new)
        return S_next, o

    S_final, o = lax.scan(step, S0, xs)
    return S_final, (unchunk(o, 2) if with_out else None)


def gdn_bidir(lat, ctx, with_ctx_out):
    q, k, v, g, beta = lat
    qc, kc, vc, gc, bc = ctx
    S0 = jnp.zeros((q.shape[0], GDN_HEADS, GDN_DK, GDN_DV), jnp.float32)
    f = lambda t: jnp.flip(t, axis=2)
    s_f, oc_f = gdn_chunk_scan(qc, kc, vc, gc[0], bc[0], S0, with_ctx_out)
    s_b, oc_b = gdn_chunk_scan(f(qc), f(kc), f(vc), f(gc[1]), f(bc[1]), S0, with_ctx_out)
    _, o_f = gdn_chunk_scan(q, k, v, g[0], beta[0], s_f, True)
    _, o_b = gdn_chunk_scan(f(q), f(k), f(v), f(g[1]), f(beta[1]), s_b, True)
    o = o_f + f(o_b)
    o_c = (oc_f + f(oc_b)) if with_ctx_out else None
    return o, o_c


def gdn_gated_norm(o, gate, w):
    o = jnp.swapaxes(o, 1, 2)
    o = o * lax.rsqrt(jnp.mean(o * o, -1, keepdims=True) + NORM_EPS)
    Bsz, L = o.shape[:2]
    return (o.astype(gate.dtype) * w * jax.nn.silu(gate.reshape(Bsz, L, GDN_HEADS, GDN_DV))).reshape(Bsz, L, GDN_V)


def ssm_prep(xbc, dt_raw, A_log, dt_bias):
    Bsz, L, _ = xbc.shape
    xbc = xbc.astype(jnp.float32)
    xs = xbc[..., :SSM_INNER].reshape(Bsz, L, SSM_GROUPS, SSM_HPG, SSM_HEAD_DIM)
    Bm = xbc[..., SSM_INNER:SSM_INNER + SSM_GN].reshape(Bsz, L, SSM_GROUPS, SSM_STATE)
    Cm = xbc[..., SSM_INNER + SSM_GN:].reshape(Bsz, L, SSM_GROUPS, SSM_STATE)
    dt = jax.nn.softplus(dt_raw.astype(jnp.float32).reshape(Bsz, L, 2, SSM_HEADS) + dt_bias.astype(jnp.float32))
    dt = jnp.moveaxis(dt, 2, 0).reshape(2, Bsz, L, SSM_GROUPS, SSM_HPG)
    a_neg = -jnp.exp(A_log.astype(jnp.float32)).reshape(2, SSM_GROUPS, SSM_HPG)
    return xs, dt, a_neg, Bm, Cm


def ssd_chunk_scan(x, dt, a_neg, Bm, Cm, S0, with_out):
    incl = jnp.tril(jnp.ones((CHUNK, CHUNK), bool))
    xs = (chunk_front(x * dt[..., None], 1), chunk_front(dt * a_neg, 1), chunk_front(Bm, 1), chunk_front(Cm, 1))

    def step(S, inp):
        xdt, la, bc, cc = inp
        gcum = jnp.cumsum(la, axis=1)
        glast = gcum[:, -1]
        S_next = jnp.exp(glast)[..., None, None] * S + jnp.einsum('bcgn,bcgh,bcghp->bghnp', bc, jnp.exp(glast[:, None] - gcum), xdt)
        if not with_out:
            return S_next, None
        gt = jnp.moveaxis(gcum, 1, -1)
        diff = gt[..., :, None] - gt[..., None, :]
        decay = jnp.where(incl, jnp.exp(jnp.where(incl, diff, 0.0)), 0.0)
        cb = jnp.einsum('bign,bjgn->bgij', cc, bc)
        y = jnp.einsum('bgij,bghij,bjghp->bighp', cb, decay, xdt) + jnp.einsum('bign,bigh,bghnp->bighp', cc, jnp.exp(gcum), S)
        return S_next, y

    S_final, y = lax.scan(step, S0, xs)
    return S_final, (unchunk(y, 1) if with_out else None)


def ssd_bidir(lat, ctx, a_neg, D_skip, with_ctx_out):
    x, dt, Bm, Cm = lat
    xc, dtc, Bc, Cc = ctx
    S0 = jnp.zeros((x.shape[0], SSM_GROUPS, SSM_HPG, SSM_STATE, SSM_HEAD_DIM), jnp.float32)
    f = lambda t: jnp.flip(t, axis=1)
    d = D_skip.astype(jnp.float32).reshape(SSM_GROUPS, SSM_HPG, 1)
    s_f, yc_f = ssd_chunk_scan(xc, dtc[0], a_neg[0], Bc, Cc, S0, with_ctx_out)
    s_b, yc_b = ssd_chunk_scan(f(xc), f(dtc[1]), a_neg[1], f(Bc), f(Cc), S0, with_ctx_out)
    _, y_f = ssd_chunk_scan(x, dt[0], a_neg[0], Bm, Cm, s_f, True)
    _, y_b = ssd_chunk_scan(f(x), f(dt[1]), a_neg[1], f(Bm), f(Cm), s_b, True)
    y = y_f + f(y_b) + d * x
    y_c = (yc_f + f(yc_b) + d * xc) if with_ctx_out else None
    return y, y_c


def ssm_gated_norm(y, z, w):
    t = y * jax.nn.silu(z.astype(jnp.float32))
    shp = t.shape
    t = t.reshape(shp[:-1] + (SSM_GROUPS, -1))
    t = t * lax.rsqrt(jnp.mean(t * t, -1, keepdims=True) + NORM_EPS)
    return t.reshape(shp).astype(z.dtype) * w


def raster_to_cols_grid(t, rows):
    Bsz, L, C = t.shape
    return t.reshape(Bsz, rows, GRID_W, C).transpose(0, 2, 1, 3)


def cols_to_raster(t, rows):
    Bsz, L, C = t.shape
    return t.reshape(Bsz, GRID_W, rows, C).transpose(0, 2, 1, 3).reshape(Bsz, L, C)


def mixer(h, hc, rows, w_in, gdn_conv_w, gdn_A_log, gdn_dt_bias, gdn_norm_w,
          ssm_conv_w, ssm_conv_b, ssm_A_log, ssm_dt_bias, ssm_D, ssm_norm_w,
          w_proj_gdn, w_proj_ssm, w_out, with_ctx_out):
    Bsz, L, _ = h.shape
    qkv, gout, a_raw, b_raw, z, xbc, dt_raw, gate_a, gate_b = split_cols(h @ w_in)
    qkv_c, gout_c, a_c, b_c, z_c, xbc_c, dt_c, gate_ac, gate_bc = split_cols(hc @ w_in)

    qkv = jax.nn.silu(dwconv(qkv.reshape(Bsz, rows, GRID_W, GDN_QKV), gdn_conv_w).reshape(Bsz, L, GDN_QKV))
    qkv_c = jax.nn.silu(dwconv(qkv_c, gdn_conv_w))
    o, o_c = gdn_bidir(gdn_prep(qkv, a_raw, b_raw, gdn_A_log, gdn_dt_bias),
                       gdn_prep(qkv_c, a_c, b_c, gdn_A_log, gdn_dt_bias), with_ctx_out)
    y_a = gdn_gated_norm(o, gout, gdn_norm_w)

    xbc = jax.nn.silu(dwconv(raster_to_cols_grid(xbc, rows), ssm_conv_w) + ssm_conv_b).reshape(Bsz, L, SSM_XBC)
    dt_cols = raster_to_cols_grid(dt_raw, rows).reshape(Bsz, L, 2 * SSM_HEADS)
    xbc_c = jax.nn.silu(dwconv(xbc_c, ssm_conv_w) + ssm_conv_b)
    xs, dt, a_neg, Bm, Cm = ssm_prep(xbc, dt_cols, ssm_A_log, ssm_dt_bias)
    xs_c, dt_cc, _, Bc, Cc = ssm_prep(xbc_c, dt_c, ssm_A_log, ssm_dt_bias)
    y, y_c = ssd_bidir((xs, dt, Bm, Cm), (xs_c, dt_cc, Bc, Cc), a_neg, ssm_D, with_ctx_out)
    y_b = ssm_gated_norm(cols_to_raster(y.reshape(Bsz, L, SSM_INNER), rows), z, ssm_norm_w)

    out = (jax.nn.sigmoid(gate_a) * (y_a @ w_proj_gdn) + jax.nn.sigmoid(gate_b) * (y_b @ w_proj_ssm)) @ w_out
    if not with_ctx_out:
        return out, None
    Lc = hc.shape[1]
    y_ac = gdn_gated_norm(o_c, gout_c, gdn_norm_w)
    y_bc = ssm_gated_norm(y_c.reshape(Bsz, Lc, SSM_INNER), z_c, ssm_norm_w)
    out_c = (jax.nn.sigmoid(gate_ac) * (y_ac @ w_proj_gdn) + jax.nn.sigmoid(gate_bc) * (y_bc @ w_proj_ssm)) @ w_out
    return out, out_c


def inv_softplus_dt(k, shape):
    dt = jnp.exp(jax.random.uniform(k, shape, jnp.float32, math.log(1e-3), math.log(1e-1)))
    return dt + jnp.log(-jnp.expm1(-dt))


def setup_inputs(seed: int = 0) -> dict:
    key = jax.random.key(seed)
    ks = jax.random.split(key, 32)
    f32 = jnp.float32
    nrm = lambda k, shape, s: jax.random.normal(k, shape, f32) * s
    return {
        'x': nrm(ks[0], (BATCH, SEQ, D_MODEL), 1.0),
        'c': nrm(ks[1], (BATCH, D_MODEL), 1.0),
        'ctx': nrm(ks[2], (BATCH, CTX_LEN, D_MODEL), 1.0),
        'c_ctx': nrm(ks[3], (D_MODEL,), 1.0),
        'w_mod': nrm(ks[4], (DEPTH, D_MODEL, 6 * D_MODEL), 0.5 * D_MODEL ** -0.5),
        'b_mod': nrm(ks[5], (DEPTH, 6 * D_MODEL), 0.01),
        'w_in': nrm(ks[6], (DEPTH, D_MODEL, IN_DIM), D_MODEL ** -0.5),
        'gdn_conv_w': nrm(ks[7], (DEPTH, CONV_K, GDN_QKV), CONV_K ** -0.5),
        'gdn_A_log': jnp.log(jax.random.uniform(ks[8], (DEPTH, 2, GDN_HEADS), f32, 1.0, 16.0)),
        'gdn_dt_bias': inv_softplus_dt(ks[9], (DEPTH, 2, GDN_HEADS)),
        'gdn_norm_w': 1.0 + nrm(ks[10], (DEPTH, GDN_DV), 0.02),
        'ssm_conv_w': nrm(ks[11], (DEPTH, CONV_K, SSM_XBC), CONV_K ** -0.5),
        'ssm_conv_b': nrm(ks[12], (DEPTH, SSM_XBC), 0.01),
        'ssm_A_log': jnp.log(jax.random.uniform(ks[13], (DEPTH, 2, SSM_HEADS), f32, 1.0, 16.0)),
        'ssm_dt_bias': inv_softplus_dt(ks[14], (DEPTH, 2, SSM_HEADS)),
        'ssm_D': 1.0 + nrm(ks[15], (DEPTH, SSM_HEADS), 0.02),
        'ssm_norm_w': 1.0 + nrm(ks[16], (DEPTH, SSM_INNER), 0.02),
        'w_proj_gdn': nrm(ks[17], (DEPTH, GDN_V, D_MODEL), GDN_V ** -0.5),
        'w_proj_ssm': nrm(ks[18], (DEPTH, SSM_INNER, D_MODEL), SSM_INNER ** -0.5),
        'w_out': nrm(ks[19], (DEPTH, D_MODEL, D_MODEL), D_MODEL ** -0.5 * DEEPNORM_BETA),
        'ln1_g': 1.0 + nrm(ks[20], (DEPTH, D_MODEL), 0.02),
        'ln1_b': nrm(ks[21], (DEPTH, D_MODEL), 0.01),
        'w_ff1': nrm(ks[22], (DEPTH, D_MODEL, D_FF), D_MODEL ** -0.5),
        'b_ff1': nrm(ks[23], (DEPTH, D_FF), 0.01),
        'w_ff2': nrm(ks[24], (DEPTH, D_FF, D_MODEL), D_FF ** -0.5 * DEEPNORM_BETA),
        'b_ff2': nrm(ks[25], (DEPTH, D_MODEL), 0.01),
        'ln2_g': 1.0 + nrm(ks[26], (DEPTH, D_MODEL), 0.02),
        'ln2_b': nrm(ks[27], (DEPTH, D_MODEL), 0.01),
    }


def reference(x, c, ctx, c_ctx, w_mod, b_mod, w_in, gdn_conv_w, gdn_A_log, gdn_dt_bias, gdn_norm_w,
              ssm_conv_w, ssm_conv_b, ssm_A_log, ssm_dt_bias, ssm_D, ssm_norm_w,
              w_proj_gdn, w_proj_ssm, w_out, ln1_g, ln1_b, w_ff1, b_ff1, w_ff2, b_ff2, ln2_g, ln2_b):
    rows = x.shape[1] // GRID_W
    for l in range(DEPTH):
        last = l == DEPTH - 1
        mod = jax.nn.silu(c) @ w_mod[l] + b_mod[l]
        mod_c = jax.nn.silu(c_ctx) @ w_mod[l] + b_mod[l]
        sh1, sc1, g1, sh2, sc2, g2 = jnp.split(mod[:, None, :], 6, axis=-1)
        sh1c, sc1c, g1c, sh2c, sc2c, g2c = jnp.split(mod_c, 6)
        mix, mix_c = mixer(x * (1.0 + sc1) + sh1, ctx * (1.0 + sc1c) + sh1c, rows, w_in[l],
                           gdn_conv_w[l], gdn_A_log[l], gdn_dt_bias[l], gdn_norm_w[l],
                           ssm_conv_w[l], ssm_conv_b[l], ssm_A_log[l], ssm_dt_bias[l], ssm_D[l], ssm_norm_w[l],
                           w_proj_gdn[l], w_proj_ssm[l], w_out[l], not last)
        x = layer_norm(DEEPNORM_ALPHA * x + g1 * mix, ln1_g[l], ln1_b[l])
        x = layer_norm(DEEPNORM_ALPHA * x + g2 * mlp(x * (1.0 + sc2) + sh2, w_ff1[l], b_ff1[l], w_ff2[l], b_ff2[l]), ln2_g[l], ln2_b[l])
        if not last:
            ctx = layer_norm(DEEPNORM_ALPHA * ctx + g1c * mix_c, ln1_g[l], ln1_b[l])
            ctx = layer_norm(DEEPNORM_ALPHA * ctx + g2c * mlp(ctx * (1.0 + sc2c) + sh2c, w_ff1[l], b_ff1[l], w_ff2[l], b_ff2[l]), ln2_g[l], ln2_b[l])
    return x
```

```python
import functools

import jax
import jax.numpy as jnp
from jax import lax
from jax.experimental import pallas as pl
from jax.experimental.pallas import tpu as pltpu

F32 = jnp.float32
BF16 = jnp.bfloat16
HIGHEST = lax.Precision.HIGHEST

D_MODEL = 1024
GRID_W = 64
GDN_HEADS = 8
GDN_DK = 128
GDN_DV = 128
GDN_QK = GDN_HEADS * GDN_DK
GDN_V = GDN_HEADS * GDN_DV
GDN_QKV = 2 * GDN_QK + GDN_V
SSM_INNER = 2 * D_MODEL
SSM_HEAD_DIM = 64
SSM_HEADS = SSM_INNER // SSM_HEAD_DIM
SSM_GROUPS = 8
SSM_HPG = SSM_HEADS // SSM_GROUPS
SSM_STATE = 128
SSM_GN = SSM_GROUPS * SSM_STATE
SSM_XBC = SSM_INNER + 2 * SSM_GN
CONV_K = 5
CHUNK = 64
D_FF = 4 * D_MODEL
LN_EPS = 1e-5
NORM_EPS = 1e-6
IN_SPLITS = (GDN_QKV, GDN_V, 2 * GDN_HEADS, 2 * GDN_HEADS, SSM_INNER, SSM_XBC, 2 * SSM_HEADS, D_MODEL, D_MODEL)

LANES = 128
SUBLANES = 8
VMEM_LIMIT_BYTES = 56 * 1024 * 1024

QKV_OFF = 0
GOUT_OFF = QKV_OFF + GDN_QKV
XBC_OFF = GOUT_OFF + GDN_V
Z_OFF = XBC_OFF + SSM_XBC
GA_OFF = Z_OFF + SSM_INNER
GB_OFF = GA_OFF + D_MODEL
BIG_N = GB_OFF + D_MODEL
SM_A = 0
SM_B = SM_A + 2 * GDN_HEADS
SM_DT = SM_B + 2 * GDN_HEADS
SM_DT_B = SM_DT + SSM_HEADS
SMALL_N = LANES

GDN_CHUNKS_PER_STEP = 4
GDN_TB = GDN_CHUNKS_PER_STEP * CHUNK


def _params(n_axes):
    return pltpu.CompilerParams(dimension_semantics=("arbitrary",) * n_axes, vmem_limit_bytes=VMEM_LIMIT_BYTES)


def _silu(t):
    return t * jax.nn.sigmoid(t)


def _softplus(t):
    return jnp.maximum(t, 0.0) + jnp.log(1.0 + jnp.exp(-jnp.abs(t)))


def _mm(a, b):
    return jnp.dot(a.astype(BF16), b.astype(BF16), preferred_element_type=F32)


def _mm_nt(a, b):
    return lax.dot_general(a.astype(BF16), b.astype(BF16), (((1,), (1,)), ((), ())), preferred_element_type=F32)


def _mm_tn(a, b):
    return lax.dot_general(a.astype(BF16), b.astype(BF16), (((0,), (0,)), ((), ())), preferred_element_type=F32)


def _tri_masks(lower):
    ii = lax.broadcasted_iota(jnp.int32, (CHUNK, CHUNK), 0)
    jj = lax.broadcasted_iota(jnp.int32, (CHUNK, CHUNK), 1)
    if lower:
        return ii >= jj, ii > jj
    return ii <= jj, ii < jj


def _conv_seg(x, w, seg):
    n = x.shape[0]
    pos = lax.broadcasted_iota(jnp.int32, (n, 1), 0) & (seg - 1)
    out = None
    for j in range(CONV_K):
        d = j - CONV_K // 2
        if d == 0:
            term = x * w[j:j + 1, :]
        else:
            shifted = pltpu.roll(x, shift=(-d) % n, axis=0)
            valid = (pos + d >= 0) & (pos + d < seg)
            term = jnp.where(valid, shifted, 0.0) * w[j:j + 1, :]
        out = term if out is None else out + term
    return out


def _mod_kernel(c_ref, w_ref, b_ref, o_ref):
    o_ref[...] = _mm(_silu(c_ref[...]), w_ref[...]) + b_ref[...]


def _modulation(cc, w, b):
    rows, d = cc.shape
    n = w.shape[1]
    tn = 1536
    return pl.pallas_call(
        _mod_kernel,
        out_shape=jax.ShapeDtypeStruct((rows, n), F32),
        grid=(n // tn,),
        in_specs=[pl.BlockSpec((rows, d), lambda j: (0, 0)),
                  pl.BlockSpec((d, tn), lambda j: (0, j)),
                  pl.BlockSpec((1, tn), lambda j: (0, j))],
        out_specs=pl.BlockSpec((rows, tn), lambda j: (0, j)),
        compiler_params=_params(1),
        name="modulation",
    )(cc, w, b.reshape(1, n))


def _inproj_kernel(x_ref, sc_ref, sh_ref, w_ref, o_ref):
    h = x_ref[...] * (1.0 + sc_ref[...]) + sh_ref[...]
    o_ref[...] = jnp.dot(h.astype(BF16), w_ref[...], preferred_element_type=F32)


def _inproj(x, sc, sh, w, tm, tn, name):
    bsz, length, d = x.shape
    n = w.shape[1]
    return pl.pallas_call(
        _inproj_kernel,
        out_shape=jax.ShapeDtypeStruct((bsz, length, n), F32),
        grid=(bsz, length // tm, n // tn),
        in_specs=[pl.BlockSpec((None, tm, d), lambda b, i, j: (b, i, 0)),
                  pl.BlockSpec((None, 1, d), lambda b, i, j: (b, 0, 0)),
                  pl.BlockSpec((None, 1, d), lambda b, i, j: (b, 0, 0)),
                  pl.BlockSpec((d, tn), lambda b, i, j: (0, j))],
        out_specs=pl.BlockSpec((None, tm, tn), lambda b, i, j: (b, i, j)),
        compiler_params=_params(3),
        name=name,
    )(x, sc, sh, w)


def _chunk_cumsums(t):
    ii = lax.broadcasted_iota(jnp.int32, (CHUNK, CHUNK), 0)
    jj = lax.broadcasted_iota(jnp.int32, (CHUNK, CHUNK), 1)
    ltri = (ii >= jj).astype(F32)
    utri = (ii <= jj).astype(F32)
    fwd = jnp.dot(ltri, t, precision=HIGHEST, preferred_element_type=F32)
    bwd = jnp.dot(utri, t, precision=HIGHEST, preferred_element_type=F32)
    return fwd, bwd


def _gdn_prep_kernel(s_ref, alog_ref, dtb_ref, o_ref, *, rows):
    s = s_ref[...]
    lane = lax.broadcasted_iota(jnp.int32, (1, LANES), 1)
    g = -jnp.exp(alog_ref[...]) * _softplus(s + dtb_ref[...])
    beta = jax.nn.sigmoid(s)
    for c in range(rows // CHUNK):
        sl = slice(c * CHUNK, (c + 1) * CHUNK)
        fwd, bwd = _chunk_cumsums(g[sl, :])
        gc = jnp.where(lane < SM_A + GDN_HEADS, fwd, bwd)
        o_ref[sl, :] = jnp.where(lane < SM_B, gc, beta[sl, :])


def _gdn_prep(small, a_log, dt_bias, tb):
    bsz, length, _ = small.shape
    pad = lambda t: jnp.pad(t.reshape(1, -1).astype(F32), ((0, 0), (SM_A, LANES - SM_A - 2 * GDN_HEADS)))
    return pl.pallas_call(
        functools.partial(_gdn_prep_kernel, rows=tb),
        out_shape=jax.ShapeDtypeStruct((bsz, length, LANES), F32),
        grid=(bsz, length // tb),
        in_specs=[pl.BlockSpec((None, tb, LANES), lambda b, i: (b, i, 0)),
                  pl.BlockSpec((1, LANES), lambda b, i: (0, 0)),
                  pl.BlockSpec((1, LANES), lambda b, i: (0, 0))],
        out_specs=pl.BlockSpec((None, tb, LANES), lambda b, i: (b, i, 0)),
        compiler_params=_params(2),
        name="gdn_prep",
    )(small, pad(a_log), pad(dt_bias))


def _tri_inverse(a):
    ii = lax.broadcasted_iota(jnp.int32, (CHUNK, CHUNK), 0)
    jj = lax.broadcasted_iota(jnp.int32, (CHUNK, CHUNK), 1)
    t = jnp.where(ii == jj, 1.0, 0.0) - a
    x = _mm(a, a)
    power = 2
    while 2 * power < CHUNK:
        t = t + _mm(t, x)
        x = _mm(x, x)
        power *= 2
    return t + _mm(t, x)


def _gdn_chunk(q, k, v, gcol, grow, bcol, state, lower):
    incl, strict = _tri_masks(lower)
    glast = grow[:, CHUNK - 1:CHUNK] if lower else grow[:, 0:1]
    kk = _mm_nt(k, k)
    qk = _mm_nt(q, k)
    dec = jnp.exp(jnp.where(incl, gcol - grow, 0.0))
    a_mat = jnp.where(strict, bcol * kk * dec, 0.0)
    t_inv = _tri_inverse(a_mat)
    eg = jnp.exp(gcol)
    rhs = jnp.concatenate([(bcol * eg) * k, bcol * v], axis=1)
    wu = _mm(t_inv, rhs)
    w_blk, u_blk = wu[:, :GDN_DK], wu[:, GDN_DK:]
    v_new = u_blk - _mm(w_blk, state)
    o = _mm(q * eg, state) + _mm(jnp.where(incl, qk * dec, 0.0), v_new)
    state = jnp.exp(glast) * state + _mm_tn(k * jnp.exp(glast - gcol), v_new)
    return o, state


def _gdn_kernel(qf_ref, kf_ref, vf_ref, qb_ref, kb_ref, vb_ref, pf_ref, pb_ref, ptf_ref, ptb_ref,
                wq_ref, wk_ref, wv_ref, s0_ref, of_ref, ob_ref, sout_ref, state_ref, *, seg, n_steps):
    h = pl.program_id(1)
    step = pl.program_id(2)

    @pl.when(step == 0)
    def _():
        state_ref[...] = s0_ref[...]

    lane = lax.broadcasted_iota(jnp.int32, (1, LANES), 1)

    def col(p, idx):
        return jnp.sum(jnp.where(lane == idx, p, 0.0), axis=1, keepdims=True)

    def prep(q_ref, k_ref, v_ref):
        q = _silu(_conv_seg(q_ref[...], wq_ref[...], seg))
        k = _silu(_conv_seg(k_ref[...], wk_ref[...], seg))
        v = _silu(_conv_seg(v_ref[...], wv_ref[...], seg))
        q = q * lax.rsqrt(jnp.sum(q * q, axis=1, keepdims=True) + NORM_EPS) * (GDN_DK ** -0.5)
        k = k * lax.rsqrt(jnp.sum(k * k, axis=1, keepdims=True) + NORM_EPS)
        return q, k, v

    qf, kf, vf = prep(qf_ref, kf_ref, vf_ref)
    qb, kb, vb = prep(qb_ref, kb_ref, vb_ref)
    pf = pf_ref[...]
    pb = pb_ref[...]
    gcol_f = col(pf, SM_A + h)
    bcol_f = col(pf, SM_B + h)
    gcol_b = col(pb, SM_A + GDN_HEADS + h)
    bcol_b = col(pb, SM_B + GDN_HEADS + h)
    grow_f = ptf_ref[pl.ds(SM_A + h, 1), :]
    grow_b = ptb_ref[pl.ds(SM_A + GDN_HEADS + h, 1), :]

    s_f = state_ref[0]
    s_b = state_ref[1]
    for c in range(GDN_CHUNKS_PER_STEP):
        sl = slice(c * CHUNK, (c + 1) * CHUNK)
        o, s_f = _gdn_chunk(qf[sl], kf[sl], vf[sl], gcol_f[sl], grow_f[:, sl], bcol_f[sl], s_f, True)
        of_ref[sl, :] = o
        cb = GDN_CHUNKS_PER_STEP - 1 - c
        sl = slice(cb * CHUNK, (cb + 1) * CHUNK)
        o, s_b = _gdn_chunk(qb[sl], kb[sl], vb[sl], gcol_b[sl], grow_b[:, sl], bcol_b[sl], s_b, False)
        ob_ref[sl, :] = o
    state_ref[0] = s_f
    state_ref[1] = s_b

    @pl.when(step == n_steps - 1)
    def _():
        sout_ref[0] = s_f
        sout_ref[1] = s_b


def _gdn_scan(big, p, pt, conv_w, s0, seg):
    bsz, length, _ = big.shape
    tb = GDN_TB
    n_steps = length // tb
    qoff, koff, voff = QKV_OFF // LANES, (QKV_OFF + GDN_QK) // LANES, (QKV_OFF + 2 * GDN_QK) // LANES
    fwd = lambda off: pl.BlockSpec((None, tb, LANES), lambda b, h, s: (b, s, off + h))
    bwd = lambda off: pl.BlockSpec((None, tb, LANES), lambda b, h, s: (b, n_steps - 1 - s, off + h))
    wspec = lambda off: pl.BlockSpec((CONV_K, LANES), lambda b, h, s: (0, off + h))
    state_spec = pl.BlockSpec((None, None, 2, GDN_DK, GDN_DV), lambda b, h, s: (b, h, 0, 0, 0))
    return pl.pallas_call(
        functools.partial(_gdn_kernel, seg=seg, n_steps=n_steps),
        out_shape=(jax.ShapeDtypeStruct((bsz, length, GDN_V), F32),
                   jax.ShapeDtypeStruct((bsz, length, GDN_V), F32),
                   jax.ShapeDtypeStruct((bsz, GDN_HEADS, 2, GDN_DK, GDN_DV), F32)),
        grid=(bsz, GDN_HEADS, n_steps),
        in_specs=[fwd(qoff), fwd(koff), fwd(voff), bwd(qoff), bwd(koff), bwd(voff),
                  pl.BlockSpec((None, tb, LANES), lambda b, h, s: (b, s, 0)),
                  pl.BlockSpec((None, tb, LANES), lambda b, h, s: (b, n_steps - 1 - s, 0)),
                  pl.BlockSpec((None, 4 * SUBLANES, tb), lambda b, h, s: (b, 0, s)),
                  pl.BlockSpec((None, 4 * SUBLANES, tb), lambda b, h, s: (b, 0, n_steps - 1 - s)),
                  wspec(qoff), wspec(koff), wspec(voff), state_spec],
        out_specs=(pl.BlockSpec((None, tb, LANES), lambda b, h, s: (b, s, h)),
                   pl.BlockSpec((None, tb, LANES), lambda b, h, s: (b, n_steps - 1 - s, h)),
                   state_spec),
        scratch_shapes=[pltpu.VMEM((2, GDN_DK, GDN_DV), F32)],
        compiler_params=_params(3),
        name="gdn_scan",
    )(big, big, big, big, big, big, p, p, pt, pt, conv_w, conv_w, conv_w, s0)


def _gdn_branch(big, small, big_c, small_c, conv_w, a_log, dt_bias):
    bsz = big.shape[0]
    s0 = jnp.zeros((bsz, GDN_HEADS, 2, GDN_DK, GDN_DV), F32)
    rows_t = lambda p: jnp.swapaxes(p[:, :, :4 * SUBLANES], 1, 2)
    p_c = _gdn_prep(small_c, a_log, dt_bias, GDN_TB)
    oc_f, oc_b, s_ctx = _gdn_scan(big_c, p_c, rows_t(p_c), conv_w, s0, big_c.shape[1])
    p = _gdn_prep(small, a_log, dt_bias, 2 * GDN_TB)
    o_f, o_b, _ = _gdn_scan(big, p, rows_t(p), conv_w, s_ctx, GRID_W)
    return (o_f, o_b), (oc_f, oc_b)


def _ssd_prep_kernel(s_ref, alog_ref, dtb_ref, dt_ref, gc_ref, *, rows):
    s = s_ref[...]
    lane = lax.broadcasted_iota(jnp.int32, (1, LANES), 1)
    dt = _softplus(s + dtb_ref[...])
    la = dt * (-jnp.exp(alog_ref[...]))
    dt_ref[...] = dt
    for c in range(rows // CHUNK):
        sl = slice(c * CHUNK, (c + 1) * CHUNK)
        fwd, bwd = _chunk_cumsums(la[sl, :])
        gc_ref[sl, :] = jnp.where(lane < SM_DT_B, fwd, bwd)


def _ssd_prep(small_cols, a_log, dt_bias, n_seg):
    bsz, rows, _ = small_cols.shape
    pad = lambda t: jnp.pad(t.reshape(1, -1).astype(F32), ((0, 0), (SM_DT, LANES - SM_DT - 2 * SSM_HEADS)))
    out = jax.ShapeDtypeStruct((bsz, n_seg, rows, LANES), F32)
    return pl.pallas_call(
        functools.partial(_ssd_prep_kernel, rows=rows),
        out_shape=(out, out),
        grid=(bsz, n_seg),
        in_specs=[pl.BlockSpec((None, rows, LANES), lambda b, c: (b, 0, c)),
                  pl.BlockSpec((1, LANES), lambda b, c: (0, 0)),
                  pl.BlockSpec((1, LANES), lambda b, c: (0, 0))],
        out_specs=(pl.BlockSpec((None, None, rows, LANES), lambda b, c: (b, c, 0, 0)),
                   pl.BlockSpec((None, None, rows, LANES), lambda b, c: (b, c, 0, 0))),
        compiler_params=_params(2),
        name="ssd_prep",
    )(small_cols, pad(a_log), pad(dt_bias))


def _ssd_chunk(x, bm, cm, dt_cols, gc_cols, gc_rows, state, lower, d_skip):
    incl, _ = _tri_masks(lower)
    cb = _mm_nt(cm, bm)
    p = SSM_HEAD_DIM
    y_intra, x_decayed, e_rows, e_last = [], [], [], []
    for j in range(SSM_HPG):
        gcol, grow = gc_cols[j], gc_rows[j]
        glast = grow[:, CHUNK - 1:CHUNK] if lower else grow[:, 0:1]
        xdt = x[:, j * p:(j + 1) * p] * dt_cols[j]
        decay = jnp.where(incl, jnp.exp(jnp.where(incl, gcol - grow, 0.0)), 0.0)
        y_intra.append(_mm(cb * decay, xdt))
        x_decayed.append(xdt * jnp.exp(glast - gcol))
        e_rows.append(jnp.broadcast_to(jnp.exp(gcol), (CHUNK, p)))
        e_last.append(jnp.broadcast_to(jnp.exp(glast), (1, p)))
    y = jnp.concatenate(y_intra, axis=1) + _mm(cm, state) * jnp.concatenate(e_rows, axis=1)
    if d_skip is not None:
        y = y + d_skip * x
    state = jnp.concatenate(e_last, axis=1) * state + _mm_tn(bm, jnp.concatenate(x_decayed, axis=1))
    return y, state


def _ssd_kernel(xf_ref, bf_ref, cf_ref, xb_ref, bb_ref, cb_ref, dtf_ref, dtb_ref, gcf_ref, gcb_ref, gtf_ref, gtb_ref,
                wx_ref, wb_ref, wc_ref, bx_ref, bbias_ref, bc_ref, d_ref, s0_ref,
                yf_ref, yb_ref, sout_ref, state_ref, *, rows, n_seg):
    g = pl.program_id(1)
    step = pl.program_id(2)

    @pl.when(step == 0)
    def _():
        state_ref[...] = s0_ref[...]

    lane = lax.broadcasted_iota(jnp.int32, (1, LANES), 1)

    def col(t, idx):
        return jnp.sum(jnp.where(lane == idx, t, 0.0), axis=1, keepdims=True)

    def prep(x_ref, b_ref, c_ref):
        x = _silu(_conv_seg(x_ref[...], wx_ref[...], rows) + bx_ref[...])
        bm = _silu(_conv_seg(b_ref[...], wb_ref[...], rows) + bbias_ref[...])
        cm = _silu(_conv_seg(c_ref[...], wc_ref[...], rows) + bc_ref[...])
        return x, bm, cm

    xf, bmf, cmf = prep(xf_ref, bf_ref, cf_ref)
    xb, bmb, cmb = prep(xb_ref, bb_ref, cb_ref)
    base_f = SM_DT + SSM_HPG * g
    base_b = SM_DT_B + SSM_HPG * g
    dtf, dtb, gcf, gcb = dtf_ref[...], dtb_ref[...], gcf_ref[...], gcb_ref[...]
    dt_cols_f = [col(dtf, base_f + j) for j in range(SSM_HPG)]
    gc_cols_f = [col(gcf, base_f + j) for j in range(SSM_HPG)]
    dt_cols_b = [col(dtb, base_b + j) for j in range(SSM_HPG)]
    gc_cols_b = [col(gcb, base_b + j) for j in range(SSM_HPG)]
    gc_rows_f = [gtf_ref[pl.ds(base_f + j, 1), :] for j in range(SSM_HPG)]
    gc_rows_b = [gtb_ref[pl.ds(base_b + j, 1), :] for j in range(SSM_HPG)]
    d_skip = d_ref[...]

    s_f = state_ref[0]
    s_b = state_ref[1]
    n_chunks = rows // CHUNK
    for c in range(n_chunks):
        sl = slice(c * CHUNK, (c + 1) * CHUNK)
        y, s_f = _ssd_chunk(xf[sl], bmf[sl], cmf[sl], [t[sl] for t in dt_cols_f], [t[sl] for t in gc_cols_f],
                            [t[:, sl] for t in gc_rows_f], s_f, True, d_skip)
        yf_ref[sl, :] = y
        sl = slice((n_chunks - 1 - c) * CHUNK, (n_chunks - c) * CHUNK)
        y, s_b = _ssd_chunk(xb[sl], bmb[sl], cmb[sl], [t[sl] for t in dt_cols_b], [t[sl] for t in gc_cols_b],
                            [t[:, sl] for t in gc_rows_b], s_b, False, None)
        yb_ref[sl, :] = y
    state_ref[0] = s_f
    state_ref[1] = s_b

    @pl.when(step == n_seg - 1)
    def _():
        sout_ref[0] = s_f
        sout_ref[1] = s_b


def _ssd_scan(big_cols, qdt, qgc, qgct, conv_w, conv_b, d_exp, s0, n_seg):
    bsz, rows, _ = big_cols.shape
    gw = SSM_HPG * SSM_HEAD_DIM
    x_blocks, n_blocks = BIG_N // gw, BIG_N // SSM_STATE
    xoff, boff, coff = XBC_OFF // gw, (XBC_OFF + SSM_INNER) // SSM_STATE, (XBC_OFF + SSM_INNER + SSM_GN) // SSM_STATE
    seg_f = lambda s: s
    seg_b = lambda s: n_seg - 1 - s
    xspec = lambda seg: pl.BlockSpec((None, rows, gw), lambda b, g, s: (b, 0, seg(s) * x_blocks + xoff + g))
    nspec = lambda seg, off: pl.BlockSpec((None, rows, SSM_STATE), lambda b, g, s: (b, 0, seg(s) * n_blocks + off + g))
    qspec = lambda seg: pl.BlockSpec((None, None, rows, LANES), lambda b, g, s: (b, seg(s), 0, 0))
    qtspec = lambda seg: pl.BlockSpec((None, None, LANES, rows), lambda b, g, s: (b, seg(s), 0, 0))
    wxspec = lambda r: pl.BlockSpec((r, gw), lambda b, g, s: (0, g))
    wnspec = lambda r, off: pl.BlockSpec((r, SSM_STATE), lambda b, g, s: (0, off + g))
    state_spec = pl.BlockSpec((None, None, 2, SSM_STATE, gw), lambda b, g, s: (b, g, 0, 0, 0))
    yspec = lambda seg: pl.BlockSpec((None, rows, gw), lambda b, g, s: (b, 0, seg(s) * SSM_GROUPS + g))
    nb, nc = SSM_INNER // SSM_STATE, (SSM_INNER + SSM_GN) // SSM_STATE
    y_shape = jax.ShapeDtypeStruct((bsz, rows, n_seg * SSM_INNER), F32)
    return pl.pallas_call(
        functools.partial(_ssd_kernel, rows=rows, n_seg=n_seg),
        out_shape=(y_shape, y_shape, jax.ShapeDtypeStruct((bsz, SSM_GROUPS, 2, SSM_STATE, gw), F32)),
        grid=(bsz, SSM_GROUPS, n_seg),
        in_specs=[xspec(seg_f), nspec(seg_f, boff), nspec(seg_f, coff),
                  xspec(seg_b), nspec(seg_b, boff), nspec(seg_b, coff),
                  qspec(seg_f), qspec(seg_b), qspec(seg_f), qspec(seg_b), qtspec(seg_f), qtspec(seg_b),
                  wxspec(CONV_K), wnspec(CONV_K, nb), wnspec(CONV_K, nc),
                  wxspec(1), wnspec(1, nb), wnspec(1, nc), wxspec(1), state_spec],
        out_specs=(yspec(seg_f), yspec(seg_b), state_spec),
        scratch_shapes=[pltpu.VMEM((2, SSM_STATE, gw), F32)],
        compiler_params=_params(3),
        name="ssd_scan",
    )(big_cols, big_cols, big_cols, big_cols, big_cols, big_cols, qdt, qdt, qgc, qgc, qgct, qgct,
      conv_w, conv_w, conv_w, conv_b, conv_b, conv_b, d_exp, s0)


def _ssd_branch(big, small, big_c, small_c, conv_w, conv_b, a_log, dt_bias, d_skip):
    bsz, length, _ = big.shape
    rows = length // GRID_W
    assert rows % CHUNK == 0 and big_c.shape[1] % CHUNK == 0
    conv_b = conv_b.reshape(1, SSM_XBC)
    d_exp = jnp.repeat(d_skip.astype(F32), SSM_HEAD_DIM).reshape(1, SSM_INNER)
    s0 = jnp.zeros((bsz, SSM_GROUPS, 2, SSM_STATE, SSM_HPG * SSM_HEAD_DIM), F32)
    qdt, qgc = _ssd_prep(small_c, a_log, dt_bias, 1)
    yc_f, yc_b, s_ctx = _ssd_scan(big_c, qdt, qgc, jnp.swapaxes(qgc, 2, 3), conv_w, conv_b, d_exp, s0, 1)
    qdt, qgc = _ssd_prep(small.reshape(bsz, rows, GRID_W * SMALL_N), a_log, dt_bias, GRID_W)
    y_f, y_b, _ = _ssd_scan(big.reshape(bsz, rows, GRID_W * BIG_N), qdt, qgc, jnp.swapaxes(qgc, 2, 3),
                            conv_w, conv_b, d_exp, s_ctx, GRID_W)
    y_f = y_f.reshape(bsz, length, SSM_INNER)
    y_b = y_b.reshape(bsz, length, SSM_INNER)
    return (y_f, y_b), (yc_f, yc_b)


def _layer_norm(r, g, b):
    mu = jnp.mean(r, axis=1, keepdims=True)
    var = jnp.mean(jnp.square(r - mu), axis=1, keepdims=True)
    return (r - mu) * lax.rsqrt(var + LN_EPS) * g + b


def _merge_kernel(x_ref, gout_ref, z_ref, ga_ref, gb_ref, of_ref, ob_ref, yf_ref, yb_ref,
                  nwa_ref, nwb_ref, wpg_ref, wps_ref, wout_ref, g1_ref, lng_ref, lnb_ref, o_ref, *, alpha):
    o = of_ref[...] + ob_ref[...]
    normed = []
    for h in range(GDN_HEADS):
        oh = o[:, h * GDN_DV:(h + 1) * GDN_DV]
        normed.append(oh * lax.rsqrt(jnp.mean(oh * oh, axis=1, keepdims=True) + NORM_EPS))
    y_a = jnp.concatenate(normed, axis=1) * nwa_ref[...] * _silu(gout_ref[...])
    t = (yf_ref[...] + yb_ref[...]) * _silu(z_ref[...])
    gw = SSM_INNER // SSM_GROUPS
    normed = []
    for g in range(SSM_GROUPS):
        tg = t[:, g * gw:(g + 1) * gw]
        normed.append(tg * lax.rsqrt(jnp.mean(tg * tg, axis=1, keepdims=True) + NORM_EPS))
    y_b = jnp.concatenate(normed, axis=1) * nwb_ref[...]
    mix = (jax.nn.sigmoid(ga_ref[...]) * jnp.dot(y_a.astype(BF16), wpg_ref[...], preferred_element_type=F32)
           + jax.nn.sigmoid(gb_ref[...]) * jnp.dot(y_b.astype(BF16), wps_ref[...], preferred_element_type=F32))
    out = jnp.dot(mix.astype(BF16), wout_ref[...], preferred_element_type=F32)
    o_ref[...] = _layer_norm(alpha * x_ref[...] + g1_ref[...] * out, lng_ref[...], lnb_ref[...])


def _merge(x, big, o_pair, y_pair, nwa, nwb, wpg, wps, wout, g1, ln_g, ln_b, alpha, tm):
    bsz, length, d = x.shape
    row = lambda width, off: pl.BlockSpec((None, tm, width), lambda b, i: (b, i, off // width))
    const = lambda shape: pl.BlockSpec(shape, lambda b, i: (0,) * len(shape))
    return pl.pallas_call(
        functools.partial(_merge_kernel, alpha=alpha),
        out_shape=jax.ShapeDtypeStruct((bsz, length, d), F32),
        grid=(bsz, length // tm),
        in_specs=[row(d, 0), row(GDN_V, GOUT_OFF), row(SSM_INNER, Z_OFF), row(d, GA_OFF), row(d, GB_OFF),
                  row(GDN_V, 0), row(GDN_V, 0), row(SSM_INNER, 0), row(SSM_INNER, 0),
                  const((1, GDN_V)), const((1, SSM_INNER)), const((GDN_V, d)), const((SSM_INNER, d)), const((d, d)),
                  pl.BlockSpec((None, 1, d), lambda b, i: (b, 0, 0)), const((1, d)), const((1, d))],
        out_specs=row(d, 0),
        compiler_params=_params(2),
        name="merge",
    )(x, big, big, big, big, o_pair[0], o_pair[1], y_pair[0], y_pair[1], nwa, nwb, wpg, wps, wout, g1, ln_g, ln_b)


def _mlp_kernel(x_ref, sc_ref, sh_ref, g2_ref, w1_ref, b1_ref, w2_ref, b2_ref, lng_ref, lnb_ref, o_ref, *, alpha):
    x = x_ref[...]
    h = (x * (1.0 + sc_ref[...]) + sh_ref[...]).astype(BF16)
    acc = None
    tf = D_MODEL
    for c in range(D_FF // tf):
        u = jnp.dot(h, w1_ref[:, c * tf:(c + 1) * tf], preferred_element_type=F32) + b1_ref[:, c * tf:(c + 1) * tf]
        u = jnp.square(jnp.maximum(u, 0.0))
        part = jnp.dot(u.astype(BF16), w2_ref[c * tf:(c + 1) * tf, :], preferred_element_type=F32)
        acc = part if acc is None else acc + part
    f = acc + b2_ref[...]
    o_ref[...] = _layer_norm(alpha * x + g2_ref[...] * f, lng_ref[...], lnb_ref[...])


def _mlp(x, sc, sh, g2, w1, b1, w2, b2, ln_g, ln_b, alpha, tm):
    bsz, length, d = x.shape
    const = lambda shape: pl.BlockSpec(shape, lambda b, i: (0,) * len(shape))
    mod = pl.BlockSpec((None, 1, d), lambda b, i: (b, 0, 0))
    return pl.pallas_call(
        functools.partial(_mlp_kernel, alpha=alpha),
        out_shape=jax.ShapeDtypeStruct((bsz, length, d), F32),
        grid=(bsz, length // tm),
        in_specs=[pl.BlockSpec((None, tm, d), lambda b, i: (b, i, 0)), mod, mod, mod,
                  const((d, D_FF)), const((1, D_FF)), const((D_FF, d)), const((1, d)), const((1, d)), const((1, d))],
        out_specs=pl.BlockSpec((None, tm, d), lambda b, i: (b, i, 0)),
        compiler_params=_params(2),
        name="mlp",
    )(x, sc, sh, g2, w1, b1, w2, b2, ln_g, ln_b)


def _split_w_in(w_in):
    pts, acc = [], 0
    for s in IN_SPLITS[:-1]:
        acc += s
        pts.append(acc)
    qkv, gout, a_raw, b_raw, z, xbc, dt_raw, gate_a, gate_b = jnp.split(w_in, pts, axis=1)
    big = jnp.concatenate([qkv, gout, xbc, z, gate_a, gate_b], axis=1).astype(BF16)
    pad = jnp.zeros((w_in.shape[0], SMALL_N - SM_DT - 2 * SSM_HEADS), w_in.dtype)
    small = jnp.concatenate([a_raw, b_raw, dt_raw, pad], axis=1).astype(BF16)
    return big, small


def kernel(x, c, ctx, c_ctx, w_mod, b_mod, w_in, gdn_conv_w, gdn_A_log, gdn_dt_bias, gdn_norm_w,
           ssm_conv_w, ssm_conv_b, ssm_A_log, ssm_dt_bias, ssm_D, ssm_norm_w,
           w_proj_gdn, w_proj_ssm, w_out, ln1_g, ln1_b, w_ff1, b_ff1, w_ff2, b_ff2, ln2_g, ln2_b):
    bsz, length, d = x.shape
    ctx_len = ctx.shape[1]
    depth = w_mod.shape[0]
    alpha = float((2 * depth) ** 0.25)
    mod_rows = -(-(bsz + 1) // (2 * SUBLANES)) * (2 * SUBLANES)
    cc = jnp.concatenate([c, c_ctx[None, :], jnp.zeros((mod_rows - bsz - 1, d), c.dtype)], axis=0)
    row2 = lambda t: t.reshape(1, -1)
    tm_lat = 1024
    tm_ctx = ctx_len

    for l in range(depth):
        last = l == depth - 1
        mod = _modulation(cc, w_mod[l], b_mod[l])
        lat = [mod[:bsz, i * d:(i + 1) * d].reshape(bsz, 1, d) for i in range(6)]
        cxm = [jnp.broadcast_to(mod[bsz, i * d:(i + 1) * d].reshape(1, 1, d), (bsz, 1, d)) for i in range(6)]
        w_big, w_small = _split_w_in(w_in[l])

        big = _inproj(x, lat[1], lat[0], w_big, tm_lat, 1024, "inproj_wide")
        small = _inproj(x, lat[1], lat[0], w_small, tm_lat, SMALL_N, "inproj_narrow")
        big_c = _inproj(ctx, cxm[1], cxm[0], w_big, tm_ctx, 1024, "inproj_wide_ctx")
        small_c = _inproj(ctx, cxm[1], cxm[0], w_small, tm_ctx, SMALL_N, "inproj_narrow_ctx")

        o_pair, oc_pair = _gdn_branch(big, small, big_c, small_c, gdn_conv_w[l], gdn_A_log[l], gdn_dt_bias[l])
        y_pair, yc_pair = _ssd_branch(big, small, big_c, small_c, ssm_conv_w[l], ssm_conv_b[l], ssm_A_log[l],
                                      ssm_dt_bias[l], ssm_D[l])

        nwa = jnp.tile(gdn_norm_w[l], GDN_HEADS).reshape(1, GDN_V)
        nwb = row2(ssm_norm_w[l])
        wpg, wps, wo = w_proj_gdn[l].astype(BF16), w_proj_ssm[l].astype(BF16), w_out[l].astype(BF16)
        w1, w2 = w_ff1[l].astype(BF16), w_ff2[l].astype(BF16)
        merge_args = (nwa, nwb, wpg, wps, wo)
        ln1 = (row2(ln1_g[l]), row2(ln1_b[l]))
        mlp_w = (w1, row2(b_ff1[l]), w2, row2(b_ff2[l]), row2(ln2_g[l]), row2(ln2_b[l]))

        x1 = _merge(x, big, o_pair, y_pair, *merge_args, lat[2], *ln1, alpha, 256)
        x = _mlp(x1, lat[4], lat[3], lat[5], *mlp_w, alpha, 512)
        if not last:
            c1 = _merge(ctx, big_c, oc_pair, yc_pair, *merge_args, cxm[2], *ln1, alpha, ctx_len)
            ctx = _mlp(c1, cxm[4], cxm[3], cxm[5], *mlp_w, alpha, ctx_len)
    return x
```

```python
import functools

import jax
import jax.numpy as jnp
from jax import lax
from jax.experimental import pallas as pl
from jax.experimental.pallas import tpu as pltpu

F32 = jnp.float32
BF16 = jnp.bfloat16
HIGHEST = lax.Precision.HIGHEST

D_MODEL = 1024
GRID_W = 64
GDN_HEADS = 8
GDN_DK = 128
GDN_DV = 128
GDN_QK = GDN_HEADS * GDN_DK
GDN_V = GDN_HEADS * GDN_DV
GDN_QKV = 2 * GDN_QK + GDN_V
SSM_INNER = 2 * D_MODEL
SSM_HEAD_DIM = 64
SSM_HEADS = SSM_INNER // SSM_HEAD_DIM
SSM_GROUPS = 8
SSM_HPG = SSM_HEADS // SSM_GROUPS
SSM_STATE = 128
SSM_GN = SSM_GROUPS * SSM_STATE
SSM_XBC = SSM_INNER + 2 * SSM_GN
CONV_K = 5
CHUNK = 64
D_FF = 4 * D_MODEL
LN_EPS = 1e-5
NORM_EPS = 1e-6
IN_SPLITS = (GDN_QKV, GDN_V, 2 * GDN_HEADS, 2 * GDN_HEADS, SSM_INNER, SSM_XBC, 2 * SSM_HEADS, D_MODEL, D_MODEL)

LANES = 128
SUBLANES = 8
VMEM_LIMIT_BYTES = 56 * 1024 * 1024

QKV_OFF = 0
GOUT_OFF = QKV_OFF + GDN_QKV
Z_OFF = GOUT_OFF + GDN_V
GA_OFF = Z_OFF + SSM_INNER
GB_OFF = GA_OFF + D_MODEL
BIG_N = GB_OFF + D_MODEL
SM_A = 0
SM_B = SM_A + 2 * GDN_HEADS
SM_DT = SM_B + 2 * GDN_HEADS
SM_DT_B = SM_DT + SSM_HEADS
SMALL_N = LANES

GDN_CHUNKS_PER_STEP = 4
GDN_TB = GDN_CHUNKS_PER_STEP * CHUNK
SSD_GROUPS_PER_STEP = 4


def _params(n_axes):
    return pltpu.CompilerParams(dimension_semantics=("arbitrary",) * n_axes, vmem_limit_bytes=VMEM_LIMIT_BYTES)


def _silu(t):
    return t * jax.nn.sigmoid(t)


def _softplus(t):
    return jnp.maximum(t, 0.0) + jnp.log(1.0 + jnp.exp(-jnp.abs(t)))


def _mm(a, b):
    return jnp.dot(a.astype(BF16), b.astype(BF16), preferred_element_type=F32)


def _mm_nt(a, b):
    return lax.dot_general(a.astype(BF16), b.astype(BF16), (((1,), (1,)), ((), ())), preferred_element_type=F32)


def _mm_tn(a, b):
    return lax.dot_general(a.astype(BF16), b.astype(BF16), (((0,), (0,)), ((), ())), preferred_element_type=F32)


def _tri_masks(lower):
    ii = lax.broadcasted_iota(jnp.int32, (CHUNK, CHUNK), 0)
    jj = lax.broadcasted_iota(jnp.int32, (CHUNK, CHUNK), 1)
    if lower:
        return ii >= jj, ii > jj
    return ii <= jj, ii < jj


def _conv_seg(x, w, seg):
    n = x.shape[0]
    pos = lax.broadcasted_iota(jnp.int32, (n, 1), 0) & (seg - 1)
    out = None
    for j in range(CONV_K):
        d = j - CONV_K // 2
        if d == 0:
            term = x * w[j:j + 1, :]
        else:
            shifted = pltpu.roll(x, shift=(-d) % n, axis=0)
            valid = (pos + d >= 0) & (pos + d < seg)
            term = jnp.where(valid, shifted, 0.0) * w[j:j + 1, :]
        out = term if out is None else out + term
    return out


def _mod_kernel(c_ref, w_ref, b_ref, o_ref):
    o_ref[...] = _mm(_silu(c_ref[...]), w_ref[...]) + b_ref[...]


def _modulation(cc, w, b):
    rows, d = cc.shape
    n = w.shape[1]
    tn = 1536
    return pl.pallas_call(
        _mod_kernel,
        out_shape=jax.ShapeDtypeStruct((rows, n), F32),
        grid=(n // tn,),
        in_specs=[pl.BlockSpec((rows, d), lambda j: (0, 0)),
                  pl.BlockSpec((d, tn), lambda j: (0, j)),
                  pl.BlockSpec((1, tn), lambda j: (0, j))],
        out_specs=pl.BlockSpec((rows, tn), lambda j: (0, j)),
        compiler_params=_params(1),
        name="modulation",
    )(cc, w, b.reshape(1, n))


def _inproj_kernel(x_ref, sc_ref, sh_ref, w_ref, o_ref):
    h = x_ref[...] * (1.0 + sc_ref[...]) + sh_ref[...]
    o_ref[...] = jnp.dot(h.astype(BF16), w_ref[...], preferred_element_type=F32)


def _inproj(x, sc, sh, w, tm, tn, name):
    bsz, length, d = x.shape
    n = w.shape[1]
    return pl.pallas_call(
        _inproj_kernel,
        out_shape=jax.ShapeDtypeStruct((bsz, length, n), F32),
        grid=(bsz, length // tm, n // tn),
        in_specs=[pl.BlockSpec((None, tm, d), lambda b, i, j: (b, i, 0)),
                  pl.BlockSpec((None, 1, d), lambda b, i, j: (b, 0, 0)),
                  pl.BlockSpec((None, 1, d), lambda b, i, j: (b, 0, 0)),
                  pl.BlockSpec((d, tn), lambda b, i, j: (0, j))],
        out_specs=pl.BlockSpec((None, tm, tn), lambda b, i, j: (b, i, j)),
        compiler_params=_params(3),
        name=name,
    )(x, sc, sh, w)


def _inproj_cols_kernel(x_ref, sc_ref, sh_ref, w_ref, o_ref, *, tn):
    h = (x_ref[...] * (1.0 + sc_ref[...]) + sh_ref[...]).astype(BF16)
    n_tok = h.shape[0]
    n_rows = n_tok // GRID_W
    n_out = w_ref.shape[1]
    ii = lax.broadcasted_iota(jnp.int32, (n_tok, n_tok), 0)
    jj = lax.broadcasted_iota(jnp.int32, (n_tok, n_tok), 1)
    src = (ii & (n_rows - 1)) * GRID_W + lax.shift_right_logical(ii, n_rows.bit_length() - 1)
    perm = jnp.where(jj == src, 1.0, 0.0).astype(BF16)
    hp = jnp.dot(perm, h, preferred_element_type=F32).astype(BF16)
    for n0 in range(0, n_out, tn):
        res = jnp.dot(hp, w_ref[:, n0:n0 + tn], preferred_element_type=F32)
        for c in range(GRID_W):
            o_ref[:, c * n_out + n0:c * n_out + n0 + tn] = res[c * n_rows:(c + 1) * n_rows, :]


def _inproj_cols(x, sc, sh, w, name):
    bsz, length, d = x.shape
    n = w.shape[1]
    n_rows = SUBLANES
    tm = n_rows * GRID_W
    return pl.pallas_call(
        functools.partial(_inproj_cols_kernel, tn=512),
        out_shape=jax.ShapeDtypeStruct((bsz, length // GRID_W, GRID_W * n), F32),
        grid=(bsz, length // tm),
        in_specs=[pl.BlockSpec((None, tm, d), lambda b, i: (b, i, 0)),
                  pl.BlockSpec((None, 1, d), lambda b, i: (b, 0, 0)),
                  pl.BlockSpec((None, 1, d), lambda b, i: (b, 0, 0)),
                  pl.BlockSpec((d, n), lambda b, i: (0, 0))],
        out_specs=pl.BlockSpec((None, n_rows, GRID_W * n), lambda b, i: (b, i, 0)),
        compiler_params=_params(2),
        name=name,
    )(x, sc, sh, w)


def _chunk_cumsums(t):
    ii = lax.broadcasted_iota(jnp.int32, (CHUNK, CHUNK), 0)
    jj = lax.broadcasted_iota(jnp.int32, (CHUNK, CHUNK), 1)
    ltri = (ii >= jj).astype(F32)
    utri = (ii <= jj).astype(F32)
    fwd = jnp.dot(ltri, t, precision=HIGHEST, preferred_element_type=F32)
    bwd = jnp.dot(utri, t, precision=HIGHEST, preferred_element_type=F32)
    return fwd, bwd


def _gdn_prep_kernel(s_ref, alog_ref, dtb_ref, o_ref, *, rows):
    s = s_ref[...]
    lane = lax.broadcasted_iota(jnp.int32, (1, LANES), 1)
    g = -jnp.exp(alog_ref[...]) * _softplus(s + dtb_ref[...])
    beta = jax.nn.sigmoid(s)
    for c in range(rows // CHUNK):
        sl = slice(c * CHUNK, (c + 1) * CHUNK)
        fwd, bwd = _chunk_cumsums(g[sl, :])
        gc = jnp.where(lane < SM_A + GDN_HEADS, fwd, bwd)
        o_ref[sl, :] = jnp.where(lane < SM_B, gc, beta[sl, :])


def _gdn_prep(small, a_log, dt_bias, tb):
    bsz, length, _ = small.shape
    pad = lambda t: jnp.pad(t.reshape(1, -1).astype(F32), ((0, 0), (SM_A, LANES - SM_A - 2 * GDN_HEADS)))
    return pl.pallas_call(
        functools.partial(_gdn_prep_kernel, rows=tb),
        out_shape=jax.ShapeDtypeStruct((bsz, length, LANES), F32),
        grid=(bsz, length // tb),
        in_specs=[pl.BlockSpec((None, tb, LANES), lambda b, i: (b, i, 0)),
                  pl.BlockSpec((1, LANES), lambda b, i: (0, 0)),
                  pl.BlockSpec((1, LANES), lambda b, i: (0, 0))],
        out_specs=pl.BlockSpec((None, tb, LANES), lambda b, i: (b, i, 0)),
        compiler_params=_params(2),
        name="gdn_prep",
    )(small, pad(a_log), pad(dt_bias))


def _tri_inverse(mats):
    ii = lax.broadcasted_iota(jnp.int32, (CHUNK, CHUNK), 0)
    jj = lax.broadcasted_iota(jnp.int32, (CHUNK, CHUNK), 1)
    eye = jnp.where(ii == jj, 1.0, 0.0)
    ts = [eye - a for a in mats]
    xs = [_mm(a, a) for a in mats]
    power = 2
    while 2 * power < CHUNK:
        ts = [t + _mm(t, x) for t, x in zip(ts, xs)]
        xs = [_mm(x, x) for x in xs]
        power *= 2
    return [t + _mm(t, x) for t, x in zip(ts, xs)]


def _gdn_chunks(items, masks):
    qs, ks, vs, gcols, grows, bcols, states, lowers = zip(*items)
    n = len(items)
    incl = [masks[lo][0] for lo in lowers]
    strict = [masks[lo][1] for lo in lowers]
    glast = [grows[i][:, CHUNK - 1:CHUNK] if lowers[i] else grows[i][:, 0:1] for i in range(n)]
    kk = [_mm_nt(k, k) for k in ks]
    qk = [_mm_nt(q, k) for q, k in zip(qs, ks)]
    dec = [jnp.exp(jnp.where(incl[i], gcols[i] - grows[i], 0.0)) for i in range(n)]
    a_mats = [jnp.where(strict[i], bcols[i] * kk[i] * dec[i], 0.0) for i in range(n)]
    t_inv = _tri_inverse(a_mats)
    eg = [jnp.exp(g) for g in gcols]
    rhs = [jnp.concatenate([(bcols[i] * eg[i]) * ks[i], bcols[i] * vs[i]], axis=1) for i in range(n)]
    wu = [_mm(t, r) for t, r in zip(t_inv, rhs)]
    s16 = [s.astype(BF16) for s in states]
    ws = [_mm(wu[i][:, :GDN_DK], s16[i]) for i in range(n)]
    qs_state = [_mm(qs[i] * eg[i], s16[i]) for i in range(n)]
    v_new = [wu[i][:, GDN_DK:] - ws[i] for i in range(n)]
    o_intra = [_mm(jnp.where(incl[i], qk[i] * dec[i], 0.0), v_new[i]) for i in range(n)]
    kv = [_mm_tn(ks[i] * jnp.exp(glast[i] - gcols[i]), v_new[i]) for i in range(n)]
    return [(qs_state[i] + o_intra[i], jnp.exp(glast[i]) * states[i] + kv[i]) for i in range(n)]


def _lane_column(t, idx, lane):
    if isinstance(idx, int):
        return t[:, idx:idx + 1]
    return jnp.sum(jnp.where(lane == idx, t, 0.0), axis=1, keepdims=True)


def _gdn_kernel(qf_ref, kf_ref, vf_ref, qb_ref, kb_ref, vb_ref, pf_ref, pb_ref, ptf_ref, ptb_ref,
                wq_ref, wk_ref, wv_ref, s0_ref, of_ref, ob_ref, sout_ref, state_ref, *, seg, n_steps, hs, nc):
    step = pl.program_id(2)
    head0 = 0 if hs == GDN_HEADS else pl.program_id(1) * hs

    @pl.when(step == 0)
    def _():
        state_ref[...] = s0_ref[...]

    lane = lax.broadcasted_iota(jnp.int32, (1, LANES), 1)

    def prep(q_ref, k_ref, v_ref):
        q = _silu(_conv_seg(q_ref[...], wq_ref[...], seg))
        k = _silu(_conv_seg(k_ref[...], wk_ref[...], seg))
        v = _silu(_conv_seg(v_ref[...], wv_ref[...], seg))
        heads = []
        for hh in range(hs):
            sl = slice(hh * GDN_DK, (hh + 1) * GDN_DK)
            qh, kh = q[:, sl], k[:, sl]
            qh = qh * lax.rsqrt(jnp.sum(qh * qh, axis=1, keepdims=True) + NORM_EPS) * (GDN_DK ** -0.5)
            kh = kh * lax.rsqrt(jnp.sum(kh * kh, axis=1, keepdims=True) + NORM_EPS)
            heads.append((qh, kh, v[:, sl]))
        return heads

    heads_f = prep(qf_ref, kf_ref, vf_ref)
    heads_b = prep(qb_ref, kb_ref, vb_ref)
    pf = pf_ref[...]
    pb = pb_ref[...]
    masks = {True: _tri_masks(True), False: _tri_masks(False)}
    cols = []
    for hh in range(hs):
        head = head0 + hh
        cols.append((_lane_column(pf, SM_A + head, lane), _lane_column(pf, SM_B + head, lane),
                     _lane_column(pb, SM_A + GDN_HEADS + head, lane), _lane_column(pb, SM_B + GDN_HEADS + head, lane)))
    states = [[state_ref[hh, 0], state_ref[hh, 1]] for hh in range(hs)]
    for c in range(nc):
        cb = nc - 1 - c
        sl_f = slice(c * CHUNK, (c + 1) * CHUNK)
        sl_b = slice(cb * CHUNK, (cb + 1) * CHUNK)
        items = []
        for hh in range(hs):
            head = head0 + hh
            gcol_f, bcol_f, gcol_b, bcol_b = cols[hh]
            qf, kf, vf = heads_f[hh]
            qb, kb, vb = heads_b[hh]
            grow_f = ptf_ref[c, pl.ds(SM_A + head, 1), :]
            grow_b = ptb_ref[cb, pl.ds(SM_A + GDN_HEADS + head, 1), :]
            items.append((qf[sl_f], kf[sl_f], vf[sl_f], gcol_f[sl_f], grow_f, bcol_f[sl_f], states[hh][0], True))
            items.append((qb[sl_b], kb[sl_b], vb[sl_b], gcol_b[sl_b], grow_b, bcol_b[sl_b], states[hh][1], False))
        results = _gdn_chunks(items, masks)
        for hh in range(hs):
            hl = slice(hh * GDN_DV, (hh + 1) * GDN_DV)
            of_ref[sl_f, hl], states[hh][0] = results[2 * hh]
            ob_ref[sl_b, hl], states[hh][1] = results[2 * hh + 1]
    for hh in range(hs):
        state_ref[hh, 0] = states[hh][0]
        state_ref[hh, 1] = states[hh][1]

    @pl.when(step == n_steps - 1)
    def _():
        sout_ref[...] = state_ref[...]


def _gdn_scan(big, p, pt, conv_w, s0, seg, hs, nc):
    bsz, length, _ = big.shape
    tb = nc * CHUNK
    n_steps = length // tb
    width = hs * GDN_DK
    qoff, koff, voff = QKV_OFF // width, (QKV_OFF + GDN_QK) // width, (QKV_OFF + 2 * GDN_QK) // width
    fwd = lambda off: pl.BlockSpec((None, tb, width), lambda b, h, s: (b, s, off + h))
    bwd = lambda off: pl.BlockSpec((None, tb, width), lambda b, h, s: (b, n_steps - 1 - s, off + h))
    wspec = lambda off: pl.BlockSpec((CONV_K, width), lambda b, h, s: (0, off + h))
    state_spec = pl.BlockSpec((None, hs, 2, GDN_DK, GDN_DV), lambda b, h, s: (b, h, 0, 0, 0))
    return pl.pallas_call(
        functools.partial(_gdn_kernel, seg=seg, n_steps=n_steps, hs=hs, nc=nc),
        out_shape=(jax.ShapeDtypeStruct((bsz, length, GDN_V), F32),
                   jax.ShapeDtypeStruct((bsz, length, GDN_V), F32),
                   jax.ShapeDtypeStruct((bsz, GDN_HEADS, 2, GDN_DK, GDN_DV), F32)),
        grid=(bsz, GDN_HEADS // hs, n_steps),
        in_specs=[fwd(qoff), fwd(koff), fwd(voff), bwd(qoff), bwd(koff), bwd(voff),
                  pl.BlockSpec((None, tb, LANES), lambda b, h, s: (b, s, 0)),
                  pl.BlockSpec((None, tb, LANES), lambda b, h, s: (b, n_steps - 1 - s, 0)),
                  pl.BlockSpec((None, nc, 4 * SUBLANES, CHUNK), lambda b, h, s: (b, s, 0, 0)),
                  pl.BlockSpec((None, nc, 4 * SUBLANES, CHUNK), lambda b, h, s: (b, n_steps - 1 - s, 0, 0)),
                  wspec(qoff), wspec(koff), wspec(voff), state_spec],
        out_specs=(pl.BlockSpec((None, tb, width), lambda b, h, s: (b, s, h)),
                   pl.BlockSpec((None, tb, width), lambda b, h, s: (b, n_steps - 1 - s, h)),
                   state_spec),
        scratch_shapes=[pltpu.VMEM((hs, 2, GDN_DK, GDN_DV), F32)],
        compiler_params=_params(3),
        name="gdn_scan",
    )(big, big, big, big, big, big, p, p, pt, pt, conv_w, conv_w, conv_w, s0)


def _gdn_branch(big, small, big_c, small_c, conv_w, a_log, dt_bias):
    bsz = big.shape[0]
    s0 = jnp.zeros((bsz, GDN_HEADS, 2, GDN_DK, GDN_DV), F32)

    def rows_t(p):
        chunks = p[:, :, :4 * SUBLANES].reshape(bsz, p.shape[1] // CHUNK, CHUNK, 4 * SUBLANES)
        return jnp.swapaxes(chunks, 2, 3)

    ctx_len = big_c.shape[1]
    p_c = _gdn_prep(small_c, a_log, dt_bias, ctx_len)
    oc_f, oc_b, s_ctx = _gdn_scan(big_c, p_c, rows_t(p_c), conv_w, s0, ctx_len, 1, ctx_len // CHUNK)
    p = _gdn_prep(small, a_log, dt_bias, 2 * GDN_TB)
    o_f, o_b, _ = _gdn_scan(big, p, rows_t(p), conv_w, s_ctx, GRID_W, GDN_HEADS, 1)
    return (o_f, o_b), (oc_f, oc_b)


def _ssd_prep_kernel(s_ref, alog_ref, dtb_ref, dt_ref, gc_ref, *, rows):
    s = s_ref[...]
    lane = lax.broadcasted_iota(jnp.int32, (1, LANES), 1)
    dt = _softplus(s + dtb_ref[...])
    la = dt * (-jnp.exp(alog_ref[...]))
    dt_ref[...] = dt
    for c in range(rows // CHUNK):
        sl = slice(c * CHUNK, (c + 1) * CHUNK)
        fwd, bwd = _chunk_cumsums(la[sl, :])
        gc_ref[sl, :] = jnp.where(lane < SM_DT_B, fwd, bwd)


def _ssd_prep(small_cols, a_log, dt_bias, n_seg):
    bsz, rows, _ = small_cols.shape
    pad = lambda t: jnp.pad(t.reshape(1, -1).astype(F32), ((0, 0), (SM_DT, LANES - SM_DT - 2 * SSM_HEADS)))
    out = jax.ShapeDtypeStruct((bsz, n_seg, rows, LANES), F32)
    return pl.pallas_call(
        functools.partial(_ssd_prep_kernel, rows=rows),
        out_shape=(out, out),
        grid=(bsz, n_seg),
        in_specs=[pl.BlockSpec((None, rows, LANES), lambda b, c: (b, 0, c)),
                  pl.BlockSpec((1, LANES), lambda b, c: (0, 0)),
                  pl.BlockSpec((1, LANES), lambda b, c: (0, 0))],
        out_specs=(pl.BlockSpec((None, None, rows, LANES), lambda b, c: (b, c, 0, 0)),
                   pl.BlockSpec((None, None, rows, LANES), lambda b, c: (b, c, 0, 0))),
        compiler_params=_params(2),
        name="ssd_prep",
    )(small_cols, pad(a_log), pad(dt_bias))


def _ssd_chunks(items, masks):
    n = len(items)
    p = SSM_HEAD_DIM
    heads = range(SSM_HPG)
    cb = [_mm_nt(it[2], it[1]) for it in items]
    inter = [_mm(it[2], it[6]) for it in items]
    xdt, lhs, x_decayed, e_rows, e_last = [], [], [], [], []
    for x, bm, cm, dt_cols, gc_cols, gc_rows, state, lower, d_skip in items:
        incl = masks[lower][0]
        for j in heads:
            gcol, grow = gc_cols[j], gc_rows[j]
            glast = grow[:, CHUNK - 1:CHUNK] if lower else grow[:, 0:1]
            xj = x[:, j * p:(j + 1) * p] * dt_cols[j]
            xdt.append(xj)
            lhs.append(jnp.where(incl, jnp.exp(jnp.where(incl, gcol - grow, 0.0)), 0.0))
            x_decayed.append(xj * jnp.exp(glast - gcol))
            e_rows.append(jnp.broadcast_to(jnp.exp(gcol), (CHUNK, p)))
            e_last.append(jnp.broadcast_to(jnp.exp(glast), (1, p)))
    intra = [_mm(cb[i // SSM_HPG] * lhs[i], xdt[i]) for i in range(n * SSM_HPG)]
    upd = [_mm_tn(items[i][1], jnp.concatenate(x_decayed[i * SSM_HPG:(i + 1) * SSM_HPG], axis=1)) for i in range(n)]
    out = []
    for i, it in enumerate(items):
        hsl = slice(i * SSM_HPG, (i + 1) * SSM_HPG)
        y = jnp.concatenate(intra[hsl], axis=1) + inter[i] * jnp.concatenate(e_rows[hsl], axis=1)
        if it[8] is not None:
            y = y + it[8] * it[0]
        out.append((y, jnp.concatenate(e_last[hsl], axis=1) * it[6] + upd[i]))
    return out


def _ssd_kernel(xf_ref, bf_ref, cf_ref, xb_ref, bb_ref, cb_ref, dtf_ref, dtb_ref, gcf_ref, gcb_ref, gtf_ref, gtb_ref,
                wx_ref, wb_ref, wc_ref, bx_ref, bbias_ref, bc_ref, d_ref, s0_ref,
                yf_ref, yb_ref, sout_ref, state_ref, *, rows, n_seg, gs):
    step = pl.program_id(2)
    group0 = 0 if gs == SSM_GROUPS else pl.program_id(1) * gs

    @pl.when(step == 0)
    def _():
        state_ref[...] = s0_ref[...]

    lane = lax.broadcasted_iota(jnp.int32, (1, LANES), 1)

    def prep(x_ref, b_ref, c_ref):
        x = _silu(_conv_seg(x_ref[...], wx_ref[...], rows) + bx_ref[...])
        bm = _silu(_conv_seg(b_ref[...], wb_ref[...], rows) + bbias_ref[...])
        cm = _silu(_conv_seg(c_ref[...], wc_ref[...], rows) + bc_ref[...])
        return x, bm, cm

    xf, bmf, cmf = prep(xf_ref, bf_ref, cf_ref)
    xb, bmb, cmb = prep(xb_ref, bb_ref, cb_ref)
    dtf, dtb, gcf, gcb = dtf_ref[...], dtb_ref[...], gcf_ref[...], gcb_ref[...]
    d_all = d_ref[...]
    gw = SSM_HPG * SSM_HEAD_DIM
    n_chunks = rows // CHUNK
    masks = {True: _tri_masks(True), False: _tri_masks(False)}
    per_group = []
    for gg in range(gs):
        base_f = SM_DT + SSM_HPG * (group0 + gg)
        base_b = SM_DT_B + SSM_HPG * (group0 + gg)
        per_group.append((
            [_lane_column(dtf, base_f + j, lane) for j in range(SSM_HPG)],
            [_lane_column(gcf, base_f + j, lane) for j in range(SSM_HPG)],
            [gtf_ref[pl.ds(base_f + j, 1), :] for j in range(SSM_HPG)],
            [_lane_column(dtb, base_b + j, lane) for j in range(SSM_HPG)],
            [_lane_column(gcb, base_b + j, lane) for j in range(SSM_HPG)],
            [gtb_ref[pl.ds(base_b + j, 1), :] for j in range(SSM_HPG)]))
    states = [[state_ref[gg, 0], state_ref[gg, 1]] for gg in range(gs)]
    for c in range(n_chunks):
        sl_f = slice(c * CHUNK, (c + 1) * CHUNK)
        sl_b = slice((n_chunks - 1 - c) * CHUNK, (n_chunks - c) * CHUNK)
        items = []
        for gg in range(gs):
            dt_f, gc_f, gr_f, dt_b, gc_b, gr_b = per_group[gg]
            xl = slice(gg * gw, (gg + 1) * gw)
            nl = slice(gg * SSM_STATE, (gg + 1) * SSM_STATE)
            items.append((xf[sl_f, xl], bmf[sl_f, nl], cmf[sl_f, nl], [t[sl_f] for t in dt_f], [t[sl_f] for t in gc_f],
                          [t[:, sl_f] for t in gr_f], states[gg][0], True, d_all[:, xl]))
            items.append((xb[sl_b, xl], bmb[sl_b, nl], cmb[sl_b, nl], [t[sl_b] for t in dt_b], [t[sl_b] for t in gc_b],
                          [t[:, sl_b] for t in gr_b], states[gg][1], False, None))
        results = _ssd_chunks(items, masks)
        for gg in range(gs):
            xl = slice(gg * gw, (gg + 1) * gw)
            yf_ref[sl_f, xl], states[gg][0] = results[2 * gg]
            yb_ref[sl_b, xl], states[gg][1] = results[2 * gg + 1]
    for gg in range(gs):
        state_ref[gg, 0] = states[gg][0]
        state_ref[gg, 1] = states[gg][1]

    @pl.when(step == n_seg - 1)
    def _():
        sout_ref[...] = state_ref[...]


def _ssd_scan(xbc_cols, qdt, qgc, qgct, conv_w, conv_b, d_exp, s0, n_seg, gs):
    bsz, rows, _ = xbc_cols.shape
    xw, nw = gs * SSM_HPG * SSM_HEAD_DIM, gs * SSM_STATE
    x_blocks, n_blocks = SSM_XBC // xw, SSM_XBC // nw
    boff, coff = SSM_INNER // nw, (SSM_INNER + SSM_GN) // nw
    seg_f = lambda s: s
    seg_b = lambda s: n_seg - 1 - s
    xspec = lambda seg: pl.BlockSpec((None, rows, xw), lambda b, g, s: (b, 0, seg(s) * x_blocks + g))
    nspec = lambda seg, off: pl.BlockSpec((None, rows, nw), lambda b, g, s: (b, 0, seg(s) * n_blocks + off + g))
    qspec = lambda seg: pl.BlockSpec((None, None, rows, LANES), lambda b, g, s: (b, seg(s), 0, 0))
    qtspec = lambda seg: pl.BlockSpec((None, None, LANES, rows), lambda b, g, s: (b, seg(s), 0, 0))
    wxspec = lambda r: pl.BlockSpec((r, xw), lambda b, g, s: (0, g))
    wnspec = lambda r, off: pl.BlockSpec((r, nw), lambda b, g, s: (0, off + g))
    state_spec = pl.BlockSpec((None, gs, 2, SSM_STATE, SSM_HPG * SSM_HEAD_DIM), lambda b, g, s: (b, g, 0, 0, 0))
    yspec = lambda seg: pl.BlockSpec((None, rows, xw), lambda b, g, s: (b, 0, seg(s) * (SSM_INNER // xw) + g))
    y_shape = jax.ShapeDtypeStruct((bsz, rows, n_seg * SSM_INNER), F32)
    return pl.pallas_call(
        functools.partial(_ssd_kernel, rows=rows, n_seg=n_seg, gs=gs),
        out_shape=(y_shape, y_shape,
                   jax.ShapeDtypeStruct((bsz, SSM_GROUPS, 2, SSM_STATE, SSM_HPG * SSM_HEAD_DIM), F32)),
        grid=(bsz, SSM_GROUPS // gs, n_seg),
        in_specs=[xspec(seg_f), nspec(seg_f, boff), nspec(seg_f, coff),
                  xspec(seg_b), nspec(seg_b, boff), nspec(seg_b, coff),
                  qspec(seg_f), qspec(seg_b), qspec(seg_f), qspec(seg_b), qtspec(seg_f), qtspec(seg_b),
                  wxspec(CONV_K), wnspec(CONV_K, boff), wnspec(CONV_K, coff),
                  wxspec(1), wnspec(1, boff), wnspec(1, coff), wxspec(1), state_spec],
        out_specs=(yspec(seg_f), yspec(seg_b), state_spec),
        scratch_shapes=[pltpu.VMEM((gs, 2, SSM_STATE, SSM_HPG * SSM_HEAD_DIM), F32)],
        compiler_params=_params(3),
        name="ssd_scan",
    )(xbc_cols, xbc_cols, xbc_cols, xbc_cols, xbc_cols, xbc_cols, qdt, qdt, qgc, qgc, qgct, qgct,
      conv_w, conv_w, conv_w, conv_b, conv_b, conv_b, d_exp, s0)


def _ssd_branch(xbc_cols, small, xbc_c, small_c, conv_w, conv_b, a_log, dt_bias, d_skip):
    bsz, rows, _ = xbc_cols.shape
    length = rows * GRID_W
    assert rows % CHUNK == 0 and xbc_c.shape[1] % CHUNK == 0
    conv_b = conv_b.reshape(1, SSM_XBC)
    d_exp = jnp.repeat(d_skip.astype(F32), SSM_HEAD_DIM).reshape(1, SSM_INNER)
    s0 = jnp.zeros((bsz, SSM_GROUPS, 2, SSM_STATE, SSM_HPG * SSM_HEAD_DIM), F32)
    qdt, qgc = _ssd_prep(small_c, a_log, dt_bias, 1)
    yc_f, yc_b, s_ctx = _ssd_scan(xbc_c, qdt, qgc, jnp.swapaxes(qgc, 2, 3), conv_w, conv_b, d_exp, s0, 1, 1)
    qdt, qgc = _ssd_prep(small.reshape(bsz, rows, GRID_W * SMALL_N), a_log, dt_bias, GRID_W)
    y_f, y_b, _ = _ssd_scan(xbc_cols, qdt, qgc, jnp.swapaxes(qgc, 2, 3), conv_w, conv_b, d_exp, s_ctx,
                            GRID_W, SSD_GROUPS_PER_STEP)
    y_f = y_f.reshape(bsz, length, SSM_INNER)
    y_b = y_b.reshape(bsz, length, SSM_INNER)
    return (y_f, y_b), (yc_f, yc_b)


def _layer_norm(r, g, b):
    mu = jnp.mean(r, axis=1, keepdims=True)
    var = jnp.mean(jnp.square(r - mu), axis=1, keepdims=True)
    return (r - mu) * lax.rsqrt(var + LN_EPS) * g + b


def _merge_kernel(x_ref, gout_ref, z_ref, ga_ref, gb_ref, of_ref, ob_ref, yf_ref, yb_ref,
                  nwa_ref, nwb_ref, wpg_ref, wps_ref, wout_ref, g1_ref, lng_ref, lnb_ref, o_ref, *, alpha):
    o = of_ref[...] + ob_ref[...]
    normed = []
    for h in range(GDN_HEADS):
        oh = o[:, h * GDN_DV:(h + 1) * GDN_DV]
        normed.append(oh * lax.rsqrt(jnp.mean(oh * oh, axis=1, keepdims=True) + NORM_EPS))
    y_a = jnp.concatenate(normed, axis=1) * nwa_ref[...] * _silu(gout_ref[...])
    t = (yf_ref[...] + yb_ref[...]) * _silu(z_ref[...])
    gw = SSM_INNER // SSM_GROUPS
    normed = []
    for g in range(SSM_GROUPS):
        tg = t[:, g * gw:(g + 1) * gw]
        normed.append(tg * lax.rsqrt(jnp.mean(tg * tg, axis=1, keepdims=True) + NORM_EPS))
    y_b = jnp.concatenate(normed, axis=1) * nwb_ref[...]
    mix = (jax.nn.sigmoid(ga_ref[...]) * jnp.dot(y_a.astype(BF16), wpg_ref[...], preferred_element_type=F32)
           + jax.nn.sigmoid(gb_ref[...]) * jnp.dot(y_b.astype(BF16), wps_ref[...], preferred_element_type=F32))
    out = jnp.dot(mix.astype(BF16), wout_ref[...], preferred_element_type=F32)
    o_ref[...] = _layer_norm(alpha * x_ref[...] + g1_ref[...] * out, lng_ref[...], lnb_ref[...])


def _merge(x, big, o_pair, y_pair, nwa, nwb, wpg, wps, wout, g1, ln_g, ln_b, alpha, tm):
    bsz, length, d = x.shape
    row = lambda width, off: pl.BlockSpec((None, tm, width), lambda b, i: (b, i, off // width))
    const = lambda shape: pl.BlockSpec(shape, lambda b, i: (0,) * len(shape))
    return pl.pallas_call(
        functools.partial(_merge_kernel, alpha=alpha),
        out_shape=jax.ShapeDtypeStruct((bsz, length, d), F32),
        grid=(bsz, length // tm),
        in_specs=[row(d, 0), row(GDN_V, GOUT_OFF), row(SSM_INNER, Z_OFF), row(d, GA_OFF), row(d, GB_OFF),
                  row(GDN_V, 0), row(GDN_V, 0), row(SSM_INNER, 0), row(SSM_INNER, 0),
                  const((1, GDN_V)), const((1, SSM_INNER)), const((GDN_V, d)), const((SSM_INNER, d)), const((d, d)),
                  pl.BlockSpec((None, 1, d), lambda b, i: (b, 0, 0)), const((1, d)), const((1, d))],
        out_specs=row(d, 0),
        compiler_params=_params(2),
        name="merge",
    )(x, big, big, big, big, o_pair[0], o_pair[1], y_pair[0], y_pair[1], nwa, nwb, wpg, wps, wout, g1, ln_g, ln_b)


def _mlp_kernel(x_ref, sc_ref, sh_ref, g2_ref, w1_ref, b1_ref, w2_ref, b2_ref, lng_ref, lnb_ref, o_ref, *, alpha):
    x = x_ref[...]
    h = (x * (1.0 + sc_ref[...]) + sh_ref[...]).astype(BF16)
    acc = None
    tf = D_MODEL
    for c in range(D_FF // tf):
        u = jnp.dot(h, w1_ref[:, c * tf:(c + 1) * tf], preferred_element_type=F32) + b1_ref[:, c * tf:(c + 1) * tf]
        u = jnp.square(jnp.maximum(u, 0.0))
        part = jnp.dot(u.astype(BF16), w2_ref[c * tf:(c + 1) * tf, :], preferred_element_type=F32)
        acc = part if acc is None else acc + part
    f = acc + b2_ref[...]
    o_ref[...] = _layer_norm(alpha * x + g2_ref[...] * f, lng_ref[...], lnb_ref[...])


def _mlp(x, sc, sh, g2, w1, b1, w2, b2, ln_g, ln_b, alpha, tm):
    bsz, length, d = x.shape
    const = lambda shape: pl.BlockSpec(shape, lambda b, i: (0,) * len(shape))
    mod = pl.BlockSpec((None, 1, d), lambda b, i: (b, 0, 0))
    return pl.pallas_call(
        functools.partial(_mlp_kernel, alpha=alpha),
        out_shape=jax.ShapeDtypeStruct((bsz, length, d), F32),
        grid=(bsz, length // tm),
        in_specs=[pl.BlockSpec((None, tm, d), lambda b, i: (b, i, 0)), mod, mod, mod,
                  const((d, D_FF)), const((1, D_FF)), const((D_FF, d)), const((1, d)), const((1, d)), const((1, d))],
        out_specs=pl.BlockSpec((None, tm, d), lambda b, i: (b, i, 0)),
        compiler_params=_params(2),
        name="mlp",
    )(x, sc, sh, g2, w1, b1, w2, b2, ln_g, ln_b)


def _split_w_in(w_in):
    pts, acc = [], 0
    for s in IN_SPLITS[:-1]:
        acc += s
        pts.append(acc)
    qkv, gout, a_raw, b_raw, z, xbc, dt_raw, gate_a, gate_b = jnp.split(w_in, pts, axis=1)
    big = jnp.concatenate([qkv, gout, z, gate_a, gate_b], axis=1).astype(BF16)
    pad = jnp.zeros((w_in.shape[0], SMALL_N - SM_DT - 2 * SSM_HEADS), w_in.dtype)
    small = jnp.concatenate([a_raw, b_raw, dt_raw, pad], axis=1).astype(BF16)
    return big, xbc.astype(BF16), small


def kernel(x, c, ctx, c_ctx, w_mod, b_mod, w_in, gdn_conv_w, gdn_A_log, gdn_dt_bias, gdn_norm_w,
           ssm_conv_w, ssm_conv_b, ssm_A_log, ssm_dt_bias, ssm_D, ssm_norm_w,
           w_proj_gdn, w_proj_ssm, w_out, ln1_g, ln1_b, w_ff1, b_ff1, w_ff2, b_ff2, ln2_g, ln2_b):
    bsz, length, d = x.shape
    ctx_len = ctx.shape[1]
    depth = w_mod.shape[0]
    alpha = float((2 * depth) ** 0.25)
    mod_rows = -(-(bsz + 1) // (2 * SUBLANES)) * (2 * SUBLANES)
    cc = jnp.concatenate([c, c_ctx[None, :], jnp.zeros((mod_rows - bsz - 1, d), c.dtype)], axis=0)
    row2 = lambda t: t.reshape(1, -1)
    tm_lat = 1024
    tm_ctx = ctx_len

    for l in range(depth):
        last = l == depth - 1
        mod = _modulation(cc, w_mod[l], b_mod[l])
        lat = [mod[:bsz, i * d:(i + 1) * d].reshape(bsz, 1, d) for i in range(6)]
        cxm = [jnp.broadcast_to(mod[bsz, i * d:(i + 1) * d].reshape(1, 1, d), (bsz, 1, d)) for i in range(6)]
        w_big, w_xbc, w_small = _split_w_in(w_in[l])

        big = _inproj(x, lat[1], lat[0], w_big, tm_lat, 1024, "inproj_wide")
        xbc_cols = _inproj_cols(x, lat[1], lat[0], w_xbc, "inproj_xbc")
        small = _inproj(x, lat[1], lat[0], w_small, tm_lat, SMALL_N, "inproj_narrow")
        big_c = _inproj(ctx, cxm[1], cxm[0], w_big, tm_ctx, 1024, "inproj_wide_ctx")
        xbc_c = _inproj(ctx, cxm[1], cxm[0], w_xbc, tm_ctx, 1024, "inproj_xbc_ctx")
        small_c = _inproj(ctx, cxm[1], cxm[0], w_small, tm_ctx, SMALL_N, "inproj_narrow_ctx")

        o_pair, oc_pair = _gdn_branch(big, small, big_c, small_c, gdn_conv_w[l], gdn_A_log[l], gdn_dt_bias[l])
        y_pair, yc_pair = _ssd_branch(xbc_cols, small, xbc_c, small_c, ssm_conv_w[l], ssm_conv_b[l], ssm_A_log[l],
                                      ssm_dt_bias[l], ssm_D[l])

        nwa = jnp.tile(gdn_norm_w[l], GDN_HEADS).reshape(1, GDN_V)
        nwb = row2(ssm_norm_w[l])
        wpg, wps, wo = w_proj_gdn[l].astype(BF16), w_proj_ssm[l].astype(BF16), w_out[l].astype(BF16)
        w1, w2 = w_ff1[l].astype(BF16), w_ff2[l].astype(BF16)
        merge_args = (nwa, nwb, wpg, wps, wo)
        ln1 = (row2(ln1_g[l]), row2(ln1_b[l]))
        mlp_w = (w1, row2(b_ff1[l]), w2, row2(b_ff2[l]), row2(ln2_g[l]), row2(ln2_b[l]))

        x1 = _merge(x, big, o_pair, y_pair, *merge_args, lat[2], *ln1, alpha, 256)
        x = _mlp(x1, lat[4], lat[3], lat[5], *mlp_w, alpha, 512)
        if not last:
            c1 = _merge(ctx, big_c, oc_pair, yc_pair, *merge_args, cxm[2], *ln1, alpha, ctx_len)
            ctx = _mlp(c1, cxm[4], cxm[3], cxm[5], *mlp_w, alpha, ctx_len)
    return x
```

```python
import functools

import jax
import jax.numpy as jnp
from jax import lax
from jax.experimental import pallas as pl
from jax.experimental.pallas import tpu as pltpu

F32 = jnp.float32
BF16 = jnp.bfloat16
HIGHEST = lax.Precision.HIGHEST

D_MODEL = 1024
GRID_W = 64
GDN_HEADS = 8
GDN_DK = 128
GDN_DV = 128
GDN_QK = GDN_HEADS * GDN_DK
GDN_V = GDN_HEADS * GDN_DV
GDN_QKV = 2 * GDN_QK + GDN_V
SSM_INNER = 2 * D_MODEL
SSM_HEAD_DIM = 64
SSM_HEADS = SSM_INNER // SSM_HEAD_DIM
SSM_GROUPS = 8
SSM_HPG = SSM_HEADS // SSM_GROUPS
SSM_STATE = 128
SSM_GN = SSM_GROUPS * SSM_STATE
SSM_XBC = SSM_INNER + 2 * SSM_GN
CONV_K = 5
CHUNK = 64
D_FF = 4 * D_MODEL
LN_EPS = 1e-5
NORM_EPS = 1e-6
IN_SPLITS = (GDN_QKV, GDN_V, 2 * GDN_HEADS, 2 * GDN_HEADS, SSM_INNER, SSM_XBC, 2 * SSM_HEADS, D_MODEL, D_MODEL)

LANES = 128
SUBLANES = 8
VMEM_LIMIT_BYTES = 56 * 1024 * 1024

Z_OFF = 0
GOUT_OFF = Z_OFF + SSM_INNER
GA_OFF = GOUT_OFF + GDN_V
GB_OFF = GA_OFF + D_MODEL
BIG_N = GB_OFF + D_MODEL
SM_A = 0
SM_B = SM_A + 2 * GDN_HEADS
SM_DT = SM_B + 2 * GDN_HEADS
SM_DT_B = SM_DT + SSM_HEADS
SMALL_N = LANES

GDN_CHUNKS_PER_STEP = 4
GDN_TB = GDN_CHUNKS_PER_STEP * CHUNK
SSD_GROUPS_PER_STEP = 4


def _params(n_axes):
    return pltpu.CompilerParams(dimension_semantics=("arbitrary",) * n_axes, vmem_limit_bytes=VMEM_LIMIT_BYTES)


def _silu(t):
    return t * jax.nn.sigmoid(t)


def _softplus(t):
    return jnp.maximum(t, 0.0) + jnp.log(1.0 + jnp.exp(-jnp.abs(t)))


def _mm(a, b):
    return jnp.dot(a.astype(BF16), b.astype(BF16), preferred_element_type=F32)


def _mm_nt(a, b):
    return lax.dot_general(a.astype(BF16), b.astype(BF16), (((1,), (1,)), ((), ())), preferred_element_type=F32)


def _mm_tn(a, b):
    return lax.dot_general(a.astype(BF16), b.astype(BF16), (((0,), (0,)), ((), ())), preferred_element_type=F32)


def _tri_masks(lower):
    ii = lax.broadcasted_iota(jnp.int32, (CHUNK, CHUNK), 0)
    jj = lax.broadcasted_iota(jnp.int32, (CHUNK, CHUNK), 1)
    if lower:
        return ii >= jj, ii > jj
    return ii <= jj, ii < jj


def _conv_seg(x, w, seg):
    n = x.shape[0]
    assert seg & (seg - 1) == 0 and n % seg == 0
    pos = lax.broadcasted_iota(jnp.int32, (n, 1), 0) & (seg - 1)
    out = None
    for j in range(CONV_K):
        d = j - CONV_K // 2
        if d == 0:
            term = x * w[j:j + 1, :]
        else:
            shifted = pltpu.roll(x, shift=(-d) % n, axis=0)
            valid = (pos + d >= 0) & (pos + d < seg)
            term = jnp.where(valid, shifted, 0.0) * w[j:j + 1, :]
        out = term if out is None else out + term
    return out


def _mod_kernel(c_ref, w_ref, b_ref, o_ref):
    o_ref[...] = _mm(_silu(c_ref[...]), w_ref[...]) + b_ref[...]


def _modulation(cc, w, b):
    rows, d = cc.shape
    n = w.shape[1]
    tn = 1536
    return pl.pallas_call(
        _mod_kernel,
        out_shape=jax.ShapeDtypeStruct((rows, n), F32),
        grid=(n // tn,),
        in_specs=[pl.BlockSpec((rows, d), lambda j: (0, 0)),
                  pl.BlockSpec((d, tn), lambda j: (0, j)),
                  pl.BlockSpec((1, tn), lambda j: (0, j))],
        out_specs=pl.BlockSpec((rows, tn), lambda j: (0, j)),
        compiler_params=_params(1),
        name="modulation",
    )(cc, w, b.reshape(1, n))


def _inproj_kernel(x_ref, sc_ref, sh_ref, w_ref, o_ref):
    h = x_ref[...] * (1.0 + sc_ref[...]) + sh_ref[...]
    o_ref[...] = jnp.dot(h.astype(BF16), w_ref[...], preferred_element_type=F32)


def _inproj(x, sc, sh, w, tm, tn, name):
    bsz, length, d = x.shape
    n = w.shape[1]
    return pl.pallas_call(
        _inproj_kernel,
        out_shape=jax.ShapeDtypeStruct((bsz, length, n), F32),
        grid=(bsz, length // tm, n // tn),
        in_specs=[pl.BlockSpec((None, tm, d), lambda b, i, j: (b, i, 0)),
                  pl.BlockSpec((None, 1, d), lambda b, i, j: (b, 0, 0)),
                  pl.BlockSpec((None, 1, d), lambda b, i, j: (b, 0, 0)),
                  pl.BlockSpec((d, tn), lambda b, i, j: (0, j))],
        out_specs=pl.BlockSpec((None, tm, tn), lambda b, i, j: (b, i, j)),
        compiler_params=_params(3),
        name=name,
    )(x, sc, sh, w)


def _proj_conv_kernel(*refs, seg, cols, has_bias, norm_tiles):
    x_ref, sc_ref, sh_ref, w_ref, cw_ref = refs[:5]
    cb_ref = refs[5] if has_bias else None
    o_ref, h_ref = refs[5 + has_bias:]
    j = pl.program_id(2)
    n_tok = h_ref.shape[0]

    @pl.when(j == 0)
    def _():
        x = x_ref[...]
        if cols:
            cps = x.shape[1]
            x = x.reshape(n_tok, x.shape[2])
        h = (x * (1.0 + sc_ref[...]) + sh_ref[...]).astype(BF16)
        if cols:
            ii = lax.broadcasted_iota(jnp.int32, (n_tok, n_tok), 0)
            jj = lax.broadcasted_iota(jnp.int32, (n_tok, n_tok), 1)
            src = (ii & (seg - 1)) * cps + lax.shift_right_logical(ii, seg.bit_length() - 1)
            perm = jnp.where(jj == src, 1.0, 0.0).astype(BF16)
            h = jnp.dot(perm, h, preferred_element_type=F32).astype(BF16)
        h_ref[...] = h

    y = jnp.dot(h_ref[...], w_ref[...], preferred_element_type=F32)
    y = _conv_seg(y, cw_ref[...], seg)
    if has_bias:
        y = y + cb_ref[...]
    y = _silu(y)

    def store(t):
        o_ref[...] = t.reshape(o_ref.shape)

    if norm_tiles:
        @pl.when(j < norm_tiles)
        def _():
            scale = jnp.where(j == 0, GDN_DK ** -0.5, 1.0)
            heads = []
            for hh in range(y.shape[1] // GDN_DK):
                yh = y[:, hh * GDN_DK:(hh + 1) * GDN_DK]
                heads.append(yh * (lax.rsqrt(jnp.sum(yh * yh, axis=1, keepdims=True) + NORM_EPS) * scale))
            store(jnp.concatenate(heads, axis=1))

        @pl.when(j >= norm_tiles)
        def _():
            store(y)
    else:
        store(y)


def _proj_conv(x, sc, sh, w, conv_w, conv_b, seg, cols, norm_tiles, tm, tn, name):
    bsz, length, d = x.shape
    n = w.shape[1]
    has_bias = conv_b is not None
    if cols:
        rows, cps = length // GRID_W, SUBLANES
        assert seg == rows and seg & (seg - 1) == 0
        n_tok = rows * cps
        x_in = x.reshape(bsz, rows, GRID_W, d)
        x_spec = pl.BlockSpec((None, rows, cps, d), lambda b, i, j: (b, 0, i, 0))
        out_shape = jax.ShapeDtypeStruct((bsz, GRID_W, rows, n), F32)
        out_spec = pl.BlockSpec((None, cps, rows, tn), lambda b, i, j: (b, i, 0, j))
        grid = (bsz, GRID_W // cps, n // tn)
    else:
        n_tok = tm
        x_in = x
        x_spec = pl.BlockSpec((None, tm, d), lambda b, i, j: (b, i, 0))
        out_shape = jax.ShapeDtypeStruct((bsz, length, n), F32)
        out_spec = pl.BlockSpec((None, tm, tn), lambda b, i, j: (b, i, j))
        grid = (bsz, length // tm, n // tn)
    mod = pl.BlockSpec((None, 1, d), lambda b, i, j: (b, 0, 0))
    in_specs = [x_spec, mod, mod, pl.BlockSpec((d, tn), lambda b, i, j: (0, j)),
                pl.BlockSpec((CONV_K, tn), lambda b, i, j: (0, j))]
    args = [x_in, sc, sh, w, conv_w]
    if has_bias:
        in_specs.append(pl.BlockSpec((1, tn), lambda b, i, j: (0, j)))
        args.append(conv_b.reshape(1, n))
    return pl.pallas_call(
        functools.partial(_proj_conv_kernel, seg=seg, cols=cols, has_bias=has_bias, norm_tiles=norm_tiles),
        out_shape=out_shape,
        grid=grid,
        in_specs=in_specs,
        out_specs=out_spec,
        scratch_shapes=[pltpu.VMEM((n_tok, d), BF16)],
        compiler_params=_params(3),
        name=name,
    )(*args)


def _chunk_cumsums(t):
    ii = lax.broadcasted_iota(jnp.int32, (CHUNK, CHUNK), 0)
    jj = lax.broadcasted_iota(jnp.int32, (CHUNK, CHUNK), 1)
    ltri = (ii >= jj).astype(F32)
    utri = (ii <= jj).astype(F32)
    fwd = jnp.dot(ltri, t, precision=HIGHEST, preferred_element_type=F32)
    bwd = jnp.dot(utri, t, precision=HIGHEST, preferred_element_type=F32)
    return fwd, bwd


def _gdn_prep_kernel(s_ref, alog_ref, dtb_ref, o_ref, *, rows):
    s = s_ref[...]
    lane = lax.broadcasted_iota(jnp.int32, (1, LANES), 1)
    g = -jnp.exp(alog_ref[...]) * _softplus(s + dtb_ref[...])
    beta = jax.nn.sigmoid(s)
    for c in range(rows // CHUNK):
        sl = slice(c * CHUNK, (c + 1) * CHUNK)
        fwd, bwd = _chunk_cumsums(g[sl, :])
        gc = jnp.where(lane < SM_A + GDN_HEADS, fwd, bwd)
        o_ref[sl, :] = jnp.where(lane < SM_B, gc, beta[sl, :])


def _gdn_prep(small, a_log, dt_bias, tb):
    bsz, length, _ = small.shape
    pad = lambda t: jnp.pad(t.reshape(1, -1).astype(F32), ((0, 0), (SM_A, LANES - SM_A - 2 * GDN_HEADS)))
    return pl.pallas_call(
        functools.partial(_gdn_prep_kernel, rows=tb),
        out_shape=jax.ShapeDtypeStruct((bsz, length, LANES), F32),
        grid=(bsz, length // tb),
        in_specs=[pl.BlockSpec((None, tb, LANES), lambda b, i: (b, i, 0)),
                  pl.BlockSpec((1, LANES), lambda b, i: (0, 0)),
                  pl.BlockSpec((1, LANES), lambda b, i: (0, 0))],
        out_specs=pl.BlockSpec((None, tb, LANES), lambda b, i: (b, i, 0)),
        compiler_params=_params(2),
        name="gdn_prep",
    )(small, pad(a_log), pad(dt_bias))


def _tri_inverse(mats):
    ii = lax.broadcasted_iota(jnp.int32, (CHUNK, CHUNK), 0)
    jj = lax.broadcasted_iota(jnp.int32, (CHUNK, CHUNK), 1)
    eye = jnp.where(ii == jj, 1.0, 0.0)
    ts = [eye - a for a in mats]
    xs = [_mm(a, a) for a in mats]
    power = 2
    while 2 * power < CHUNK:
        ts = [t + _mm(t, x) for t, x in zip(ts, xs)]
        xs = [_mm(x, x) for x in xs]
        power *= 2
    return [t + _mm(t, x) for t, x in zip(ts, xs)]


def _gdn_chunks(items, masks):
    qs, ks, vs, gcols, grows, bcols, states, lowers = zip(*items)
    n = len(items)
    incl = [masks[lo][0] for lo in lowers]
    strict = [masks[lo][1] for lo in lowers]
    glast = [grows[i][:, CHUNK - 1:CHUNK] if lowers[i] else grows[i][:, 0:1] for i in range(n)]
    kk = [_mm_nt(k, k) for k in ks]
    qk = [_mm_nt(q, k) for q, k in zip(qs, ks)]
    dec = [jnp.exp(jnp.where(incl[i], gcols[i] - grows[i], 0.0)) for i in range(n)]
    a_mats = [jnp.where(strict[i], bcols[i] * kk[i] * dec[i], 0.0) for i in range(n)]
    t_inv = _tri_inverse(a_mats)
    eg = [jnp.exp(g) for g in gcols]
    rhs = [jnp.concatenate([(bcols[i] * eg[i]) * ks[i], bcols[i] * vs[i]], axis=1) for i in range(n)]
    wu = [_mm(t, r) for t, r in zip(t_inv, rhs)]
    s16 = [s.astype(BF16) for s in states]
    ws = [_mm(wu[i][:, :GDN_DK], s16[i]) for i in range(n)]
    qs_state = [_mm(qs[i] * eg[i], s16[i]) for i in range(n)]
    v_new = [wu[i][:, GDN_DK:] - ws[i] for i in range(n)]
    o_intra = [_mm(jnp.where(incl[i], qk[i] * dec[i], 0.0), v_new[i]) for i in range(n)]
    kv = [_mm_tn(ks[i] * jnp.exp(glast[i] - gcols[i]), v_new[i]) for i in range(n)]
    return [(qs_state[i] + o_intra[i], jnp.exp(glast[i]) * states[i] + kv[i]) for i in range(n)]


def _lane_column(t, idx, lane):
    if isinstance(idx, int):
        return t[:, idx:idx + 1]
    return jnp.sum(jnp.where(lane == idx, t, 0.0), axis=1, keepdims=True)


def _gdn_kernel(qf_ref, kf_ref, vf_ref, qb_ref, kb_ref, vb_ref, pf_ref, pb_ref, ptf_ref, ptb_ref,
                s0_ref, of_ref, ob_ref, sout_ref, state_ref, *, n_steps, hs, nc):
    step = pl.program_id(2)
    head0 = 0 if hs == GDN_HEADS else pl.program_id(1) * hs

    @pl.when(step == 0)
    def _():
        state_ref[...] = s0_ref[...]

    lane = lax.broadcasted_iota(jnp.int32, (1, LANES), 1)

    def heads_of(q_ref, k_ref, v_ref):
        q, k, v = q_ref[...], k_ref[...], v_ref[...]
        return [(q[:, hh * GDN_DK:(hh + 1) * GDN_DK], k[:, hh * GDN_DK:(hh + 1) * GDN_DK],
                 v[:, hh * GDN_DV:(hh + 1) * GDN_DV]) for hh in range(hs)]

    heads_f = heads_of(qf_ref, kf_ref, vf_ref)
    heads_b = heads_of(qb_ref, kb_ref, vb_ref)
    pf = pf_ref[...]
    pb = pb_ref[...]
    masks = {True: _tri_masks(True), False: _tri_masks(False)}
    cols = []
    for hh in range(hs):
        head = head0 + hh
        cols.append((_lane_column(pf, SM_A + head, lane), _lane_column(pf, SM_B + head, lane),
                     _lane_column(pb, SM_A + GDN_HEADS + head, lane), _lane_column(pb, SM_B + GDN_HEADS + head, lane)))
    states = [[state_ref[hh, 0], state_ref[hh, 1]] for hh in range(hs)]
    for c in range(nc):
        cb = nc - 1 - c
        sl_f = slice(c * CHUNK, (c + 1) * CHUNK)
        sl_b = slice(cb * CHUNK, (cb + 1) * CHUNK)
        items = []
        for hh in range(hs):
            head = head0 + hh
            gcol_f, bcol_f, gcol_b, bcol_b = cols[hh]
            qf, kf, vf = heads_f[hh]
            qb, kb, vb = heads_b[hh]
            grow_f = ptf_ref[c, pl.ds(SM_A + head, 1), :]
            grow_b = ptb_ref[cb, pl.ds(SM_A + GDN_HEADS + head, 1), :]
            items.append((qf[sl_f], kf[sl_f], vf[sl_f], gcol_f[sl_f], grow_f, bcol_f[sl_f], states[hh][0], True))
            items.append((qb[sl_b], kb[sl_b], vb[sl_b], gcol_b[sl_b], grow_b, bcol_b[sl_b], states[hh][1], False))
        results = _gdn_chunks(items, masks)
        for hh in range(hs):
            hl = slice(hh * GDN_DV, (hh + 1) * GDN_DV)
            of_ref[sl_f, hl], states[hh][0] = results[2 * hh]
            ob_ref[sl_b, hl], states[hh][1] = results[2 * hh + 1]
    for hh in range(hs):
        state_ref[hh, 0] = states[hh][0]
        state_ref[hh, 1] = states[hh][1]

    @pl.when(step == n_steps - 1)
    def _():
        sout_ref[...] = state_ref[...]


def _gdn_scan(qkv, p, pt, s0, hs, nc):
    bsz, length, _ = qkv.shape
    tb = nc * CHUNK
    n_steps = length // tb
    width = hs * GDN_DK
    qoff, koff, voff = 0, GDN_QK // width, 2 * GDN_QK // width
    fwd = lambda off: pl.BlockSpec((None, tb, width), lambda b, h, s: (b, s, off + h))
    bwd = lambda off: pl.BlockSpec((None, tb, width), lambda b, h, s: (b, n_steps - 1 - s, off + h))
    state_spec = pl.BlockSpec((None, hs, 2, GDN_DK, GDN_DV), lambda b, h, s: (b, h, 0, 0, 0))
    return pl.pallas_call(
        functools.partial(_gdn_kernel, n_steps=n_steps, hs=hs, nc=nc),
        out_shape=(jax.ShapeDtypeStruct((bsz, length, GDN_V), F32),
                   jax.ShapeDtypeStruct((bsz, length, GDN_V), F32),
                   jax.ShapeDtypeStruct((bsz, GDN_HEADS, 2, GDN_DK, GDN_DV), F32)),
        grid=(bsz, GDN_HEADS // hs, n_steps),
        in_specs=[fwd(qoff), fwd(koff), fwd(voff), bwd(qoff), bwd(koff), bwd(voff),
                  pl.BlockSpec((None, tb, LANES), lambda b, h, s: (b, s, 0)),
                  pl.BlockSpec((None, tb, LANES), lambda b, h, s: (b, n_steps - 1 - s, 0)),
                  pl.BlockSpec((None, nc, 4 * SUBLANES, CHUNK), lambda b, h, s: (b, s, 0, 0)),
                  pl.BlockSpec((None, nc, 4 * SUBLANES, CHUNK), lambda b, h, s: (b, n_steps - 1 - s, 0, 0)),
                  state_spec],
        out_specs=(pl.BlockSpec((None, tb, width), lambda b, h, s: (b, s, h)),
                   pl.BlockSpec((None, tb, width), lambda b, h, s: (b, n_steps - 1 - s, h)),
                   state_spec),
        scratch_shapes=[pltpu.VMEM((hs, 2, GDN_DK, GDN_DV), F32)],
        compiler_params=_params(3),
        name="gdn_scan",
    )(qkv, qkv, qkv, qkv, qkv, qkv, p, p, pt, pt, s0)


def _gdn_branch(qkv, small, qkv_c, small_c, a_log, dt_bias):
    bsz = qkv.shape[0]
    s0 = jnp.zeros((bsz, GDN_HEADS, 2, GDN_DK, GDN_DV), F32)

    def rows_t(p):
        chunks = p[:, :, :4 * SUBLANES].reshape(bsz, p.shape[1] // CHUNK, CHUNK, 4 * SUBLANES)
        return jnp.swapaxes(chunks, 2, 3)

    ctx_len = qkv_c.shape[1]
    p_c = _gdn_prep(small_c, a_log, dt_bias, ctx_len)
    oc_f, oc_b, s_ctx = _gdn_scan(qkv_c, p_c, rows_t(p_c), s0, GDN_HEADS, 1)
    p = _gdn_prep(small, a_log, dt_bias, 2 * GDN_TB)
    o_f, o_b, _ = _gdn_scan(qkv, p, rows_t(p), s_ctx, GDN_HEADS, 1)
    return (o_f, o_b), (oc_f, oc_b)


def _ssd_prep_kernel(s_ref, alog_ref, dtb_ref, dt_ref, gc_ref, *, rows):
    s = s_ref[...]
    lane = lax.broadcasted_iota(jnp.int32, (1, LANES), 1)
    dt = _softplus(s + dtb_ref[...])
    la = dt * (-jnp.exp(alog_ref[...]))
    dt_ref[...] = dt
    for c in range(rows // CHUNK):
        sl = slice(c * CHUNK, (c + 1) * CHUNK)
        fwd, bwd = _chunk_cumsums(la[sl, :])
        gc_ref[sl, :] = jnp.where(lane < SM_DT_B, fwd, bwd)


def _ssd_prep(small_cols, a_log, dt_bias, n_seg):
    bsz, rows, _ = small_cols.shape
    pad = lambda t: jnp.pad(t.reshape(1, -1).astype(F32), ((0, 0), (SM_DT, LANES - SM_DT - 2 * SSM_HEADS)))
    out = jax.ShapeDtypeStruct((bsz, n_seg, rows, LANES), F32)
    return pl.pallas_call(
        functools.partial(_ssd_prep_kernel, rows=rows),
        out_shape=(out, out),
        grid=(bsz, n_seg),
        in_specs=[pl.BlockSpec((None, rows, LANES), lambda b, c: (b, 0, c)),
                  pl.BlockSpec((1, LANES), lambda b, c: (0, 0)),
                  pl.BlockSpec((1, LANES), lambda b, c: (0, 0))],
        out_specs=(pl.BlockSpec((None, None, rows, LANES), lambda b, c: (b, c, 0, 0)),
                   pl.BlockSpec((None, None, rows, LANES), lambda b, c: (b, c, 0, 0))),
        compiler_params=_params(2),
        name="ssd_prep",
    )(small_cols, pad(a_log), pad(dt_bias))


def _ssd_chunks(items, masks):
    n = len(items)
    p = SSM_HEAD_DIM
    heads = range(SSM_HPG)
    cb = [_mm_nt(it[2], it[1]) for it in items]
    inter = [_mm(it[2], it[6]) for it in items]
    xdt, lhs, x_decayed, e_rows, e_last = [], [], [], [], []
    for x, bm, cm, dt_cols, gc_cols, gc_rows, state, lower, d_skip in items:
        incl = masks[lower][0]
        for j in heads:
            gcol, grow = gc_cols[j], gc_rows[j]
            glast = grow[:, CHUNK - 1:CHUNK] if lower else grow[:, 0:1]
            xj = x[:, j * p:(j + 1) * p] * dt_cols[j]
            xdt.append(xj)
            lhs.append(jnp.where(incl, jnp.exp(jnp.where(incl, gcol - grow, 0.0)), 0.0))
            x_decayed.append(xj * jnp.exp(glast - gcol))
            e_rows.append(jnp.broadcast_to(jnp.exp(gcol), (CHUNK, p)))
            e_last.append(jnp.broadcast_to(jnp.exp(glast), (1, p)))
    intra = [_mm(cb[i // SSM_HPG] * lhs[i], xdt[i]) for i in range(n * SSM_HPG)]
    upd = [_mm_tn(items[i][1], jnp.concatenate(x_decayed[i * SSM_HPG:(i + 1) * SSM_HPG], axis=1)) for i in range(n)]
    out = []
    for i, it in enumerate(items):
        hsl = slice(i * SSM_HPG, (i + 1) * SSM_HPG)
        y = jnp.concatenate(intra[hsl], axis=1) + inter[i] * jnp.concatenate(e_rows[hsl], axis=1)
        if it[8] is not None:
            y = y + it[8] * it[0]
        out.append((y, jnp.concatenate(e_last[hsl], axis=1) * it[6] + upd[i]))
    return out


def _ssd_kernel(xf_ref, bf_ref, cf_ref, xb_ref, bb_ref, cb_ref, dtf_ref, dtb_ref, gcf_ref, gcb_ref, gtf_ref, gtb_ref,
                d_ref, s0_ref, yf_ref, yb_ref, sout_ref, state_ref, *, rows, n_seg, gs):
    step = pl.program_id(2)
    group0 = 0 if gs == SSM_GROUPS else pl.program_id(1) * gs

    @pl.when(step == 0)
    def _():
        state_ref[...] = s0_ref[...]

    lane = lax.broadcasted_iota(jnp.int32, (1, LANES), 1)
    xf, bmf, cmf = xf_ref[...], bf_ref[...], cf_ref[...]
    xb, bmb, cmb = xb_ref[...], bb_ref[...], cb_ref[...]
    dtf, dtb, gcf, gcb = dtf_ref[...], dtb_ref[...], gcf_ref[...], gcb_ref[...]
    d_all = d_ref[...]
    gw = SSM_HPG * SSM_HEAD_DIM
    n_chunks = rows // CHUNK
    masks = {True: _tri_masks(True), False: _tri_masks(False)}
    per_group = []
    for gg in range(gs):
        base_f = SM_DT + SSM_HPG * (group0 + gg)
        base_b = SM_DT_B + SSM_HPG * (group0 + gg)
        per_group.append((
            [_lane_column(dtf, base_f + j, lane) for j in range(SSM_HPG)],
            [_lane_column(gcf, base_f + j, lane) for j in range(SSM_HPG)],
            [gtf_ref[pl.ds(base_f + j, 1), :] for j in range(SSM_HPG)],
            [_lane_column(dtb, base_b + j, lane) for j in range(SSM_HPG)],
            [_lane_column(gcb, base_b + j, lane) for j in range(SSM_HPG)],
            [gtb_ref[pl.ds(base_b + j, 1), :] for j in range(SSM_HPG)]))
    states = [[state_ref[gg, 0], state_ref[gg, 1]] for gg in range(gs)]
    for c in range(n_chunks):
        sl_f = slice(c * CHUNK, (c + 1) * CHUNK)
        sl_b = slice((n_chunks - 1 - c) * CHUNK, (n_chunks - c) * CHUNK)
        items = []
        for gg in range(gs):
            dt_f, gc_f, gr_f, dt_b, gc_b, gr_b = per_group[gg]
            xl = slice(gg * gw, (gg + 1) * gw)
            nl = slice(gg * SSM_STATE, (gg + 1) * SSM_STATE)
            items.append((xf[sl_f, xl], bmf[sl_f, nl], cmf[sl_f, nl], [t[sl_f] for t in dt_f], [t[sl_f] for t in gc_f],
                          [t[:, sl_f] for t in gr_f], states[gg][0], True, d_all[:, xl]))
            items.append((xb[sl_b, xl], bmb[sl_b, nl], cmb[sl_b, nl], [t[sl_b] for t in dt_b], [t[sl_b] for t in gc_b],
                          [t[:, sl_b] for t in gr_b], states[gg][1], False, None))
        results = _ssd_chunks(items, masks)
        for gg in range(gs):
            xl = slice(gg * gw, (gg + 1) * gw)
            yf_ref[sl_f, xl], states[gg][0] = results[2 * gg]
            yb_ref[sl_b, xl], states[gg][1] = results[2 * gg + 1]
    for gg in range(gs):
        state_ref[gg, 0] = states[gg][0]
        state_ref[gg, 1] = states[gg][1]

    @pl.when(step == n_seg - 1)
    def _():
        sout_ref[...] = state_ref[...]


def _ssd_scan(xbc, qdt, qgc, qgct, d_exp, s0, gs):
    bsz, n_seg, rows, _ = xbc.shape
    xw, nw = gs * SSM_HPG * SSM_HEAD_DIM, gs * SSM_STATE
    boff, coff = SSM_INNER // nw, (SSM_INNER + SSM_GN) // nw
    seg_f = lambda s: s
    seg_b = lambda s: n_seg - 1 - s
    xspec = lambda seg: pl.BlockSpec((None, None, rows, xw), lambda b, g, s: (b, seg(s), 0, g))
    nspec = lambda seg, off: pl.BlockSpec((None, None, rows, nw), lambda b, g, s: (b, seg(s), 0, off + g))
    qspec = lambda seg: pl.BlockSpec((None, None, rows, LANES), lambda b, g, s: (b, seg(s), 0, 0))
    qtspec = lambda seg: pl.BlockSpec((None, None, LANES, rows), lambda b, g, s: (b, seg(s), 0, 0))
    state_spec = pl.BlockSpec((None, gs, 2, SSM_STATE, SSM_HPG * SSM_HEAD_DIM), lambda b, g, s: (b, g, 0, 0, 0))
    y_shape = jax.ShapeDtypeStruct((bsz, n_seg, rows, SSM_INNER), F32)
    return pl.pallas_call(
        functools.partial(_ssd_kernel, rows=rows, n_seg=n_seg, gs=gs),
        out_shape=(y_shape, y_shape,
                   jax.ShapeDtypeStruct((bsz, SSM_GROUPS, 2, SSM_STATE, SSM_HPG * SSM_HEAD_DIM), F32)),
        grid=(bsz, SSM_GROUPS // gs, n_seg),
        in_specs=[xspec(seg_f), nspec(seg_f, boff), nspec(seg_f, coff),
                  xspec(seg_b), nspec(seg_b, boff), nspec(seg_b, coff),
                  qspec(seg_f), qspec(seg_b), qspec(seg_f), qspec(seg_b), qtspec(seg_f), qtspec(seg_b),
                  pl.BlockSpec((1, xw), lambda b, g, s: (0, g)), state_spec],
        out_specs=(xspec(seg_f), xspec(seg_b), state_spec),
        scratch_shapes=[pltpu.VMEM((gs, 2, SSM_STATE, SSM_HPG * SSM_HEAD_DIM), F32)],
        compiler_params=_params(3),
        name="ssd_scan",
    )(xbc, xbc, xbc, xbc, xbc, xbc, qdt, qdt, qgc, qgc, qgct, qgct, d_exp, s0)


def _ssd_branch(xbc, small, xbc_c, small_c, a_log, dt_bias, d_skip):
    bsz, n_seg, rows, _ = xbc.shape
    assert rows % CHUNK == 0 and xbc_c.shape[2] % CHUNK == 0
    d_exp = jnp.repeat(d_skip.astype(F32), SSM_HEAD_DIM).reshape(1, SSM_INNER)
    s0 = jnp.zeros((bsz, SSM_GROUPS, 2, SSM_STATE, SSM_HPG * SSM_HEAD_DIM), F32)
    qdt, qgc = _ssd_prep(small_c, a_log, dt_bias, 1)
    yc_f, yc_b, s_ctx = _ssd_scan(xbc_c, qdt, qgc, jnp.swapaxes(qgc, 2, 3), d_exp, s0, SSD_GROUPS_PER_STEP)
    qdt, qgc = _ssd_prep(small.reshape(bsz, rows, n_seg * SMALL_N), a_log, dt_bias, n_seg)
    y_f, y_b, _ = _ssd_scan(xbc, qdt, qgc, jnp.swapaxes(qgc, 2, 3), d_exp, s_ctx, SSD_GROUPS_PER_STEP)
    to_raster = lambda y: jnp.swapaxes(y, 1, 2).reshape(bsz, rows * n_seg, SSM_INNER)
    return (to_raster(y_f), to_raster(y_b)), (yc_f[:, 0], yc_b[:, 0])


def _layer_norm(r, g, b):
    mu = jnp.mean(r, axis=1, keepdims=True)
    var = jnp.mean(jnp.square(r - mu), axis=1, keepdims=True)
    return (r - mu) * lax.rsqrt(var + LN_EPS) * g + b


def _merge_kernel(x_ref, gout_ref, z_ref, ga_ref, gb_ref, of_ref, ob_ref, yf_ref, yb_ref,
                  nwa_ref, nwb_ref, wpg_ref, wps_ref, wout_ref, g1_ref, lng_ref, lnb_ref, o_ref, *, alpha):
    o = of_ref[...] + ob_ref[...]
    normed = []
    for h in range(GDN_HEADS):
        oh = o[:, h * GDN_DV:(h + 1) * GDN_DV]
        normed.append(oh * lax.rsqrt(jnp.mean(oh * oh, axis=1, keepdims=True) + NORM_EPS))
    y_a = jnp.concatenate(normed, axis=1) * nwa_ref[...] * _silu(gout_ref[...])
    t = (yf_ref[...] + yb_ref[...]) * _silu(z_ref[...])
    gw = SSM_INNER // SSM_GROUPS
    normed = []
    for g in range(SSM_GROUPS):
        tg = t[:, g * gw:(g + 1) * gw]
        normed.append(tg * lax.rsqrt(jnp.mean(tg * tg, axis=1, keepdims=True) + NORM_EPS))
    y_b = jnp.concatenate(normed, axis=1) * nwb_ref[...]
    mix = (jax.nn.sigmoid(ga_ref[...]) * jnp.dot(y_a.astype(BF16), wpg_ref[...], preferred_element_type=F32)
           + jax.nn.sigmoid(gb_ref[...]) * jnp.dot(y_b.astype(BF16), wps_ref[...], preferred_element_type=F32))
    out = jnp.dot(mix.astype(BF16), wout_ref[...], preferred_element_type=F32)
    o_ref[...] = _layer_norm(alpha * x_ref[...] + g1_ref[...] * out, lng_ref[...], lnb_ref[...])


def _merge(x, big, o_pair, y_pair, nwa, nwb, wpg, wps, wout, g1, ln_g, ln_b, alpha, tm):
    bsz, length, d = x.shape
    row = lambda width, off: pl.BlockSpec((None, tm, width), lambda b, i: (b, i, off // width))
    const = lambda shape: pl.BlockSpec(shape, lambda b, i: (0,) * len(shape))
    return pl.pallas_call(
        functools.partial(_merge_kernel, alpha=alpha),
        out_shape=jax.ShapeDtypeStruct((bsz, length, d), F32),
        grid=(bsz, length // tm),
        in_specs=[row(d, 0), row(GDN_V, GOUT_OFF), row(SSM_INNER, Z_OFF), row(d, GA_OFF), row(d, GB_OFF),
                  row(GDN_V, 0), row(GDN_V, 0), row(SSM_INNER, 0), row(SSM_INNER, 0),
                  const((1, GDN_V)), const((1, SSM_INNER)), const((GDN_V, d)), const((SSM_INNER, d)), const((d, d)),
                  pl.BlockSpec((None, 1, d), lambda b, i: (b, 0, 0)), const((1, d)), const((1, d))],
        out_specs=row(d, 0),
        compiler_params=_params(2),
        name="merge",
    )(x, big, big, big, big, o_pair[0], o_pair[1], y_pair[0], y_pair[1], nwa, nwb, wpg, wps, wout, g1, ln_g, ln_b)


def _mlp_kernel(x_ref, sc_ref, sh_ref, g2_ref, w1_ref, b1_ref, w2_ref, b2_ref, lng_ref, lnb_ref, o_ref, *, alpha):
    x = x_ref[...]
    h = (x * (1.0 + sc_ref[...]) + sh_ref[...]).astype(BF16)
    acc = None
    tf = D_MODEL
    for c in range(D_FF // tf):
        u = jnp.dot(h, w1_ref[:, c * tf:(c + 1) * tf], preferred_element_type=F32) + b1_ref[:, c * tf:(c + 1) * tf]
        u = jnp.square(jnp.maximum(u, 0.0))
        part = jnp.dot(u.astype(BF16), w2_ref[c * tf:(c + 1) * tf, :], preferred_element_type=F32)
        acc = part if acc is None else acc + part
    f = acc + b2_ref[...]
    o_ref[...] = _layer_norm(alpha * x + g2_ref[...] * f, lng_ref[...], lnb_ref[...])


def _mlp(x, sc, sh, g2, w1, b1, w2, b2, ln_g, ln_b, alpha, tm):
    bsz, length, d = x.shape
    const = lambda shape: pl.BlockSpec(shape, lambda b, i: (0,) * len(shape))
    mod = pl.BlockSpec((None, 1, d), lambda b, i: (b, 0, 0))
    return pl.pallas_call(
        functools.partial(_mlp_kernel, alpha=alpha),
        out_shape=jax.ShapeDtypeStruct((bsz, length, d), F32),
        grid=(bsz, length // tm),
        in_specs=[pl.BlockSpec((None, tm, d), lambda b, i: (b, i, 0)), mod, mod, mod,
                  const((d, D_FF)), const((1, D_FF)), const((D_FF, d)), const((1, d)), const((1, d)), const((1, d))],
        out_specs=pl.BlockSpec((None, tm, d), lambda b, i: (b, i, 0)),
        compiler_params=_params(2),
        name="mlp",
    )(x, sc, sh, g2, w1, b1, w2, b2, ln_g, ln_b)


def _split_w_in(w_in):
    pts, acc = [], 0
    for s in IN_SPLITS[:-1]:
        acc += s
        pts.append(acc)
    qkv, gout, a_raw, b_raw, z, xbc, dt_raw, gate_a, gate_b = jnp.split(w_in, pts, axis=1)
    big = jnp.concatenate([z, gout, gate_a, gate_b], axis=1).astype(BF16)
    pad = jnp.zeros((w_in.shape[0], SMALL_N - SM_DT - 2 * SSM_HEADS), w_in.dtype)
    small = jnp.concatenate([a_raw, b_raw, dt_raw, pad], axis=1).astype(BF16)
    return big, qkv.astype(BF16), xbc.astype(BF16), small


def kernel(x, c, ctx, c_ctx, w_mod, b_mod, w_in, gdn_conv_w, gdn_A_log, gdn_dt_bias, gdn_norm_w,
           ssm_conv_w, ssm_conv_b, ssm_A_log, ssm_dt_bias, ssm_D, ssm_norm_w,
           w_proj_gdn, w_proj_ssm, w_out, ln1_g, ln1_b, w_ff1, b_ff1, w_ff2, b_ff2, ln2_g, ln2_b):
    bsz, length, d = x.shape
    ctx_len = ctx.shape[1]
    depth = w_mod.shape[0]
    alpha = float((2 * depth) ** 0.25)
    mod_rows = -(-(bsz + 1) // (2 * SUBLANES)) * (2 * SUBLANES)
    cc = jnp.concatenate([c, c_ctx[None, :], jnp.zeros((mod_rows - bsz - 1, d), c.dtype)], axis=0)
    row2 = lambda t: t.reshape(1, -1)
    tm_lat = 1024
    tm_ctx = ctx_len

    for l in range(depth):
        last = l == depth - 1
        mod = _modulation(cc, w_mod[l], b_mod[l])
        lat = [mod[:bsz, i * d:(i + 1) * d].reshape(bsz, 1, d) for i in range(6)]
        cxm = [jnp.broadcast_to(mod[bsz, i * d:(i + 1) * d].reshape(1, 1, d), (bsz, 1, d)) for i in range(6)]
        w_big, w_qkv, w_xbc, w_small = _split_w_in(w_in[l])
        rows = length // GRID_W

        big = _inproj(x, lat[1], lat[0], w_big, tm_lat, 1024, "inproj_gates")
        small = _inproj(x, lat[1], lat[0], w_small, tm_lat, SMALL_N, "inproj_narrow")
        qkv = _proj_conv(x, lat[1], lat[0], w_qkv, gdn_conv_w[l], None, GRID_W, False, 2, tm_lat, GDN_QK, "inproj_qkv")
        xbc = _proj_conv(x, lat[1], lat[0], w_xbc, ssm_conv_w[l], ssm_conv_b[l], rows, True, 0, None, 1024,
                         "inproj_xbc")
        big_c = _inproj(ctx, cxm[1], cxm[0], w_big, tm_ctx, 1024, "inproj_gates_ctx")
        small_c = _inproj(ctx, cxm[1], cxm[0], w_small, tm_ctx, SMALL_N, "inproj_narrow_ctx")
        qkv_c = _proj_conv(ctx, cxm[1], cxm[0], w_qkv, gdn_conv_w[l], None, ctx_len, False, 2, tm_ctx, GDN_QK,
                           "inproj_qkv_ctx")
        xbc_c = _proj_conv(ctx, cxm[1], cxm[0], w_xbc, ssm_conv_w[l], ssm_conv_b[l], ctx_len, False, 0, tm_ctx, 1024,
                           "inproj_xbc_ctx")[:, None]

        o_pair, oc_pair = _gdn_branch(qkv, small, qkv_c, small_c, gdn_A_log[l], gdn_dt_bias[l])
        y_pair, yc_pair = _ssd_branch(xbc, small, xbc_c, small_c, ssm_A_log[l], ssm_dt_bias[l], ssm_D[l])

        nwa = jnp.tile(gdn_norm_w[l], GDN_HEADS).reshape(1, GDN_V)
        nwb = row2(ssm_norm_w[l])
        wpg, wps, wo = w_proj_gdn[l].astype(BF16), w_proj_ssm[l].astype(BF16), w_out[l].astype(BF16)
        w1, w2 = w_ff1[l].astype(BF16), w_ff2[l].astype(BF16)
        merge_args = (nwa, nwb, wpg, wps, wo)
        ln1 = (row2(ln1_g[l]), row2(ln1_b[l]))
        mlp_w = (w1, row2(b_ff1[l]), w2, row2(b_ff2[l]), row2(ln2_g[l]), row2(ln2_b[l]))

        x1 = _merge(x, big, o_pair, y_pair, *merge_args, lat[2], *ln1, alpha, 256)
        x = _mlp(x1, lat[4], lat[3], lat[5], *mlp_w, alpha, 512)
        if not last:
            c1 = _merge(ctx, big_c, oc_pair, yc_pair, *merge_args, cxm[2], *ln1, alpha, ctx_len)
            ctx = _mlp(c1, cxm[4], cxm[3], cxm[5], *mlp_w, alpha, ctx_len)
    return x
```

```python
import functools

import jax
import jax.numpy as jnp
from jax import lax
from jax.experimental import pallas as pl
from jax.experimental.pallas import tpu as pltpu

F32 = jnp.float32
BF16 = jnp.bfloat16
HIGHEST = lax.Precision.HIGHEST

D_MODEL = 1024
GRID_W = 64
GDN_HEADS = 8
GDN_DK = 128
GDN_DV = 128
GDN_QK = GDN_HEADS * GDN_DK
GDN_V = GDN_HEADS * GDN_DV
GDN_QKV = 2 * GDN_QK + GDN_V
SSM_INNER = 2 * D_MODEL
SSM_HEAD_DIM = 64
SSM_HEADS = SSM_INNER // SSM_HEAD_DIM
SSM_GROUPS = 8
SSM_HPG = SSM_HEADS // SSM_GROUPS
SSM_STATE = 128
SSM_GN = SSM_GROUPS * SSM_STATE
SSM_XBC = SSM_INNER + 2 * SSM_GN
CONV_K = 5
CHUNK = 64
D_FF = 4 * D_MODEL
LN_EPS = 1e-5
NORM_EPS = 1e-6
IN_SPLITS = (GDN_QKV, GDN_V, 2 * GDN_HEADS, 2 * GDN_HEADS, SSM_INNER, SSM_XBC, 2 * SSM_HEADS, D_MODEL, D_MODEL)

LANES = 128
SUBLANES = 8
VMEM_LIMIT_BYTES = 56 * 1024 * 1024

Z_OFF = 0
GOUT_OFF = Z_OFF + SSM_INNER
GA_OFF = GOUT_OFF + GDN_V
GB_OFF = GA_OFF + D_MODEL
BIG_N = GB_OFF + D_MODEL
SM_A = 0
SM_B = SM_A + 2 * GDN_HEADS
SM_DT = SM_B + 2 * GDN_HEADS
SM_DT_B = SM_DT + SSM_HEADS
SMALL_N = LANES

GDN_CHUNKS_PER_STEP = 1
GDN_PREP_ROWS = 512
SSD_GROUPS_PER_STEP = 4


def _params(n_axes):
    return pltpu.CompilerParams(dimension_semantics=("arbitrary",) * n_axes, vmem_limit_bytes=VMEM_LIMIT_BYTES)


def _silu(t):
    return t * jax.nn.sigmoid(t)


def _softplus(t):
    return jnp.maximum(t, 0.0) + jnp.log(1.0 + jnp.exp(-jnp.abs(t)))


def _mm(a, b):
    return jnp.dot(a.astype(BF16), b.astype(BF16), preferred_element_type=F32)


def _mm_nt(a, b):
    return lax.dot_general(a.astype(BF16), b.astype(BF16), (((1,), (1,)), ((), ())), preferred_element_type=F32)


def _mm_tn(a, b):
    return lax.dot_general(a.astype(BF16), b.astype(BF16), (((0,), (0,)), ((), ())), preferred_element_type=F32)


def _tri_masks(lower):
    ii = lax.broadcasted_iota(jnp.int32, (CHUNK, CHUNK), 0)
    jj = lax.broadcasted_iota(jnp.int32, (CHUNK, CHUNK), 1)
    if lower:
        return ii >= jj, ii > jj
    return ii <= jj, ii < jj


def _conv_seg(x, w, seg):
    n = x.shape[0]
    assert seg & (seg - 1) == 0 and n % seg == 0
    pos = lax.broadcasted_iota(jnp.int32, (n, 1), 0) & (seg - 1)
    out = None
    for j in range(CONV_K):
        d = j - CONV_K // 2
        if d == 0:
            term = x * w[j:j + 1, :]
        else:
            shifted = pltpu.roll(x, shift=(-d) % n, axis=0)
            valid = (pos + d >= 0) & (pos + d < seg)
            term = jnp.where(valid, shifted, 0.0) * w[j:j + 1, :]
        out = term if out is None else out + term
    return out


def _mod_kernel(c_ref, w_ref, b_ref, o_ref):
    o_ref[...] = _mm(_silu(c_ref[...]), w_ref[...]) + b_ref[...]


def _modulation(cc, w, b):
    rows, d = cc.shape
    n = w.shape[1]
    tn = 1536
    return pl.pallas_call(
        _mod_kernel,
        out_shape=jax.ShapeDtypeStruct((rows, n), F32),
        grid=(n // tn,),
        in_specs=[pl.BlockSpec((rows, d), lambda j: (0, 0)),
                  pl.BlockSpec((d, tn), lambda j: (0, j)),
                  pl.BlockSpec((1, tn), lambda j: (0, j))],
        out_specs=pl.BlockSpec((rows, tn), lambda j: (0, j)),
        compiler_params=_params(1),
        name="modulation",
    )(cc, w, b.reshape(1, n))


def _inproj_kernel(x_ref, sc_ref, sh_ref, w_ref, o_ref):
    h = x_ref[...] * (1.0 + sc_ref[...]) + sh_ref[...]
    o_ref[...] = jnp.dot(h.astype(BF16), w_ref[...], preferred_element_type=F32)


def _inproj(x, sc, sh, w, tm, tn, name):
    bsz, length, d = x.shape
    n = w.shape[1]
    return pl.pallas_call(
        _inproj_kernel,
        out_shape=jax.ShapeDtypeStruct((bsz, length, n), F32),
        grid=(bsz, length // tm, n // tn),
        in_specs=[pl.BlockSpec((None, tm, d), lambda b, i, j: (b, i, 0)),
                  pl.BlockSpec((None, 1, d), lambda b, i, j: (b, 0, 0)),
                  pl.BlockSpec((None, 1, d), lambda b, i, j: (b, 0, 0)),
                  pl.BlockSpec((d, tn), lambda b, i, j: (0, j))],
        out_specs=pl.BlockSpec((None, tm, tn), lambda b, i, j: (b, i, j)),
        compiler_params=_params(3),
        name=name,
    )(x, sc, sh, w)


def _proj_conv_kernel(*refs, seg, cols, has_bias, norm_tiles):
    x_ref, sc_ref, sh_ref, w_ref, cw_ref = refs[:5]
    cb_ref = refs[5] if has_bias else None
    o_ref, h_ref = refs[5 + has_bias:]
    j = pl.program_id(2)
    n_tok = h_ref.shape[0]

    @pl.when(j == 0)
    def _():
        x = x_ref[...]
        if cols:
            cps = x.shape[1]
            x = x.reshape(n_tok, x.shape[2])
        h = (x * (1.0 + sc_ref[...]) + sh_ref[...]).astype(BF16)
        if cols:
            ii = lax.broadcasted_iota(jnp.int32, (n_tok, n_tok), 0)
            jj = lax.broadcasted_iota(jnp.int32, (n_tok, n_tok), 1)
            src = (ii & (seg - 1)) * cps + lax.shift_right_logical(ii, seg.bit_length() - 1)
            perm = jnp.where(jj == src, 1.0, 0.0).astype(BF16)
            h = jnp.dot(perm, h, preferred_element_type=F32).astype(BF16)
        h_ref[...] = h

    y = jnp.dot(h_ref[...], w_ref[...], preferred_element_type=F32)
    y = _conv_seg(y, cw_ref[...], seg)
    if has_bias:
        y = y + cb_ref[...]
    y = _silu(y)

    def store(t):
        o_ref[...] = t.reshape(o_ref.shape)

    if norm_tiles:
        @pl.when(j < norm_tiles)
        def _():
            scale = jnp.where(j == 0, GDN_DK ** -0.5, 1.0)
            heads = []
            for hh in range(y.shape[1] // GDN_DK):
                yh = y[:, hh * GDN_DK:(hh + 1) * GDN_DK]
                heads.append(yh * (lax.rsqrt(jnp.sum(yh * yh, axis=1, keepdims=True) + NORM_EPS) * scale))
            store(jnp.concatenate(heads, axis=1))

        @pl.when(j >= norm_tiles)
        def _():
            store(y)
    else:
        store(y)


def _proj_conv(x, sc, sh, w, conv_w, conv_b, seg, cols, norm_tiles, tm, tn, name):
    bsz, length, d = x.shape
    n = w.shape[1]
    has_bias = conv_b is not None
    if cols:
        rows, cps = length // GRID_W, SUBLANES
        assert seg == rows and seg & (seg - 1) == 0
        n_tok = rows * cps
        x_in = x.reshape(bsz, rows, GRID_W, d)
        x_spec = pl.BlockSpec((None, rows, cps, d), lambda b, i, j: (b, 0, i, 0))
        out_shape = jax.ShapeDtypeStruct((bsz, GRID_W, rows, n), F32)
        out_spec = pl.BlockSpec((None, cps, rows, tn), lambda b, i, j: (b, i, 0, j))
        grid = (bsz, GRID_W // cps, n // tn)
    else:
        n_tok = tm
        x_in = x
        x_spec = pl.BlockSpec((None, tm, d), lambda b, i, j: (b, i, 0))
        out_shape = jax.ShapeDtypeStruct((bsz, length, n), F32)
        out_spec = pl.BlockSpec((None, tm, tn), lambda b, i, j: (b, i, j))
        grid = (bsz, length // tm, n // tn)
    mod = pl.BlockSpec((None, 1, d), lambda b, i, j: (b, 0, 0))
    in_specs = [x_spec, mod, mod, pl.BlockSpec((d, tn), lambda b, i, j: (0, j)),
                pl.BlockSpec((CONV_K, tn), lambda b, i, j: (0, j))]
    args = [x_in, sc, sh, w, conv_w]
    if has_bias:
        in_specs.append(pl.BlockSpec((1, tn), lambda b, i, j: (0, j)))
        args.append(conv_b.reshape(1, n))
    return pl.pallas_call(
        functools.partial(_proj_conv_kernel, seg=seg, cols=cols, has_bias=has_bias, norm_tiles=norm_tiles),
        out_shape=out_shape,
        grid=grid,
        in_specs=in_specs,
        out_specs=out_spec,
        scratch_shapes=[pltpu.VMEM((n_tok, d), BF16)],
        compiler_params=_params(3),
        name=name,
    )(*args)


def _chunk_cumsums(t):
    ii = lax.broadcasted_iota(jnp.int32, (CHUNK, CHUNK), 0)
    jj = lax.broadcasted_iota(jnp.int32, (CHUNK, CHUNK), 1)
    ltri = (ii >= jj).astype(F32)
    utri = (ii <= jj).astype(F32)
    fwd = jnp.dot(ltri, t, precision=HIGHEST, preferred_element_type=F32)
    bwd = jnp.dot(utri, t, precision=HIGHEST, preferred_element_type=F32)
    return fwd, bwd


def _gdn_prep_kernel(s_ref, alog_ref, dtb_ref, o_ref, *, rows):
    s = s_ref[...]
    lane = lax.broadcasted_iota(jnp.int32, (1, LANES), 1)
    g = -jnp.exp(alog_ref[...]) * _softplus(s + dtb_ref[...])
    beta = jax.nn.sigmoid(s)
    for c in range(rows // CHUNK):
        sl = slice(c * CHUNK, (c + 1) * CHUNK)
        fwd, bwd = _chunk_cumsums(g[sl, :])
        gc = jnp.where(lane < SM_A + GDN_HEADS, fwd, bwd)
        o_ref[sl, :] = jnp.where(lane < SM_B, gc, beta[sl, :])


def _gdn_prep(small, a_log, dt_bias, tb):
    bsz, length, _ = small.shape
    pad = lambda t: jnp.pad(t.reshape(1, -1).astype(F32), ((0, 0), (SM_A, LANES - SM_A - 2 * GDN_HEADS)))
    return pl.pallas_call(
        functools.partial(_gdn_prep_kernel, rows=tb),
        out_shape=jax.ShapeDtypeStruct((bsz, length, LANES), F32),
        grid=(bsz, length // tb),
        in_specs=[pl.BlockSpec((None, tb, LANES), lambda b, i: (b, i, 0)),
                  pl.BlockSpec((1, LANES), lambda b, i: (0, 0)),
                  pl.BlockSpec((1, LANES), lambda b, i: (0, 0))],
        out_specs=pl.BlockSpec((None, tb, LANES), lambda b, i: (b, i, 0)),
        compiler_params=_params(2),
        name="gdn_prep",
    )(small, pad(a_log), pad(dt_bias))


def _tri_inverse(mats):
    ii = lax.broadcasted_iota(jnp.int32, (CHUNK, CHUNK), 0)
    jj = lax.broadcasted_iota(jnp.int32, (CHUNK, CHUNK), 1)
    eye = jnp.where(ii == jj, 1.0, 0.0)
    zero = jnp.zeros((CHUNK, CHUNK), F32)
    right = lax.broadcasted_iota(jnp.int32, (CHUNK, 2 * CHUNK), 1) >= CHUNK
    zs = [jnp.concatenate([zero, eye], axis=1) + _mm(a, jnp.concatenate([a, -eye], axis=1)) for a in mats]
    power = 2
    while power < CHUNK:
        ps = [_mm(z[:, :CHUNK], z) for z in zs]
        zs = [p + jnp.where(right, z, 0.0) for p, z in zip(ps, zs)]
        power *= 2
    return zs


def _gdn_local(items, masks):
    qs, ks, vs, gcols, grows, bcols, lowers = zip(*items)
    n = len(items)
    incl = [masks[lo][0] for lo in lowers]
    strict = [masks[lo][1] for lo in lowers]
    glast = [grows[i][:, CHUNK - 1:CHUNK] if lowers[i] else grows[i][:, 0:1] for i in range(n)]
    k16 = [k.astype(BF16) for k in ks]
    kq = [_mm_nt(jnp.concatenate([k16[i], qs[i].astype(BF16)], axis=0), k16[i]) for i in range(n)]
    dec = [jnp.exp(jnp.where(incl[i], gcols[i] - grows[i], 0.0)) for i in range(n)]
    a_mats = [jnp.where(strict[i], bcols[i] * kq[i][:CHUNK] * dec[i], 0.0) for i in range(n)]
    t_inv = _tri_inverse(a_mats)
    eg = [jnp.exp(g) for g in gcols]
    rhs = [jnp.concatenate([(bcols[i] * eg[i]) * ks[i], bcols[i] * vs[i]], axis=1) for i in range(n)]
    pad = jnp.zeros((CHUNK, GDN_DK + GDN_DV), F32)
    wu = [_mm(t, jnp.concatenate([pad, r], axis=0)) for t, r in zip(t_inv, rhs)]
    lhs = [jnp.concatenate([wu[i][:, :GDN_DK].astype(BF16), (qs[i] * eg[i]).astype(BF16)], axis=0) for i in range(n)]
    return [(lhs[i], wu[i][:, GDN_DK:], incl[i], kq[i][CHUNK:], dec[i], ks[i], glast[i], gcols[i]) for i in range(n)]


def _gdn_on_state(local, states):
    n = len(local)
    on_state = [_mm(local[i][0], states[i]) for i in range(n)]
    v_new = [local[i][1] - on_state[i][:CHUNK] for i in range(n)]
    o_intra = [_mm(jnp.where(lo[2], lo[3] * lo[4], 0.0), v_new[i]) for i, lo in enumerate(local)]
    kv = [_mm_tn(lo[5] * jnp.exp(lo[6] - lo[7]), v_new[i]) for i, lo in enumerate(local)]
    return [(on_state[i][CHUNK:] + o_intra[i], jnp.exp(local[i][6]) * states[i] + kv[i]) for i in range(n)]


def _lane_column(t, idx, lane):
    if isinstance(idx, int):
        return t[:, idx:idx + 1]
    return jnp.sum(jnp.where(lane == idx, t, 0.0), axis=1, keepdims=True)


def _gdn_kernel(qf_ref, kf_ref, vf_ref, qb_ref, kb_ref, vb_ref, pf_ref, pb_ref, ptf_ref, ptb_ref,
                s0_ref, of_ref, ob_ref, sout_ref, state_ref, *, n_steps, hs, nc):
    step = pl.program_id(2)
    head0 = 0 if hs == GDN_HEADS else pl.program_id(1) * hs

    @pl.when(step == 0)
    def _():
        state_ref[...] = s0_ref[...]

    lane = lax.broadcasted_iota(jnp.int32, (1, LANES), 1)

    def heads_of(q_ref, k_ref, v_ref):
        q, k, v = q_ref[...], k_ref[...], v_ref[...]
        return [(q[:, hh * GDN_DK:(hh + 1) * GDN_DK], k[:, hh * GDN_DK:(hh + 1) * GDN_DK],
                 v[:, hh * GDN_DV:(hh + 1) * GDN_DV]) for hh in range(hs)]

    heads_f = heads_of(qf_ref, kf_ref, vf_ref)
    heads_b = heads_of(qb_ref, kb_ref, vb_ref)
    pf = pf_ref[...]
    pb = pb_ref[...]
    masks = {True: _tri_masks(True), False: _tri_masks(False)}
    cols = []
    for hh in range(hs):
        head = head0 + hh
        cols.append((_lane_column(pf, SM_A + head, lane), _lane_column(pf, SM_B + head, lane),
                     _lane_column(pb, SM_A + GDN_HEADS + head, lane), _lane_column(pb, SM_B + GDN_HEADS + head, lane)))
    items = []
    for c in range(nc):
        cb = nc - 1 - c
        sl_f = slice(c * CHUNK, (c + 1) * CHUNK)
        sl_b = slice(cb * CHUNK, (cb + 1) * CHUNK)
        for hh in range(hs):
            head = head0 + hh
            gcol_f, bcol_f, gcol_b, bcol_b = cols[hh]
            qf, kf, vf = heads_f[hh]
            qb, kb, vb = heads_b[hh]
            grow_f = ptf_ref[c, pl.ds(SM_A + head, 1), :]
            grow_b = ptb_ref[cb, pl.ds(SM_A + GDN_HEADS + head, 1), :]
            items.append((qf[sl_f], kf[sl_f], vf[sl_f], gcol_f[sl_f], grow_f, bcol_f[sl_f], True))
            items.append((qb[sl_b], kb[sl_b], vb[sl_b], gcol_b[sl_b], grow_b, bcol_b[sl_b], False))
    local = _gdn_local(items, masks)
    states = [state_ref[hh, d] for hh in range(hs) for d in range(2)]
    for c in range(nc):
        cb = nc - 1 - c
        results = _gdn_on_state(local[c * 2 * hs:(c + 1) * 2 * hs], states)
        states = [r[1] for r in results]
        for hh in range(hs):
            hl = slice(hh * GDN_DV, (hh + 1) * GDN_DV)
            of_ref[c * CHUNK:(c + 1) * CHUNK, hl] = results[2 * hh][0]
            ob_ref[cb * CHUNK:(cb + 1) * CHUNK, hl] = results[2 * hh + 1][0]
    for hh in range(hs):
        state_ref[hh, 0] = states[2 * hh]
        state_ref[hh, 1] = states[2 * hh + 1]

    @pl.when(step == n_steps - 1)
    def _():
        sout_ref[...] = state_ref[...]


def _gdn_scan(qkv, p, pt, s0, hs, nc):
    bsz, length, _ = qkv.shape
    tb = nc * CHUNK
    n_steps = length // tb
    width = hs * GDN_DK
    qoff, koff, voff = 0, GDN_QK // width, 2 * GDN_QK // width
    fwd = lambda off: pl.BlockSpec((None, tb, width), lambda b, h, s: (b, s, off + h))
    bwd = lambda off: pl.BlockSpec((None, tb, width), lambda b, h, s: (b, n_steps - 1 - s, off + h))
    state_spec = pl.BlockSpec((None, hs, 2, GDN_DK, GDN_DV), lambda b, h, s: (b, h, 0, 0, 0))
    return pl.pallas_call(
        functools.partial(_gdn_kernel, n_steps=n_steps, hs=hs, nc=nc),
        out_shape=(jax.ShapeDtypeStruct((bsz, length, GDN_V), F32),
                   jax.ShapeDtypeStruct((bsz, length, GDN_V), F32),
                   jax.ShapeDtypeStruct((bsz, GDN_HEADS, 2, GDN_DK, GDN_DV), F32)),
        grid=(bsz, GDN_HEADS // hs, n_steps),
        in_specs=[fwd(qoff), fwd(koff), fwd(voff), bwd(qoff), bwd(koff), bwd(voff),
                  pl.BlockSpec((None, tb, LANES), lambda b, h, s: (b, s, 0)),
                  pl.BlockSpec((None, tb, LANES), lambda b, h, s: (b, n_steps - 1 - s, 0)),
                  pl.BlockSpec((None, nc, 4 * SUBLANES, CHUNK), lambda b, h, s: (b, s, 0, 0)),
                  pl.BlockSpec((None, nc, 4 * SUBLANES, CHUNK), lambda b, h, s: (b, n_steps - 1 - s, 0, 0)),
                  state_spec],
        out_specs=(pl.BlockSpec((None, tb, width), lambda b, h, s: (b, s, h)),
                   pl.BlockSpec((None, tb, width), lambda b, h, s: (b, n_steps - 1 - s, h)),
                   state_spec),
        scratch_shapes=[pltpu.VMEM((hs, 2, GDN_DK, GDN_DV), F32)],
        compiler_params=_params(3),
        name="gdn_scan",
    )(qkv, qkv, qkv, qkv, qkv, qkv, p, p, pt, pt, s0)


def _gdn_branch(qkv, small, qkv_c, small_c, a_log, dt_bias):
    bsz = qkv.shape[0]
    s0 = jnp.zeros((bsz, GDN_HEADS, 2, GDN_DK, GDN_DV), F32)

    def rows_t(p):
        chunks = p[:, :, :4 * SUBLANES].reshape(bsz, p.shape[1] // CHUNK, CHUNK, 4 * SUBLANES)
        return jnp.swapaxes(chunks, 2, 3)

    ctx_len = qkv_c.shape[1]
    p_c = _gdn_prep(small_c, a_log, dt_bias, ctx_len)
    oc_f, oc_b, s_ctx = _gdn_scan(qkv_c, p_c, rows_t(p_c), s0, GDN_HEADS, GDN_CHUNKS_PER_STEP)
    p = _gdn_prep(small, a_log, dt_bias, GDN_PREP_ROWS)
    o_f, o_b, _ = _gdn_scan(qkv, p, rows_t(p), s_ctx, GDN_HEADS, GDN_CHUNKS_PER_STEP)
    return (o_f, o_b), (oc_f, oc_b)


SSD_QUANTITIES = 4
SSD_HEADS_PER_STEP = SSD_GROUPS_PER_STEP * SSM_HPG
SSD_STEP_WIDTH = SSD_HEADS_PER_STEP * SSM_HEAD_DIM
BF16_PIECES = 3


def _ssd_prep_kernel(s_ref, alog_ref, dtb_ref, o_ref, *, rows):
    s = s_ref[...]
    lane = lax.broadcasted_iota(jnp.int32, (1, LANES), 1)
    fwd_lane = lane < SM_DT_B
    dt = _softplus(s + dtb_ref[...])
    la = dt * (-jnp.exp(alog_ref[...]))
    o_ref[:, 0:LANES] = dt
    for c in range(rows // CHUNK):
        sl = slice(c * CHUNK, (c + 1) * CHUNK)
        fwd, bwd = _chunk_cumsums(la[sl, :])
        gc = jnp.where(fwd_lane, fwd, bwd)
        g_last = jnp.where(fwd_lane, gc[CHUNK - 1:CHUNK, :], gc[0:1, :])
        o_ref[sl, LANES:2 * LANES] = gc
        o_ref[sl, 2 * LANES:3 * LANES] = jnp.exp(gc)
        o_ref[sl, 3 * LANES:4 * LANES] = dt[sl, :] * jnp.exp(g_last - gc)


def _ssd_prep(small_cols, a_log, dt_bias, n_seg):
    bsz, rows, _ = small_cols.shape
    pad = lambda t: jnp.pad(t.reshape(1, -1).astype(F32), ((0, 0), (SM_DT, LANES - SM_DT - 2 * SSM_HEADS)))
    return pl.pallas_call(
        functools.partial(_ssd_prep_kernel, rows=rows),
        out_shape=jax.ShapeDtypeStruct((bsz, n_seg, rows, SSD_QUANTITIES * LANES), F32),
        grid=(bsz, n_seg),
        in_specs=[pl.BlockSpec((None, rows, LANES), lambda b, c: (b, 0, c)),
                  pl.BlockSpec((1, LANES), lambda b, c: (0, 0)),
                  pl.BlockSpec((1, LANES), lambda b, c: (0, 0))],
        out_specs=pl.BlockSpec((None, None, rows, SSD_QUANTITIES * LANES), lambda b, c: (b, c, 0, 0)),
        compiler_params=_params(2),
        name="ssd_prep",
    )(small_cols, pad(a_log), pad(dt_bias))


def _ssd_layouts(q):
    bsz, n_seg, rows, _ = q.shape
    steps = SSM_GROUPS // SSD_GROUPS_PER_STEP
    q = q.reshape(bsz, n_seg, rows, SSD_QUANTITIES, LANES)[..., SM_DT:SM_DT + 2 * SSM_HEADS]
    q = q.reshape(bsz, n_seg, rows, SSD_QUANTITIES, 2, steps, SSD_HEADS_PER_STEP)
    out = []
    for d in range(2):
        qd = q[:, :, :, :, d]
        cols = jnp.transpose(qd, (0, 1, 4, 2, 3, 5)).reshape(bsz, n_seg, steps, rows, SSD_QUANTITIES * SSD_HEADS_PER_STEP)
        gc = qd[:, :, :, 1].reshape(bsz, n_seg, rows // CHUNK, CHUNK, steps, SSD_HEADS_PER_STEP)
        grow = jnp.transpose(gc, (0, 1, 4, 2, 5, 3)).reshape(bsz, n_seg, steps, rows // CHUNK, SSD_STEP_WIDTH)
        out.append((cols, grow))
    return out


def _ssd_expand_matrix():
    k = SSD_QUANTITIES * SSD_HEADS_PER_STEP
    src = jnp.arange(k)
    dst = jnp.arange(k * SSM_HEAD_DIM) // SSM_HEAD_DIM
    one = (src[:, None] == dst[None, :]).astype(BF16)
    return jnp.concatenate([one] * BF16_PIECES, axis=0)


def _ssd_chunks(items, incl, block_diag):
    n = len(items)
    cb = [_mm_nt(it[4], jnp.concatenate([it[3].astype(BF16)] * SSM_HPG, axis=0)) for it in items]
    inter = [_mm(it[4], it[9]) for it in items]
    lhs = []
    for i, it in enumerate(items):
        m = incl[it[10]]
        lhs.append(jnp.where(m, cb[i] * jnp.exp(jnp.where(m, it[5] - it[6], 0.0)), 0.0))
    rhs = [jnp.where(block_diag, jnp.concatenate([it[1].astype(BF16)] * SSM_HPG, axis=0), 0.0) for it in items]
    intra = [_mm(lhs[i], rhs[i]) for i in range(n)]
    upd = [_mm_tn(it[3], it[2]) for it in items]
    out = []
    for i, it in enumerate(items):
        y = intra[i] + inter[i] * it[7]
        if it[11] is not None:
            y = y + it[11] * it[0]
        out.append((y, it[8] * it[9] + upd[i]))
    return out


def _ssd_kernel(xf_ref, bf_ref, cf_ref, xb_ref, bb_ref, cb_ref, colf_ref, colb_ref, growf_ref, growb_ref,
                e_ref, d_ref, s0_ref, yf_ref, yb_ref, sout_ref, state_ref, *, rows, n_seg, gs):
    step = pl.program_id(2)

    @pl.when(step == 0)
    def _():
        state_ref[...] = s0_ref[...]

    gw = SSM_HPG * SSM_HEAD_DIM
    width = gs * gw

    def expand(col_ref):
        c = col_ref[...]
        hi = c.astype(BF16)
        r1 = c - hi.astype(F32)
        mid = r1.astype(BF16)
        lo = (r1 - mid.astype(F32)).astype(BF16)
        ex = jnp.dot(jnp.concatenate([hi, mid, lo], axis=1), e_ref[...], preferred_element_type=F32)
        return [ex[:, q * width:(q + 1) * width] for q in range(SSD_QUANTITIES)]

    xf, xb = xf_ref[...], xb_ref[...]
    dtx_f, gcx_f, egx_f, q4x_f = expand(colf_ref)
    dtx_b, gcx_b, egx_b, q4x_b = expand(colb_ref)
    xdt_f, xdec_f = xf * dtx_f, xf * q4x_f
    xdt_b, xdec_b = xb * dtx_b, xb * q4x_b
    bmf, cmf, bmb, cmb = bf_ref[...], cf_ref[...], bb_ref[...], cb_ref[...]
    d_all = d_ref[...]

    row = lax.broadcasted_iota(jnp.int32, (CHUNK, gw), 0)
    tok = lax.broadcasted_iota(jnp.int32, (CHUNK, gw), 1) & (CHUNK - 1)
    incl = {True: row >= tok, False: row <= tok}
    block_diag = (lax.shift_right_logical(lax.broadcasted_iota(jnp.int32, (gw, gw), 0), CHUNK.bit_length() - 1)
                  == lax.shift_right_logical(lax.broadcasted_iota(jnp.int32, (gw, gw), 1), SSM_HEAD_DIM.bit_length() - 1))

    n_chunks = rows // CHUNK
    states = [[state_ref[gg, 0], state_ref[gg, 1]] for gg in range(gs)]
    for c in range(n_chunks):
        cb = n_chunks - 1 - c
        sl_f = slice(c * CHUNK, (c + 1) * CHUNK)
        sl_b = slice(cb * CHUNK, (cb + 1) * CHUNK)
        last_f = slice(c * CHUNK + CHUNK - 1, (c + 1) * CHUNK)
        last_b = slice(cb * CHUNK, cb * CHUNK + 1)
        items = []
        for gg in range(gs):
            xl = slice(gg * gw, (gg + 1) * gw)
            nl = slice(gg * SSM_STATE, (gg + 1) * SSM_STATE)
            items.append((xf[sl_f, xl], xdt_f[sl_f, xl], xdec_f[sl_f, xl], bmf[sl_f, nl], cmf[sl_f, nl], gcx_f[sl_f, xl],
                          growf_ref[c:c + 1, xl], egx_f[sl_f, xl], egx_f[last_f, xl], states[gg][0], True, d_all[:, xl]))
            items.append((xb[sl_b, xl], xdt_b[sl_b, xl], xdec_b[sl_b, xl], bmb[sl_b, nl], cmb[sl_b, nl], gcx_b[sl_b, xl],
                          growb_ref[cb:cb + 1, xl], egx_b[sl_b, xl], egx_b[last_b, xl], states[gg][1], False, None))
        results = _ssd_chunks(items, incl, block_diag)
        for gg in range(gs):
            xl = slice(gg * gw, (gg + 1) * gw)
            yf_ref[sl_f, xl], states[gg][0] = results[2 * gg]
            yb_ref[sl_b, xl], states[gg][1] = results[2 * gg + 1]
    for gg in range(gs):
        state_ref[gg, 0] = states[gg][0]
        state_ref[gg, 1] = states[gg][1]

    @pl.when(step == n_seg - 1)
    def _():
        sout_ref[...] = state_ref[...]


def _ssd_scan(xbc, layouts, expand, d_exp, s0):
    bsz, n_seg, rows, _ = xbc.shape
    gs = SSD_GROUPS_PER_STEP
    xw, nw = SSD_STEP_WIDTH, gs * SSM_STATE
    boff, coff = SSM_INNER // nw, (SSM_INNER + SSM_GN) // nw
    (cols_f, grow_f), (cols_b, grow_b) = layouts
    n_cols = cols_f.shape[-1]
    seg_f = lambda s: s
    seg_b = lambda s: n_seg - 1 - s
    xspec = lambda seg: pl.BlockSpec((None, None, rows, xw), lambda b, g, s: (b, seg(s), 0, g))
    nspec = lambda seg, off: pl.BlockSpec((None, None, rows, nw), lambda b, g, s: (b, seg(s), 0, off + g))
    cspec = lambda seg: pl.BlockSpec((None, None, None, rows, n_cols), lambda b, g, s: (b, seg(s), g, 0, 0))
    rspec = lambda seg: pl.BlockSpec((None, None, None, rows // CHUNK, xw), lambda b, g, s: (b, seg(s), g, 0, 0))
    state_spec = pl.BlockSpec((None, gs, 2, SSM_STATE, SSM_HPG * SSM_HEAD_DIM), lambda b, g, s: (b, g, 0, 0, 0))
    y_shape = jax.ShapeDtypeStruct((bsz, n_seg, rows, SSM_INNER), F32)
    return pl.pallas_call(
        functools.partial(_ssd_kernel, rows=rows, n_seg=n_seg, gs=gs),
        out_shape=(y_shape, y_shape,
                   jax.ShapeDtypeStruct((bsz, SSM_GROUPS, 2, SSM_STATE, SSM_HPG * SSM_HEAD_DIM), F32)),
        grid=(bsz, SSM_GROUPS // gs, n_seg),
        in_specs=[xspec(seg_f), nspec(seg_f, boff), nspec(seg_f, coff),
                  xspec(seg_b), nspec(seg_b, boff), nspec(seg_b, coff),
                  cspec(seg_f), cspec(seg_b), rspec(seg_f), rspec(seg_b),
                  pl.BlockSpec(expand.shape, lambda b, g, s: (0, 0)),
                  pl.BlockSpec((1, xw), lambda b, g, s: (0, g)), state_spec],
        out_specs=(xspec(seg_f), xspec(seg_b), state_spec),
        scratch_shapes=[pltpu.VMEM((gs, 2, SSM_STATE, SSM_HPG * SSM_HEAD_DIM), F32)],
        compiler_params=_params(3),
        name="ssd_scan",
    )(xbc, xbc, xbc, xbc, xbc, xbc, cols_f, cols_b, grow_f, grow_b, expand, d_exp, s0)


def _ssd_branch(xbc, small, xbc_c, small_c, a_log, dt_bias, d_skip):
    bsz, n_seg, rows, _ = xbc.shape
    assert rows % CHUNK == 0 and xbc_c.shape[2] % CHUNK == 0
    d_exp = jnp.repeat(d_skip.astype(F32), SSM_HEAD_DIM).reshape(1, SSM_INNER)
    s0 = jnp.zeros((bsz, SSM_GROUPS, 2, SSM_STATE, SSM_HPG * SSM_HEAD_DIM), F32)
    expand = _ssd_expand_matrix()
    yc_f, yc_b, s_ctx = _ssd_scan(xbc_c, _ssd_layouts(_ssd_prep(small_c, a_log, dt_bias, 1)), expand, d_exp, s0)
    q = _ssd_prep(small.reshape(bsz, rows, n_seg * SMALL_N), a_log, dt_bias, n_seg)
    y_f, y_b, _ = _ssd_scan(xbc, _ssd_layouts(q), expand, d_exp, s_ctx)
    to_raster = lambda y: jnp.swapaxes(y, 1, 2).reshape(bsz, rows * n_seg, SSM_INNER)
    return (to_raster(y_f), to_raster(y_b)), (yc_f[:, 0], yc_b[:, 0])


def _layer_norm(r, g, b):
    mu = jnp.mean(r, axis=1, keepdims=True)
    var = jnp.mean(jnp.square(r - mu), axis=1, keepdims=True)
    return (r - mu) * lax.rsqrt(var + LN_EPS) * g + b


def _merge_kernel(x_ref, gout_ref, z_ref, ga_ref, gb_ref, of_ref, ob_ref, yf_ref, yb_ref,
                  nwa_ref, nwb_ref, wpg_ref, wps_ref, wout_ref, g1_ref, lng_ref, lnb_ref, o_ref, *, alpha):
    o = of_ref[...] + ob_ref[...]
    normed = []
    for h in range(GDN_HEADS):
        oh = o[:, h * GDN_DV:(h + 1) * GDN_DV]
        normed.append(oh * lax.rsqrt(jnp.mean(oh * oh, axis=1, keepdims=True) + NORM_EPS))
    y_a = jnp.concatenate(normed, axis=1) * nwa_ref[...] * _silu(gout_ref[...])
    t = (yf_ref[...] + yb_ref[...]) * _silu(z_ref[...])
    gw = SSM_INNER // SSM_GROUPS
    normed = []
    for g in range(SSM_GROUPS):
        tg = t[:, g * gw:(g + 1) * gw]
        normed.append(tg * lax.rsqrt(jnp.mean(tg * tg, axis=1, keepdims=True) + NORM_EPS))
    y_b = jnp.concatenate(normed, axis=1) * nwb_ref[...]
    mix = (jax.nn.sigmoid(ga_ref[...]) * jnp.dot(y_a.astype(BF16), wpg_ref[...], preferred_element_type=F32)
           + jax.nn.sigmoid(gb_ref[...]) * jnp.dot(y_b.astype(BF16), wps_ref[...], preferred_element_type=F32))
    out = jnp.dot(mix.astype(BF16), wout_ref[...], preferred_element_type=F32)
    o_ref[...] = _layer_norm(alpha * x_ref[...] + g1_ref[...] * out, lng_ref[...], lnb_ref[...])


def _merge(x, big, o_pair, y_pair, nwa, nwb, wpg, wps, wout, g1, ln_g, ln_b, alpha, tm):
    bsz, length, d = x.shape
    row = lambda width, off: pl.BlockSpec((None, tm, width), lambda b, i: (b, i, off // width))
    const = lambda shape: pl.BlockSpec(shape, lambda b, i: (0,) * len(shape))
    return pl.pallas_call(
        functools.partial(_merge_kernel, alpha=alpha),
        out_shape=jax.ShapeDtypeStruct((bsz, length, d), F32),
        grid=(bsz, length // tm),
        in_specs=[row(d, 0), row(GDN_V, GOUT_OFF), row(SSM_INNER, Z_OFF), row(d, GA_OFF), row(d, GB_OFF),
                  row(GDN_V, 0), row(GDN_V, 0), row(SSM_INNER, 0), row(SSM_INNER, 0),
                  const((1, GDN_V)), const((1, SSM_INNER)), const((GDN_V, d)), const((SSM_INNER, d)), const((d, d)),
                  pl.BlockSpec((None, 1, d), lambda b, i: (b, 0, 0)), const((1, d)), const((1, d))],
        out_specs=row(d, 0),
        compiler_params=_params(2),
        name="merge",
    )(x, big, big, big, big, o_pair[0], o_pair[1], y_pair[0], y_pair[1], nwa, nwb, wpg, wps, wout, g1, ln_g, ln_b)


def _mlp_kernel(x_ref, sc_ref, sh_ref, g2_ref, w1_ref, b1_ref, w2_ref, b2_ref, lng_ref, lnb_ref, o_ref, *, alpha):
    x = x_ref[...]
    h = (x * (1.0 + sc_ref[...]) + sh_ref[...]).astype(BF16)
    acc = None
    tf = D_MODEL
    for c in range(D_FF // tf):
        u = jnp.dot(h, w1_ref[:, c * tf:(c + 1) * tf], preferred_element_type=F32) + b1_ref[:, c * tf:(c + 1) * tf]
        u = jnp.square(jnp.maximum(u, 0.0))
        part = jnp.dot(u.astype(BF16), w2_ref[c * tf:(c + 1) * tf, :], preferred_element_type=F32)
        acc = part if acc is None else acc + part
    f = acc + b2_ref[...]
    o_ref[...] = _layer_norm(alpha * x + g2_ref[...] * f, lng_ref[...], lnb_ref[...])


def _mlp(x, sc, sh, g2, w1, b1, w2, b2, ln_g, ln_b, alpha, tm):
    bsz, length, d = x.shape
    const = lambda shape: pl.BlockSpec(shape, lambda b, i: (0,) * len(shape))
    mod = pl.BlockSpec((None, 1, d), lambda b, i: (b, 0, 0))
    return pl.pallas_call(
        functools.partial(_mlp_kernel, alpha=alpha),
        out_shape=jax.ShapeDtypeStruct((bsz, length, d), F32),
        grid=(bsz, length // tm),
        in_specs=[pl.BlockSpec((None, tm, d), lambda b, i: (b, i, 0)), mod, mod, mod,
                  const((d, D_FF)), const((1, D_FF)), const((D_FF, d)), const((1, d)), const((1, d)), const((1, d))],
        out_specs=pl.BlockSpec((None, tm, d), lambda b, i: (b, i, 0)),
        compiler_params=_params(2),
        name="mlp",
    )(x, sc, sh, g2, w1, b1, w2, b2, ln_g, ln_b)


def _split_w_in(w_in):
    pts, acc = [], 0
    for s in IN_SPLITS[:-1]:
        acc += s
        pts.append(acc)
    qkv, gout, a_raw, b_raw, z, xbc, dt_raw, gate_a, gate_b = jnp.split(w_in, pts, axis=1)
    big = jnp.concatenate([z, gout, gate_a, gate_b], axis=1).astype(BF16)
    pad = jnp.zeros((w_in.shape[0], SMALL_N - SM_DT - 2 * SSM_HEADS), w_in.dtype)
    small = jnp.concatenate([a_raw, b_raw, dt_raw, pad], axis=1).astype(BF16)
    return big, qkv.astype(BF16), xbc.astype(BF16), small


def kernel(x, c, ctx, c_ctx, w_mod, b_mod, w_in, gdn_conv_w, gdn_A_log, gdn_dt_bias, gdn_norm_w,
           ssm_conv_w, ssm_conv_b, ssm_A_log, ssm_dt_bias, ssm_D, ssm_norm_w,
           w_proj_gdn, w_proj_ssm, w_out, ln1_g, ln1_b, w_ff1, b_ff1, w_ff2, b_ff2, ln2_g, ln2_b):
    bsz, length, d = x.shape
    ctx_len = ctx.shape[1]
    depth = w_mod.shape[0]
    alpha = float((2 * depth) ** 0.25)
    mod_rows = -(-(bsz + 1) // (2 * SUBLANES)) * (2 * SUBLANES)
    cc = jnp.concatenate([c, c_ctx[None, :], jnp.zeros((mod_rows - bsz - 1, d), c.dtype)], axis=0)
    row2 = lambda t: t.reshape(1, -1)
    tm_lat = 1024
    tm_ctx = ctx_len

    for l in range(depth):
        last = l == depth - 1
        mod = _modulation(cc, w_mod[l], b_mod[l])
        lat = [mod[:bsz, i * d:(i + 1) * d].reshape(bsz, 1, d) for i in range(6)]
        cxm = [jnp.broadcast_to(mod[bsz, i * d:(i + 1) * d].reshape(1, 1, d), (bsz, 1, d)) for i in range(6)]
        w_big, w_qkv, w_xbc, w_small = _split_w_in(w_in[l])
        rows = length // GRID_W

        big = _inproj(x, lat[1], lat[0], w_big, tm_lat, 1024, "inproj_gates")
        small = _inproj(x, lat[1], lat[0], w_small, tm_lat, SMALL_N, "inproj_narrow")
        qkv = _proj_conv(x, lat[1], lat[0], w_qkv, gdn_conv_w[l], None, GRID_W, False, 2, tm_lat, GDN_QK, "inproj_qkv")
        xbc = _proj_conv(x, lat[1], lat[0], w_xbc, ssm_conv_w[l], ssm_conv_b[l], rows, True, 0, None, 1024,
                         "inproj_xbc")
        big_c = _inproj(ctx, cxm[1], cxm[0], w_big, tm_ctx, 1024, "inproj_gates_ctx")
        small_c = _inproj(ctx, cxm[1], cxm[0], w_small, tm_ctx, SMALL_N, "inproj_narrow_ctx")
        qkv_c = _proj_conv(ctx, cxm[1], cxm[0], w_qkv, gdn_conv_w[l], None, ctx_len, False, 2, tm_ctx, GDN_QK,
                           "inproj_qkv_ctx")
        xbc_c = _proj_conv(ctx, cxm[1], cxm[0], w_xbc, ssm_conv_w[l], ssm_conv_b[l], ctx_len, False, 0, tm_ctx, 1024,
                           "inproj_xbc_ctx")[:, None]

        o_pair, oc_pair = _gdn_branch(qkv, small, qkv_c, small_c, gdn_A_log[l], gdn_dt_bias[l])
        y_pair, yc_pair = _ssd_branch(xbc, small, xbc_c, small_c, ssm_A_log[l], ssm_dt_bias[l], ssm_D[l])

        nwa = jnp.tile(gdn_norm_w[l], GDN_HEADS).reshape(1, GDN_V)
        nwb = row2(ssm_norm_w[l])
        wpg, wps, wo = w_proj_gdn[l].astype(BF16), w_proj_ssm[l].astype(BF16), w_out[l].astype(BF16)
        w1, w2 = w_ff1[l].astype(BF16), w_ff2[l].astype(BF16)
        merge_args = (nwa, nwb, wpg, wps, wo)
        ln1 = (row2(ln1_g[l]), row2(ln1_b[l]))
        mlp_w = (w1, row2(b_ff1[l]), w2, row2(b_ff2[l]), row2(ln2_g[l]), row2(ln2_b[l]))

        x1 = _merge(x, big, o_pair, y_pair, *merge_args, lat[2], *ln1, alpha, 256)
        x = _mlp(x1, lat[4], lat[3], lat[5], *mlp_w, alpha, 512)
        if not last:
            c1 = _merge(ctx, big_c, oc_pair, yc_pair, *merge_args, cxm[2], *ln1, alpha, ctx_len)
            ctx = _mlp(c1, cxm[4], cxm[3], cxm[5], *mlp_w, alpha, ctx_len)
    return x
```

```python
import functools

import jax
import jax.numpy as jnp
from jax import lax
from jax.experimental import pallas as pl
from jax.experimental.pallas import tpu as pltpu

F32 = jnp.float32
BF16 = jnp.bfloat16

D_MODEL = 1024
GRID_W = 64
GDN_HEADS = 8
GDN_DK = 128
GDN_DV = 128
GDN_QK = GDN_HEADS * GDN_DK
GDN_V = GDN_HEADS * GDN_DV
GDN_QKV = 2 * GDN_QK + GDN_V
SSM_INNER = 2 * D_MODEL
SSM_HEAD_DIM = 64
SSM_HEADS = SSM_INNER // SSM_HEAD_DIM
SSM_GROUPS = 8
SSM_HPG = SSM_HEADS // SSM_GROUPS
SSM_STATE = 128
SSM_GN = SSM_GROUPS * SSM_STATE
SSM_XBC = SSM_INNER + 2 * SSM_GN
CONV_K = 5
CHUNK = 64
D_FF = 4 * D_MODEL
LN_EPS = 1e-5
NORM_EPS = 1e-6
IN_SPLITS = (GDN_QKV, GDN_V, 2 * GDN_HEADS, 2 * GDN_HEADS, SSM_INNER, SSM_XBC, 2 * SSM_HEADS, D_MODEL, D_MODEL)

LANES = 128
SUBLANES = 8
VMEM_LIMIT_BYTES = 56 * 1024 * 1024

Z_OFF = 0
GOUT_OFF = Z_OFF + SSM_INNER
GA_OFF = GOUT_OFF + GDN_V
GB_OFF = GA_OFF + D_MODEL
BIG_N = GB_OFF + D_MODEL
SM_A = 0
SM_B = SM_A + 2 * GDN_HEADS
SM_DT = SM_B + 2 * GDN_HEADS
SM_DT_B = SM_DT + SSM_HEADS
SMALL_N = LANES

GDN_CHUNKS_PER_STEP = 1
GDN_PREP_ROWS = 512
SCAN_OUT_DTYPE = BF16
MERGE_ROWS = 512
MLP_ROWS = 1024
SSD_GROUPS_PER_STEP = 4


def _params(n_axes):
    return pltpu.CompilerParams(dimension_semantics=("arbitrary",) * n_axes, vmem_limit_bytes=VMEM_LIMIT_BYTES)


def _silu(t):
    return t * jax.nn.sigmoid(t)


def _softplus(t):
    return jnp.maximum(t, 0.0) + jnp.log(1.0 + jnp.exp(-jnp.abs(t)))


def _mm(a, b):
    return jnp.dot(a.astype(BF16), b.astype(BF16), preferred_element_type=F32)


def _mm_nt(a, b):
    return lax.dot_general(a.astype(BF16), b.astype(BF16), (((1,), (1,)), ((), ())), preferred_element_type=F32)


def _mm_tn(a, b):
    return lax.dot_general(a.astype(BF16), b.astype(BF16), (((0,), (0,)), ((), ())), preferred_element_type=F32)


def _tri_masks(lower):
    ii = lax.broadcasted_iota(jnp.int32, (CHUNK, CHUNK), 0)
    jj = lax.broadcasted_iota(jnp.int32, (CHUNK, CHUNK), 1)
    if lower:
        return ii >= jj, ii > jj
    return ii <= jj, ii < jj


def _conv_seg(x, w, seg):
    n = x.shape[0]
    assert seg & (seg - 1) == 0 and n % seg == 0
    pos = lax.broadcasted_iota(jnp.int32, (n, 1), 0) & (seg - 1)
    out = None
    for j in range(CONV_K):
        d = j - CONV_K // 2
        if d == 0:
            term = x * w[j:j + 1, :]
        else:
            shifted = pltpu.roll(x, shift=(-d) % n, axis=0)
            valid = (pos + d >= 0) & (pos + d < seg)
            term = jnp.where(valid, shifted, 0.0) * w[j:j + 1, :]
        out = term if out is None else out + term
    return out


def _mod_kernel(c_ref, w_ref, b_ref, o_ref):
    o_ref[...] = _mm(_silu(c_ref[...]), w_ref[...]) + b_ref[...]


def _modulation(cc, w, b):
    rows, d = cc.shape
    n = w.shape[1]
    tn = 1536
    return pl.pallas_call(
        _mod_kernel,
        out_shape=jax.ShapeDtypeStruct((rows, n), F32),
        grid=(n // tn,),
        in_specs=[pl.BlockSpec((rows, d), lambda j: (0, 0)),
                  pl.BlockSpec((d, tn), lambda j: (0, j)),
                  pl.BlockSpec((1, tn), lambda j: (0, j))],
        out_specs=pl.BlockSpec((rows, tn), lambda j: (0, j)),
        compiler_params=_params(1),
        name="modulation",
    )(cc, w, b.reshape(1, n))


def _inproj_kernel(x_ref, sc_ref, sh_ref, w_ref, o_ref):
    h = x_ref[...] * (1.0 + sc_ref[...]) + sh_ref[...]
    o_ref[...] = jnp.dot(h.astype(BF16), w_ref[...], preferred_element_type=F32)


def _inproj(x, sc, sh, w, tm, tn, name):
    bsz, length, d = x.shape
    n = w.shape[1]
    return pl.pallas_call(
        _inproj_kernel,
        out_shape=jax.ShapeDtypeStruct((bsz, length, n), F32),
        grid=(bsz, length // tm, n // tn),
        in_specs=[pl.BlockSpec((None, tm, d), lambda b, i, j: (b, i, 0)),
                  pl.BlockSpec((None, 1, d), lambda b, i, j: (b, 0, 0)),
                  pl.BlockSpec((None, 1, d), lambda b, i, j: (b, 0, 0)),
                  pl.BlockSpec((d, tn), lambda b, i, j: (0, j))],
        out_specs=pl.BlockSpec((None, tm, tn), lambda b, i, j: (b, i, j)),
        compiler_params=_params(3),
        name=name,
    )(x, sc, sh, w)


def _proj_conv_kernel(*refs, seg, cols, has_bias, norm_tiles):
    x_ref, sc_ref, sh_ref, w_ref, cw_ref = refs[:5]
    cb_ref = refs[5] if has_bias else None
    o_ref, h_ref = refs[5 + has_bias:]
    j = pl.program_id(2)
    n_tok = h_ref.shape[0]

    @pl.when(j == 0)
    def _():
        x = x_ref[...]
        if cols:
            cps = x.shape[1]
            x = x.reshape(n_tok, x.shape[2])
        h = (x * (1.0 + sc_ref[...]) + sh_ref[...]).astype(BF16)
        if cols:
            ii = lax.broadcasted_iota(jnp.int32, (n_tok, n_tok), 0)
            jj = lax.broadcasted_iota(jnp.int32, (n_tok, n_tok), 1)
            src = (ii & (seg - 1)) * cps + lax.shift_right_logical(ii, seg.bit_length() - 1)
            perm = jnp.where(jj == src, 1.0, 0.0).astype(BF16)
            h = jnp.dot(perm, h, preferred_element_type=F32).astype(BF16)
        h_ref[...] = h

    y = jnp.dot(h_ref[...], w_ref[...], preferred_element_type=F32)
    y = _conv_seg(y, cw_ref[...], seg)
    if has_bias:
        y = y + cb_ref[...]
    y = _silu(y)

    def store(t):
        o_ref[...] = t.reshape(o_ref.shape)

    if norm_tiles:
        @pl.when(j < norm_tiles)
        def _():
            scale = jnp.where(j == 0, GDN_DK ** -0.5, 1.0)
            heads = []
            for hh in range(y.shape[1] // GDN_DK):
                yh = y[:, hh * GDN_DK:(hh + 1) * GDN_DK]
                heads.append(yh * (lax.rsqrt(jnp.sum(yh * yh, axis=1, keepdims=True) + NORM_EPS) * scale))
            store(jnp.concatenate(heads, axis=1))

        @pl.when(j >= norm_tiles)
        def _():
            store(y)
    else:
        store(y)


def _proj_conv(x, sc, sh, w, conv_w, conv_b, seg, cols, norm_tiles, tm, tn, name):
    bsz, length, d = x.shape
    n = w.shape[1]
    has_bias = conv_b is not None
    if cols:
        rows, cps = length // GRID_W, SUBLANES
        assert seg == rows and seg & (seg - 1) == 0
        n_tok = rows * cps
        x_in = x.reshape(bsz, rows, GRID_W, d)
        x_spec = pl.BlockSpec((None, rows, cps, d), lambda b, i, j: (b, 0, i, 0))
        out_shape = jax.ShapeDtypeStruct((bsz, GRID_W, rows, n), F32)
        out_spec = pl.BlockSpec((None, cps, rows, tn), lambda b, i, j: (b, i, 0, j))
        grid = (bsz, GRID_W // cps, n // tn)
    else:
        n_tok = tm
        x_in = x
        x_spec = pl.BlockSpec((None, tm, d), lambda b, i, j: (b, i, 0))
        out_shape = jax.ShapeDtypeStruct((bsz, length, n), F32)
        out_spec = pl.BlockSpec((None, tm, tn), lambda b, i, j: (b, i, j))
        grid = (bsz, length // tm, n // tn)
    mod = pl.BlockSpec((None, 1, d), lambda b, i, j: (b, 0, 0))
    in_specs = [x_spec, mod, mod, pl.BlockSpec((d, tn), lambda b, i, j: (0, j)),
                pl.BlockSpec((CONV_K, tn), lambda b, i, j: (0, j))]
    args = [x_in, sc, sh, w, conv_w]
    if has_bias:
        in_specs.append(pl.BlockSpec((1, tn), lambda b, i, j: (0, j)))
        args.append(conv_b.reshape(1, n))
    return pl.pallas_call(
        functools.partial(_proj_conv_kernel, seg=seg, cols=cols, has_bias=has_bias, norm_tiles=norm_tiles),
        out_shape=out_shape,
        grid=grid,
        in_specs=in_specs,
        out_specs=out_spec,
        scratch_shapes=[pltpu.VMEM((n_tok, d), BF16)],
        compiler_params=_params(3),
        name=name,
    )(*args)


def _chunk_cumsums(t):
    ii = lax.broadcasted_iota(jnp.int32, (2 * CHUNK, CHUNK), 0)
    jj = lax.broadcasted_iota(jnp.int32, (2 * CHUNK, CHUNK), 1)
    ones = ((ii < CHUNK) & (ii >= jj)) | ((ii >= CHUNK) & (ii - CHUNK <= jj))
    tri = jnp.where(ones, 1.0, 0.0).astype(BF16)
    hi = t.astype(BF16)
    r1 = t - hi.astype(F32)
    mid = r1.astype(BF16)
    lo = (r1 - mid.astype(F32)).astype(BF16)
    w = t.shape[1]
    sums = jnp.dot(tri, jnp.concatenate([hi, mid, lo], axis=1), preferred_element_type=F32)
    sums = sums[:, :w] + sums[:, w:2 * w] + sums[:, 2 * w:]
    return sums[:CHUNK], sums[CHUNK:]


def _gdn_prep_kernel(s_ref, alog_ref, dtb_ref, o_ref, *, rows):
    s = s_ref[...]
    lane = lax.broadcasted_iota(jnp.int32, (1, LANES), 1)
    g = -jnp.exp(alog_ref[...]) * _softplus(s + dtb_ref[...])
    beta = jax.nn.sigmoid(s)
    for c in range(rows // CHUNK):
        sl = slice(c * CHUNK, (c + 1) * CHUNK)
        fwd, bwd = _chunk_cumsums(g[sl, :])
        gc = jnp.where(lane < SM_A + GDN_HEADS, fwd, bwd)
        o_ref[sl, :] = jnp.where(lane < SM_B, gc, beta[sl, :])


def _gdn_prep(small, a_log, dt_bias, tb):
    bsz, length, _ = small.shape
    pad = lambda t: jnp.pad(t.reshape(1, -1).astype(F32), ((0, 0), (SM_A, LANES - SM_A - 2 * GDN_HEADS)))
    return pl.pallas_call(
        functools.partial(_gdn_prep_kernel, rows=tb),
        out_shape=jax.ShapeDtypeStruct((bsz, length, LANES), F32),
        grid=(bsz, length // tb),
        in_specs=[pl.BlockSpec((None, tb, LANES), lambda b, i: (b, i, 0)),
                  pl.BlockSpec((1, LANES), lambda b, i: (0, 0)),
                  pl.BlockSpec((1, LANES), lambda b, i: (0, 0))],
        out_specs=pl.BlockSpec((None, tb, LANES), lambda b, i: (b, i, 0)),
        compiler_params=_params(2),
        name="gdn_prep",
    )(small, pad(a_log), pad(dt_bias))


def _tri_inverse(mats):
    ii = lax.broadcasted_iota(jnp.int32, (CHUNK, CHUNK), 0)
    jj = lax.broadcasted_iota(jnp.int32, (CHUNK, CHUNK), 1)
    eye = jnp.where(ii == jj, 1.0, 0.0)
    zero = jnp.zeros((CHUNK, CHUNK), F32)
    right = lax.broadcasted_iota(jnp.int32, (CHUNK, 2 * CHUNK), 1) >= CHUNK
    zs = [jnp.concatenate([zero, eye], axis=1) + _mm(a, jnp.concatenate([a, -eye], axis=1)) for a in mats]
    power = 2
    while power < CHUNK:
        ps = [_mm(z[:, :CHUNK], z) for z in zs]
        zs = [p + jnp.where(right, z, 0.0) for p, z in zip(ps, zs)]
        power *= 2
    return zs


def _gdn_local(items, masks):
    qs, ks, vs, gcols, grows, bcols, lowers = zip(*items)
    n = len(items)
    incl = [masks[lo][0] for lo in lowers]
    strict = [masks[lo][1] for lo in lowers]
    glast = [grows[i][:, CHUNK - 1:CHUNK] if lowers[i] else grows[i][:, 0:1] for i in range(n)]
    k16 = [k.astype(BF16) for k in ks]
    kq = [_mm_nt(jnp.concatenate([k16[i], qs[i].astype(BF16)], axis=0), k16[i]) for i in range(n)]
    dec = [jnp.exp(jnp.where(incl[i], gcols[i] - grows[i], 0.0)) for i in range(n)]
    a_mats = [jnp.where(strict[i], bcols[i] * kq[i][:CHUNK] * dec[i], 0.0) for i in range(n)]
    t_inv = _tri_inverse(a_mats)
    eg = [jnp.exp(g) for g in gcols]
    rhs = [jnp.concatenate([(bcols[i] * eg[i]) * ks[i], bcols[i] * vs[i]], axis=1) for i in range(n)]
    pad = jnp.zeros((CHUNK, GDN_DK + GDN_DV), F32)
    wu = [_mm(t, jnp.concatenate([pad, r], axis=0)) for t, r in zip(t_inv, rhs)]
    lhs = [jnp.concatenate([wu[i][:, :GDN_DK].astype(BF16), (qs[i] * eg[i]).astype(BF16)], axis=0) for i in range(n)]
    return [(lhs[i], wu[i][:, GDN_DK:], incl[i], kq[i][CHUNK:], dec[i], ks[i], glast[i], gcols[i]) for i in range(n)]


def _gdn_on_state(local, states):
    n = len(local)
    on_state = [_mm(local[i][0], states[i]) for i in range(n)]
    v_new = [local[i][1] - on_state[i][:CHUNK] for i in range(n)]
    o_intra = [_mm(jnp.where(lo[2], lo[3] * lo[4], 0.0), v_new[i]) for i, lo in enumerate(local)]
    kv = [_mm_tn(lo[5] * jnp.exp(lo[6] - lo[7]), v_new[i]) for i, lo in enumerate(local)]
    return [(on_state[i][CHUNK:] + o_intra[i], jnp.exp(local[i][6]) * states[i] + kv[i]) for i in range(n)]


def _lane_column(t, idx, lane):
    if isinstance(idx, int):
        return t[:, idx:idx + 1]
    return jnp.sum(jnp.where(lane == idx, t, 0.0), axis=1, keepdims=True)


def _gdn_kernel(qf_ref, kf_ref, vf_ref, qb_ref, kb_ref, vb_ref, pf_ref, pb_ref, ptf_ref, ptb_ref,
                s0_ref, of_ref, ob_ref, sout_ref, state_ref, *, n_steps, hs, nc):
    step = pl.program_id(2)
    head0 = 0 if hs == GDN_HEADS else pl.program_id(1) * hs

    @pl.when(step == 0)
    def _():
        state_ref[...] = s0_ref[...]

    lane = lax.broadcasted_iota(jnp.int32, (1, LANES), 1)

    def heads_of(q_ref, k_ref, v_ref):
        q, k, v = q_ref[...], k_ref[...], v_ref[...]
        return [(q[:, hh * GDN_DK:(hh + 1) * GDN_DK], k[:, hh * GDN_DK:(hh + 1) * GDN_DK],
                 v[:, hh * GDN_DV:(hh + 1) * GDN_DV]) for hh in range(hs)]

    heads_f = heads_of(qf_ref, kf_ref, vf_ref)
    heads_b = heads_of(qb_ref, kb_ref, vb_ref)
    pf = pf_ref[...]
    pb = pb_ref[...]
    masks = {True: _tri_masks(True), False: _tri_masks(False)}
    cols = []
    for hh in range(hs):
        head = head0 + hh
        cols.append((_lane_column(pf, SM_A + head, lane), _lane_column(pf, SM_B + head, lane),
                     _lane_column(pb, SM_A + GDN_HEADS + head, lane), _lane_column(pb, SM_B + GDN_HEADS + head, lane)))
    items = []
    for c in range(nc):
        cb = nc - 1 - c
        sl_f = slice(c * CHUNK, (c + 1) * CHUNK)
        sl_b = slice(cb * CHUNK, (cb + 1) * CHUNK)
        for hh in range(hs):
            head = head0 + hh
            gcol_f, bcol_f, gcol_b, bcol_b = cols[hh]
            qf, kf, vf = heads_f[hh]
            qb, kb, vb = heads_b[hh]
            grow_f = ptf_ref[c, pl.ds(SM_A + head, 1), :]
            grow_b = ptb_ref[cb, pl.ds(SM_A + GDN_HEADS + head, 1), :]
            items.append((qf[sl_f], kf[sl_f], vf[sl_f], gcol_f[sl_f], grow_f, bcol_f[sl_f], True))
            items.append((qb[sl_b], kb[sl_b], vb[sl_b], gcol_b[sl_b], grow_b, bcol_b[sl_b], False))
    local = _gdn_local(items, masks)
    states = [state_ref[hh, d] for hh in range(hs) for d in range(2)]
    for c in range(nc):
        cb = nc - 1 - c
        results = _gdn_on_state(local[c * 2 * hs:(c + 1) * 2 * hs], states)
        states = [r[1] for r in results]
        for hh in range(hs):
            hl = slice(hh * GDN_DV, (hh + 1) * GDN_DV)
            of_ref[c * CHUNK:(c + 1) * CHUNK, hl] = results[2 * hh][0].astype(of_ref.dtype)
            ob_ref[cb * CHUNK:(cb + 1) * CHUNK, hl] = results[2 * hh + 1][0].astype(ob_ref.dtype)
    for hh in range(hs):
        state_ref[hh, 0] = states[2 * hh]
        state_ref[hh, 1] = states[2 * hh + 1]

    @pl.when(step == n_steps - 1)
    def _():
        sout_ref[...] = state_ref[...]


def _gdn_scan(qkv, p, pt, s0, hs, nc):
    bsz, length, _ = qkv.shape
    tb = nc * CHUNK
    n_steps = length // tb
    width = hs * GDN_DK
    qoff, koff, voff = 0, GDN_QK // width, 2 * GDN_QK // width
    fwd = lambda off: pl.BlockSpec((None, tb, width), lambda b, h, s: (b, s, off + h))
    bwd = lambda off: pl.BlockSpec((None, tb, width), lambda b, h, s: (b, n_steps - 1 - s, off + h))
    state_spec = pl.BlockSpec((None, hs, 2, GDN_DK, GDN_DV), lambda b, h, s: (b, h, 0, 0, 0))
    return pl.pallas_call(
        functools.partial(_gdn_kernel, n_steps=n_steps, hs=hs, nc=nc),
        out_shape=(jax.ShapeDtypeStruct((bsz, length, GDN_V), SCAN_OUT_DTYPE),
                   jax.ShapeDtypeStruct((bsz, length, GDN_V), SCAN_OUT_DTYPE),
                   jax.ShapeDtypeStruct((bsz, GDN_HEADS, 2, GDN_DK, GDN_DV), F32)),
        grid=(bsz, GDN_HEADS // hs, n_steps),
        in_specs=[fwd(qoff), fwd(koff), fwd(voff), bwd(qoff), bwd(koff), bwd(voff),
                  pl.BlockSpec((None, tb, LANES), lambda b, h, s: (b, s, 0)),
                  pl.BlockSpec((None, tb, LANES), lambda b, h, s: (b, n_steps - 1 - s, 0)),
                  pl.BlockSpec((None, nc, 4 * SUBLANES, CHUNK), lambda b, h, s: (b, s, 0, 0)),
                  pl.BlockSpec((None, nc, 4 * SUBLANES, CHUNK), lambda b, h, s: (b, n_steps - 1 - s, 0, 0)),
                  state_spec],
        out_specs=(pl.BlockSpec((None, tb, width), lambda b, h, s: (b, s, h)),
                   pl.BlockSpec((None, tb, width), lambda b, h, s: (b, n_steps - 1 - s, h)),
                   state_spec),
        scratch_shapes=[pltpu.VMEM((hs, 2, GDN_DK, GDN_DV), F32)],
        compiler_params=_params(3),
        name="gdn_scan",
    )(qkv, qkv, qkv, qkv, qkv, qkv, p, p, pt, pt, s0)


def _gdn_branch(qkv, small, qkv_c, small_c, a_log, dt_bias):
    bsz = qkv.shape[0]
    s0 = jnp.zeros((bsz, GDN_HEADS, 2, GDN_DK, GDN_DV), F32)

    def rows_t(p):
        chunks = p[:, :, :4 * SUBLANES].reshape(bsz, p.shape[1] // CHUNK, CHUNK, 4 * SUBLANES)
        return jnp.swapaxes(chunks, 2, 3)

    ctx_len = qkv_c.shape[1]
    p_c = _gdn_prep(small_c, a_log, dt_bias, ctx_len)
    oc_f, oc_b, s_ctx = _gdn_scan(qkv_c, p_c, rows_t(p_c), s0, GDN_HEADS, GDN_CHUNKS_PER_STEP)
    p = _gdn_prep(small, a_log, dt_bias, GDN_PREP_ROWS)
    o_f, o_b, _ = _gdn_scan(qkv, p, rows_t(p), s_ctx, GDN_HEADS, GDN_CHUNKS_PER_STEP)
    return (o_f, o_b), (oc_f, oc_b)


SSD_QUANTITIES = 4
SSD_HEADS_PER_STEP = SSD_GROUPS_PER_STEP * SSM_HPG
SSD_STEP_WIDTH = SSD_HEADS_PER_STEP * SSM_HEAD_DIM
BF16_PIECES = 3


def _ssd_prep_kernel(s_ref, alog_ref, dtb_ref, o_ref, *, rows):
    s = s_ref[...]
    lane = lax.broadcasted_iota(jnp.int32, (1, LANES), 1)
    fwd_lane = lane < SM_DT_B
    dt = _softplus(s + dtb_ref[...])
    la = dt * (-jnp.exp(alog_ref[...]))
    o_ref[:, 0:LANES] = dt
    for c in range(rows // CHUNK):
        sl = slice(c * CHUNK, (c + 1) * CHUNK)
        fwd, bwd = _chunk_cumsums(la[sl, :])
        gc = jnp.where(fwd_lane, fwd, bwd)
        g_last = jnp.where(fwd_lane, gc[CHUNK - 1:CHUNK, :], gc[0:1, :])
        o_ref[sl, LANES:2 * LANES] = gc
        o_ref[sl, 2 * LANES:3 * LANES] = jnp.exp(gc)
        o_ref[sl, 3 * LANES:4 * LANES] = dt[sl, :] * jnp.exp(g_last - gc)


def _ssd_prep(small_cols, a_log, dt_bias, n_seg):
    bsz, rows, _ = small_cols.shape
    pad = lambda t: jnp.pad(t.reshape(1, -1).astype(F32), ((0, 0), (SM_DT, LANES - SM_DT - 2 * SSM_HEADS)))
    return pl.pallas_call(
        functools.partial(_ssd_prep_kernel, rows=rows),
        out_shape=jax.ShapeDtypeStruct((bsz, n_seg, rows, SSD_QUANTITIES * LANES), F32),
        grid=(bsz, n_seg),
        in_specs=[pl.BlockSpec((None, rows, LANES), lambda b, c: (b, 0, c)),
                  pl.BlockSpec((1, LANES), lambda b, c: (0, 0)),
                  pl.BlockSpec((1, LANES), lambda b, c: (0, 0))],
        out_specs=pl.BlockSpec((None, None, rows, SSD_QUANTITIES * LANES), lambda b, c: (b, c, 0, 0)),
        compiler_params=_params(2),
        name="ssd_prep",
    )(small_cols, pad(a_log), pad(dt_bias))


def _ssd_layouts(q):
    bsz, n_seg, rows, _ = q.shape
    steps = SSM_GROUPS // SSD_GROUPS_PER_STEP
    q = q.reshape(bsz, n_seg, rows, SSD_QUANTITIES, LANES)[..., SM_DT:SM_DT + 2 * SSM_HEADS]
    q = q.reshape(bsz, n_seg, rows, SSD_QUANTITIES, 2, steps, SSD_HEADS_PER_STEP)
    out = []
    for d in range(2):
        qd = q[:, :, :, :, d]
        cols = jnp.transpose(qd, (0, 1, 4, 2, 3, 5)).reshape(bsz, n_seg, steps, rows, SSD_QUANTITIES * SSD_HEADS_PER_STEP)
        gc = qd[:, :, :, 1].reshape(bsz, n_seg, rows // CHUNK, CHUNK, steps, SSD_HEADS_PER_STEP)
        grow = jnp.transpose(gc, (0, 1, 4, 2, 5, 3)).reshape(bsz, n_seg, steps, rows // CHUNK, SSD_STEP_WIDTH)
        out.append((cols, grow))
    return out


def _ssd_expand_matrix():
    k = SSD_QUANTITIES * SSD_HEADS_PER_STEP
    src = jnp.arange(k)
    dst = jnp.arange(k * SSM_HEAD_DIM) // SSM_HEAD_DIM
    one = (src[:, None] == dst[None, :]).astype(BF16)
    return jnp.concatenate([one] * BF16_PIECES, axis=0)


def _ssd_chunks(items, incl, block_diag):
    n = len(items)
    cb = [_mm_nt(it[4], jnp.concatenate([it[3].astype(BF16)] * SSM_HPG, axis=0)) for it in items]
    inter = [_mm(it[4], it[9]) for it in items]
    lhs = []
    for i, it in enumerate(items):
        m = incl[it[10]]
        lhs.append(jnp.where(m, cb[i] * jnp.exp(jnp.where(m, it[5] - it[6], 0.0)), 0.0))
    rhs = [jnp.where(block_diag, jnp.concatenate([it[1].astype(BF16)] * SSM_HPG, axis=0), 0.0) for it in items]
    intra = [_mm(lhs[i], rhs[i]) for i in range(n)]
    upd = [_mm_tn(it[3], it[2]) for it in items]
    out = []
    for i, it in enumerate(items):
        y = intra[i] + inter[i] * it[7]
        if it[11] is not None:
            y = y + it[11] * it[0]
        out.append((y, it[8] * it[9] + upd[i]))
    return out


def _ssd_kernel(xf_ref, bf_ref, cf_ref, xb_ref, bb_ref, cb_ref, colf_ref, colb_ref, growf_ref, growb_ref,
                e_ref, d_ref, s0_ref, yf_ref, yb_ref, sout_ref, state_ref, *, rows, n_seg, gs):
    step = pl.program_id(2)

    @pl.when(step == 0)
    def _():
        state_ref[...] = s0_ref[...]

    gw = SSM_HPG * SSM_HEAD_DIM
    width = gs * gw

    def expand(col_ref):
        c = col_ref[...]
        hi = c.astype(BF16)
        r1 = c - hi.astype(F32)
        mid = r1.astype(BF16)
        lo = (r1 - mid.astype(F32)).astype(BF16)
        ex = jnp.dot(jnp.concatenate([hi, mid, lo], axis=1), e_ref[...], preferred_element_type=F32)
        return [ex[:, q * width:(q + 1) * width] for q in range(SSD_QUANTITIES)]

    xf, xb = xf_ref[...], xb_ref[...]
    dtx_f, gcx_f, egx_f, q4x_f = expand(colf_ref)
    dtx_b, gcx_b, egx_b, q4x_b = expand(colb_ref)
    xdt_f, xdec_f = xf * dtx_f, xf * q4x_f
    xdt_b, xdec_b = xb * dtx_b, xb * q4x_b
    bmf, cmf, bmb, cmb = bf_ref[...], cf_ref[...], bb_ref[...], cb_ref[...]
    d_all = d_ref[...]

    row = lax.broadcasted_iota(jnp.int32, (CHUNK, gw), 0)
    tok = lax.broadcasted_iota(jnp.int32, (CHUNK, gw), 1) & (CHUNK - 1)
    incl = {True: row >= tok, False: row <= tok}
    block_diag = (lax.shift_right_logical(lax.broadcasted_iota(jnp.int32, (gw, gw), 0), CHUNK.bit_length() - 1)
                  == lax.shift_right_logical(lax.broadcasted_iota(jnp.int32, (gw, gw), 1), SSM_HEAD_DIM.bit_length() - 1))

    n_chunks = rows // CHUNK
    states = [[state_ref[gg, 0], state_ref[gg, 1]] for gg in range(gs)]
    for c in range(n_chunks):
        cb = n_chunks - 1 - c
        sl_f = slice(c * CHUNK, (c + 1) * CHUNK)
        sl_b = slice(cb * CHUNK, (cb + 1) * CHUNK)
        last_f = slice(c * CHUNK + CHUNK - 1, (c + 1) * CHUNK)
        last_b = slice(cb * CHUNK, cb * CHUNK + 1)
        items = []
        for gg in range(gs):
            xl = slice(gg * gw, (gg + 1) * gw)
            nl = slice(gg * SSM_STATE, (gg + 1) * SSM_STATE)
            items.append((xf[sl_f, xl], xdt_f[sl_f, xl], xdec_f[sl_f, xl], bmf[sl_f, nl], cmf[sl_f, nl], gcx_f[sl_f, xl],
                          growf_ref[c:c + 1, xl], egx_f[sl_f, xl], egx_f[last_f, xl], states[gg][0], True, d_all[:, xl]))
            items.append((xb[sl_b, xl], xdt_b[sl_b, xl], xdec_b[sl_b, xl], bmb[sl_b, nl], cmb[sl_b, nl], gcx_b[sl_b, xl],
                          growb_ref[cb:cb + 1, xl], egx_b[sl_b, xl], egx_b[last_b, xl], states[gg][1], False, None))
        results = _ssd_chunks(items, incl, block_diag)
        for gg in range(gs):
            xl = slice(gg * gw, (gg + 1) * gw)
            yf_ref[sl_f, xl] = results[2 * gg][0].astype(yf_ref.dtype)
            yb_ref[sl_b, xl] = results[2 * gg + 1][0].astype(yb_ref.dtype)
            states[gg] = [results[2 * gg][1], results[2 * gg + 1][1]]
    for gg in range(gs):
        state_ref[gg, 0] = states[gg][0]
        state_ref[gg, 1] = states[gg][1]

    @pl.when(step == n_seg - 1)
    def _():
        sout_ref[...] = state_ref[...]


def _ssd_scan(xbc, layouts, expand, d_exp, s0):
    bsz, n_seg, rows, _ = xbc.shape
    gs = SSD_GROUPS_PER_STEP
    xw, nw = SSD_STEP_WIDTH, gs * SSM_STATE
    boff, coff = SSM_INNER // nw, (SSM_INNER + SSM_GN) // nw
    (cols_f, grow_f), (cols_b, grow_b) = layouts
    n_cols = cols_f.shape[-1]
    seg_f = lambda s: s
    seg_b = lambda s: n_seg - 1 - s
    xspec = lambda seg: pl.BlockSpec((None, None, rows, xw), lambda b, g, s: (b, seg(s), 0, g))
    nspec = lambda seg, off: pl.BlockSpec((None, None, rows, nw), lambda b, g, s: (b, seg(s), 0, off + g))
    cspec = lambda seg: pl.BlockSpec((None, None, None, rows, n_cols), lambda b, g, s: (b, seg(s), g, 0, 0))
    rspec = lambda seg: pl.BlockSpec((None, None, None, rows // CHUNK, xw), lambda b, g, s: (b, seg(s), g, 0, 0))
    state_spec = pl.BlockSpec((None, gs, 2, SSM_STATE, SSM_HPG * SSM_HEAD_DIM), lambda b, g, s: (b, g, 0, 0, 0))
    y_shape = jax.ShapeDtypeStruct((bsz, n_seg, rows, SSM_INNER), SCAN_OUT_DTYPE)
    return pl.pallas_call(
        functools.partial(_ssd_kernel, rows=rows, n_seg=n_seg, gs=gs),
        out_shape=(y_shape, y_shape,
                   jax.ShapeDtypeStruct((bsz, SSM_GROUPS, 2, SSM_STATE, SSM_HPG * SSM_HEAD_DIM), F32)),
        grid=(bsz, SSM_GROUPS // gs, n_seg),
        in_specs=[xspec(seg_f), nspec(seg_f, boff), nspec(seg_f, coff),
                  xspec(seg_b), nspec(seg_b, boff), nspec(seg_b, coff),
                  cspec(seg_f), cspec(seg_b), rspec(seg_f), rspec(seg_b),
                  pl.BlockSpec(expand.shape, lambda b, g, s: (0, 0)),
                  pl.BlockSpec((1, xw), lambda b, g, s: (0, g)), state_spec],
        out_specs=(xspec(seg_f), xspec(seg_b), state_spec),
        scratch_shapes=[pltpu.VMEM((gs, 2, SSM_STATE, SSM_HPG * SSM_HEAD_DIM), F32)],
        compiler_params=_params(3),
        name="ssd_scan",
    )(xbc, xbc, xbc, xbc, xbc, xbc, cols_f, cols_b, grow_f, grow_b, expand, d_exp, s0)


def _ssd_branch(xbc, small, xbc_c, small_c, a_log, dt_bias, d_skip):
    bsz, n_seg, rows, _ = xbc.shape
    assert rows % CHUNK == 0 and xbc_c.shape[2] % CHUNK == 0
    d_exp = jnp.repeat(d_skip.astype(F32), SSM_HEAD_DIM).reshape(1, SSM_INNER)
    s0 = jnp.zeros((bsz, SSM_GROUPS, 2, SSM_STATE, SSM_HPG * SSM_HEAD_DIM), F32)
    expand = _ssd_expand_matrix()
    yc_f, yc_b, s_ctx = _ssd_scan(xbc_c, _ssd_layouts(_ssd_prep(small_c, a_log, dt_bias, 1)), expand, d_exp, s0)
    q = _ssd_prep(small.reshape(bsz, rows, n_seg * SMALL_N), a_log, dt_bias, n_seg)
    y_f, y_b, _ = _ssd_scan(xbc, _ssd_layouts(q), expand, d_exp, s_ctx)
    to_raster = lambda y: jnp.swapaxes(y, 1, 2).reshape(bsz, rows * n_seg, SSM_INNER)
    return (to_raster(y_f), to_raster(y_b)), (yc_f[:, 0], yc_b[:, 0])


def _layer_norm(r, g, b):
    mu = jnp.mean(r, axis=1, keepdims=True)
    var = jnp.mean(jnp.square(r - mu), axis=1, keepdims=True)
    return (r - mu) * lax.rsqrt(var + LN_EPS) * g + b


def _merge_kernel(x_ref, sc_ref, sh_ref, wg_ref, of_ref, ob_ref, yf_ref, yb_ref,
                  nwa_ref, nwb_ref, wpg_ref, wps_ref, wout_ref, g1_ref, lng_ref, lnb_ref, o_ref, *, alpha):
    x = x_ref[...]
    h = (x * (1.0 + sc_ref[...]) + sh_ref[...]).astype(BF16)
    gate = lambda off, width: jnp.dot(h, wg_ref[:, off:off + width], preferred_element_type=F32)
    o = of_ref[...].astype(F32) + ob_ref[...].astype(F32)
    normed = []
    for hh in range(GDN_HEADS):
        oh = o[:, hh * GDN_DV:(hh + 1) * GDN_DV]
        normed.append(oh * lax.rsqrt(jnp.mean(oh * oh, axis=1, keepdims=True) + NORM_EPS))
    y_a = jnp.concatenate(normed, axis=1) * nwa_ref[...] * _silu(gate(GOUT_OFF, GDN_V))
    t = (yf_ref[...].astype(F32) + yb_ref[...].astype(F32)) * _silu(gate(Z_OFF, SSM_INNER))
    gw = SSM_INNER // SSM_GROUPS
    normed = []
    for g in range(SSM_GROUPS):
        tg = t[:, g * gw:(g + 1) * gw]
        normed.append(tg * lax.rsqrt(jnp.mean(tg * tg, axis=1, keepdims=True) + NORM_EPS))
    y_b = jnp.concatenate(normed, axis=1) * nwb_ref[...]
    d = x.shape[1]
    mix = (jax.nn.sigmoid(gate(GA_OFF, d)) * jnp.dot(y_a.astype(BF16), wpg_ref[...], preferred_element_type=F32)
           + jax.nn.sigmoid(gate(GB_OFF, d)) * jnp.dot(y_b.astype(BF16), wps_ref[...], preferred_element_type=F32))
    out = jnp.dot(mix.astype(BF16), wout_ref[...], preferred_element_type=F32)
    o_ref[...] = _layer_norm(alpha * x + g1_ref[...] * out, lng_ref[...], lnb_ref[...])


def _merge(x, sc, sh, w_gates, o_pair, y_pair, nwa, nwb, wpg, wps, wout, g1, ln_g, ln_b, alpha, tm):
    bsz, length, d = x.shape
    row = lambda width: pl.BlockSpec((None, tm, width), lambda b, i: (b, i, 0))
    const = lambda shape: pl.BlockSpec(shape, lambda b, i: (0,) * len(shape), pipeline_mode=pl.Buffered(1))
    mod = pl.BlockSpec((None, 1, d), lambda b, i: (b, 0, 0))
    return pl.pallas_call(
        functools.partial(_merge_kernel, alpha=alpha),
        out_shape=jax.ShapeDtypeStruct((bsz, length, d), F32),
        grid=(bsz, length // tm),
        in_specs=[row(d), mod, mod, const((d, BIG_N)),
                  row(GDN_V), row(GDN_V), row(SSM_INNER), row(SSM_INNER),
                  const((1, GDN_V)), const((1, SSM_INNER)), const((GDN_V, d)), const((SSM_INNER, d)), const((d, d)),
                  mod, const((1, d)), const((1, d))],
        out_specs=row(d),
        compiler_params=_params(2),
        name="merge",
    )(x, sc, sh, w_gates, o_pair[0], o_pair[1], y_pair[0], y_pair[1], nwa, nwb, wpg, wps, wout, g1, ln_g, ln_b)


def _mlp_kernel(x_ref, sc_ref, sh_ref, g2_ref, w1_ref, b1_ref, w2_ref, b2_ref, lng_ref, lnb_ref, o_ref, *, alpha):
    x = x_ref[...]
    h = (x * (1.0 + sc_ref[...]) + sh_ref[...]).astype(BF16)
    acc = None
    tf = D_MODEL
    for c in range(D_FF // tf):
        u = jnp.dot(h, w1_ref[:, c * tf:(c + 1) * tf], preferred_element_type=F32) + b1_ref[:, c * tf:(c + 1) * tf]
        u = jnp.square(jnp.maximum(u, 0.0))
        part = jnp.dot(u.astype(BF16), w2_ref[c * tf:(c + 1) * tf, :], preferred_element_type=F32)
        acc = part if acc is None else acc + part
    f = acc + b2_ref[...]
    o_ref[...] = _layer_norm(alpha * x + g2_ref[...] * f, lng_ref[...], lnb_ref[...])


def _mlp(x, sc, sh, g2, w1, b1, w2, b2, ln_g, ln_b, alpha, tm):
    bsz, length, d = x.shape
    const = lambda shape: pl.BlockSpec(shape, lambda b, i: (0,) * len(shape), pipeline_mode=pl.Buffered(1))
    mod = pl.BlockSpec((None, 1, d), lambda b, i: (b, 0, 0))
    return pl.pallas_call(
        functools.partial(_mlp_kernel, alpha=alpha),
        out_shape=jax.ShapeDtypeStruct((bsz, length, d), F32),
        grid=(bsz, length // tm),
        in_specs=[pl.BlockSpec((None, tm, d), lambda b, i: (b, i, 0)), mod, mod, mod,
                  const((d, D_FF)), const((1, D_FF)), const((D_FF, d)), const((1, d)), const((1, d)), const((1, d))],
        out_specs=pl.BlockSpec((None, tm, d), lambda b, i: (b, i, 0)),
        compiler_params=_params(2),
        name="mlp",
    )(x, sc, sh, g2, w1, b1, w2, b2, ln_g, ln_b)


def _split_w_in(w_in):
    pts, acc = [], 0
    for s in IN_SPLITS[:-1]:
        acc += s
        pts.append(acc)
    qkv, gout, a_raw, b_raw, z, xbc, dt_raw, gate_a, gate_b = jnp.split(w_in, pts, axis=1)
    big = jnp.concatenate([z, gout, gate_a, gate_b], axis=1).astype(BF16)
    pad = jnp.zeros((w_in.shape[0], SMALL_N - SM_DT - 2 * SSM_HEADS), w_in.dtype)
    small = jnp.concatenate([a_raw, b_raw, dt_raw, pad], axis=1).astype(BF16)
    return big, qkv.astype(BF16), xbc.astype(BF16), small


def kernel(x, c, ctx, c_ctx, w_mod, b_mod, w_in, gdn_conv_w, gdn_A_log, gdn_dt_bias, gdn_norm_w,
           ssm_conv_w, ssm_conv_b, ssm_A_log, ssm_dt_bias, ssm_D, ssm_norm_w,
           w_proj_gdn, w_proj_ssm, w_out, ln1_g, ln1_b, w_ff1, b_ff1, w_ff2, b_ff2, ln2_g, ln2_b):
    bsz, length, d = x.shape
    ctx_len = ctx.shape[1]
    depth = w_mod.shape[0]
    alpha = float((2 * depth) ** 0.25)
    mod_rows = -(-(bsz + 1) // (2 * SUBLANES)) * (2 * SUBLANES)
    cc = jnp.concatenate([c, c_ctx[None, :], jnp.zeros((mod_rows - bsz - 1, d), c.dtype)], axis=0)
    row2 = lambda t: t.reshape(1, -1)
    tm_lat = 1024
    tm_ctx = ctx_len

    for l in range(depth):
        last = l == depth - 1
        mod = _modulation(cc, w_mod[l], b_mod[l])
        lat = [mod[:bsz, i * d:(i + 1) * d].reshape(bsz, 1, d) for i in range(6)]
        cxm = [jnp.broadcast_to(mod[bsz, i * d:(i + 1) * d].reshape(1, 1, d), (bsz, 1, d)) for i in range(6)]
        w_big, w_qkv, w_xbc, w_small = _split_w_in(w_in[l])
        rows = length // GRID_W

        small = _inproj(x, lat[1], lat[0], w_small, tm_lat, SMALL_N, "inproj_narrow")
        qkv = _proj_conv(x, lat[1], lat[0], w_qkv, gdn_conv_w[l], None, GRID_W, False, 2, tm_lat, GDN_QK, "inproj_qkv")
        xbc = _proj_conv(x, lat[1], lat[0], w_xbc, ssm_conv_w[l], ssm_conv_b[l], rows, True, 0, None, 1024,
                         "inproj_xbc")
        small_c = _inproj(ctx, cxm[1], cxm[0], w_small, tm_ctx, SMALL_N, "inproj_narrow_ctx")
        qkv_c = _proj_conv(ctx, cxm[1], cxm[0], w_qkv, gdn_conv_w[l], None, ctx_len, False, 2, tm_ctx, GDN_QK,
                           "inproj_qkv_ctx")
        xbc_c = _proj_conv(ctx, cxm[1], cxm[0], w_xbc, ssm_conv_w[l], ssm_conv_b[l], ctx_len, False, 0, tm_ctx, 1024,
                           "inproj_xbc_ctx")[:, None]

        o_pair, oc_pair = _gdn_branch(qkv, small, qkv_c, small_c, gdn_A_log[l], gdn_dt_bias[l])
        y_pair, yc_pair = _ssd_branch(xbc, small, xbc_c, small_c, ssm_A_log[l], ssm_dt_bias[l], ssm_D[l])

        nwa = jnp.tile(gdn_norm_w[l], GDN_HEADS).reshape(1, GDN_V)
        nwb = row2(ssm_norm_w[l])
        wpg, wps, wo = w_proj_gdn[l].astype(BF16), w_proj_ssm[l].astype(BF16), w_out[l].astype(BF16)
        w1, w2 = w_ff1[l].astype(BF16), w_ff2[l].astype(BF16)
        merge_args = (nwa, nwb, wpg, wps, wo)
        ln1 = (row2(ln1_g[l]), row2(ln1_b[l]))
        mlp_w = (w1, row2(b_ff1[l]), w2, row2(b_ff2[l]), row2(ln2_g[l]), row2(ln2_b[l]))

        x1 = _merge(x, lat[1], lat[0], w_big, o_pair, y_pair, *merge_args, lat[2], *ln1, alpha, MERGE_ROWS)
        x = _mlp(x1, lat[4], lat[3], lat[5], *mlp_w, alpha, MLP_ROWS)
        if not last:
            c1 = _merge(ctx, cxm[1], cxm[0], w_big, oc_pair, yc_pair, *merge_args, cxm[2], *ln1, alpha, ctx_len)
            ctx = _mlp(c1, cxm[4], cxm[3], cxm[5], *mlp_w, alpha, ctx_len)
    return x
```

```python
import functools

import jax
import jax.numpy as jnp
from jax import lax
from jax.experimental import pallas as pl
from jax.experimental.pallas import tpu as pltpu

F32 = jnp.float32
BF16 = jnp.bfloat16

D_MODEL = 1024
GRID_W = 64
GDN_HEADS = 8
GDN_DK = 128
GDN_DV = 128
GDN_QK = GDN_HEADS * GDN_DK
GDN_V = GDN_HEADS * GDN_DV
GDN_QKV = 2 * GDN_QK + GDN_V
SSM_INNER = 2 * D_MODEL
SSM_HEAD_DIM = 64
SSM_HEADS = SSM_INNER // SSM_HEAD_DIM
SSM_GROUPS = 8
SSM_HPG = SSM_HEADS // SSM_GROUPS
SSM_STATE = 128
SSM_GN = SSM_GROUPS * SSM_STATE
SSM_XBC = SSM_INNER + 2 * SSM_GN
CONV_K = 5
CHUNK = 64
D_FF = 4 * D_MODEL
LN_EPS = 1e-5
NORM_EPS = 1e-6
IN_SPLITS = (GDN_QKV, GDN_V, 2 * GDN_HEADS, 2 * GDN_HEADS, SSM_INNER, SSM_XBC, 2 * SSM_HEADS, D_MODEL, D_MODEL)

LANES = 128
SUBLANES = 8
VMEM_LIMIT_BYTES = 56 * 1024 * 1024

Z_OFF = 0
GOUT_OFF = Z_OFF + SSM_INNER
GA_OFF = GOUT_OFF + GDN_V
GB_OFF = GA_OFF + D_MODEL
BIG_N = GB_OFF + D_MODEL
SM_A = 0
SM_B = SM_A + 2 * GDN_HEADS
SM_DT = SM_B + 2 * GDN_HEADS
SM_DT_B = SM_DT + SSM_HEADS
SMALL_N = LANES

GDN_CHUNKS_PER_STEP = 2
GDN_PREP_ROWS = 512
SCAN_OUT_DTYPE = BF16
MERGE_ROWS = 512
MLP_ROWS = 1024
SSD_GROUPS_PER_STEP = 4


def _params(n_axes):
    return pltpu.CompilerParams(dimension_semantics=("arbitrary",) * n_axes, vmem_limit_bytes=VMEM_LIMIT_BYTES)


def _silu(t):
    return t * jax.nn.sigmoid(t)


def _softplus(t):
    return jnp.maximum(t, 0.0) + jnp.log(1.0 + jnp.exp(-jnp.abs(t)))


def _mm(a, b):
    return jnp.dot(a.astype(BF16), b.astype(BF16), preferred_element_type=F32)


def _mm_nt(a, b):
    return lax.dot_general(a.astype(BF16), b.astype(BF16), (((1,), (1,)), ((), ())), preferred_element_type=F32)


def _mm_tn(a, b):
    return lax.dot_general(a.astype(BF16), b.astype(BF16), (((0,), (0,)), ((), ())), preferred_element_type=F32)


def _tri_masks(lower):
    ii = lax.broadcasted_iota(jnp.int32, (CHUNK, CHUNK), 0)
    jj = lax.broadcasted_iota(jnp.int32, (CHUNK, CHUNK), 1)
    if lower:
        return ii >= jj, ii > jj
    return ii <= jj, ii < jj


def _conv_seg(x, w, seg):
    n = x.shape[0]
    assert seg & (seg - 1) == 0 and n % seg == 0
    pos = lax.broadcasted_iota(jnp.int32, (n, 1), 0) & (seg - 1)
    out = None
    for j in range(CONV_K):
        d = j - CONV_K // 2
        if d == 0:
            term = x * w[j:j + 1, :]
        else:
            shifted = pltpu.roll(x, shift=(-d) % n, axis=0)
            valid = (pos + d >= 0) & (pos + d < seg)
            term = jnp.where(valid, shifted, 0.0) * w[j:j + 1, :]
        out = term if out is None else out + term
    return out


def _mod_kernel(c_ref, w_ref, b_ref, o_ref):
    o_ref[...] = _mm(_silu(c_ref[...]), w_ref[...]) + b_ref[...]


def _modulation(cc, w, b):
    rows, d = cc.shape
    n = w.shape[1]
    tn = 1536
    return pl.pallas_call(
        _mod_kernel,
        out_shape=jax.ShapeDtypeStruct((rows, n), F32),
        grid=(n // tn,),
        in_specs=[pl.BlockSpec((rows, d), lambda j: (0, 0)),
                  pl.BlockSpec((d, tn), lambda j: (0, j)),
                  pl.BlockSpec((1, tn), lambda j: (0, j))],
        out_specs=pl.BlockSpec((rows, tn), lambda j: (0, j)),
        compiler_params=_params(1),
        name="modulation",
    )(cc, w, b.reshape(1, n))


def _inproj_kernel(x_ref, sc_ref, sh_ref, w_ref, o_ref):
    h = x_ref[...] * (1.0 + sc_ref[...]) + sh_ref[...]
    o_ref[...] = jnp.dot(h.astype(BF16), w_ref[...], preferred_element_type=F32)


def _inproj(x, sc, sh, w, tm, tn, name):
    bsz, length, d = x.shape
    n = w.shape[1]
    return pl.pallas_call(
        _inproj_kernel,
        out_shape=jax.ShapeDtypeStruct((bsz, length, n), F32),
        grid=(bsz, length // tm, n // tn),
        in_specs=[pl.BlockSpec((None, tm, d), lambda b, i, j: (b, i, 0)),
                  pl.BlockSpec((None, 1, d), lambda b, i, j: (b, 0, 0)),
                  pl.BlockSpec((None, 1, d), lambda b, i, j: (b, 0, 0)),
                  pl.BlockSpec((d, tn), lambda b, i, j: (0, j))],
        out_specs=pl.BlockSpec((None, tm, tn), lambda b, i, j: (b, i, j)),
        compiler_params=_params(3),
        name=name,
    )(x, sc, sh, w)


def _proj_conv_kernel(*refs, seg, cols, has_bias, norm_tiles):
    x_ref, sc_ref, sh_ref, w_ref, cw_ref = refs[:5]
    k = 5
    cb_ref = refs[k] if has_bias else None
    k += has_bias
    perm_ref = refs[k] if cols else None
    o_ref, h_ref = refs[k + cols:]
    j = pl.program_id(2)
    n_tok = h_ref.shape[0]

    @pl.when(j == 0)
    def _():
        x = x_ref[...]
        if cols:
            x = x.reshape(n_tok, x.shape[2])
        h = (x * (1.0 + sc_ref[...]) + sh_ref[...]).astype(BF16)
        if cols:
            h = jnp.dot(perm_ref[...], h, preferred_element_type=F32).astype(BF16)
        h_ref[...] = h

    y = jnp.dot(h_ref[...], w_ref[...], preferred_element_type=F32)
    y = _conv_seg(y, cw_ref[...], seg)
    if has_bias:
        y = y + cb_ref[...]
    y = _silu(y)

    def store(t):
        o_ref[...] = t.reshape(o_ref.shape)

    if norm_tiles:
        @pl.when(j < norm_tiles)
        def _():
            scale = jnp.where(j == 0, GDN_DK ** -0.5, 1.0)
            heads = []
            for hh in range(y.shape[1] // GDN_DK):
                yh = y[:, hh * GDN_DK:(hh + 1) * GDN_DK]
                heads.append(yh * (lax.rsqrt(jnp.sum(yh * yh, axis=1, keepdims=True) + NORM_EPS) * scale))
            store(jnp.concatenate(heads, axis=1))

        @pl.when(j >= norm_tiles)
        def _():
            store(y)
    else:
        store(y)


def _proj_conv(x, sc, sh, w, conv_w, conv_b, seg, cols, norm_tiles, tm, tn, name):
    bsz, length, d = x.shape
    n = w.shape[1]
    has_bias = conv_b is not None
    if cols:
        rows, cps = length // GRID_W, SUBLANES
        assert seg == rows and seg & (seg - 1) == 0
        n_tok = rows * cps
        x_in = x.reshape(bsz, rows, GRID_W, d)
        x_spec = pl.BlockSpec((None, rows, cps, d), lambda b, i, j: (b, 0, i, 0))
        out_shape = jax.ShapeDtypeStruct((bsz, GRID_W, rows, n), F32)
        out_spec = pl.BlockSpec((None, cps, rows, tn), lambda b, i, j: (b, i, 0, j))
        grid = (bsz, GRID_W // cps, n // tn)
    else:
        n_tok = tm
        x_in = x
        x_spec = pl.BlockSpec((None, tm, d), lambda b, i, j: (b, i, 0))
        out_shape = jax.ShapeDtypeStruct((bsz, length, n), F32)
        out_spec = pl.BlockSpec((None, tm, tn), lambda b, i, j: (b, i, j))
        grid = (bsz, length // tm, n // tn)
    mod = pl.BlockSpec((None, 1, d), lambda b, i, j: (b, 0, 0))
    in_specs = [x_spec, mod, mod, pl.BlockSpec((d, tn), lambda b, i, j: (0, j)),
                pl.BlockSpec((CONV_K, tn), lambda b, i, j: (0, j))]
    args = [x_in, sc, sh, w, conv_w]
    if has_bias:
        in_specs.append(pl.BlockSpec((1, tn), lambda b, i, j: (0, j)))
        args.append(conv_b.reshape(1, n))
    if cols:
        dst = jnp.arange(n_tok)
        src = (dst % rows) * cps + dst // rows
        in_specs.append(pl.BlockSpec((n_tok, n_tok), lambda b, i, j: (0, 0), pipeline_mode=pl.Buffered(1)))
        args.append((src[:, None] == jnp.arange(n_tok)[None, :]).astype(BF16))
    return pl.pallas_call(
        functools.partial(_proj_conv_kernel, seg=seg, cols=cols, has_bias=has_bias, norm_tiles=norm_tiles),
        out_shape=out_shape,
        grid=grid,
        in_specs=in_specs,
        out_specs=out_spec,
        scratch_shapes=[pltpu.VMEM((n_tok, d), BF16)],
        compiler_params=_params(3),
        name=name,
    )(*args)


def _chunk_cumsums(t):
    ii = lax.broadcasted_iota(jnp.int32, (2 * CHUNK, CHUNK), 0)
    jj = lax.broadcasted_iota(jnp.int32, (2 * CHUNK, CHUNK), 1)
    ones = ((ii < CHUNK) & (ii >= jj)) | ((ii >= CHUNK) & (ii - CHUNK <= jj))
    tri = jnp.where(ones, 1.0, 0.0).astype(BF16)
    hi = t.astype(BF16)
    r1 = t - hi.astype(F32)
    mid = r1.astype(BF16)
    lo = (r1 - mid.astype(F32)).astype(BF16)
    w = t.shape[1]
    sums = jnp.dot(tri, jnp.concatenate([hi, mid, lo], axis=1), preferred_element_type=F32)
    sums = sums[:, :w] + sums[:, w:2 * w] + sums[:, 2 * w:]
    return sums[:CHUNK], sums[CHUNK:]


def _gdn_prep_kernel(s_ref, alog_ref, dtb_ref, o_ref, *, rows):
    s = s_ref[...]
    lane = lax.broadcasted_iota(jnp.int32, (1, LANES), 1)
    g = -jnp.exp(alog_ref[...]) * _softplus(s + dtb_ref[...])
    beta = jax.nn.sigmoid(s)
    for c in range(rows // CHUNK):
        sl = slice(c * CHUNK, (c + 1) * CHUNK)
        fwd, bwd = _chunk_cumsums(g[sl, :])
        gc = jnp.where(lane < SM_A + GDN_HEADS, fwd, bwd)
        o_ref[sl, :] = jnp.where(lane < SM_B, gc, beta[sl, :])


def _gdn_prep(small, a_log, dt_bias, tb):
    bsz, length, _ = small.shape
    pad = lambda t: jnp.pad(t.reshape(1, -1).astype(F32), ((0, 0), (SM_A, LANES - SM_A - 2 * GDN_HEADS)))
    return pl.pallas_call(
        functools.partial(_gdn_prep_kernel, rows=tb),
        out_shape=jax.ShapeDtypeStruct((bsz, length, LANES), F32),
        grid=(bsz, length // tb),
        in_specs=[pl.BlockSpec((None, tb, LANES), lambda b, i: (b, i, 0)),
                  pl.BlockSpec((1, LANES), lambda b, i: (0, 0)),
                  pl.BlockSpec((1, LANES), lambda b, i: (0, 0))],
        out_specs=pl.BlockSpec((None, tb, LANES), lambda b, i: (b, i, 0)),
        compiler_params=_params(2),
        name="gdn_prep",
    )(small, pad(a_log), pad(dt_bias))


def _tri_inverse(mats):
    ii = lax.broadcasted_iota(jnp.int32, (CHUNK, CHUNK), 0)
    jj = lax.broadcasted_iota(jnp.int32, (CHUNK, CHUNK), 1)
    eye = jnp.where(ii == jj, 1.0, 0.0)
    zero = jnp.zeros((CHUNK, CHUNK), F32)
    right = lax.broadcasted_iota(jnp.int32, (CHUNK, 2 * CHUNK), 1) >= CHUNK
    zs = [jnp.concatenate([zero, eye], axis=1) + _mm(a, jnp.concatenate([a, -eye], axis=1)) for a in mats]
    power = 2
    while power < CHUNK:
        ps = [_mm(z[:, :CHUNK], z) for z in zs]
        zs = [p + jnp.where(right, z, 0.0) for p, z in zip(ps, zs)]
        power *= 2
    return zs


def _gdn_local(items, masks):
    qs, ks, vs, gcols, grows, bcols, lowers = zip(*items)
    n = len(items)
    incl = [masks[lo][0] for lo in lowers]
    strict = [masks[lo][1] for lo in lowers]
    glast = [grows[i][:, CHUNK - 1:CHUNK] if lowers[i] else grows[i][:, 0:1] for i in range(n)]
    k16 = [k.astype(BF16) for k in ks]
    kq = [_mm_nt(jnp.concatenate([k16[i], qs[i].astype(BF16)], axis=0), k16[i]) for i in range(n)]
    dec = [jnp.exp(jnp.where(incl[i], gcols[i] - grows[i], 0.0)) for i in range(n)]
    a_mats = [jnp.where(strict[i], bcols[i] * kq[i][:CHUNK] * dec[i], 0.0) for i in range(n)]
    t_inv = _tri_inverse(a_mats)
    eg = [jnp.exp(g) for g in gcols]
    rhs = [jnp.concatenate([(bcols[i] * eg[i]) * ks[i], bcols[i] * vs[i]], axis=1) for i in range(n)]
    pad = jnp.zeros((CHUNK, GDN_DK + GDN_DV), F32)
    wu = [_mm(t, jnp.concatenate([pad, r], axis=0)) for t, r in zip(t_inv, rhs)]
    kd_wu = [_mm_tn(ks[i] * jnp.exp(glast[i] - gcols[i]), wu[i]) for i in range(n)]
    qk_wu = [_mm(jnp.where(incl[i], kq[i][CHUNK:] * dec[i], 0.0), wu[i]) for i in range(n)]
    lhs = [jnp.concatenate([kd_wu[i][:, :GDN_DK].astype(BF16), (qs[i] * eg[i] - qk_wu[i][:, :GDN_DK]).astype(BF16)],
                           axis=0) for i in range(n)]
    return [(lhs[i], kd_wu[i][:, GDN_DK:], qk_wu[i][:, GDN_DK:], jnp.exp(glast[i])) for i in range(n)]


def _gdn_on_state(local, states):
    n = len(local)
    on_state = [_mm(local[i][0], states[i]) for i in range(n)]
    return [(on_state[i][GDN_DK:] + local[i][2], local[i][3] * states[i] - on_state[i][:GDN_DK] + local[i][1])
            for i in range(n)]


def _lane_column(t, idx, lane):
    if isinstance(idx, int):
        return t[:, idx:idx + 1]
    return jnp.sum(jnp.where(lane == idx, t, 0.0), axis=1, keepdims=True)


def _gdn_kernel(qf_ref, kf_ref, vf_ref, qb_ref, kb_ref, vb_ref, pf_ref, pb_ref, ptf_ref, ptb_ref,
                s0_ref, of_ref, ob_ref, sout_ref, state_ref, *, n_steps, hs, nc):
    step = pl.program_id(2)
    head0 = 0 if hs == GDN_HEADS else pl.program_id(1) * hs

    @pl.when(step == 0)
    def _():
        state_ref[...] = s0_ref[...]

    lane = lax.broadcasted_iota(jnp.int32, (1, LANES), 1)

    def heads_of(q_ref, k_ref, v_ref):
        q, k, v = q_ref[...], k_ref[...], v_ref[...]
        return [(q[:, hh * GDN_DK:(hh + 1) * GDN_DK], k[:, hh * GDN_DK:(hh + 1) * GDN_DK],
                 v[:, hh * GDN_DV:(hh + 1) * GDN_DV]) for hh in range(hs)]

    heads_f = heads_of(qf_ref, kf_ref, vf_ref)
    heads_b = heads_of(qb_ref, kb_ref, vb_ref)
    pf = pf_ref[...]
    pb = pb_ref[...]
    masks = {True: _tri_masks(True), False: _tri_masks(False)}
    cols = []
    for hh in range(hs):
        head = head0 + hh
        cols.append((_lane_column(pf, SM_A + head, lane), _lane_column(pf, SM_B + head, lane),
                     _lane_column(pb, SM_A + GDN_HEADS + head, lane), _lane_column(pb, SM_B + GDN_HEADS + head, lane)))
    items = []
    for c in range(nc):
        cb = nc - 1 - c
        sl_f = slice(c * CHUNK, (c + 1) * CHUNK)
        sl_b = slice(cb * CHUNK, (cb + 1) * CHUNK)
        for hh in range(hs):
            head = head0 + hh
            gcol_f, bcol_f, gcol_b, bcol_b = cols[hh]
            qf, kf, vf = heads_f[hh]
            qb, kb, vb = heads_b[hh]
            grow_f = ptf_ref[c, pl.ds(SM_A + head, 1), :]
            grow_b = ptb_ref[cb, pl.ds(SM_A + GDN_HEADS + head, 1), :]
            items.append((qf[sl_f], kf[sl_f], vf[sl_f], gcol_f[sl_f], grow_f, bcol_f[sl_f], True))
            items.append((qb[sl_b], kb[sl_b], vb[sl_b], gcol_b[sl_b], grow_b, bcol_b[sl_b], False))
    local = _gdn_local(items, masks)
    states = [state_ref[hh, d] for hh in range(hs) for d in range(2)]
    for c in range(nc):
        cb = nc - 1 - c
        results = _gdn_on_state(local[c * 2 * hs:(c + 1) * 2 * hs], states)
        states = [r[1] for r in results]
        for hh in range(hs):
            hl = slice(hh * GDN_DV, (hh + 1) * GDN_DV)
            of_ref[c * CHUNK:(c + 1) * CHUNK, hl] = results[2 * hh][0].astype(of_ref.dtype)
            ob_ref[cb * CHUNK:(cb + 1) * CHUNK, hl] = results[2 * hh + 1][0].astype(ob_ref.dtype)
    for hh in range(hs):
        state_ref[hh, 0] = states[2 * hh]
        state_ref[hh, 1] = states[2 * hh + 1]

    @pl.when(step == n_steps - 1)
    def _():
        sout_ref[...] = state_ref[...]


def _gdn_scan(qkv, p, pt, s0, hs, nc):
    bsz, length, _ = qkv.shape
    tb = nc * CHUNK
    n_steps = length // tb
    width = hs * GDN_DK
    qoff, koff, voff = 0, GDN_QK // width, 2 * GDN_QK // width
    fwd = lambda off: pl.BlockSpec((None, tb, width), lambda b, h, s: (b, s, off + h))
    bwd = lambda off: pl.BlockSpec((None, tb, width), lambda b, h, s: (b, n_steps - 1 - s, off + h))
    state_spec = pl.BlockSpec((None, hs, 2, GDN_DK, GDN_DV), lambda b, h, s: (b, h, 0, 0, 0))
    return pl.pallas_call(
        functools.partial(_gdn_kernel, n_steps=n_steps, hs=hs, nc=nc),
        out_shape=(jax.ShapeDtypeStruct((bsz, length, GDN_V), SCAN_OUT_DTYPE),
                   jax.ShapeDtypeStruct((bsz, length, GDN_V), SCAN_OUT_DTYPE),
                   jax.ShapeDtypeStruct((bsz, GDN_HEADS, 2, GDN_DK, GDN_DV), F32)),
        grid=(bsz, GDN_HEADS // hs, n_steps),
        in_specs=[fwd(qoff), fwd(koff), fwd(voff), bwd(qoff), bwd(koff), bwd(voff),
                  pl.BlockSpec((None, tb, LANES), lambda b, h, s: (b, s, 0)),
                  pl.BlockSpec((None, tb, LANES), lambda b, h, s: (b, n_steps - 1 - s, 0)),
                  pl.BlockSpec((None, nc, 4 * SUBLANES, CHUNK), lambda b, h, s: (b, s, 0, 0)),
                  pl.BlockSpec((None, nc, 4 * SUBLANES, CHUNK), lambda b, h, s: (b, n_steps - 1 - s, 0, 0)),
                  state_spec],
        out_specs=(pl.BlockSpec((None, tb, width), lambda b, h, s: (b, s, h)),
                   pl.BlockSpec((None, tb, width), lambda b, h, s: (b, n_steps - 1 - s, h)),
                   state_spec),
        scratch_shapes=[pltpu.VMEM((hs, 2, GDN_DK, GDN_DV), F32)],
        compiler_params=_params(3),
        name="gdn_scan",
    )(qkv, qkv, qkv, qkv, qkv, qkv, p, p, pt, pt, s0)


def _gdn_branch(qkv, small, qkv_c, small_c, a_log, dt_bias):
    bsz = qkv.shape[0]
    s0 = jnp.zeros((bsz, GDN_HEADS, 2, GDN_DK, GDN_DV), F32)

    def rows_t(p):
        chunks = p[:, :, :4 * SUBLANES].reshape(bsz, p.shape[1] // CHUNK, CHUNK, 4 * SUBLANES)
        return jnp.swapaxes(chunks, 2, 3)

    ctx_len = qkv_c.shape[1]
    p_c = _gdn_prep(small_c, a_log, dt_bias, ctx_len)
    oc_f, oc_b, s_ctx = _gdn_scan(qkv_c, p_c, rows_t(p_c), s0, GDN_HEADS, GDN_CHUNKS_PER_STEP)
    p = _gdn_prep(small, a_log, dt_bias, GDN_PREP_ROWS)
    o_f, o_b, _ = _gdn_scan(qkv, p, rows_t(p), s_ctx, GDN_HEADS, GDN_CHUNKS_PER_STEP)
    return (o_f, o_b), (oc_f, oc_b)


SSD_QUANTITIES = 4
SSD_HEADS_PER_STEP = SSD_GROUPS_PER_STEP * SSM_HPG
SSD_STEP_WIDTH = SSD_HEADS_PER_STEP * SSM_HEAD_DIM
BF16_PIECES = 3


def _ssd_prep_kernel(s_ref, alog_ref, dtb_ref, o_ref, *, rows):
    lane = lax.broadcasted_iota(jnp.int32, (1, LANES), 1)
    fwd_lane = lane < SM_DT_B
    neg_a = -jnp.exp(alog_ref[...])
    for seg in range(o_ref.shape[0]):
        dt = _softplus(s_ref[:, seg * LANES:(seg + 1) * LANES] + dtb_ref[...])
        la = dt * neg_a
        o_ref[seg, :, 0:LANES] = dt
        for c in range(rows // CHUNK):
            sl = slice(c * CHUNK, (c + 1) * CHUNK)
            fwd, bwd = _chunk_cumsums(la[sl, :])
            gc = jnp.where(fwd_lane, fwd, bwd)
            g_last = jnp.where(fwd_lane, gc[CHUNK - 1:CHUNK, :], gc[0:1, :])
            o_ref[seg, sl, LANES:2 * LANES] = gc
            o_ref[seg, sl, 2 * LANES:3 * LANES] = jnp.exp(gc)
            o_ref[seg, sl, 3 * LANES:4 * LANES] = dt[sl, :] * jnp.exp(g_last - gc)


def _ssd_prep(small_cols, a_log, dt_bias, n_seg):
    bsz, rows, _ = small_cols.shape
    sps = min(n_seg, SUBLANES)
    pad = lambda t: jnp.pad(t.reshape(1, -1).astype(F32), ((0, 0), (SM_DT, LANES - SM_DT - 2 * SSM_HEADS)))
    return pl.pallas_call(
        functools.partial(_ssd_prep_kernel, rows=rows),
        out_shape=jax.ShapeDtypeStruct((bsz, n_seg, rows, SSD_QUANTITIES * LANES), F32),
        grid=(bsz, n_seg // sps),
        in_specs=[pl.BlockSpec((None, rows, sps * LANES), lambda b, c: (b, 0, c)),
                  pl.BlockSpec((1, LANES), lambda b, c: (0, 0)),
                  pl.BlockSpec((1, LANES), lambda b, c: (0, 0))],
        out_specs=pl.BlockSpec((None, sps, rows, SSD_QUANTITIES * LANES), lambda b, c: (b, c, 0, 0)),
        compiler_params=_params(2),
        name="ssd_prep",
    )(small_cols, pad(a_log), pad(dt_bias))


def _ssd_layouts(q):
    bsz, n_seg, rows, _ = q.shape
    steps = SSM_GROUPS // SSD_GROUPS_PER_STEP
    q = q.reshape(bsz, n_seg, rows, SSD_QUANTITIES, LANES)[..., SM_DT:SM_DT + 2 * SSM_HEADS]
    q = q.reshape(bsz, n_seg, rows, SSD_QUANTITIES, 2, steps, SSD_HEADS_PER_STEP)
    out = []
    for d in range(2):
        qd = q[:, :, :, :, d]
        cols = jnp.transpose(qd, (0, 1, 4, 2, 3, 5)).reshape(bsz, n_seg, steps, rows, SSD_QUANTITIES * SSD_HEADS_PER_STEP)
        gc = qd[:, :, :, 1].reshape(bsz, n_seg, rows // CHUNK, CHUNK, steps, SSD_HEADS_PER_STEP)
        grow = jnp.transpose(gc, (0, 1, 4, 2, 5, 3)).reshape(bsz, n_seg, steps, rows // CHUNK, SSD_STEP_WIDTH)
        out.append((cols, grow))
    return out


def _ssd_expand_matrix():
    k = SSD_QUANTITIES * SSD_HEADS_PER_STEP
    src = jnp.arange(k)
    dst = jnp.arange(k * SSM_HEAD_DIM) // SSM_HEAD_DIM
    one = (src[:, None] == dst[None, :]).astype(BF16)
    return jnp.concatenate([one] * BF16_PIECES, axis=0)


def _ssd_chunks(items, incl, block_diag):
    n = len(items)
    cb = [_mm_nt(it[4], jnp.concatenate([it[3].astype(BF16)] * SSM_HPG, axis=0)) for it in items]
    inter = [_mm(it[4], it[9]) for it in items]
    lhs = []
    for i, it in enumerate(items):
        m = incl[it[10]]
        lhs.append(jnp.where(m, cb[i] * jnp.exp(jnp.where(m, it[5] - it[6], 0.0)), 0.0))
    rhs = [jnp.where(block_diag, jnp.concatenate([it[1].astype(BF16)] * SSM_HPG, axis=0), 0.0) for it in items]
    intra = [_mm(lhs[i], rhs[i]) for i in range(n)]
    upd = [_mm_tn(it[3], it[2]) for it in items]
    out = []
    for i, it in enumerate(items):
        y = intra[i] + inter[i] * it[7]
        if it[11] is not None:
            y = y + it[11] * it[0]
        out.append((y, it[8] * it[9] + upd[i]))
    return out


def _ssd_kernel(xf_ref, bf_ref, cf_ref, xb_ref, bb_ref, cb_ref, colf_ref, colb_ref, growf_ref, growb_ref,
                e_ref, d_ref, s0_ref, yf_ref, yb_ref, sout_ref, state_ref, *, rows, n_seg, gs):
    step = pl.program_id(2)

    @pl.when(step == 0)
    def _():
        state_ref[...] = s0_ref[...]

    gw = SSM_HPG * SSM_HEAD_DIM
    width = gs * gw

    def expand(col_ref):
        c = col_ref[...]
        hi = c.astype(BF16)
        r1 = c - hi.astype(F32)
        mid = r1.astype(BF16)
        lo = (r1 - mid.astype(F32)).astype(BF16)
        ex = jnp.dot(jnp.concatenate([hi, mid, lo], axis=1), e_ref[...], preferred_element_type=F32)
        return [ex[:, q * width:(q + 1) * width] for q in range(SSD_QUANTITIES)]

    xf, xb = xf_ref[...], xb_ref[...]
    dtx_f, gcx_f, egx_f, q4x_f = expand(colf_ref)
    dtx_b, gcx_b, egx_b, q4x_b = expand(colb_ref)
    xdt_f, xdec_f = xf * dtx_f, xf * q4x_f
    xdt_b, xdec_b = xb * dtx_b, xb * q4x_b
    bmf, cmf, bmb, cmb = bf_ref[...], cf_ref[...], bb_ref[...], cb_ref[...]
    d_all = d_ref[...]

    row = lax.broadcasted_iota(jnp.int32, (CHUNK, gw), 0)
    tok = lax.broadcasted_iota(jnp.int32, (CHUNK, gw), 1) & (CHUNK - 1)
    incl = {True: row >= tok, False: row <= tok}
    block_diag = (lax.shift_right_logical(lax.broadcasted_iota(jnp.int32, (gw, gw), 0), CHUNK.bit_length() - 1)
                  == lax.shift_right_logical(lax.broadcasted_iota(jnp.int32, (gw, gw), 1), SSM_HEAD_DIM.bit_length() - 1))

    n_chunks = rows // CHUNK
    states = [[state_ref[gg, 0], state_ref[gg, 1]] for gg in range(gs)]
    for c in range(n_chunks):
        cb = n_chunks - 1 - c
        sl_f = slice(c * CHUNK, (c + 1) * CHUNK)
        sl_b = slice(cb * CHUNK, (cb + 1) * CHUNK)
        last_f = slice(c * CHUNK + CHUNK - 1, (c + 1) * CHUNK)
        last_b = slice(cb * CHUNK, cb * CHUNK + 1)
        items = []
        for gg in range(gs):
            xl = slice(gg * gw, (gg + 1) * gw)
            nl = slice(gg * SSM_STATE, (gg + 1) * SSM_STATE)
            items.append((xf[sl_f, xl], xdt_f[sl_f, xl], xdec_f[sl_f, xl], bmf[sl_f, nl], cmf[sl_f, nl], gcx_f[sl_f, xl],
                          growf_ref[c:c + 1, xl], egx_f[sl_f, xl], egx_f[last_f, xl], states[gg][0], True, d_all[:, xl]))
            items.append((xb[sl_b, xl], xdt_b[sl_b, xl], xdec_b[sl_b, xl], bmb[sl_b, nl], cmb[sl_b, nl], gcx_b[sl_b, xl],
                          growb_ref[cb:cb + 1, xl], egx_b[sl_b, xl], egx_b[last_b, xl], states[gg][1], False, None))
        results = _ssd_chunks(items, incl, block_diag)
        for gg in range(gs):
            xl = slice(gg * gw, (gg + 1) * gw)
            yf_ref[sl_f, xl] = results[2 * gg][0].astype(yf_ref.dtype)
            yb_ref[sl_b, xl] = results[2 * gg + 1][0].astype(yb_ref.dtype)
            states[gg] = [results[2 * gg][1], results[2 * gg + 1][1]]
    for gg in range(gs):
        state_ref[gg, 0] = states[gg][0]
        state_ref[gg, 1] = states[gg][1]

    @pl.when(step == n_seg - 1)
    def _():
        sout_ref[...] = state_ref[...]


def _ssd_scan(xbc, layouts, expand, d_exp, s0):
    bsz, n_seg, rows, _ = xbc.shape
    gs = SSD_GROUPS_PER_STEP
    xw, nw = SSD_STEP_WIDTH, gs * SSM_STATE
    boff, coff = SSM_INNER // nw, (SSM_INNER + SSM_GN) // nw
    (cols_f, grow_f), (cols_b, grow_b) = layouts
    n_cols = cols_f.shape[-1]
    seg_f = lambda s: s
    seg_b = lambda s: n_seg - 1 - s
    xspec = lambda seg: pl.BlockSpec((None, None, rows, xw), lambda b, g, s: (b, seg(s), 0, g))
    nspec = lambda seg, off: pl.BlockSpec((None, None, rows, nw), lambda b, g, s: (b, seg(s), 0, off + g))
    cspec = lambda seg: pl.BlockSpec((None, None, None, rows, n_cols), lambda b, g, s: (b, seg(s), g, 0, 0))
    rspec = lambda seg: pl.BlockSpec((None, None, None, rows // CHUNK, xw), lambda b, g, s: (b, seg(s), g, 0, 0))
    state_spec = pl.BlockSpec((None, gs, 2, SSM_STATE, SSM_HPG * SSM_HEAD_DIM), lambda b, g, s: (b, g, 0, 0, 0))
    y_shape = jax.ShapeDtypeStruct((bsz, n_seg, rows, SSM_INNER), SCAN_OUT_DTYPE)
    return pl.pallas_call(
        functools.partial(_ssd_kernel, rows=rows, n_seg=n_seg, gs=gs),
        out_shape=(y_shape, y_shape,
                   jax.ShapeDtypeStruct((bsz, SSM_GROUPS, 2, SSM_STATE, SSM_HPG * SSM_HEAD_DIM), F32)),
        grid=(bsz, SSM_GROUPS // gs, n_seg),
        in_specs=[xspec(seg_f), nspec(seg_f, boff), nspec(seg_f, coff),
                  xspec(seg_b), nspec(seg_b, boff), nspec(seg_b, coff),
                  cspec(seg_f), cspec(seg_b), rspec(seg_f), rspec(seg_b),
                  pl.BlockSpec(expand.shape, lambda b, g, s: (0, 0)),
                  pl.BlockSpec((1, xw), lambda b, g, s: (0, g)), state_spec],
        out_specs=(xspec(seg_f), xspec(seg_b), state_spec),
        scratch_shapes=[pltpu.VMEM((gs, 2, SSM_STATE, SSM_HPG * SSM_HEAD_DIM), F32)],
        compiler_params=_params(3),
        name="ssd_scan",
    )(xbc, xbc, xbc, xbc, xbc, xbc, cols_f, cols_b, grow_f, grow_b, expand, d_exp, s0)


def _ssd_branch(xbc, small, xbc_c, small_c, a_log, dt_bias, d_skip):
    bsz, n_seg, rows, _ = xbc.shape
    assert rows % CHUNK == 0 and xbc_c.shape[2] % CHUNK == 0
    d_exp = jnp.repeat(d_skip.astype(F32), SSM_HEAD_DIM).reshape(1, SSM_INNER)
    s0 = jnp.zeros((bsz, SSM_GROUPS, 2, SSM_STATE, SSM_HPG * SSM_HEAD_DIM), F32)
    expand = _ssd_expand_matrix()
    yc_f, yc_b, s_ctx = _ssd_scan(xbc_c, _ssd_layouts(_ssd_prep(small_c, a_log, dt_bias, 1)), expand, d_exp, s0)
    q = _ssd_prep(small.reshape(bsz, rows, n_seg * SMALL_N), a_log, dt_bias, n_seg)
    y_f, y_b, _ = _ssd_scan(xbc, _ssd_layouts(q), expand, d_exp, s_ctx)
    to_raster = lambda y: jnp.swapaxes(y, 1, 2).reshape(bsz, rows * n_seg, SSM_INNER)
    return (to_raster(y_f), to_raster(y_b)), (yc_f[:, 0], yc_b[:, 0])


def _layer_norm(r, g, b):
    mu = jnp.mean(r, axis=1, keepdims=True)
    var = jnp.mean(jnp.square(r - mu), axis=1, keepdims=True)
    return (r - mu) * lax.rsqrt(var + LN_EPS) * g + b


def _merge_kernel(x_ref, sc_ref, sh_ref, wg_ref, of_ref, ob_ref, yf_ref, yb_ref,
                  nwa_ref, nwb_ref, wpg_ref, wps_ref, wout_ref, g1_ref, lng_ref, lnb_ref, o_ref, *, alpha):
    x = x_ref[...]
    h = (x * (1.0 + sc_ref[...]) + sh_ref[...]).astype(BF16)
    gate = lambda off, width: jnp.dot(h, wg_ref[:, off:off + width], preferred_element_type=F32)
    o = of_ref[...].astype(F32) + ob_ref[...].astype(F32)
    normed = []
    for hh in range(GDN_HEADS):
        oh = o[:, hh * GDN_DV:(hh + 1) * GDN_DV]
        normed.append(oh * lax.rsqrt(jnp.mean(oh * oh, axis=1, keepdims=True) + NORM_EPS))
    y_a = jnp.concatenate(normed, axis=1) * nwa_ref[...] * _silu(gate(GOUT_OFF, GDN_V))
    t = (yf_ref[...].astype(F32) + yb_ref[...].astype(F32)) * _silu(gate(Z_OFF, SSM_INNER))
    gw = SSM_INNER // SSM_GROUPS
    normed = []
    for g in range(SSM_GROUPS):
        tg = t[:, g * gw:(g + 1) * gw]
        normed.append(tg * lax.rsqrt(jnp.mean(tg * tg, axis=1, keepdims=True) + NORM_EPS))
    y_b = jnp.concatenate(normed, axis=1) * nwb_ref[...]
    d = x.shape[1]
    mix = (jax.nn.sigmoid(gate(GA_OFF, d)) * jnp.dot(y_a.astype(BF16), wpg_ref[...], preferred_element_type=F32)
           + jax.nn.sigmoid(gate(GB_OFF, d)) * jnp.dot(y_b.astype(BF16), wps_ref[...], preferred_element_type=F32))
    out = jnp.dot(mix.astype(BF16), wout_ref[...], preferred_element_type=F32)
    o_ref[...] = _layer_norm(alpha * x + g1_ref[...] * out, lng_ref[...], lnb_ref[...])


def _merge(x, sc, sh, w_gates, o_pair, y_pair, nwa, nwb, wpg, wps, wout, g1, ln_g, ln_b, alpha, tm):
    bsz, length, d = x.shape
    row = lambda width: pl.BlockSpec((None, tm, width), lambda b, i: (b, i, 0))
    const = lambda shape: pl.BlockSpec(shape, lambda b, i: (0,) * len(shape), pipeline_mode=pl.Buffered(1))
    mod = pl.BlockSpec((None, 1, d), lambda b, i: (b, 0, 0))
    return pl.pallas_call(
        functools.partial(_merge_kernel, alpha=alpha),
        out_shape=jax.ShapeDtypeStruct((bsz, length, d), F32),
        grid=(bsz, length // tm),
        in_specs=[row(d), mod, mod, const((d, BIG_N)),
                  row(GDN_V), row(GDN_V), row(SSM_INNER), row(SSM_INNER),
                  const((1, GDN_V)), const((1, SSM_INNER)), const((GDN_V, d)), const((SSM_INNER, d)), const((d, d)),
                  mod, const((1, d)), const((1, d))],
        out_specs=row(d),
        compiler_params=_params(2),
        name="merge",
    )(x, sc, sh, w_gates, o_pair[0], o_pair[1], y_pair[0], y_pair[1], nwa, nwb, wpg, wps, wout, g1, ln_g, ln_b)


def _mlp_kernel(x_ref, sc_ref, sh_ref, g2_ref, w1_ref, b1_ref, w2_ref, b2_ref, lng_ref, lnb_ref, o_ref, *, alpha):
    x = x_ref[...]
    h = (x * (1.0 + sc_ref[...]) + sh_ref[...]).astype(BF16)
    acc = None
    tf = D_MODEL
    for c in range(D_FF // tf):
        u = jnp.dot(h, w1_ref[:, c * tf:(c + 1) * tf], preferred_element_type=F32) + b1_ref[:, c * tf:(c + 1) * tf]
        u = jnp.square(jnp.maximum(u, 0.0))
        part = jnp.dot(u.astype(BF16), w2_ref[c * tf:(c + 1) * tf, :], preferred_element_type=F32)
        acc = part if acc is None else acc + part
    f = acc + b2_ref[...]
    o_ref[...] = _layer_norm(alpha * x + g2_ref[...] * f, lng_ref[...], lnb_ref[...])


def _mlp(x, sc, sh, g2, w1, b1, w2, b2, ln_g, ln_b, alpha, tm):
    bsz, length, d = x.shape
    const = lambda shape: pl.BlockSpec(shape, lambda b, i: (0,) * len(shape), pipeline_mode=pl.Buffered(1))
    mod = pl.BlockSpec((None, 1, d), lambda b, i: (b, 0, 0))
    return pl.pallas_call(
        functools.partial(_mlp_kernel, alpha=alpha),
        out_shape=jax.ShapeDtypeStruct((bsz, length, d), F32),
        grid=(bsz, length // tm),
        in_specs=[pl.BlockSpec((None, tm, d), lambda b, i: (b, i, 0)), mod, mod, mod,
                  const((d, D_FF)), const((1, D_FF)), const((D_FF, d)), const((1, d)), const((1, d)), const((1, d))],
        out_specs=pl.BlockSpec((None, tm, d), lambda b, i: (b, i, 0)),
        compiler_params=_params(2),
        name="mlp",
    )(x, sc, sh, g2, w1, b1, w2, b2, ln_g, ln_b)


def _split_w_in(w_in):
    pts, acc = [], 0
    for s in IN_SPLITS[:-1]:
        acc += s
        pts.append(acc)
    qkv, gout, a_raw, b_raw, z, xbc, dt_raw, gate_a, gate_b = jnp.split(w_in, pts, axis=1)
    big = jnp.concatenate([z, gout, gate_a, gate_b], axis=1).astype(BF16)
    pad = jnp.zeros((w_in.shape[0], SMALL_N - SM_DT - 2 * SSM_HEADS), w_in.dtype)
    small = jnp.concatenate([a_raw, b_raw, dt_raw, pad], axis=1).astype(BF16)
    return big, qkv.astype(BF16), xbc.astype(BF16), small


def kernel(x, c, ctx, c_ctx, w_mod, b_mod, w_in, gdn_conv_w, gdn_A_log, gdn_dt_bias, gdn_norm_w,
           ssm_conv_w, ssm_conv_b, ssm_A_log, ssm_dt_bias, ssm_D, ssm_norm_w,
           w_proj_gdn, w_proj_ssm, w_out, ln1_g, ln1_b, w_ff1, b_ff1, w_ff2, b_ff2, ln2_g, ln2_b):
    bsz, length, d = x.shape
    ctx_len = ctx.shape[1]
    depth = w_mod.shape[0]
    alpha = float((2 * depth) ** 0.25)
    mod_rows = -(-(bsz + 1) // (2 * SUBLANES)) * (2 * SUBLANES)
    cc = jnp.concatenate([c, c_ctx[None, :], jnp.zeros((mod_rows - bsz - 1, d), c.dtype)], axis=0)
    row2 = lambda t: t.reshape(1, -1)
    tm_lat = 1024
    tm_ctx = ctx_len

    for l in range(depth):
        last = l == depth - 1
        mod = _modulation(cc, w_mod[l], b_mod[l])
        lat = [mod[:bsz, i * d:(i + 1) * d].reshape(bsz, 1, d) for i in range(6)]
        cxm = [jnp.broadcast_to(mod[bsz, i * d:(i + 1) * d].reshape(1, 1, d), (bsz, 1, d)) for i in range(6)]
        w_big, w_qkv, w_xbc, w_small = _split_w_in(w_in[l])
        rows = length // GRID_W

        small = _inproj(x, lat[1], lat[0], w_small, tm_lat, SMALL_N, "inproj_narrow")
        qkv = _proj_conv(x, lat[1], lat[0], w_qkv, gdn_conv_w[l], None, GRID_W, False, 2, tm_lat, GDN_QK, "inproj_qkv")
        xbc = _proj_conv(x, lat[1], lat[0], w_xbc, ssm_conv_w[l], ssm_conv_b[l], rows, True, 0, None, 1024,
                         "inproj_xbc")
        small_c = _inproj(ctx, cxm[1], cxm[0], w_small, tm_ctx, SMALL_N, "inproj_narrow_ctx")
        qkv_c = _proj_conv(ctx, cxm[1], cxm[0], w_qkv, gdn_conv_w[l], None, ctx_len, False, 2, tm_ctx, GDN_QK,
                           "inproj_qkv_ctx")
        xbc_c = _proj_conv(ctx, cxm[1], cxm[0], w_xbc, ssm_conv_w[l], ssm_conv_b[l], ctx_len, False, 0, tm_ctx, 1024,
                           "inproj_xbc_ctx")[:, None]

        o_pair, oc_pair = _gdn_branch(qkv, small, qkv_c, small_c, gdn_A_log[l], gdn_dt_bias[l])
        y_pair, yc_pair = _ssd_branch(xbc, small, xbc_c, small_c, ssm_A_log[l], ssm_dt_bias[l], ssm_D[l])

        nwa = jnp.tile(gdn_norm_w[l], GDN_HEADS).reshape(1, GDN_V)
        nwb = row2(ssm_norm_w[l])
        wpg, wps, wo = w_proj_gdn[l].astype(BF16), w_proj_ssm[l].astype(BF16), w_out[l].astype(BF16)
        w1, w2 = w_ff1[l].astype(BF16), w_ff2[l].astype(BF16)
        merge_args = (nwa, nwb, wpg, wps, wo)
        ln1 = (row2(ln1_g[l]), row2(ln1_b[l]))
        mlp_w = (w1, row2(b_ff1[l]), w2, row2(b_ff2[l]), row2(ln2_g[l]), row2(ln2_b[l]))

        x1 = _merge(x, lat[1], lat[0], w_big, o_pair, y_pair, *merge_args, lat[2], *ln1, alpha, MERGE_ROWS)
        x = _mlp(x1, lat[4], lat[3], lat[5], *mlp_w, alpha, MLP_ROWS)
        if not last:
            c1 = _merge(ctx, cxm[1], cxm[0], w_big, oc_pair, yc_pair, *merge_args, cxm[2], *ln1, alpha, ctx_len)
            ctx = _mlp(c1, cxm[4], cxm[3], cxm[5], *mlp_w, alpha, ctx_len)
    return x
```

```python
import functools

import jax
import jax.numpy as jnp
from jax import lax
from jax.experimental import pallas as pl
from jax.experimental.pallas import tpu as pltpu

F32 = jnp.float32
BF16 = jnp.bfloat16

D_MODEL = 1024
GRID_W = 64
GDN_HEADS = 8
GDN_DK = 128
GDN_DV = 128
GDN_QK = GDN_HEADS * GDN_DK
GDN_V = GDN_HEADS * GDN_DV
GDN_QKV = 2 * GDN_QK + GDN_V
SSM_INNER = 2 * D_MODEL
SSM_HEAD_DIM = 64
SSM_HEADS = SSM_INNER // SSM_HEAD_DIM
SSM_GROUPS = 8
SSM_HPG = SSM_HEADS // SSM_GROUPS
SSM_STATE = 128
SSM_GN = SSM_GROUPS * SSM_STATE
SSM_XBC = SSM_INNER + 2 * SSM_GN
CONV_K = 5
CHUNK = 64
D_FF = 4 * D_MODEL
LN_EPS = 1e-5
NORM_EPS = 1e-6
IN_SPLITS = (GDN_QKV, GDN_V, 2 * GDN_HEADS, 2 * GDN_HEADS, SSM_INNER, SSM_XBC, 2 * SSM_HEADS, D_MODEL, D_MODEL)

LANES = 128
SUBLANES = 8
VMEM_LIMIT_BYTES = 56 * 1024 * 1024

Z_OFF = 0
GOUT_OFF = Z_OFF + SSM_INNER
GA_OFF = GOUT_OFF + GDN_V
GB_OFF = GA_OFF + D_MODEL
BIG_N = GB_OFF + D_MODEL
SM_A = 0
SM_B = SM_A + 2 * GDN_HEADS
SM_DT = SM_B + 2 * GDN_HEADS
SM_DT_B = SM_DT + SSM_HEADS
SMALL_N = LANES

GDN_CHUNKS_PER_STEP = 2
GDN_PREP_ROWS = 512
SCAN_OUT_DTYPE = BF16
MERGE_ROWS = 512
MLP_ROWS = 1024
SSD_GROUPS_PER_STEP = 4
SSD_COLUMNS_PER_STEP = 4


def _params(n_axes):
    return pltpu.CompilerParams(dimension_semantics=("arbitrary",) * n_axes, vmem_limit_bytes=VMEM_LIMIT_BYTES)


def _silu(t):
    return t * jax.nn.sigmoid(t)


def _softplus(t):
    return jnp.maximum(t, 0.0) + jnp.log(1.0 + jnp.exp(-jnp.abs(t)))


def _mm(a, b):
    return jnp.dot(a.astype(BF16), b.astype(BF16), preferred_element_type=F32)


def _mm_nt(a, b):
    return lax.dot_general(a.astype(BF16), b.astype(BF16), (((1,), (1,)), ((), ())), preferred_element_type=F32)


def _mm_tn(a, b):
    return lax.dot_general(a.astype(BF16), b.astype(BF16), (((0,), (0,)), ((), ())), preferred_element_type=F32)


def _tri_masks(lower):
    ii = lax.broadcasted_iota(jnp.int32, (CHUNK, CHUNK), 0)
    jj = lax.broadcasted_iota(jnp.int32, (CHUNK, CHUNK), 1)
    if lower:
        return ii >= jj, ii > jj
    return ii <= jj, ii < jj


def _conv_seg(x, w, seg):
    n = x.shape[0]
    assert seg & (seg - 1) == 0 and n % seg == 0
    pos = lax.broadcasted_iota(jnp.int32, (n, 1), 0) & (seg - 1)
    out = None
    for j in range(CONV_K):
        d = j - CONV_K // 2
        if d == 0:
            term = x * w[j:j + 1, :]
        else:
            shifted = pltpu.roll(x, shift=(-d) % n, axis=0)
            valid = (pos + d >= 0) & (pos + d < seg)
            term = jnp.where(valid, shifted, 0.0) * w[j:j + 1, :]
        out = term if out is None else out + term
    return out


def _mod_kernel(c_ref, w_ref, b_ref, o_ref):
    o_ref[...] = _mm(_silu(c_ref[...]), w_ref[...]) + b_ref[...]


def _modulation(cc, w, b):
    rows, d = cc.shape
    n = w.shape[1]
    tn = 1536
    return pl.pallas_call(
        _mod_kernel,
        out_shape=jax.ShapeDtypeStruct((rows, n), F32),
        grid=(n // tn,),
        in_specs=[pl.BlockSpec((rows, d), lambda j: (0, 0)),
                  pl.BlockSpec((d, tn), lambda j: (0, j)),
                  pl.BlockSpec((1, tn), lambda j: (0, j))],
        out_specs=pl.BlockSpec((rows, tn), lambda j: (0, j)),
        compiler_params=_params(1),
        name="modulation",
    )(cc, w, b.reshape(1, n))


def _inproj_kernel(x_ref, sc_ref, sh_ref, w_ref, o_ref):
    h = x_ref[...] * (1.0 + sc_ref[...]) + sh_ref[...]
    o_ref[...] = jnp.dot(h.astype(BF16), w_ref[...], preferred_element_type=F32)


def _inproj(x, sc, sh, w, tm, tn, name):
    bsz, length, d = x.shape
    n = w.shape[1]
    return pl.pallas_call(
        _inproj_kernel,
        out_shape=jax.ShapeDtypeStruct((bsz, length, n), F32),
        grid=(bsz, length // tm, n // tn),
        in_specs=[pl.BlockSpec((None, tm, d), lambda b, i, j: (b, i, 0)),
                  pl.BlockSpec((None, 1, d), lambda b, i, j: (b, 0, 0)),
                  pl.BlockSpec((None, 1, d), lambda b, i, j: (b, 0, 0)),
                  pl.BlockSpec((d, tn), lambda b, i, j: (0, j))],
        out_specs=pl.BlockSpec((None, tm, tn), lambda b, i, j: (b, i, j)),
        compiler_params=_params(3),
        name=name,
    )(x, sc, sh, w)


def _proj_conv_kernel(*refs, seg, cols, has_bias, norm_tiles):
    x_ref, sc_ref, sh_ref, w_ref, cw_ref = refs[:5]
    k = 5
    cb_ref = refs[k] if has_bias else None
    k += has_bias
    perm_ref = refs[k] if cols else None
    o_ref, h_ref = refs[k + cols:]
    j = pl.program_id(2)
    n_tok = h_ref.shape[0]

    @pl.when(j == 0)
    def _():
        x = x_ref[...]
        if cols:
            x = x.reshape(n_tok, x.shape[2])
        h = (x * (1.0 + sc_ref[...]) + sh_ref[...]).astype(BF16)
        if cols:
            h = jnp.dot(perm_ref[...], h, preferred_element_type=F32).astype(BF16)
        h_ref[...] = h

    y = jnp.dot(h_ref[...], w_ref[...], preferred_element_type=F32)
    y = _conv_seg(y, cw_ref[...], seg)
    if has_bias:
        y = y + cb_ref[...]
    y = _silu(y)

    def store(t):
        o_ref[...] = t.reshape(o_ref.shape)

    if norm_tiles:
        @pl.when(j < norm_tiles)
        def _():
            scale = jnp.where(j == 0, GDN_DK ** -0.5, 1.0)
            heads = []
            for hh in range(y.shape[1] // GDN_DK):
                yh = y[:, hh * GDN_DK:(hh + 1) * GDN_DK]
                heads.append(yh * (lax.rsqrt(jnp.sum(yh * yh, axis=1, keepdims=True) + NORM_EPS) * scale))
            store(jnp.concatenate(heads, axis=1))

        @pl.when(j >= norm_tiles)
        def _():
            store(y)
    else:
        store(y)


def _proj_conv(x, sc, sh, w, conv_w, conv_b, seg, cols, norm_tiles, tm, tn, name):
    bsz, length, d = x.shape
    n = w.shape[1]
    has_bias = conv_b is not None
    if cols:
        rows, cps = length // GRID_W, SUBLANES
        assert seg == rows and seg & (seg - 1) == 0
        n_tok = rows * cps
        x_in = x.reshape(bsz, rows, GRID_W, d)
        x_spec = pl.BlockSpec((None, rows, cps, d), lambda b, i, j: (b, 0, i, 0))
        out_shape = jax.ShapeDtypeStruct((bsz, GRID_W, rows, n), F32)
        out_spec = pl.BlockSpec((None, cps, rows, tn), lambda b, i, j: (b, i, 0, j))
        grid = (bsz, GRID_W // cps, n // tn)
    else:
        n_tok = tm
        x_in = x
        x_spec = pl.BlockSpec((None, tm, d), lambda b, i, j: (b, i, 0))
        out_shape = jax.ShapeDtypeStruct((bsz, length, n), F32)
        out_spec = pl.BlockSpec((None, tm, tn), lambda b, i, j: (b, i, j))
        grid = (bsz, length // tm, n // tn)
    mod = pl.BlockSpec((None, 1, d), lambda b, i, j: (b, 0, 0))
    in_specs = [x_spec, mod, mod, pl.BlockSpec((d, tn), lambda b, i, j: (0, j)),
                pl.BlockSpec((CONV_K, tn), lambda b, i, j: (0, j))]
    args = [x_in, sc, sh, w, conv_w]
    if has_bias:
        in_specs.append(pl.BlockSpec((1, tn), lambda b, i, j: (0, j)))
        args.append(conv_b.reshape(1, n))
    if cols:
        dst = jnp.arange(n_tok)
        src = (dst % rows) * cps + dst // rows
        in_specs.append(pl.BlockSpec((n_tok, n_tok), lambda b, i, j: (0, 0), pipeline_mode=pl.Buffered(1)))
        args.append((src[:, None] == jnp.arange(n_tok)[None, :]).astype(BF16))
    return pl.pallas_call(
        functools.partial(_proj_conv_kernel, seg=seg, cols=cols, has_bias=has_bias, norm_tiles=norm_tiles),
        out_shape=out_shape,
        grid=grid,
        in_specs=in_specs,
        out_specs=out_spec,
        scratch_shapes=[pltpu.VMEM((n_tok, d), BF16)],
        compiler_params=_params(3),
        name=name,
    )(*args)


def _chunk_cumsums(t):
    ii = lax.broadcasted_iota(jnp.int32, (2 * CHUNK, CHUNK), 0)
    jj = lax.broadcasted_iota(jnp.int32, (2 * CHUNK, CHUNK), 1)
    ones = ((ii < CHUNK) & (ii >= jj)) | ((ii >= CHUNK) & (ii - CHUNK <= jj))
    tri = jnp.where(ones, 1.0, 0.0).astype(BF16)
    hi = t.astype(BF16)
    r1 = t - hi.astype(F32)
    mid = r1.astype(BF16)
    lo = (r1 - mid.astype(F32)).astype(BF16)
    w = t.shape[1]
    sums = jnp.dot(tri, jnp.concatenate([hi, mid, lo], axis=1), preferred_element_type=F32)
    sums = sums[:, :w] + sums[:, w:2 * w] + sums[:, 2 * w:]
    return sums[:CHUNK], sums[CHUNK:]


def _gdn_prep_kernel(s_ref, alog_ref, dtb_ref, o_ref, *, rows):
    s = s_ref[...]
    lane = lax.broadcasted_iota(jnp.int32, (1, LANES), 1)
    g = -jnp.exp(alog_ref[...]) * _softplus(s + dtb_ref[...])
    beta = jax.nn.sigmoid(s)
    for c in range(rows // CHUNK):
        sl = slice(c * CHUNK, (c + 1) * CHUNK)
        fwd, bwd = _chunk_cumsums(g[sl, :])
        gc = jnp.where(lane < SM_A + GDN_HEADS, fwd, bwd)
        o_ref[sl, :] = jnp.where(lane < SM_B, gc, beta[sl, :])


def _gdn_prep(small, a_log, dt_bias, tb):
    bsz, length, _ = small.shape
    pad = lambda t: jnp.pad(t.reshape(1, -1).astype(F32), ((0, 0), (SM_A, LANES - SM_A - 2 * GDN_HEADS)))
    return pl.pallas_call(
        functools.partial(_gdn_prep_kernel, rows=tb),
        out_shape=jax.ShapeDtypeStruct((bsz, length, LANES), F32),
        grid=(bsz, length // tb),
        in_specs=[pl.BlockSpec((None, tb, LANES), lambda b, i: (b, i, 0)),
                  pl.BlockSpec((1, LANES), lambda b, i: (0, 0)),
                  pl.BlockSpec((1, LANES), lambda b, i: (0, 0))],
        out_specs=pl.BlockSpec((None, tb, LANES), lambda b, i: (b, i, 0)),
        compiler_params=_params(2),
        name="gdn_prep",
    )(small, pad(a_log), pad(dt_bias))


def _tri_inverse(mats):
    ii = lax.broadcasted_iota(jnp.int32, (CHUNK, CHUNK), 0)
    jj = lax.broadcasted_iota(jnp.int32, (CHUNK, CHUNK), 1)
    eye = jnp.where(ii == jj, 1.0, 0.0)
    zero = jnp.zeros((CHUNK, CHUNK), F32)
    right = lax.broadcasted_iota(jnp.int32, (CHUNK, 2 * CHUNK), 1) >= CHUNK
    zs = [jnp.concatenate([zero, eye], axis=1) + _mm(a, jnp.concatenate([a, -eye], axis=1)) for a in mats]
    power = 2
    while power < CHUNK:
        ps = [_mm(z[:, :CHUNK], z) for z in zs]
        zs = [p + jnp.where(right, z, 0.0) for p, z in zip(ps, zs)]
        power *= 2
    return zs


def _gdn_local(items, masks):
    qs, ks, vs, gcols, grows, bcols, lowers = zip(*items)
    n = len(items)
    incl = [masks[lo][0] for lo in lowers]
    strict = [masks[lo][1] for lo in lowers]
    glast = [grows[i][:, CHUNK - 1:CHUNK] if lowers[i] else grows[i][:, 0:1] for i in range(n)]
    k16 = [k.astype(BF16) for k in ks]
    kq = [_mm_nt(jnp.concatenate([k16[i], qs[i].astype(BF16)], axis=0), k16[i]) for i in range(n)]
    dec = [jnp.exp(jnp.where(incl[i], gcols[i] - grows[i], 0.0)) for i in range(n)]
    a_mats = [jnp.where(strict[i], bcols[i] * kq[i][:CHUNK] * dec[i], 0.0) for i in range(n)]
    t_inv = _tri_inverse(a_mats)
    eg = [jnp.exp(g) for g in gcols]
    rhs = [jnp.concatenate([(bcols[i] * eg[i]) * ks[i], bcols[i] * vs[i]], axis=1) for i in range(n)]
    pad = jnp.zeros((CHUNK, GDN_DK + GDN_DV), F32)
    wu = [_mm(t, jnp.concatenate([pad, r], axis=0)) for t, r in zip(t_inv, rhs)]
    kd_wu = [_mm_tn(ks[i] * jnp.exp(glast[i] - gcols[i]), wu[i]) for i in range(n)]
    qk_wu = [_mm(jnp.where(incl[i], kq[i][CHUNK:] * dec[i], 0.0), wu[i]) for i in range(n)]
    lhs = [jnp.concatenate([kd_wu[i][:, :GDN_DK].astype(BF16), (qs[i] * eg[i] - qk_wu[i][:, :GDN_DK]).astype(BF16)],
                           axis=0) for i in range(n)]
    return [(lhs[i], kd_wu[i][:, GDN_DK:], qk_wu[i][:, GDN_DK:], jnp.exp(glast[i])) for i in range(n)]


def _gdn_on_state(local, states):
    n = len(local)
    on_state = [_mm(local[i][0], states[i]) for i in range(n)]
    return [(on_state[i][GDN_DK:] + local[i][2], local[i][3] * states[i] - on_state[i][:GDN_DK] + local[i][1])
            for i in range(n)]


def _lane_column(t, idx, lane):
    if isinstance(idx, int):
        return t[:, idx:idx + 1]
    return jnp.sum(jnp.where(lane == idx, t, 0.0), axis=1, keepdims=True)


def _gdn_kernel(qf_ref, kf_ref, vf_ref, qb_ref, kb_ref, vb_ref, pf_ref, pb_ref, ptf_ref, ptb_ref,
                s0_ref, of_ref, ob_ref, sout_ref, state_ref, *, n_steps, hs, nc):
    step = pl.program_id(2)
    head0 = 0 if hs == GDN_HEADS else pl.program_id(1) * hs

    @pl.when(step == 0)
    def _():
        state_ref[...] = s0_ref[...]

    lane = lax.broadcasted_iota(jnp.int32, (1, LANES), 1)

    def heads_of(q_ref, k_ref, v_ref):
        q, k, v = q_ref[...], k_ref[...], v_ref[...]
        return [(q[:, hh * GDN_DK:(hh + 1) * GDN_DK], k[:, hh * GDN_DK:(hh + 1) * GDN_DK],
                 v[:, hh * GDN_DV:(hh + 1) * GDN_DV]) for hh in range(hs)]

    heads_f = heads_of(qf_ref, kf_ref, vf_ref)
    heads_b = heads_of(qb_ref, kb_ref, vb_ref)
    pf = pf_ref[...]
    pb = pb_ref[...]
    masks = {True: _tri_masks(True), False: _tri_masks(False)}
    cols = []
    for hh in range(hs):
        head = head0 + hh
        cols.append((_lane_column(pf, SM_A + head, lane), _lane_column(pf, SM_B + head, lane),
                     _lane_column(pb, SM_A + GDN_HEADS + head, lane), _lane_column(pb, SM_B + GDN_HEADS + head, lane)))
    items = []
    for c in range(nc):
        cb = nc - 1 - c
        sl_f = slice(c * CHUNK, (c + 1) * CHUNK)
        sl_b = slice(cb * CHUNK, (cb + 1) * CHUNK)
        for hh in range(hs):
            head = head0 + hh
            gcol_f, bcol_f, gcol_b, bcol_b = cols[hh]
            qf, kf, vf = heads_f[hh]
            qb, kb, vb = heads_b[hh]
            grow_f = ptf_ref[c, pl.ds(SM_A + head, 1), :]
            grow_b = ptb_ref[cb, pl.ds(SM_A + GDN_HEADS + head, 1), :]
            items.append((qf[sl_f], kf[sl_f], vf[sl_f], gcol_f[sl_f], grow_f, bcol_f[sl_f], True))
            items.append((qb[sl_b], kb[sl_b], vb[sl_b], gcol_b[sl_b], grow_b, bcol_b[sl_b], False))
    local = _gdn_local(items, masks)
    states = [state_ref[hh, d] for hh in range(hs) for d in range(2)]
    for c in range(nc):
        cb = nc - 1 - c
        results = _gdn_on_state(local[c * 2 * hs:(c + 1) * 2 * hs], states)
        states = [r[1] for r in results]
        for hh in range(hs):
            hl = slice(hh * GDN_DV, (hh + 1) * GDN_DV)
            of_ref[c * CHUNK:(c + 1) * CHUNK, hl] = results[2 * hh][0].astype(of_ref.dtype)
            ob_ref[cb * CHUNK:(cb + 1) * CHUNK, hl] = results[2 * hh + 1][0].astype(ob_ref.dtype)
    for hh in range(hs):
        state_ref[hh, 0] = states[2 * hh]
        state_ref[hh, 1] = states[2 * hh + 1]

    @pl.when(step == n_steps - 1)
    def _():
        sout_ref[...] = state_ref[...]


def _gdn_scan(qkv, p, pt, s0, hs, nc):
    bsz, length, _ = qkv.shape
    tb = nc * CHUNK
    n_steps = length // tb
    width = hs * GDN_DK
    qoff, koff, voff = 0, GDN_QK // width, 2 * GDN_QK // width
    fwd = lambda off: pl.BlockSpec((None, tb, width), lambda b, h, s: (b, s, off + h))
    bwd = lambda off: pl.BlockSpec((None, tb, width), lambda b, h, s: (b, n_steps - 1 - s, off + h))
    state_spec = pl.BlockSpec((None, hs, 2, GDN_DK, GDN_DV), lambda b, h, s: (b, h, 0, 0, 0))
    return pl.pallas_call(
        functools.partial(_gdn_kernel, n_steps=n_steps, hs=hs, nc=nc),
        out_shape=(jax.ShapeDtypeStruct((bsz, length, GDN_V), SCAN_OUT_DTYPE),
                   jax.ShapeDtypeStruct((bsz, length, GDN_V), SCAN_OUT_DTYPE),
                   jax.ShapeDtypeStruct((bsz, GDN_HEADS, 2, GDN_DK, GDN_DV), F32)),
        grid=(bsz, GDN_HEADS // hs, n_steps),
        in_specs=[fwd(qoff), fwd(koff), fwd(voff), bwd(qoff), bwd(koff), bwd(voff),
                  pl.BlockSpec((None, tb, LANES), lambda b, h, s: (b, s, 0)),
                  pl.BlockSpec((None, tb, LANES), lambda b, h, s: (b, n_steps - 1 - s, 0)),
                  pl.BlockSpec((None, nc, 4 * SUBLANES, CHUNK), lambda b, h, s: (b, s, 0, 0)),
                  pl.BlockSpec((None, nc, 4 * SUBLANES, CHUNK), lambda b, h, s: (b, n_steps - 1 - s, 0, 0)),
                  state_spec],
        out_specs=(pl.BlockSpec((None, tb, width), lambda b, h, s: (b, s, h)),
                   pl.BlockSpec((None, tb, width), lambda b, h, s: (b, n_steps - 1 - s, h)),
                   state_spec),
        scratch_shapes=[pltpu.VMEM((hs, 2, GDN_DK, GDN_DV), F32)],
        compiler_params=_params(3),
        name="gdn_scan",
    )(qkv, qkv, qkv, qkv, qkv, qkv, p, p, pt, pt, s0)


def _gdn_branch(qkv, small, qkv_c, small_c, a_log, dt_bias):
    bsz = qkv.shape[0]
    s0 = jnp.zeros((bsz, GDN_HEADS, 2, GDN_DK, GDN_DV), F32)

    def rows_t(p):
        chunks = p[:, :, :4 * SUBLANES].reshape(bsz, p.shape[1] // CHUNK, CHUNK, 4 * SUBLANES)
        return jnp.swapaxes(chunks, 2, 3)

    ctx_len = qkv_c.shape[1]
    p_c = _gdn_prep(small_c, a_log, dt_bias, ctx_len)
    oc_f, oc_b, s_ctx = _gdn_scan(qkv_c, p_c, rows_t(p_c), s0, GDN_HEADS, GDN_CHUNKS_PER_STEP)
    p = _gdn_prep(small, a_log, dt_bias, GDN_PREP_ROWS)
    o_f, o_b, _ = _gdn_scan(qkv, p, rows_t(p), s_ctx, GDN_HEADS, GDN_CHUNKS_PER_STEP)
    return (o_f, o_b), (oc_f, oc_b)


SSD_QUANTITIES = 4
SSD_HEADS_PER_STEP = SSD_GROUPS_PER_STEP * SSM_HPG
SSD_STEP_WIDTH = SSD_HEADS_PER_STEP * SSM_HEAD_DIM
BF16_PIECES = 3


def _ssd_prep_kernel(s_ref, alog_ref, dtb_ref, o_ref, *, rows):
    lane = lax.broadcasted_iota(jnp.int32, (1, LANES), 1)
    fwd_lane = lane < SM_DT_B
    neg_a = -jnp.exp(alog_ref[...])
    for seg in range(o_ref.shape[0]):
        dt = _softplus(s_ref[:, seg * LANES:(seg + 1) * LANES] + dtb_ref[...])
        la = dt * neg_a
        o_ref[seg, :, 0:LANES] = dt
        for c in range(rows // CHUNK):
            sl = slice(c * CHUNK, (c + 1) * CHUNK)
            fwd, bwd = _chunk_cumsums(la[sl, :])
            gc = jnp.where(fwd_lane, fwd, bwd)
            g_last = jnp.where(fwd_lane, gc[CHUNK - 1:CHUNK, :], gc[0:1, :])
            o_ref[seg, sl, LANES:2 * LANES] = gc
            o_ref[seg, sl, 2 * LANES:3 * LANES] = jnp.exp(gc)
            o_ref[seg, sl, 3 * LANES:4 * LANES] = dt[sl, :] * jnp.exp(g_last - gc)


def _ssd_prep(small_cols, a_log, dt_bias, n_seg):
    bsz, rows, _ = small_cols.shape
    sps = min(n_seg, SUBLANES)
    pad = lambda t: jnp.pad(t.reshape(1, -1).astype(F32), ((0, 0), (SM_DT, LANES - SM_DT - 2 * SSM_HEADS)))
    return pl.pallas_call(
        functools.partial(_ssd_prep_kernel, rows=rows),
        out_shape=jax.ShapeDtypeStruct((bsz, n_seg, rows, SSD_QUANTITIES * LANES), F32),
        grid=(bsz, n_seg // sps),
        in_specs=[pl.BlockSpec((None, rows, sps * LANES), lambda b, c: (b, 0, c)),
                  pl.BlockSpec((1, LANES), lambda b, c: (0, 0)),
                  pl.BlockSpec((1, LANES), lambda b, c: (0, 0))],
        out_specs=pl.BlockSpec((None, sps, rows, SSD_QUANTITIES * LANES), lambda b, c: (b, c, 0, 0)),
        compiler_params=_params(2),
        name="ssd_prep",
    )(small_cols, pad(a_log), pad(dt_bias))


def _ssd_layouts(q):
    bsz, n_seg, rows, _ = q.shape
    steps = SSM_GROUPS // SSD_GROUPS_PER_STEP
    q = q.reshape(bsz, n_seg, rows, SSD_QUANTITIES, LANES)[..., SM_DT:SM_DT + 2 * SSM_HEADS]
    q = q.reshape(bsz, n_seg, rows, SSD_QUANTITIES, 2, steps, SSD_HEADS_PER_STEP)
    out = []
    for d in range(2):
        qd = q[:, :, :, :, d]
        cols = jnp.transpose(qd, (0, 1, 4, 2, 3, 5)).reshape(bsz, n_seg, steps, rows, SSD_QUANTITIES * SSD_HEADS_PER_STEP)
        as_rows = lambda t: jnp.transpose(
            t.reshape(bsz, n_seg, rows // CHUNK, CHUNK, steps, SSD_HEADS_PER_STEP),
            (0, 1, 4, 2, 5, 3)).reshape(bsz, n_seg, steps, rows // CHUNK, SSD_STEP_WIDTH)
        out.append((cols, jnp.concatenate([as_rows(qd[:, :, :, 1]), as_rows(qd[:, :, :, 0])], axis=-1)))
    return out


def _ssd_expand_matrix():
    k = SSD_QUANTITIES * SSD_HEADS_PER_STEP
    src = jnp.arange(k)
    dst = jnp.arange(SSD_STEP_WIDTH, k * SSM_HEAD_DIM) // SSM_HEAD_DIM
    one = (src[:, None] == dst[None, :]).astype(BF16)
    return jnp.concatenate([one] * BF16_PIECES, axis=0)


def _ssd_chunks(items, incl, block_diag):
    n = len(items)
    cb = [_mm_nt(it[4], jnp.concatenate([it[3].astype(BF16)] * SSM_HPG, axis=0)) for it in items]
    inter = [_mm(it[4], it[9]) for it in items]
    lhs = []
    for i, it in enumerate(items):
        m = incl[it[10]]
        lhs.append(jnp.where(m, cb[i] * jnp.exp(jnp.where(m, it[5] - it[6], 0.0)) * it[1], 0.0))
    rhs = [jnp.where(block_diag, jnp.concatenate([it[0].astype(BF16)] * SSM_HPG, axis=0), 0.0) for it in items]
    intra = [_mm(lhs[i], rhs[i]) for i in range(n)]
    upd = [_mm_tn(it[3], it[2]) for it in items]
    out = []
    for i, it in enumerate(items):
        y = intra[i] + inter[i] * it[7]
        if it[11] is not None:
            y = y + it[11] * it[0]
        out.append((y, it[8] * it[9] + upd[i]))
    return out


def _ssd_kernel(xf_ref, bf_ref, cf_ref, xb_ref, bb_ref, cb_ref, colf_ref, colb_ref, growf_ref, growb_ref,
                e_ref, d_ref, s0_ref, yf_ref, yb_ref, sout_ref, state_ref, *, rows, n_seg, gs):
    step = pl.program_id(2)

    @pl.when(step == 0)
    def _():
        state_ref[...] = s0_ref[...]

    gw = SSM_HPG * SSM_HEAD_DIM
    width = gs * gw

    def expand(col_ref):
        c = col_ref[...]
        hi = c.astype(BF16)
        r1 = c - hi.astype(F32)
        mid = r1.astype(BF16)
        lo = (r1 - mid.astype(F32)).astype(BF16)
        ex = jnp.dot(jnp.concatenate([hi, mid, lo], axis=1), e_ref[...], preferred_element_type=F32)
        return [ex[:, q * width:(q + 1) * width] for q in range(SSD_QUANTITIES - 1)]

    xf, xb = xf_ref[...], xb_ref[...]
    gcx_f, egx_f, q4x_f = expand(colf_ref)
    gcx_b, egx_b, q4x_b = expand(colb_ref)
    xdec_f, xdec_b = xf * q4x_f, xb * q4x_b
    bmf, cmf, bmb, cmb = bf_ref[...], cf_ref[...], bb_ref[...], cb_ref[...]
    d_all = d_ref[...]

    row = lax.broadcasted_iota(jnp.int32, (CHUNK, gw), 0)
    tok = lax.broadcasted_iota(jnp.int32, (CHUNK, gw), 1) & (CHUNK - 1)
    incl = {True: row >= tok, False: row <= tok}
    block_diag = (lax.shift_right_logical(lax.broadcasted_iota(jnp.int32, (gw, gw), 0), CHUNK.bit_length() - 1)
                  == lax.shift_right_logical(lax.broadcasted_iota(jnp.int32, (gw, gw), 1), SSM_HEAD_DIM.bit_length() - 1))

    n_chunks = rows // CHUNK
    states = [[state_ref[gg, 0], state_ref[gg, 1]] for gg in range(gs)]
    for c in range(n_chunks):
        cb = n_chunks - 1 - c
        sl_f = slice(c * CHUNK, (c + 1) * CHUNK)
        sl_b = slice(cb * CHUNK, (cb + 1) * CHUNK)
        last_f = slice(c * CHUNK + CHUNK - 1, (c + 1) * CHUNK)
        last_b = slice(cb * CHUNK, cb * CHUNK + 1)
        items = []
        for gg in range(gs):
            xl = slice(gg * gw, (gg + 1) * gw)
            nl = slice(gg * SSM_STATE, (gg + 1) * SSM_STATE)
            dl = slice(width + gg * gw, width + (gg + 1) * gw)
            items.append((xf[sl_f, xl], growf_ref[c:c + 1, dl], xdec_f[sl_f, xl], bmf[sl_f, nl], cmf[sl_f, nl],
                          gcx_f[sl_f, xl], growf_ref[c:c + 1, xl], egx_f[sl_f, xl], egx_f[last_f, xl], states[gg][0], True,
                          d_all[:, xl]))
            items.append((xb[sl_b, xl], growb_ref[cb:cb + 1, dl], xdec_b[sl_b, xl], bmb[sl_b, nl], cmb[sl_b, nl],
                          gcx_b[sl_b, xl], growb_ref[cb:cb + 1, xl], egx_b[sl_b, xl], egx_b[last_b, xl], states[gg][1], False,
                          None))
        results = _ssd_chunks(items, incl, block_diag)
        for gg in range(gs):
            xl = slice(gg * gw, (gg + 1) * gw)
            yf_ref[sl_f, xl] = results[2 * gg][0].astype(yf_ref.dtype)
            yb_ref[sl_b, xl] = results[2 * gg + 1][0].astype(yb_ref.dtype)
            states[gg] = [results[2 * gg][1], results[2 * gg + 1][1]]
    for gg in range(gs):
        state_ref[gg, 0] = states[gg][0]
        state_ref[gg, 1] = states[gg][1]

    @pl.when(step == n_seg - 1)
    def _():
        sout_ref[...] = state_ref[...]


def _ssd_scan(xbc, layouts, expand, d_exp, s0):
    bsz, n_seg, rows, _ = xbc.shape
    gs = SSD_GROUPS_PER_STEP
    xw, nw = SSD_STEP_WIDTH, gs * SSM_STATE
    boff, coff = SSM_INNER // nw, (SSM_INNER + SSM_GN) // nw
    (cols_f, grow_f), (cols_b, grow_b) = layouts
    n_cols = cols_f.shape[-1]
    seg_f = lambda s: s
    seg_b = lambda s: n_seg - 1 - s
    xspec = lambda seg: pl.BlockSpec((None, None, rows, xw), lambda b, g, s: (b, seg(s), 0, g))
    nspec = lambda seg, off: pl.BlockSpec((None, None, rows, nw), lambda b, g, s: (b, seg(s), 0, off + g))
    cspec = lambda seg: pl.BlockSpec((None, None, None, rows, n_cols), lambda b, g, s: (b, seg(s), g, 0, 0))
    rspec = lambda seg: pl.BlockSpec((None, None, None, rows // CHUNK, 2 * xw), lambda b, g, s: (b, seg(s), g, 0, 0))
    state_spec = pl.BlockSpec((None, gs, 2, SSM_STATE, SSM_HPG * SSM_HEAD_DIM), lambda b, g, s: (b, g, 0, 0, 0))
    y_shape = jax.ShapeDtypeStruct((bsz, n_seg, rows, SSM_INNER), SCAN_OUT_DTYPE)
    return pl.pallas_call(
        functools.partial(_ssd_kernel, rows=rows, n_seg=n_seg, gs=gs),
        out_shape=(y_shape, y_shape,
                   jax.ShapeDtypeStruct((bsz, SSM_GROUPS, 2, SSM_STATE, SSM_HPG * SSM_HEAD_DIM), F32)),
        grid=(bsz, SSM_GROUPS // gs, n_seg),
        in_specs=[xspec(seg_f), nspec(seg_f, boff), nspec(seg_f, coff),
                  xspec(seg_b), nspec(seg_b, boff), nspec(seg_b, coff),
                  cspec(seg_f), cspec(seg_b), rspec(seg_f), rspec(seg_b),
                  pl.BlockSpec(expand.shape, lambda b, g, s: (0, 0)),
                  pl.BlockSpec((1, xw), lambda b, g, s: (0, g)), state_spec],
        out_specs=(xspec(seg_f), xspec(seg_b), state_spec),
        scratch_shapes=[pltpu.VMEM((gs, 2, SSM_STATE, SSM_HPG * SSM_HEAD_DIM), F32)],
        compiler_params=_params(3),
        name="ssd_scan",
    )(xbc, xbc, xbc, xbc, xbc, xbc, cols_f, cols_b, grow_f, grow_b, expand, d_exp, s0)


def _ssd_branch(xbc, small, xbc_c, small_c, a_log, dt_bias, d_skip):
    bsz, n_seg, rows, _ = xbc.shape
    assert rows % CHUNK == 0 and xbc_c.shape[2] % CHUNK == 0
    d_exp = jnp.repeat(d_skip.astype(F32), SSM_HEAD_DIM).reshape(1, SSM_INNER)
    s0 = jnp.zeros((bsz, SSM_GROUPS, 2, SSM_STATE, SSM_HPG * SSM_HEAD_DIM), F32)
    expand = _ssd_expand_matrix()
    yc_f, yc_b, s_ctx = _ssd_scan(xbc_c, _ssd_layouts(_ssd_prep(small_c, a_log, dt_bias, 1)), expand, d_exp, s0)
    q = _ssd_prep(small.reshape(bsz, rows, n_seg * SMALL_N), a_log, dt_bias, n_seg)
    steps, step_rows = n_seg // SSD_COLUMNS_PER_STEP, SSD_COLUMNS_PER_STEP * rows
    merged = lambda t: t.reshape(bsz, steps, step_rows, t.shape[-1])
    y_f, y_b, _ = _ssd_scan(merged(xbc), _ssd_layouts(merged(q)), expand, d_exp, s_ctx)
    to_raster = lambda y: jnp.swapaxes(y.reshape(bsz, n_seg, rows, SSM_INNER), 1, 2).reshape(bsz, rows * n_seg, SSM_INNER)
    return (to_raster(y_f), to_raster(y_b)), (yc_f[:, 0], yc_b[:, 0])


def _layer_norm(r, g, b):
    mu = jnp.mean(r, axis=1, keepdims=True)
    var = jnp.mean(jnp.square(r - mu), axis=1, keepdims=True)
    return (r - mu) * lax.rsqrt(var + LN_EPS) * g + b


def _merge_kernel(x_ref, sc_ref, sh_ref, wg_ref, of_ref, ob_ref, yf_ref, yb_ref,
                  nwa_ref, nwb_ref, wpg_ref, wps_ref, wout_ref, g1_ref, lng_ref, lnb_ref, o_ref, *, alpha):
    x = x_ref[...]
    h = (x * (1.0 + sc_ref[...]) + sh_ref[...]).astype(BF16)
    gate = lambda off, width: jnp.dot(h, wg_ref[:, off:off + width], preferred_element_type=F32)
    o = of_ref[...].astype(F32) + ob_ref[...].astype(F32)
    normed = []
    for hh in range(GDN_HEADS):
        oh = o[:, hh * GDN_DV:(hh + 1) * GDN_DV]
        normed.append(oh * lax.rsqrt(jnp.mean(oh * oh, axis=1, keepdims=True) + NORM_EPS))
    y_a = jnp.concatenate(normed, axis=1) * nwa_ref[...] * _silu(gate(GOUT_OFF, GDN_V))
    t = (yf_ref[...].astype(F32) + yb_ref[...].astype(F32)) * _silu(gate(Z_OFF, SSM_INNER))
    gw = SSM_INNER // SSM_GROUPS
    normed = []
    for g in range(SSM_GROUPS):
        tg = t[:, g * gw:(g + 1) * gw]
        normed.append(tg * lax.rsqrt(jnp.mean(tg * tg, axis=1, keepdims=True) + NORM_EPS))
    y_b = jnp.concatenate(normed, axis=1) * nwb_ref[...]
    d = x.shape[1]
    mix = (jax.nn.sigmoid(gate(GA_OFF, d)) * jnp.dot(y_a.astype(BF16), wpg_ref[...], preferred_element_type=F32)
           + jax.nn.sigmoid(gate(GB_OFF, d)) * jnp.dot(y_b.astype(BF16), wps_ref[...], preferred_element_type=F32))
    out = jnp.dot(mix.astype(BF16), wout_ref[...], preferred_element_type=F32)
    o_ref[...] = _layer_norm(alpha * x + g1_ref[...] * out, lng_ref[...], lnb_ref[...])


def _merge(x, sc, sh, w_gates, o_pair, y_pair, nwa, nwb, wpg, wps, wout, g1, ln_g, ln_b, alpha, tm):
    bsz, length, d = x.shape
    row = lambda width: pl.BlockSpec((None, tm, width), lambda b, i: (b, i, 0))
    const = lambda shape: pl.BlockSpec(shape, lambda b, i: (0,) * len(shape), pipeline_mode=pl.Buffered(1))
    mod = pl.BlockSpec((None, 1, d), lambda b, i: (b, 0, 0))
    return pl.pallas_call(
        functools.partial(_merge_kernel, alpha=alpha),
        out_shape=jax.ShapeDtypeStruct((bsz, length, d), F32),
        grid=(bsz, length // tm),
        in_specs=[row(d), mod, mod, const((d, BIG_N)),
                  row(GDN_V), row(GDN_V), row(SSM_INNER), row(SSM_INNER),
                  const((1, GDN_V)), const((1, SSM_INNER)), const((GDN_V, d)), const((SSM_INNER, d)), const((d, d)),
                  mod, const((1, d)), const((1, d))],
        out_specs=row(d),
        compiler_params=_params(2),
        name="merge",
    )(x, sc, sh, w_gates, o_pair[0], o_pair[1], y_pair[0], y_pair[1], nwa, nwb, wpg, wps, wout, g1, ln_g, ln_b)


def _mlp_kernel(x_ref, sc_ref, sh_ref, g2_ref, w1_ref, b1_ref, w2_ref, b2_ref, lng_ref, lnb_ref, o_ref, *, alpha):
    x = x_ref[...]
    h = (x * (1.0 + sc_ref[...]) + sh_ref[...]).astype(BF16)
    acc = None
    tf = D_MODEL
    for c in range(D_FF // tf):
        u = jnp.dot(h, w1_ref[:, c * tf:(c + 1) * tf], preferred_element_type=F32) + b1_ref[:, c * tf:(c + 1) * tf]
        u = jnp.square(jnp.maximum(u, 0.0))
        part = jnp.dot(u.astype(BF16), w2_ref[c * tf:(c + 1) * tf, :], preferred_element_type=F32)
        acc = part if acc is None else acc + part
    f = acc + b2_ref[...]
    o_ref[...] = _layer_norm(alpha * x + g2_ref[...] * f, lng_ref[...], lnb_ref[...])


def _mlp(x, sc, sh, g2, w1, b1, w2, b2, ln_g, ln_b, alpha, tm):
    bsz, length, d = x.shape
    const = lambda shape: pl.BlockSpec(shape, lambda b, i: (0,) * len(shape), pipeline_mode=pl.Buffered(1))
    mod = pl.BlockSpec((None, 1, d), lambda b, i: (b, 0, 0))
    return pl.pallas_call(
        functools.partial(_mlp_kernel, alpha=alpha),
        out_shape=jax.ShapeDtypeStruct((bsz, length, d), F32),
        grid=(bsz, length // tm),
        in_specs=[pl.BlockSpec((None, tm, d), lambda b, i: (b, i, 0)), mod, mod, mod,
                  const((d, D_FF)), const((1, D_FF)), const((D_FF, d)), const((1, d)), const((1, d)), const((1, d))],
        out_specs=pl.BlockSpec((None, tm, d), lambda b, i: (b, i, 0)),
        compiler_params=_params(2),
        name="mlp",
    )(x, sc, sh, g2, w1, b1, w2, b2, ln_g, ln_b)


def _split_w_in(w_in):
    pts, acc = [], 0
    for s in IN_SPLITS[:-1]:
        acc += s
        pts.append(acc)
    qkv, gout, a_raw, b_raw, z, xbc, dt_raw, gate_a, gate_b = jnp.split(w_in, pts, axis=1)
    big = jnp.concatenate([z, gout, gate_a, gate_b], axis=1).astype(BF16)
    pad = jnp.zeros((w_in.shape[0], SMALL_N - SM_DT - 2 * SSM_HEADS), w_in.dtype)
    small = jnp.concatenate([a_raw, b_raw, dt_raw, pad], axis=1).astype(BF16)
    return big, qkv.astype(BF16), xbc.astype(BF16), small


def kernel(x, c, ctx, c_ctx, w_mod, b_mod, w_in, gdn_conv_w, gdn_A_log, gdn_dt_bias, gdn_norm_w,
           ssm_conv_w, ssm_conv_b, ssm_A_log, ssm_dt_bias, ssm_D, ssm_norm_w,
           w_proj_gdn, w_proj_ssm, w_out, ln1_g, ln1_b, w_ff1, b_ff1, w_ff2, b_ff2, ln2_g, ln2_b):
    bsz, length, d = x.shape
    ctx_len = ctx.shape[1]
    depth = w_mod.shape[0]
    alpha = float((2 * depth) ** 0.25)
    mod_rows = -(-(bsz + 1) // (2 * SUBLANES)) * (2 * SUBLANES)
    cc = jnp.concatenate([c, c_ctx[None, :], jnp.zeros((mod_rows - bsz - 1, d), c.dtype)], axis=0)
    row2 = lambda t: t.reshape(1, -1)
    tm_lat = 1024
    tm_ctx = ctx_len

    for l in range(depth):
        last = l == depth - 1
        mod = _modulation(cc, w_mod[l], b_mod[l])
        lat = [mod[:bsz, i * d:(i + 1) * d].reshape(bsz, 1, d) for i in range(6)]
        cxm = [jnp.broadcast_to(mod[bsz, i * d:(i + 1) * d].reshape(1, 1, d), (bsz, 1, d)) for i in range(6)]
        w_big, w_qkv, w_xbc, w_small = _split_w_in(w_in[l])
        rows = length // GRID_W

        small = _inproj(x, lat[1], lat[0], w_small, tm_lat, SMALL_N, "inproj_narrow")
        qkv = _proj_conv(x, lat[1], lat[0], w_qkv, gdn_conv_w[l], None, GRID_W, False, 2, tm_lat, GDN_QK, "inproj_qkv")
        xbc = _proj_conv(x, lat[1], lat[0], w_xbc, ssm_conv_w[l], ssm_conv_b[l], rows, True, 0, None, 1024,
                         "inproj_xbc")
        small_c = _inproj(ctx, cxm[1], cxm[0], w_small, tm_ctx, SMALL_N, "inproj_narrow_ctx")
        qkv_c = _proj_conv(ctx, cxm[1], cxm[0], w_qkv, gdn_conv_w[l], None, ctx_len, False, 2, tm_ctx, GDN_QK,
                           "inproj_qkv_ctx")
        xbc_c = _proj_conv(ctx, cxm[1], cxm[0], w_xbc, ssm_conv_w[l], ssm_conv_b[l], ctx_len, False, 0, tm_ctx, 1024,
                           "inproj_xbc_ctx")[:, None]

        o_pair, oc_pair = _gdn_branch(qkv, small, qkv_c, small_c, gdn_A_log[l], gdn_dt_bias[l])
        y_pair, yc_pair = _ssd_branch(xbc, small, xbc_c, small_c, ssm_A_log[l], ssm_dt_bias[l], ssm_D[l])

        nwa = jnp.tile(gdn_norm_w[l], GDN_HEADS).reshape(1, GDN_V)
        nwb = row2(ssm_norm_w[l])
        wpg, wps, wo = w_proj_gdn[l].astype(BF16), w_proj_ssm[l].astype(BF16), w_out[l].astype(BF16)
        w1, w2 = w_ff1[l].astype(BF16), w_ff2[l].astype(BF16)
        merge_args = (nwa, nwb, wpg, wps, wo)
        ln1 = (row2(ln1_g[l]), row2(ln1_b[l]))
        mlp_w = (w1, row2(b_ff1[l]), w2, row2(b_ff2[l]), row2(ln2_g[l]), row2(ln2_b[l]))

        x1 = _merge(x, lat[1], lat[0], w_big, o_pair, y_pair, *merge_args, lat[2], *ln1, alpha, MERGE_ROWS)
        x = _mlp(x1, lat[4], lat[3], lat[5], *mlp_w, alpha, MLP_ROWS)
        if not last:
            c1 = _merge(ctx, cxm[1], cxm[0], w_big, oc_pair, yc_pair, *merge_args, cxm[2], *ln1, alpha, ctx_len)
            ctx = _mlp(c1, cxm[4], cxm[3], cxm[5], *mlp_w, alpha, ctx_len)
    return x
```

```python
import functools

import jax
import jax.numpy as jnp
from jax import lax
from jax.experimental import pallas as pl
from jax.experimental.pallas import tpu as pltpu

F32 = jnp.float32
BF16 = jnp.bfloat16

D_MODEL = 1024
GRID_W = 64
GDN_HEADS = 8
GDN_DK = 128
GDN_DV = 128
GDN_QK = GDN_HEADS * GDN_DK
GDN_V = GDN_HEADS * GDN_DV
GDN_QKV = 2 * GDN_QK + GDN_V
SSM_INNER = 2 * D_MODEL
SSM_HEAD_DIM = 64
SSM_HEADS = SSM_INNER // SSM_HEAD_DIM
SSM_GROUPS = 8
SSM_HPG = SSM_HEADS // SSM_GROUPS
SSM_STATE = 128
SSM_GN = SSM_GROUPS * SSM_STATE
SSM_XBC = SSM_INNER + 2 * SSM_GN
CONV_K = 5
CHUNK = 64
D_FF = 4 * D_MODEL
LN_EPS = 1e-5
NORM_EPS = 1e-6
IN_SPLITS = (GDN_QKV, GDN_V, 2 * GDN_HEADS, 2 * GDN_HEADS, SSM_INNER, SSM_XBC, 2 * SSM_HEADS, D_MODEL, D_MODEL)

LANES = 128
SUBLANES = 8
VMEM_LIMIT_BYTES = 56 * 1024 * 1024

Z_OFF = 0
GOUT_OFF = Z_OFF + SSM_INNER
GA_OFF = GOUT_OFF + GDN_V
GB_OFF = GA_OFF + D_MODEL
BIG_N = GB_OFF + D_MODEL
SM_A = 0
SM_B = SM_A + 2 * GDN_HEADS
SM_DT = SM_B + 2 * GDN_HEADS
SM_DT_B = SM_DT + SSM_HEADS
SMALL_N = LANES

GDN_CHUNKS_PER_STEP = 2
GDN_PREP_ROWS = 512
SCAN_OUT_DTYPE = BF16
MERGE_ROWS = 512
MLP_ROWS = 1024
SSD_GROUPS_PER_STEP = 4
SSD_COLUMNS_PER_STEP = 4


def _params(n_axes):
    return pltpu.CompilerParams(dimension_semantics=("arbitrary",) * n_axes, vmem_limit_bytes=VMEM_LIMIT_BYTES)


def _silu(t):
    return t * jax.nn.sigmoid(t)


def _softplus(t):
    return jnp.maximum(t, 0.0) + jnp.log(1.0 + jnp.exp(-jnp.abs(t)))


def _mm(a, b):
    return jnp.dot(a.astype(BF16), b.astype(BF16), preferred_element_type=F32)


def _mm_nt(a, b):
    return lax.dot_general(a.astype(BF16), b.astype(BF16), (((1,), (1,)), ((), ())), preferred_element_type=F32)


def _mm_tn(a, b):
    return lax.dot_general(a.astype(BF16), b.astype(BF16), (((0,), (0,)), ((), ())), preferred_element_type=F32)


def _tri_masks(lower):
    ii = lax.broadcasted_iota(jnp.int32, (CHUNK, CHUNK), 0)
    jj = lax.broadcasted_iota(jnp.int32, (CHUNK, CHUNK), 1)
    if lower:
        return ii >= jj, ii > jj
    return ii <= jj, ii < jj


def _conv_seg(x, w, seg):
    n = x.shape[0]
    assert seg & (seg - 1) == 0 and n % seg == 0
    pos = lax.broadcasted_iota(jnp.int32, (n, 1), 0) & (seg - 1)
    out = None
    for j in range(CONV_K):
        d = j - CONV_K // 2
        if d == 0:
            term = x * w[j:j + 1, :]
        else:
            shifted = pltpu.roll(x, shift=(-d) % n, axis=0)
            valid = (pos + d >= 0) & (pos + d < seg)
            term = jnp.where(valid, shifted, 0.0) * w[j:j + 1, :]
        out = term if out is None else out + term
    return out


def _mod_kernel(c_ref, w_ref, b_ref, o_ref):
    o_ref[...] = _mm(_silu(c_ref[...]), w_ref[...]) + b_ref[...]


def _modulation(cc, w, b):
    rows, d = cc.shape
    n = w.shape[1]
    tn = 1536
    return pl.pallas_call(
        _mod_kernel,
        out_shape=jax.ShapeDtypeStruct((rows, n), F32),
        grid=(n // tn,),
        in_specs=[pl.BlockSpec((rows, d), lambda j: (0, 0)),
                  pl.BlockSpec((d, tn), lambda j: (0, j)),
                  pl.BlockSpec((1, tn), lambda j: (0, j))],
        out_specs=pl.BlockSpec((rows, tn), lambda j: (0, j)),
        compiler_params=_params(1),
        name="modulation",
    )(cc, w, b.reshape(1, n))


def _inproj_kernel(x_ref, sc_ref, sh_ref, w_ref, o_ref):
    h = x_ref[...] * (1.0 + sc_ref[...]) + sh_ref[...]
    o_ref[...] = jnp.dot(h.astype(BF16), w_ref[...], preferred_element_type=F32)


def _inproj(x, sc, sh, w, tm, tn, name):
    bsz, length, d = x.shape
    n = w.shape[1]
    return pl.pallas_call(
        _inproj_kernel,
        out_shape=jax.ShapeDtypeStruct((bsz, length, n), F32),
        grid=(bsz, length // tm, n // tn),
        in_specs=[pl.BlockSpec((None, tm, d), lambda b, i, j: (b, i, 0)),
                  pl.BlockSpec((None, 1, d), lambda b, i, j: (b, 0, 0)),
                  pl.BlockSpec((None, 1, d), lambda b, i, j: (b, 0, 0)),
                  pl.BlockSpec((d, tn), lambda b, i, j: (0, j))],
        out_specs=pl.BlockSpec((None, tm, tn), lambda b, i, j: (b, i, j)),
        compiler_params=_params(3),
        name=name,
    )(x, sc, sh, w)


def _proj_conv_kernel(*refs, seg, cols, has_bias, norm_tiles):
    x_ref, sc_ref, sh_ref, w_ref, cw_ref = refs[:5]
    k = 5
    cb_ref = refs[k] if has_bias else None
    k += has_bias
    perm_ref = refs[k] if cols else None
    k += cols
    o_ref = refs[k]
    kt_ref = refs[k + 1] if norm_tiles else None
    h_ref = refs[-1]
    j = pl.program_id(2)
    n_tok = h_ref.shape[0]

    @pl.when(j == 0)
    def _():
        x = x_ref[...]
        if cols:
            x = x.reshape(n_tok, x.shape[2])
        h = (x * (1.0 + sc_ref[...]) + sh_ref[...]).astype(BF16)
        if cols:
            h = jnp.dot(perm_ref[...], h, preferred_element_type=F32).astype(BF16)
        h_ref[...] = h

    y = jnp.dot(h_ref[...], w_ref[...], preferred_element_type=F32)
    y = _conv_seg(y, cw_ref[...], seg)
    if has_bias:
        y = y + cb_ref[...]
    y = _silu(y)

    def store(t):
        o_ref[...] = t.reshape(o_ref.shape)

    if norm_tiles:
        @pl.when(j < norm_tiles)
        def _():
            scale = jnp.where(j == 0, GDN_DK ** -0.5, 1.0)
            heads = []
            for hh in range(y.shape[1] // GDN_DK):
                yh = y[:, hh * GDN_DK:(hh + 1) * GDN_DK]
                heads.append(yh * (lax.rsqrt(jnp.sum(yh * yh, axis=1, keepdims=True) + NORM_EPS) * scale))
            store(jnp.concatenate(heads, axis=1))

            @pl.when(j == 1)
            def _():
                for hh, yh in enumerate(heads):
                    kt_ref[hh] = yh.T

        @pl.when(j >= norm_tiles)
        def _():
            store(y)
    else:
        store(y)


def _proj_conv(x, sc, sh, w, conv_w, conv_b, seg, cols, norm_tiles, tm, tn, name):
    bsz, length, d = x.shape
    n = w.shape[1]
    has_bias = conv_b is not None
    if cols:
        rows, cps = length // GRID_W, SUBLANES
        assert seg == rows and seg & (seg - 1) == 0
        n_tok = rows * cps
        x_in = x.reshape(bsz, rows, GRID_W, d)
        x_spec = pl.BlockSpec((None, rows, cps, d), lambda b, i, j: (b, 0, i, 0))
        out_shape = jax.ShapeDtypeStruct((bsz, GRID_W, rows, n), F32)
        out_spec = pl.BlockSpec((None, cps, rows, tn), lambda b, i, j: (b, i, 0, j))
        grid = (bsz, GRID_W // cps, n // tn)
    else:
        n_tok = tm
        x_in = x
        x_spec = pl.BlockSpec((None, tm, d), lambda b, i, j: (b, i, 0))
        out_shape = jax.ShapeDtypeStruct((bsz, length, n), F32)
        out_spec = pl.BlockSpec((None, tm, tn), lambda b, i, j: (b, i, j))
        grid = (bsz, length // tm, n // tn)
    mod = pl.BlockSpec((None, 1, d), lambda b, i, j: (b, 0, 0))
    in_specs = [x_spec, mod, mod, pl.BlockSpec((d, tn), lambda b, i, j: (0, j)),
                pl.BlockSpec((CONV_K, tn), lambda b, i, j: (0, j))]
    args = [x_in, sc, sh, w, conv_w]
    if has_bias:
        in_specs.append(pl.BlockSpec((1, tn), lambda b, i, j: (0, j)))
        args.append(conv_b.reshape(1, n))
    if cols:
        dst = jnp.arange(n_tok)
        src = (dst % rows) * cps + dst // rows
        in_specs.append(pl.BlockSpec((n_tok, n_tok), lambda b, i, j: (0, 0), pipeline_mode=pl.Buffered(1)))
        args.append((src[:, None] == jnp.arange(n_tok)[None, :]).astype(BF16))
    if norm_tiles:
        assert norm_tiles == 2 and tn == GDN_QK and not cols
        out_shape = (out_shape, jax.ShapeDtypeStruct((bsz, GDN_HEADS, GDN_DK, length), F32))
        out_spec = (out_spec, pl.BlockSpec((None, GDN_HEADS, GDN_DK, tm), lambda b, i, j: (b, 0, 0, i)))
    return pl.pallas_call(
        functools.partial(_proj_conv_kernel, seg=seg, cols=cols, has_bias=has_bias, norm_tiles=norm_tiles),
        out_shape=out_shape,
        grid=grid,
        in_specs=in_specs,
        out_specs=out_spec,
        scratch_shapes=[pltpu.VMEM((n_tok, d), BF16)],
        compiler_params=_params(3),
        name=name,
    )(*args)


def _chunk_cumsums(t):
    ii = lax.broadcasted_iota(jnp.int32, (2 * CHUNK, CHUNK), 0)
    jj = lax.broadcasted_iota(jnp.int32, (2 * CHUNK, CHUNK), 1)
    ones = ((ii < CHUNK) & (ii >= jj)) | ((ii >= CHUNK) & (ii - CHUNK <= jj))
    tri = jnp.where(ones, 1.0, 0.0).astype(BF16)
    hi = t.astype(BF16)
    r1 = t - hi.astype(F32)
    mid = r1.astype(BF16)
    lo = (r1 - mid.astype(F32)).astype(BF16)
    w = t.shape[1]
    sums = jnp.dot(tri, jnp.concatenate([hi, mid, lo], axis=1), preferred_element_type=F32)
    sums = sums[:, :w] + sums[:, w:2 * w] + sums[:, 2 * w:]
    return sums[:CHUNK], sums[CHUNK:]


def _gdn_prep_kernel(s_ref, alog_ref, dtb_ref, o_ref, *, rows):
    s = s_ref[...]
    lane = lax.broadcasted_iota(jnp.int32, (1, LANES), 1)
    g = -jnp.exp(alog_ref[...]) * _softplus(s + dtb_ref[...])
    beta = jax.nn.sigmoid(s)
    for c in range(rows // CHUNK):
        sl = slice(c * CHUNK, (c + 1) * CHUNK)
        fwd, bwd = _chunk_cumsums(g[sl, :])
        gc = jnp.where(lane < SM_A + GDN_HEADS, fwd, bwd)
        o_ref[sl, :] = jnp.where(lane < SM_B, gc, beta[sl, :])


def _gdn_prep(small, a_log, dt_bias, tb):
    bsz, length, _ = small.shape
    pad = lambda t: jnp.pad(t.reshape(1, -1).astype(F32), ((0, 0), (SM_A, LANES - SM_A - 2 * GDN_HEADS)))
    return pl.pallas_call(
        functools.partial(_gdn_prep_kernel, rows=tb),
        out_shape=jax.ShapeDtypeStruct((bsz, length, LANES), F32),
        grid=(bsz, length // tb),
        in_specs=[pl.BlockSpec((None, tb, LANES), lambda b, i: (b, i, 0)),
                  pl.BlockSpec((1, LANES), lambda b, i: (0, 0)),
                  pl.BlockSpec((1, LANES), lambda b, i: (0, 0))],
        out_specs=pl.BlockSpec((None, tb, LANES), lambda b, i: (b, i, 0)),
        compiler_params=_params(2),
        name="gdn_prep",
    )(small, pad(a_log), pad(dt_bias))


def _tri_inverse(mats):
    ii = lax.broadcasted_iota(jnp.int32, (CHUNK, CHUNK), 0)
    jj = lax.broadcasted_iota(jnp.int32, (CHUNK, CHUNK), 1)
    eye = jnp.where(ii == jj, 1.0, 0.0)
    zero = jnp.zeros((CHUNK, CHUNK), F32)
    right = lax.broadcasted_iota(jnp.int32, (CHUNK, 2 * CHUNK), 1) >= CHUNK
    zs = [jnp.concatenate([zero, eye], axis=1) + _mm(a, jnp.concatenate([a, -eye], axis=1)) for a in mats]
    power = 2
    while power < CHUNK:
        ps = [_mm(z[:, :CHUNK], z) for z in zs]
        zs = [p + jnp.where(right, z, 0.0) for p, z in zip(ps, zs)]
        power *= 2
    return zs


def _gdn_local(items, masks):
    qs, ks, vs, kts, gcols, grows, bcols, lowers = zip(*items)
    n = len(items)
    incl = [masks[lo][0] for lo in lowers]
    strict = [masks[lo][1] for lo in lowers]
    glast = [grows[i][:, CHUNK - 1:CHUNK] if lowers[i] else grows[i][:, 0:1] for i in range(n)]
    kq = [_mm(jnp.concatenate([ks[i].astype(BF16), qs[i].astype(BF16)], axis=0), kts[i]) for i in range(n)]
    dec = [jnp.exp(jnp.where(incl[i], gcols[i] - grows[i], 0.0)) for i in range(n)]
    a_mats = [jnp.where(strict[i], bcols[i] * kq[i][:CHUNK] * dec[i], 0.0) for i in range(n)]
    t_inv = _tri_inverse(a_mats)
    eg = [jnp.exp(g) for g in gcols]
    rhs = [jnp.concatenate([(bcols[i] * eg[i]) * ks[i], bcols[i] * vs[i]], axis=1) for i in range(n)]
    pad = jnp.zeros((CHUNK, GDN_DK + GDN_DV), F32)
    wu = [_mm(t, jnp.concatenate([pad, r], axis=0)) for t, r in zip(t_inv, rhs)]
    kd_wu = [_mm(kts[i] * jnp.exp(glast[i] - grows[i]), wu[i]) for i in range(n)]
    qk_wu = [_mm(jnp.where(incl[i], kq[i][CHUNK:] * dec[i], 0.0), wu[i]) for i in range(n)]
    lhs = [jnp.concatenate([kd_wu[i][:, :GDN_DK].astype(BF16), (qs[i] * eg[i] - qk_wu[i][:, :GDN_DK]).astype(BF16)],
                           axis=0) for i in range(n)]
    return [(lhs[i], kd_wu[i][:, GDN_DK:], qk_wu[i][:, GDN_DK:], jnp.exp(glast[i])) for i in range(n)]


def _gdn_on_state(local, states):
    n = len(local)
    on_state = [_mm(local[i][0], states[i]) for i in range(n)]
    return [(on_state[i][GDN_DK:] + local[i][2], local[i][3] * states[i] - on_state[i][:GDN_DK] + local[i][1])
            for i in range(n)]


def _lane_column(t, idx, lane):
    if isinstance(idx, int):
        return t[:, idx:idx + 1]
    return jnp.sum(jnp.where(lane == idx, t, 0.0), axis=1, keepdims=True)


def _gdn_kernel(qf_ref, kf_ref, vf_ref, ktf_ref, qb_ref, kb_ref, vb_ref, ktb_ref, pf_ref, pb_ref, ptf_ref, ptb_ref,
                s0_ref, of_ref, ob_ref, sout_ref, state_ref, *, n_steps, hs, nc):
    step = pl.program_id(2)
    head0 = 0 if hs == GDN_HEADS else pl.program_id(1) * hs

    @pl.when(step == 0)
    def _():
        state_ref[...] = s0_ref[...]

    lane = lax.broadcasted_iota(jnp.int32, (1, LANES), 1)

    def heads_of(q_ref, k_ref, v_ref):
        q, k, v = q_ref[...], k_ref[...], v_ref[...]
        return [(q[:, hh * GDN_DK:(hh + 1) * GDN_DK], k[:, hh * GDN_DK:(hh + 1) * GDN_DK],
                 v[:, hh * GDN_DV:(hh + 1) * GDN_DV]) for hh in range(hs)]

    heads_f = heads_of(qf_ref, kf_ref, vf_ref)
    heads_b = heads_of(qb_ref, kb_ref, vb_ref)
    pf = pf_ref[...]
    pb = pb_ref[...]
    masks = {True: _tri_masks(True), False: _tri_masks(False)}
    cols = []
    for hh in range(hs):
        head = head0 + hh
        cols.append((_lane_column(pf, SM_A + head, lane), _lane_column(pf, SM_B + head, lane),
                     _lane_column(pb, SM_A + GDN_HEADS + head, lane), _lane_column(pb, SM_B + GDN_HEADS + head, lane)))
    items = []
    for c in range(nc):
        cb = nc - 1 - c
        sl_f = slice(c * CHUNK, (c + 1) * CHUNK)
        sl_b = slice(cb * CHUNK, (cb + 1) * CHUNK)
        for hh in range(hs):
            head = head0 + hh
            gcol_f, bcol_f, gcol_b, bcol_b = cols[hh]
            qf, kf, vf = heads_f[hh]
            qb, kb, vb = heads_b[hh]
            grow_f = ptf_ref[c, pl.ds(SM_A + head, 1), :]
            grow_b = ptb_ref[cb, pl.ds(SM_A + GDN_HEADS + head, 1), :]
            items.append((qf[sl_f], kf[sl_f], vf[sl_f], ktf_ref[hh, :, sl_f], gcol_f[sl_f], grow_f, bcol_f[sl_f], True))
            items.append((qb[sl_b], kb[sl_b], vb[sl_b], ktb_ref[hh, :, sl_b], gcol_b[sl_b], grow_b, bcol_b[sl_b], False))
    local = _gdn_local(items, masks)
    states = [state_ref[hh, d] for hh in range(hs) for d in range(2)]
    for c in range(nc):
        cb = nc - 1 - c
        results = _gdn_on_state(local[c * 2 * hs:(c + 1) * 2 * hs], states)
        states = [r[1] for r in results]
        for hh in range(hs):
            hl = slice(hh * GDN_DV, (hh + 1) * GDN_DV)
            of_ref[c * CHUNK:(c + 1) * CHUNK, hl] = results[2 * hh][0].astype(of_ref.dtype)
            ob_ref[cb * CHUNK:(cb + 1) * CHUNK, hl] = results[2 * hh + 1][0].astype(ob_ref.dtype)
    for hh in range(hs):
        state_ref[hh, 0] = states[2 * hh]
        state_ref[hh, 1] = states[2 * hh + 1]

    @pl.when(step == n_steps - 1)
    def _():
        sout_ref[...] = state_ref[...]


def _gdn_scan(qkv, kt, p, pt, s0, hs, nc):
    bsz, length, _ = qkv.shape
    tb = nc * CHUNK
    n_steps = length // tb
    width = hs * GDN_DK
    qoff, koff, voff = 0, GDN_QK // width, 2 * GDN_QK // width
    fwd = lambda off: pl.BlockSpec((None, tb, width), lambda b, h, s: (b, s, off + h))
    bwd = lambda off: pl.BlockSpec((None, tb, width), lambda b, h, s: (b, n_steps - 1 - s, off + h))
    state_spec = pl.BlockSpec((None, hs, 2, GDN_DK, GDN_DV), lambda b, h, s: (b, h, 0, 0, 0))
    return pl.pallas_call(
        functools.partial(_gdn_kernel, n_steps=n_steps, hs=hs, nc=nc),
        out_shape=(jax.ShapeDtypeStruct((bsz, length, GDN_V), SCAN_OUT_DTYPE),
                   jax.ShapeDtypeStruct((bsz, length, GDN_V), SCAN_OUT_DTYPE),
                   jax.ShapeDtypeStruct((bsz, GDN_HEADS, 2, GDN_DK, GDN_DV), F32)),
        grid=(bsz, GDN_HEADS // hs, n_steps),
        in_specs=[fwd(qoff), fwd(koff), fwd(voff), pl.BlockSpec((None, hs, GDN_DK, tb), lambda b, h, s: (b, h, 0, s)),
                  bwd(qoff), bwd(koff), bwd(voff),
                  pl.BlockSpec((None, hs, GDN_DK, tb), lambda b, h, s: (b, h, 0, n_steps - 1 - s)),
                  pl.BlockSpec((None, tb, LANES), lambda b, h, s: (b, s, 0)),
                  pl.BlockSpec((None, tb, LANES), lambda b, h, s: (b, n_steps - 1 - s, 0)),
                  pl.BlockSpec((None, nc, 4 * SUBLANES, CHUNK), lambda b, h, s: (b, s, 0, 0)),
                  pl.BlockSpec((None, nc, 4 * SUBLANES, CHUNK), lambda b, h, s: (b, n_steps - 1 - s, 0, 0)),
                  state_spec],
        out_specs=(pl.BlockSpec((None, tb, width), lambda b, h, s: (b, s, h)),
                   pl.BlockSpec((None, tb, width), lambda b, h, s: (b, n_steps - 1 - s, h)),
                   state_spec),
        scratch_shapes=[pltpu.VMEM((hs, 2, GDN_DK, GDN_DV), F32)],
        compiler_params=_params(3),
        name="gdn_scan",
    )(qkv, qkv, qkv, kt, qkv, qkv, qkv, kt, p, p, pt, pt, s0)


def _gdn_branch(qkv, kt, small, qkv_c, kt_c, small_c, a_log, dt_bias):
    bsz = qkv.shape[0]
    s0 = jnp.zeros((bsz, GDN_HEADS, 2, GDN_DK, GDN_DV), F32)

    def rows_t(p):
        chunks = p[:, :, :4 * SUBLANES].reshape(bsz, p.shape[1] // CHUNK, CHUNK, 4 * SUBLANES)
        return jnp.swapaxes(chunks, 2, 3)

    ctx_len = qkv_c.shape[1]
    p_c = _gdn_prep(small_c, a_log, dt_bias, ctx_len)
    oc_f, oc_b, s_ctx = _gdn_scan(qkv_c, kt_c, p_c, rows_t(p_c), s0, GDN_HEADS, GDN_CHUNKS_PER_STEP)
    p = _gdn_prep(small, a_log, dt_bias, GDN_PREP_ROWS)
    o_f, o_b, _ = _gdn_scan(qkv, kt, p, rows_t(p), s_ctx, GDN_HEADS, GDN_CHUNKS_PER_STEP)
    return (o_f, o_b), (oc_f, oc_b)


SSD_QUANTITIES = 4
SSD_HEADS_PER_STEP = SSD_GROUPS_PER_STEP * SSM_HPG
SSD_STEP_WIDTH = SSD_HEADS_PER_STEP * SSM_HEAD_DIM
BF16_PIECES = 3


def _ssd_prep_kernel(s_ref, alog_ref, dtb_ref, o_ref, *, rows):
    lane = lax.broadcasted_iota(jnp.int32, (1, LANES), 1)
    fwd_lane = lane < SM_DT_B
    neg_a = -jnp.exp(alog_ref[...])
    for seg in range(o_ref.shape[0]):
        dt = _softplus(s_ref[:, seg * LANES:(seg + 1) * LANES] + dtb_ref[...])
        la = dt * neg_a
        o_ref[seg, :, 0:LANES] = dt
        for c in range(rows // CHUNK):
            sl = slice(c * CHUNK, (c + 1) * CHUNK)
            fwd, bwd = _chunk_cumsums(la[sl, :])
            gc = jnp.where(fwd_lane, fwd, bwd)
            g_last = jnp.where(fwd_lane, gc[CHUNK - 1:CHUNK, :], gc[0:1, :])
            o_ref[seg, sl, LANES:2 * LANES] = gc
            o_ref[seg, sl, 2 * LANES:3 * LANES] = jnp.exp(gc)
            o_ref[seg, sl, 3 * LANES:4 * LANES] = dt[sl, :] * jnp.exp(g_last - gc)


def _ssd_prep(small_cols, a_log, dt_bias, n_seg):
    bsz, rows, _ = small_cols.shape
    sps = min(n_seg, SUBLANES)
    pad = lambda t: jnp.pad(t.reshape(1, -1).astype(F32), ((0, 0), (SM_DT, LANES - SM_DT - 2 * SSM_HEADS)))
    return pl.pallas_call(
        functools.partial(_ssd_prep_kernel, rows=rows),
        out_shape=jax.ShapeDtypeStruct((bsz, n_seg, rows, SSD_QUANTITIES * LANES), F32),
        grid=(bsz, n_seg // sps),
        in_specs=[pl.BlockSpec((None, rows, sps * LANES), lambda b, c: (b, 0, c)),
                  pl.BlockSpec((1, LANES), lambda b, c: (0, 0)),
                  pl.BlockSpec((1, LANES), lambda b, c: (0, 0))],
        out_specs=pl.BlockSpec((None, sps, rows, SSD_QUANTITIES * LANES), lambda b, c: (b, c, 0, 0)),
        compiler_params=_params(2),
        name="ssd_prep",
    )(small_cols, pad(a_log), pad(dt_bias))


def _ssd_layouts(q):
    bsz, n_seg, rows, _ = q.shape
    steps = SSM_GROUPS // SSD_GROUPS_PER_STEP
    q = q.reshape(bsz, n_seg, rows, SSD_QUANTITIES, LANES)[..., SM_DT:SM_DT + 2 * SSM_HEADS]
    q = q.reshape(bsz, n_seg, rows, SSD_QUANTITIES, 2, steps, SSD_HEADS_PER_STEP)
    out = []
    for d in range(2):
        qd = q[:, :, :, :, d]
        cols = jnp.transpose(qd, (0, 1, 4, 2, 3, 5)).reshape(bsz, n_seg, steps, rows, SSD_QUANTITIES * SSD_HEADS_PER_STEP)
        as_rows = lambda t: jnp.transpose(
            t.reshape(bsz, n_seg, rows // CHUNK, CHUNK, steps, SSD_HEADS_PER_STEP),
            (0, 1, 4, 2, 5, 3)).reshape(bsz, n_seg, steps, rows // CHUNK, SSD_STEP_WIDTH)
        out.append((cols, jnp.concatenate([as_rows(qd[:, :, :, 1]), as_rows(qd[:, :, :, 0])], axis=-1)))
    return out


def _ssd_expand_matrix():
    k = SSD_QUANTITIES * SSD_HEADS_PER_STEP
    src = jnp.arange(k)
    dst = jnp.arange(SSD_STEP_WIDTH, k * SSM_HEAD_DIM) // SSM_HEAD_DIM
    one = (src[:, None] == dst[None, :]).astype(BF16)
    return jnp.concatenate([one] * BF16_PIECES, axis=0)


def _ssd_chunks(items, incl, block_diag):
    n = len(items)
    cb = [_mm_nt(it[4], jnp.concatenate([it[3].astype(BF16)] * SSM_HPG, axis=0)) for it in items]
    inter = [_mm(it[4], it[9]) for it in items]
    lhs = []
    for i, it in enumerate(items):
        m = incl[it[10]]
        lhs.append(jnp.where(m, cb[i] * jnp.exp(jnp.where(m, it[5] - it[6], 0.0)) * it[1], 0.0))
    rhs = [jnp.where(block_diag, jnp.concatenate([it[0].astype(BF16)] * SSM_HPG, axis=0), 0.0) for it in items]
    intra = [_mm(lhs[i], rhs[i]) for i in range(n)]
    upd = [_mm_tn(it[3], it[2]) for it in items]
    out = []
    for i, it in enumerate(items):
        y = intra[i] + inter[i] * it[7]
        if it[11] is not None:
            y = y + it[11] * it[0]
        out.append((y, it[8] * it[9] + upd[i]))
    return out


def _ssd_kernel(xf_ref, bf_ref, cf_ref, xb_ref, bb_ref, cb_ref, colf_ref, colb_ref, growf_ref, growb_ref,
                e_ref, d_ref, s0_ref, yf_ref, yb_ref, sout_ref, state_ref, *, rows, n_seg, gs):
    step = pl.program_id(2)

    @pl.when(step == 0)
    def _():
        state_ref[...] = s0_ref[...]

    gw = SSM_HPG * SSM_HEAD_DIM
    width = gs * gw

    def expand(col_ref):
        c = col_ref[...]
        hi = c.astype(BF16)
        r1 = c - hi.astype(F32)
        mid = r1.astype(BF16)
        lo = (r1 - mid.astype(F32)).astype(BF16)
        ex = jnp.dot(jnp.concatenate([hi, mid, lo], axis=1), e_ref[...], preferred_element_type=F32)
        return [ex[:, q * width:(q + 1) * width] for q in range(SSD_QUANTITIES - 1)]

    xf, xb = xf_ref[...], xb_ref[...]
    gcx_f, egx_f, q4x_f = expand(colf_ref)
    gcx_b, egx_b, q4x_b = expand(colb_ref)
    xdec_f, xdec_b = xf * q4x_f, xb * q4x_b
    bmf, cmf, bmb, cmb = bf_ref[...], cf_ref[...], bb_ref[...], cb_ref[...]
    d_all = d_ref[...]

    row = lax.broadcasted_iota(jnp.int32, (CHUNK, gw), 0)
    tok = lax.broadcasted_iota(jnp.int32, (CHUNK, gw), 1) & (CHUNK - 1)
    incl = {True: row >= tok, False: row <= tok}
    block_diag = (lax.shift_right_logical(lax.broadcasted_iota(jnp.int32, (gw, gw), 0), CHUNK.bit_length() - 1)
                  == lax.shift_right_logical(lax.broadcasted_iota(jnp.int32, (gw, gw), 1), SSM_HEAD_DIM.bit_length() - 1))

    n_chunks = rows // CHUNK
    states = [[state_ref[gg, 0], state_ref[gg, 1]] for gg in range(gs)]
    for c in range(n_chunks):
        cb = n_chunks - 1 - c
        sl_f = slice(c * CHUNK, (c + 1) * CHUNK)
        sl_b = slice(cb * CHUNK, (cb + 1) * CHUNK)
        last_f = slice(c * CHUNK + CHUNK - 1, (c + 1) * CHUNK)
        last_b = slice(cb * CHUNK, cb * CHUNK + 1)
        items = []
        for gg in range(gs):
            xl = slice(gg * gw, (gg + 1) * gw)
            nl = slice(gg * SSM_STATE, (gg + 1) * SSM_STATE)
            dl = slice(width + gg * gw, width + (gg + 1) * gw)
            items.append((xf[sl_f, xl], growf_ref[c:c + 1, dl], xdec_f[sl_f, xl], bmf[sl_f, nl], cmf[sl_f, nl],
                          gcx_f[sl_f, xl], growf_ref[c:c + 1, xl], egx_f[sl_f, xl], egx_f[last_f, xl], states[gg][0], True,
                          d_all[:, xl]))
            items.append((xb[sl_b, xl], growb_ref[cb:cb + 1, dl], xdec_b[sl_b, xl], bmb[sl_b, nl], cmb[sl_b, nl],
                          gcx_b[sl_b, xl], growb_ref[cb:cb + 1, xl], egx_b[sl_b, xl], egx_b[last_b, xl], states[gg][1], False,
                          None))
        results = _ssd_chunks(items, incl, block_diag)
        for gg in range(gs):
            xl = slice(gg * gw, (gg + 1) * gw)
            yf_ref[sl_f, xl] = results[2 * gg][0].astype(yf_ref.dtype)
            yb_ref[sl_b, xl] = results[2 * gg + 1][0].astype(yb_ref.dtype)
            states[gg] = [results[2 * gg][1], results[2 * gg + 1][1]]
    for gg in range(gs):
        state_ref[gg, 0] = states[gg][0]
        state_ref[gg, 1] = states[gg][1]

    @pl.when(step == n_seg - 1)
    def _():
        sout_ref[...] = state_ref[...]


def _ssd_scan(xbc, layouts, expand, d_exp, s0):
    bsz, n_seg, rows, _ = xbc.shape
    gs = SSD_GROUPS_PER_STEP
    xw, nw = SSD_STEP_WIDTH, gs * SSM_STATE
    boff, coff = SSM_INNER // nw, (SSM_INNER + SSM_GN) // nw
    (cols_f, grow_f), (cols_b, grow_b) = layouts
    n_cols = cols_f.shape[-1]
    seg_f = lambda s: s
    seg_b = lambda s: n_seg - 1 - s
    xspec = lambda seg: pl.BlockSpec((None, None, rows, xw), lambda b, g, s: (b, seg(s), 0, g))
    nspec = lambda seg, off: pl.BlockSpec((None, None, rows, nw), lambda b, g, s: (b, seg(s), 0, off + g))
    cspec = lambda seg: pl.BlockSpec((None, None, None, rows, n_cols), lambda b, g, s: (b, seg(s), g, 0, 0))
    rspec = lambda seg: pl.BlockSpec((None, None, None, rows // CHUNK, 2 * xw), lambda b, g, s: (b, seg(s), g, 0, 0))
    state_spec = pl.BlockSpec((None, gs, 2, SSM_STATE, SSM_HPG * SSM_HEAD_DIM), lambda b, g, s: (b, g, 0, 0, 0))
    y_shape = jax.ShapeDtypeStruct((bsz, n_seg, rows, SSM_INNER), SCAN_OUT_DTYPE)
    return pl.pallas_call(
        functools.partial(_ssd_kernel, rows=rows, n_seg=n_seg, gs=gs),
        out_shape=(y_shape, y_shape,
                   jax.ShapeDtypeStruct((bsz, SSM_GROUPS, 2, SSM_STATE, SSM_HPG * SSM_HEAD_DIM), F32)),
        grid=(bsz, SSM_GROUPS // gs, n_seg),
        in_specs=[xspec(seg_f), nspec(seg_f, boff), nspec(seg_f, coff),
                  xspec(seg_b), nspec(seg_b, boff), nspec(seg_b, coff),
                  cspec(seg_f), cspec(seg_b), rspec(seg_f), rspec(seg_b),
                  pl.BlockSpec(expand.shape, lambda b, g, s: (0, 0)),
                  pl.BlockSpec((1, xw), lambda b, g, s: (0, g)), state_spec],
        out_specs=(xspec(seg_f), xspec(seg_b), state_spec),
        scratch_shapes=[pltpu.VMEM((gs, 2, SSM_STATE, SSM_HPG * SSM_HEAD_DIM), F32)],
        compiler_params=_params(3),
        name="ssd_scan",
    )(xbc, xbc, xbc, xbc, xbc, xbc, cols_f, cols_b, grow_f, grow_b, expand, d_exp, s0)


def _ssd_branch(xbc, small, xbc_c, small_c, a_log, dt_bias, d_skip):
    bsz, n_seg, rows, _ = xbc.shape
    assert rows % CHUNK == 0 and xbc_c.shape[2] % CHUNK == 0
    d_exp = jnp.repeat(d_skip.astype(F32), SSM_HEAD_DIM).reshape(1, SSM_INNER)
    s0 = jnp.zeros((bsz, SSM_GROUPS, 2, SSM_STATE, SSM_HPG * SSM_HEAD_DIM), F32)
    expand = _ssd_expand_matrix()
    yc_f, yc_b, s_ctx = _ssd_scan(xbc_c, _ssd_layouts(_ssd_prep(small_c, a_log, dt_bias, 1)), expand, d_exp, s0)
    q = _ssd_prep(small.reshape(bsz, rows, n_seg * SMALL_N), a_log, dt_bias, n_seg)
    steps, step_rows = n_seg // SSD_COLUMNS_PER_STEP, SSD_COLUMNS_PER_STEP * rows
    merged = lambda t: t.reshape(bsz, steps, step_rows, t.shape[-1])
    y_f, y_b, _ = _ssd_scan(merged(xbc), _ssd_layouts(merged(q)), expand, d_exp, s_ctx)
    to_raster = lambda y: jnp.swapaxes(y.reshape(bsz, n_seg, rows, SSM_INNER), 1, 2).reshape(bsz, rows * n_seg, SSM_INNER)
    return (to_raster(y_f), to_raster(y_b)), (yc_f[:, 0], yc_b[:, 0])


def _layer_norm(r, g, b):
    mu = jnp.mean(r, axis=1, keepdims=True)
    var = jnp.mean(jnp.square(r - mu), axis=1, keepdims=True)
    return (r - mu) * lax.rsqrt(var + LN_EPS) * g + b


def _merge_kernel(x_ref, sc_ref, sh_ref, wg_ref, of_ref, ob_ref, yf_ref, yb_ref,
                  nwa_ref, nwb_ref, wpg_ref, wps_ref, wout_ref, g1_ref, lng_ref, lnb_ref, o_ref, *, alpha):
    x = x_ref[...]
    h = (x * (1.0 + sc_ref[...]) + sh_ref[...]).astype(BF16)
    gate = lambda off, width: jnp.dot(h, wg_ref[:, off:off + width], preferred_element_type=F32)
    o = of_ref[...].astype(F32) + ob_ref[...].astype(F32)
    normed = []
    for hh in range(GDN_HEADS):
        oh = o[:, hh * GDN_DV:(hh + 1) * GDN_DV]
        normed.append(oh * lax.rsqrt(jnp.mean(oh * oh, axis=1, keepdims=True) + NORM_EPS))
    y_a = jnp.concatenate(normed, axis=1) * nwa_ref[...] * _silu(gate(GOUT_OFF, GDN_V))
    t = (yf_ref[...].astype(F32) + yb_ref[...].astype(F32)) * _silu(gate(Z_OFF, SSM_INNER))
    gw = SSM_INNER // SSM_GROUPS
    normed = []
    for g in range(SSM_GROUPS):
        tg = t[:, g * gw:(g + 1) * gw]
        normed.append(tg * lax.rsqrt(jnp.mean(tg * tg, axis=1, keepdims=True) + NORM_EPS))
    y_b = jnp.concatenate(normed, axis=1) * nwb_ref[...]
    d = x.shape[1]
    mix = (jax.nn.sigmoid(gate(GA_OFF, d)) * jnp.dot(y_a.astype(BF16), wpg_ref[...], preferred_element_type=F32)
           + jax.nn.sigmoid(gate(GB_OFF, d)) * jnp.dot(y_b.astype(BF16), wps_ref[...], preferred_element_type=F32))
    out = jnp.dot(mix.astype(BF16), wout_ref[...], preferred_element_type=F32)
    o_ref[...] = _layer_norm(alpha * x + g1_ref[...] * out, lng_ref[...], lnb_ref[...])


def _merge(x, sc, sh, w_gates, o_pair, y_pair, nwa, nwb, wpg, wps, wout, g1, ln_g, ln_b, alpha, tm):
    bsz, length, d = x.shape
    row = lambda width: pl.BlockSpec((None, tm, width), lambda b, i: (b, i, 0))
    const = lambda shape: pl.BlockSpec(shape, lambda b, i: (0,) * len(shape), pipeline_mode=pl.Buffered(1))
    mod = pl.BlockSpec((None, 1, d), lambda b, i: (b, 0, 0))
    return pl.pallas_call(
        functools.partial(_merge_kernel, alpha=alpha),
        out_shape=jax.ShapeDtypeStruct((bsz, length, d), F32),
        grid=(bsz, length // tm),
        in_specs=[row(d), mod, mod, const((d, BIG_N)),
                  row(GDN_V), row(GDN_V), row(SSM_INNER), row(SSM_INNER),
                  const((1, GDN_V)), const((1, SSM_INNER)), const((GDN_V, d)), const((SSM_INNER, d)), const((d, d)),
                  mod, const((1, d)), const((1, d))],
        out_specs=row(d),
        compiler_params=_params(2),
        name="merge",
    )(x, sc, sh, w_gates, o_pair[0], o_pair[1], y_pair[0], y_pair[1], nwa, nwb, wpg, wps, wout, g1, ln_g, ln_b)


def _mlp_kernel(x_ref, sc_ref, sh_ref, g2_ref, w1_ref, b1_ref, w2_ref, b2_ref, lng_ref, lnb_ref, o_ref, *, alpha):
    x = x_ref[...]
    h = (x * (1.0 + sc_ref[...]) + sh_ref[...]).astype(BF16)
    acc = None
    tf = D_MODEL
    for c in range(D_FF // tf):
        u = jnp.dot(h, w1_ref[:, c * tf:(c + 1) * tf], preferred_element_type=F32) + b1_ref[:, c * tf:(c + 1) * tf]
        u = jnp.square(jnp.maximum(u, 0.0))
        part = jnp.dot(u.astype(BF16), w2_ref[c * tf:(c + 1) * tf, :], preferred_element_type=F32)
        acc = part if acc is None else acc + part
    f = acc + b2_ref[...]
    o_ref[...] = _layer_norm(alpha * x + g2_ref[...] * f, lng_ref[...], lnb_ref[...])


def _mlp(x, sc, sh, g2, w1, b1, w2, b2, ln_g, ln_b, alpha, tm):
    bsz, length, d = x.shape
    const = lambda shape: pl.BlockSpec(shape, lambda b, i: (0,) * len(shape), pipeline_mode=pl.Buffered(1))
    mod = pl.BlockSpec((None, 1, d), lambda b, i: (b, 0, 0))
    return pl.pallas_call(
        functools.partial(_mlp_kernel, alpha=alpha),
        out_shape=jax.ShapeDtypeStruct((bsz, length, d), F32),
        grid=(bsz, length // tm),
        in_specs=[pl.BlockSpec((None, tm, d), lambda b, i: (b, i, 0)), mod, mod, mod,
                  const((d, D_FF)), const((1, D_FF)), const((D_FF, d)), const((1, d)), const((1, d)), const((1, d))],
        out_specs=pl.BlockSpec((None, tm, d), lambda b, i: (b, i, 0)),
        compiler_params=_params(2),
        name="mlp",
    )(x, sc, sh, g2, w1, b1, w2, b2, ln_g, ln_b)


def _split_w_in(w_in):
    pts, acc = [], 0
    for s in IN_SPLITS[:-1]:
        acc += s
        pts.append(acc)
    qkv, gout, a_raw, b_raw, z, xbc, dt_raw, gate_a, gate_b = jnp.split(w_in, pts, axis=1)
    big = jnp.concatenate([z, gout, gate_a, gate_b], axis=1).astype(BF16)
    pad = jnp.zeros((w_in.shape[0], SMALL_N - SM_DT - 2 * SSM_HEADS), w_in.dtype)
    small = jnp.concatenate([a_raw, b_raw, dt_raw, pad], axis=1).astype(BF16)
    return big, qkv.astype(BF16), xbc.astype(BF16), small


def kernel(x, c, ctx, c_ctx, w_mod, b_mod, w_in, gdn_conv_w, gdn_A_log, gdn_dt_bias, gdn_norm_w,
           ssm_conv_w, ssm_conv_b, ssm_A_log, ssm_dt_bias, ssm_D, ssm_norm_w,
           w_proj_gdn, w_proj_ssm, w_out, ln1_g, ln1_b, w_ff1, b_ff1, w_ff2, b_ff2, ln2_g, ln2_b):
    bsz, length, d = x.shape
    ctx_len = ctx.shape[1]
    depth = w_mod.shape[0]
    alpha = float((2 * depth) ** 0.25)
    mod_rows = -(-(bsz + 1) // (2 * SUBLANES)) * (2 * SUBLANES)
    cc = jnp.concatenate([c, c_ctx[None, :], jnp.zeros((mod_rows - bsz - 1, d), c.dtype)], axis=0)
    row2 = lambda t: t.reshape(1, -1)
    tm_lat = 1024
    tm_ctx = ctx_len

    for l in range(depth):
        last = l == depth - 1
        mod = _modulation(cc, w_mod[l], b_mod[l])
        lat = [mod[:bsz, i * d:(i + 1) * d].reshape(bsz, 1, d) for i in range(6)]
        cxm = [jnp.broadcast_to(mod[bsz, i * d:(i + 1) * d].reshape(1, 1, d), (bsz, 1, d)) for i in range(6)]
        w_big, w_qkv, w_xbc, w_small = _split_w_in(w_in[l])
        rows = length // GRID_W

        small = _inproj(x, lat[1], lat[0], w_small, tm_lat, SMALL_N, "inproj_narrow")
        qkv, kt = _proj_conv(x, lat[1], lat[0], w_qkv, gdn_conv_w[l], None, GRID_W, False, 2, tm_lat, GDN_QK,
                             "inproj_qkv")
        xbc = _proj_conv(x, lat[1], lat[0], w_xbc, ssm_conv_w[l], ssm_conv_b[l], rows, True, 0, None, 1024,
                         "inproj_xbc")
        small_c = _inproj(ctx, cxm[1], cxm[0], w_small, tm_ctx, SMALL_N, "inproj_narrow_ctx")
        qkv_c, kt_c = _proj_conv(ctx, cxm[1], cxm[0], w_qkv, gdn_conv_w[l], None, ctx_len, False, 2, tm_ctx, GDN_QK,
                                 "inproj_qkv_ctx")
        xbc_c = _proj_conv(ctx, cxm[1], cxm[0], w_xbc, ssm_conv_w[l], ssm_conv_b[l], ctx_len, False, 0, tm_ctx, 1024,
                           "inproj_xbc_ctx")[:, None]

        o_pair, oc_pair = _gdn_branch(qkv, kt, small, qkv_c, kt_c, small_c, gdn_A_log[l], gdn_dt_bias[l])
        y_pair, yc_pair = _ssd_branch(xbc, small, xbc_c, small_c, ssm_A_log[l], ssm_dt_bias[l], ssm_D[l])

        nwa = jnp.tile(gdn_norm_w[l], GDN_HEADS).reshape(1, GDN_V)
        nwb = row2(ssm_norm_w[l])
        wpg, wps, wo = w_proj_gdn[l].astype(BF16), w_proj_ssm[l].astype(BF16), w_out[l].astype(BF16)
        w1, w2 = w_ff1[l].astype(BF16), w_ff2[l].astype(BF16)
        merge_args = (nwa, nwb, wpg, wps, wo)
        ln1 = (row2(ln1_g[l]), row2(ln1_b[l]))
        mlp_w = (w1, row2(b_ff1[l]), w2, row2(b_ff2[l]), row2(ln2_g[l]), row2(ln2_b[l]))

        x1 = _merge(x, lat[1], lat[0], w_big, o_pair, y_pair, *merge_args, lat[2], *ln1, alpha, MERGE_ROWS)
        x = _mlp(x1, lat[4], lat[3], lat[5], *mlp_w, alpha, MLP_ROWS)
        if not last:
            c1 = _merge(ctx, cxm[1], cxm[0], w_big, oc_pair, yc_pair, *merge_args, cxm[2], *ln1, alpha, ctx_len)
            ctx = _mlp(c1, cxm[4], cxm[3], cxm[5], *mlp_w, alpha, ctx_len)
    return x
```

```python
import functools

import jax
import jax.numpy as jnp
from jax import lax
from jax.experimental import pallas as pl
from jax.experimental.pallas import tpu as pltpu

F32 = jnp.float32
BF16 = jnp.bfloat16

D_MODEL = 1024
GRID_W = 64
GDN_HEADS = 8
GDN_DK = 128
GDN_DV = 128
GDN_QK = GDN_HEADS * GDN_DK
GDN_V = GDN_HEADS * GDN_DV
GDN_QKV = 2 * GDN_QK + GDN_V
SSM_INNER = 2 * D_MODEL
SSM_HEAD_DIM = 64
SSM_HEADS = SSM_INNER // SSM_HEAD_DIM
SSM_GROUPS = 8
SSM_HPG = SSM_HEADS // SSM_GROUPS
SSM_STATE = 128
SSM_GN = SSM_GROUPS * SSM_STATE
SSM_XBC = SSM_INNER + 2 * SSM_GN
CONV_K = 5
CHUNK = 64
D_FF = 4 * D_MODEL
LN_EPS = 1e-5
NORM_EPS = 1e-6
IN_SPLITS = (GDN_QKV, GDN_V, 2 * GDN_HEADS, 2 * GDN_HEADS, SSM_INNER, SSM_XBC, 2 * SSM_HEADS, D_MODEL, D_MODEL)

LANES = 128
SUBLANES = 8
VMEM_LIMIT_BYTES = 56 * 1024 * 1024

Z_OFF = 0
GOUT_OFF = Z_OFF + SSM_INNER
GA_OFF = GOUT_OFF + GDN_V
GB_OFF = GA_OFF + D_MODEL
BIG_N = GB_OFF + D_MODEL
SM_A = 0
SM_B = SM_A + 2 * GDN_HEADS
SM_DT = SM_B + 2 * GDN_HEADS
SM_DT_B = SM_DT + SSM_HEADS
SMALL_N = LANES

GDN_CHUNKS_PER_STEP = 4
GDN_PREP_ROWS = 512
SCAN_OUT_DTYPE = BF16
MERGE_ROWS = 512
MLP_ROWS = 1024
SSD_GROUPS_PER_STEP = 4
SSD_COLUMNS_PER_STEP = 4


def _params(n_axes):
    return pltpu.CompilerParams(dimension_semantics=("arbitrary",) * n_axes, vmem_limit_bytes=VMEM_LIMIT_BYTES)


def _silu(t):
    return t * jax.nn.sigmoid(t)


def _softplus(t):
    return jnp.maximum(t, 0.0) + jnp.log(1.0 + jnp.exp(-jnp.abs(t)))


def _mm(a, b):
    return jnp.dot(a.astype(BF16), b.astype(BF16), preferred_element_type=F32)


def _mm_nt(a, b):
    return lax.dot_general(a.astype(BF16), b.astype(BF16), (((1,), (1,)), ((), ())), preferred_element_type=F32)


def _mm_tn(a, b):
    return lax.dot_general(a.astype(BF16), b.astype(BF16), (((0,), (0,)), ((), ())), preferred_element_type=F32)


def _tri_masks(lower):
    ii = lax.broadcasted_iota(jnp.int32, (CHUNK, CHUNK), 0)
    jj = lax.broadcasted_iota(jnp.int32, (CHUNK, CHUNK), 1)
    if lower:
        return ii >= jj, ii > jj
    return ii <= jj, ii < jj


def _conv_seg(x, w, seg):
    n = x.shape[0]
    assert seg & (seg - 1) == 0 and n % seg == 0
    pos = lax.broadcasted_iota(jnp.int32, (n, 1), 0) & (seg - 1)
    out = None
    for j in range(CONV_K):
        d = j - CONV_K // 2
        if d == 0:
            term = x * w[j:j + 1, :]
        else:
            shifted = pltpu.roll(x, shift=(-d) % n, axis=0)
            valid = (pos + d >= 0) & (pos + d < seg)
            term = jnp.where(valid, shifted, 0.0) * w[j:j + 1, :]
        out = term if out is None else out + term
    return out


def _mod_kernel(c_ref, w_ref, b_ref, o_ref):
    o_ref[...] = _mm(_silu(c_ref[...]), w_ref[...]) + b_ref[...]


def _modulation(cc, w, b):
    rows, d = cc.shape
    n = w.shape[1]
    tn = 1536
    return pl.pallas_call(
        _mod_kernel,
        out_shape=jax.ShapeDtypeStruct((rows, n), F32),
        grid=(n // tn,),
        in_specs=[pl.BlockSpec((rows, d), lambda j: (0, 0)),
                  pl.BlockSpec((d, tn), lambda j: (0, j)),
                  pl.BlockSpec((1, tn), lambda j: (0, j))],
        out_specs=pl.BlockSpec((rows, tn), lambda j: (0, j)),
        compiler_params=_params(1),
        name="modulation",
    )(cc, w, b.reshape(1, n))


def _inproj_kernel(x_ref, sc_ref, sh_ref, w_ref, o_ref):
    h = x_ref[...] * (1.0 + sc_ref[...]) + sh_ref[...]
    o_ref[...] = jnp.dot(h.astype(BF16), w_ref[...], preferred_element_type=F32)


def _inproj(x, sc, sh, w, tm, tn, name):
    bsz, length, d = x.shape
    n = w.shape[1]
    return pl.pallas_call(
        _inproj_kernel,
        out_shape=jax.ShapeDtypeStruct((bsz, length, n), F32),
        grid=(bsz, length // tm, n // tn),
        in_specs=[pl.BlockSpec((None, tm, d), lambda b, i, j: (b, i, 0)),
                  pl.BlockSpec((None, 1, d), lambda b, i, j: (b, 0, 0)),
                  pl.BlockSpec((None, 1, d), lambda b, i, j: (b, 0, 0)),
                  pl.BlockSpec((d, tn), lambda b, i, j: (0, j))],
        out_specs=pl.BlockSpec((None, tm, tn), lambda b, i, j: (b, i, j)),
        compiler_params=_params(3),
        name=name,
    )(x, sc, sh, w)


def _proj_conv_kernel(*refs, seg, cols, has_bias, norm_tiles):
    x_ref, sc_ref, sh_ref, w_ref, cw_ref = refs[:5]
    k = 5
    cb_ref = refs[k] if has_bias else None
    k += has_bias
    perm_ref = refs[k] if cols else None
    k += cols
    o_ref = refs[k]
    kt_ref = refs[k + 1] if norm_tiles else None
    h_ref = refs[-1]
    j = pl.program_id(2)
    n_tok = h_ref.shape[0]

    @pl.when(j == 0)
    def _():
        x = x_ref[...]
        if cols:
            x = x.reshape(n_tok, x.shape[2])
        h = (x * (1.0 + sc_ref[...]) + sh_ref[...]).astype(BF16)
        if cols:
            h = jnp.dot(perm_ref[...], h, preferred_element_type=F32).astype(BF16)
        h_ref[...] = h

    y = jnp.dot(h_ref[...], w_ref[...], preferred_element_type=F32)
    y = _conv_seg(y, cw_ref[...], seg)
    if has_bias:
        y = y + cb_ref[...]
    y = _silu(y)

    def store(t):
        o_ref[...] = t.reshape(o_ref.shape)

    if norm_tiles:
        @pl.when(j < norm_tiles)
        def _():
            scale = jnp.where(j == 0, GDN_DK ** -0.5, 1.0)
            heads = []
            for hh in range(y.shape[1] // GDN_DK):
                yh = y[:, hh * GDN_DK:(hh + 1) * GDN_DK]
                heads.append(yh * (lax.rsqrt(jnp.sum(yh * yh, axis=1, keepdims=True) + NORM_EPS) * scale))
            store(jnp.concatenate(heads, axis=1))

            @pl.when(j == 1)
            def _():
                for hh, yh in enumerate(heads):
                    kt_ref[hh] = yh.T

        @pl.when(j >= norm_tiles)
        def _():
            store(y)
    else:
        store(y)


def _proj_conv(x, sc, sh, w, conv_w, conv_b, seg, cols, norm_tiles, tm, tn, name):
    bsz, length, d = x.shape
    n = w.shape[1]
    has_bias = conv_b is not None
    if cols:
        rows, cps = length // GRID_W, SUBLANES
        assert seg == rows and seg & (seg - 1) == 0
        n_tok = rows * cps
        x_in = x.reshape(bsz, rows, GRID_W, d)
        x_spec = pl.BlockSpec((None, rows, cps, d), lambda b, i, j: (b, 0, i, 0))
        out_shape = jax.ShapeDtypeStruct((bsz, GRID_W, rows, n), F32)
        out_spec = pl.BlockSpec((None, cps, rows, tn), lambda b, i, j: (b, i, 0, j))
        grid = (bsz, GRID_W // cps, n // tn)
    else:
        n_tok = tm
        x_in = x
        x_spec = pl.BlockSpec((None, tm, d), lambda b, i, j: (b, i, 0))
        out_shape = jax.ShapeDtypeStruct((bsz, length, n), F32)
        out_spec = pl.BlockSpec((None, tm, tn), lambda b, i, j: (b, i, j))
        grid = (bsz, length // tm, n // tn)
    mod = pl.BlockSpec((None, 1, d), lambda b, i, j: (b, 0, 0))
    in_specs = [x_spec, mod, mod, pl.BlockSpec((d, tn), lambda b, i, j: (0, j)),
                pl.BlockSpec((CONV_K, tn), lambda b, i, j: (0, j))]
    args = [x_in, sc, sh, w, conv_w]
    if has_bias:
        in_specs.append(pl.BlockSpec((1, tn), lambda b, i, j: (0, j)))
        args.append(conv_b.reshape(1, n))
    if cols:
        dst = jnp.arange(n_tok)
        src = (dst % rows) * cps + dst // rows
        in_specs.append(pl.BlockSpec((n_tok, n_tok), lambda b, i, j: (0, 0), pipeline_mode=pl.Buffered(1)))
        args.append((src[:, None] == jnp.arange(n_tok)[None, :]).astype(BF16))
    if norm_tiles:
        assert norm_tiles == 2 and tn == GDN_QK and not cols
        out_shape = (out_shape, jax.ShapeDtypeStruct((bsz, GDN_HEADS, GDN_DK, length), F32))
        out_spec = (out_spec, pl.BlockSpec((None, GDN_HEADS, GDN_DK, tm), lambda b, i, j: (b, 0, 0, i)))
    return pl.pallas_call(
        functools.partial(_proj_conv_kernel, seg=seg, cols=cols, has_bias=has_bias, norm_tiles=norm_tiles),
        out_shape=out_shape,
        grid=grid,
        in_specs=in_specs,
        out_specs=out_spec,
        scratch_shapes=[pltpu.VMEM((n_tok, d), BF16)],
        compiler_params=_params(3),
        name=name,
    )(*args)


def _chunk_cumsums(t):
    ii = lax.broadcasted_iota(jnp.int32, (2 * CHUNK, CHUNK), 0)
    jj = lax.broadcasted_iota(jnp.int32, (2 * CHUNK, CHUNK), 1)
    ones = ((ii < CHUNK) & (ii >= jj)) | ((ii >= CHUNK) & (ii - CHUNK <= jj))
    tri = jnp.where(ones, 1.0, 0.0).astype(BF16)
    hi = t.astype(BF16)
    r1 = t - hi.astype(F32)
    mid = r1.astype(BF16)
    lo = (r1 - mid.astype(F32)).astype(BF16)
    w = t.shape[1]
    sums = jnp.dot(tri, jnp.concatenate([hi, mid, lo], axis=1), preferred_element_type=F32)
    sums = sums[:, :w] + sums[:, w:2 * w] + sums[:, 2 * w:]
    return sums[:CHUNK], sums[CHUNK:]


def _gdn_prep_kernel(s_ref, alog_ref, dtb_ref, o_ref, *, rows):
    s = s_ref[...]
    lane = lax.broadcasted_iota(jnp.int32, (1, LANES), 1)
    g = -jnp.exp(alog_ref[...]) * _softplus(s + dtb_ref[...])
    beta = jax.nn.sigmoid(s)
    for c in range(rows // CHUNK):
        sl = slice(c * CHUNK, (c + 1) * CHUNK)
        fwd, bwd = _chunk_cumsums(g[sl, :])
        gc = jnp.where(lane < SM_A + GDN_HEADS, fwd, bwd)
        o_ref[sl, :] = jnp.where(lane < SM_B, gc, beta[sl, :])


def _gdn_prep(small, a_log, dt_bias, tb):
    bsz, length, _ = small.shape
    pad = lambda t: jnp.pad(t.reshape(1, -1).astype(F32), ((0, 0), (SM_A, LANES - SM_A - 2 * GDN_HEADS)))
    return pl.pallas_call(
        functools.partial(_gdn_prep_kernel, rows=tb),
        out_shape=jax.ShapeDtypeStruct((bsz, length, LANES), F32),
        grid=(bsz, length // tb),
        in_specs=[pl.BlockSpec((None, tb, LANES), lambda b, i: (b, i, 0)),
                  pl.BlockSpec((1, LANES), lambda b, i: (0, 0)),
                  pl.BlockSpec((1, LANES), lambda b, i: (0, 0))],
        out_specs=pl.BlockSpec((None, tb, LANES), lambda b, i: (b, i, 0)),
        compiler_params=_params(2),
        name="gdn_prep",
    )(small, pad(a_log), pad(dt_bias))


def _tri_inverse(mats):
    ii = lax.broadcasted_iota(jnp.int32, (CHUNK, CHUNK), 0)
    jj = lax.broadcasted_iota(jnp.int32, (CHUNK, CHUNK), 1)
    eye = jnp.where(ii == jj, 1.0, 0.0)
    zero = jnp.zeros((CHUNK, CHUNK), F32)
    right = lax.broadcasted_iota(jnp.int32, (CHUNK, 2 * CHUNK), 1) >= CHUNK
    zs = [jnp.concatenate([zero, eye], axis=1) + _mm(a, jnp.concatenate([a, -eye], axis=1)) for a in mats]
    power = 2
    while power < CHUNK:
        ps = [_mm(z[:, :CHUNK], z) for z in zs]
        zs = [p + jnp.where(right, z, 0.0) for p, z in zip(ps, zs)]
        power *= 2
    return zs


def _gdn_local(items, masks):
    qs, ks, vs, kts, gcols, grows, bcols, lowers = zip(*items)
    n = len(items)
    incl = [masks[lo][0] for lo in lowers]
    strict = [masks[lo][1] for lo in lowers]
    glast = [grows[i][:, CHUNK - 1:CHUNK] if lowers[i] else grows[i][:, 0:1] for i in range(n)]
    kq = [_mm(jnp.concatenate([ks[i].astype(BF16), qs[i].astype(BF16)], axis=0), kts[i]) for i in range(n)]
    dec = [jnp.exp(jnp.where(incl[i], gcols[i] - grows[i], 0.0)) for i in range(n)]
    a_mats = [jnp.where(strict[i], bcols[i] * kq[i][:CHUNK] * dec[i], 0.0) for i in range(n)]
    t_inv = _tri_inverse(a_mats)
    eg = [jnp.exp(g) for g in gcols]
    rhs = [jnp.concatenate([(bcols[i] * eg[i]) * ks[i], bcols[i] * vs[i]], axis=1) for i in range(n)]
    pad = jnp.zeros((CHUNK, GDN_DK + GDN_DV), F32)
    wu = [_mm(t, jnp.concatenate([pad, r], axis=0)) for t, r in zip(t_inv, rhs)]
    lhs = [jnp.concatenate([wu[i][:, :GDN_DK].astype(BF16), (qs[i] * eg[i]).astype(BF16)], axis=0) for i in range(n)]
    qk_masked = [jnp.where(incl[i], kq[i][CHUNK:] * dec[i], 0.0).astype(BF16) for i in range(n)]
    kt_dec = [(kts[i] * jnp.exp(glast[i] - grows[i])).astype(BF16) for i in range(n)]
    return [(lhs[i], wu[i][:, GDN_DK:], qk_masked[i], kt_dec[i], jnp.exp(glast[i])) for i in range(n)]


def _gdn_on_state(local, states):
    n = len(local)
    on_state = [_mm(local[i][0], states[i]) for i in range(n)]
    v_new = [local[i][1] - on_state[i][:CHUNK] for i in range(n)]
    o_intra = [_mm(local[i][2], v_new[i]) for i in range(n)]
    kv = [_mm(local[i][3], v_new[i]) for i in range(n)]
    return [(on_state[i][CHUNK:] + o_intra[i], local[i][4] * states[i] + kv[i]) for i in range(n)]


def _lane_column(t, idx, lane):
    if isinstance(idx, int):
        return t[:, idx:idx + 1]
    return jnp.sum(jnp.where(lane == idx, t, 0.0), axis=1, keepdims=True)


def _gdn_kernel(qf_ref, kf_ref, vf_ref, ktf_ref, qb_ref, kb_ref, vb_ref, ktb_ref, pf_ref, pb_ref, ptf_ref, ptb_ref,
                s0_ref, of_ref, ob_ref, sout_ref, state_ref, *, n_steps, hs, nc):
    step = pl.program_id(2)
    head0 = 0 if hs == GDN_HEADS else pl.program_id(1) * hs

    @pl.when(step == 0)
    def _():
        state_ref[...] = s0_ref[...]

    lane = lax.broadcasted_iota(jnp.int32, (1, LANES), 1)

    def heads_of(q_ref, k_ref, v_ref):
        q, k, v = q_ref[...], k_ref[...], v_ref[...]
        return [(q[:, hh * GDN_DK:(hh + 1) * GDN_DK], k[:, hh * GDN_DK:(hh + 1) * GDN_DK],
                 v[:, hh * GDN_DV:(hh + 1) * GDN_DV]) for hh in range(hs)]

    heads_f = heads_of(qf_ref, kf_ref, vf_ref)
    heads_b = heads_of(qb_ref, kb_ref, vb_ref)
    pf = pf_ref[...]
    pb = pb_ref[...]
    masks = {True: _tri_masks(True), False: _tri_masks(False)}
    cols = []
    for hh in range(hs):
        head = head0 + hh
        cols.append((_lane_column(pf, SM_A + head, lane), _lane_column(pf, SM_B + head, lane),
                     _lane_column(pb, SM_A + GDN_HEADS + head, lane), _lane_column(pb, SM_B + GDN_HEADS + head, lane)))
    items = []
    for c in range(nc):
        cb = nc - 1 - c
        sl_f = slice(c * CHUNK, (c + 1) * CHUNK)
        sl_b = slice(cb * CHUNK, (cb + 1) * CHUNK)
        for hh in range(hs):
            head = head0 + hh
            gcol_f, bcol_f, gcol_b, bcol_b = cols[hh]
            qf, kf, vf = heads_f[hh]
            qb, kb, vb = heads_b[hh]
            grow_f = ptf_ref[c, pl.ds(SM_A + head, 1), :]
            grow_b = ptb_ref[cb, pl.ds(SM_A + GDN_HEADS + head, 1), :]
            items.append((qf[sl_f], kf[sl_f], vf[sl_f], ktf_ref[hh, :, sl_f], gcol_f[sl_f], grow_f, bcol_f[sl_f], True))
            items.append((qb[sl_b], kb[sl_b], vb[sl_b], ktb_ref[hh, :, sl_b], gcol_b[sl_b], grow_b, bcol_b[sl_b], False))
    local = _gdn_local(items, masks)
    states = [state_ref[hh, d] for hh in range(hs) for d in range(2)]
    for c in range(nc):
        cb = nc - 1 - c
        results = _gdn_on_state(local[c * 2 * hs:(c + 1) * 2 * hs], states)
        states = [r[1] for r in results]
        for hh in range(hs):
            hl = slice(hh * GDN_DV, (hh + 1) * GDN_DV)
            of_ref[c * CHUNK:(c + 1) * CHUNK, hl] = results[2 * hh][0].astype(of_ref.dtype)
            ob_ref[cb * CHUNK:(cb + 1) * CHUNK, hl] = results[2 * hh + 1][0].astype(ob_ref.dtype)
    for hh in range(hs):
        state_ref[hh, 0] = states[2 * hh]
        state_ref[hh, 1] = states[2 * hh + 1]

    @pl.when(step == n_steps - 1)
    def _():
        sout_ref[...] = state_ref[...]


def _gdn_scan(qkv, kt, p, pt, s0, hs, nc):
    bsz, length, _ = qkv.shape
    tb = nc * CHUNK
    n_steps = length // tb
    width = hs * GDN_DK
    qoff, koff, voff = 0, GDN_QK // width, 2 * GDN_QK // width
    fwd = lambda off: pl.BlockSpec((None, tb, width), lambda b, h, s: (b, s, off + h))
    bwd = lambda off: pl.BlockSpec((None, tb, width), lambda b, h, s: (b, n_steps - 1 - s, off + h))
    state_spec = pl.BlockSpec((None, hs, 2, GDN_DK, GDN_DV), lambda b, h, s: (b, h, 0, 0, 0))
    return pl.pallas_call(
        functools.partial(_gdn_kernel, n_steps=n_steps, hs=hs, nc=nc),
        out_shape=(jax.ShapeDtypeStruct((bsz, length, GDN_V), SCAN_OUT_DTYPE),
                   jax.ShapeDtypeStruct((bsz, length, GDN_V), SCAN_OUT_DTYPE),
                   jax.ShapeDtypeStruct((bsz, GDN_HEADS, 2, GDN_DK, GDN_DV), F32)),
        grid=(bsz, GDN_HEADS // hs, n_steps),
        in_specs=[fwd(qoff), fwd(koff), fwd(voff), pl.BlockSpec((None, hs, GDN_DK, tb), lambda b, h, s: (b, h, 0, s)),
                  bwd(qoff), bwd(koff), bwd(voff),
                  pl.BlockSpec((None, hs, GDN_DK, tb), lambda b, h, s: (b, h, 0, n_steps - 1 - s)),
                  pl.BlockSpec((None, tb, LANES), lambda b, h, s: (b, s, 0)),
                  pl.BlockSpec((None, tb, LANES), lambda b, h, s: (b, n_steps - 1 - s, 0)),
                  pl.BlockSpec((None, nc, 4 * SUBLANES, CHUNK), lambda b, h, s: (b, s, 0, 0)),
                  pl.BlockSpec((None, nc, 4 * SUBLANES, CHUNK), lambda b, h, s: (b, n_steps - 1 - s, 0, 0)),
                  state_spec],
        out_specs=(pl.BlockSpec((None, tb, width), lambda b, h, s: (b, s, h)),
                   pl.BlockSpec((None, tb, width), lambda b, h, s: (b, n_steps - 1 - s, h)),
                   state_spec),
        scratch_shapes=[pltpu.VMEM((hs, 2, GDN_DK, GDN_DV), F32)],
        compiler_params=_params(3),
        name="gdn_scan",
    )(qkv, qkv, qkv, kt, qkv, qkv, qkv, kt, p, p, pt, pt, s0)


def _gdn_branch(qkv, kt, small, qkv_c, kt_c, small_c, a_log, dt_bias):
    bsz = qkv.shape[0]
    s0 = jnp.zeros((bsz, GDN_HEADS, 2, GDN_DK, GDN_DV), F32)

    def rows_t(p):
        chunks = p[:, :, :4 * SUBLANES].reshape(bsz, p.shape[1] // CHUNK, CHUNK, 4 * SUBLANES)
        return jnp.swapaxes(chunks, 2, 3)

    ctx_len = qkv_c.shape[1]
    p_c = _gdn_prep(small_c, a_log, dt_bias, ctx_len)
    oc_f, oc_b, s_ctx = _gdn_scan(qkv_c, kt_c, p_c, rows_t(p_c), s0, GDN_HEADS, GDN_CHUNKS_PER_STEP)
    p = _gdn_prep(small, a_log, dt_bias, GDN_PREP_ROWS)
    o_f, o_b, _ = _gdn_scan(qkv, kt, p, rows_t(p), s_ctx, GDN_HEADS, GDN_CHUNKS_PER_STEP)
    return (o_f, o_b), (oc_f, oc_b)


SSD_QUANTITIES = 4
SSD_HEADS_PER_STEP = SSD_GROUPS_PER_STEP * SSM_HPG
SSD_STEP_WIDTH = SSD_HEADS_PER_STEP * SSM_HEAD_DIM
BF16_PIECES = 3


def _ssd_prep_kernel(s_ref, alog_ref, dtb_ref, o_ref, *, rows):
    lane = lax.broadcasted_iota(jnp.int32, (1, LANES), 1)
    fwd_lane = lane < SM_DT_B
    neg_a = -jnp.exp(alog_ref[...])
    for seg in range(o_ref.shape[0]):
        dt = _softplus(s_ref[:, seg * LANES:(seg + 1) * LANES] + dtb_ref[...])
        la = dt * neg_a
        o_ref[seg, :, 0:LANES] = dt
        for c in range(rows // CHUNK):
            sl = slice(c * CHUNK, (c + 1) * CHUNK)
            fwd, bwd = _chunk_cumsums(la[sl, :])
            gc = jnp.where(fwd_lane, fwd, bwd)
            g_last = jnp.where(fwd_lane, gc[CHUNK - 1:CHUNK, :], gc[0:1, :])
            o_ref[seg, sl, LANES:2 * LANES] = gc
            o_ref[seg, sl, 2 * LANES:3 * LANES] = jnp.exp(gc)
            o_ref[seg, sl, 3 * LANES:4 * LANES] = dt[sl, :] * jnp.exp(g_last - gc)


def _ssd_prep(small_cols, a_log, dt_bias, n_seg):
    bsz, rows, _ = small_cols.shape
    sps = min(n_seg, SUBLANES)
    pad = lambda t: jnp.pad(t.reshape(1, -1).astype(F32), ((0, 0), (SM_DT, LANES - SM_DT - 2 * SSM_HEADS)))
    return pl.pallas_call(
        functools.partial(_ssd_prep_kernel, rows=rows),
        out_shape=jax.ShapeDtypeStruct((bsz, n_seg, rows, SSD_QUANTITIES * LANES), F32),
        grid=(bsz, n_seg // sps),
        in_specs=[pl.BlockSpec((None, rows, sps * LANES), lambda b, c: (b, 0, c)),
                  pl.BlockSpec((1, LANES), lambda b, c: (0, 0)),
                  pl.BlockSpec((1, LANES), lambda b, c: (0, 0))],
        out_specs=pl.BlockSpec((None, sps, rows, SSD_QUANTITIES * LANES), lambda b, c: (b, c, 0, 0)),
        compiler_params=_params(2),
        name="ssd_prep",
    )(small_cols, pad(a_log), pad(dt_bias))


def _ssd_layouts(q):
    bsz, n_seg, rows, _ = q.shape
    steps = SSM_GROUPS // SSD_GROUPS_PER_STEP
    q = q.reshape(bsz, n_seg, rows, SSD_QUANTITIES, LANES)[..., SM_DT:SM_DT + 2 * SSM_HEADS]
    q = q.reshape(bsz, n_seg, rows, SSD_QUANTITIES, 2, steps, SSD_HEADS_PER_STEP)
    out = []
    for d in range(2):
        qd = q[:, :, :, :, d]
        cols = jnp.transpose(qd, (0, 1, 4, 2, 3, 5)).reshape(bsz, n_seg, steps, rows, SSD_QUANTITIES * SSD_HEADS_PER_STEP)
        as_rows = lambda t: jnp.transpose(
            t.reshape(bsz, n_seg, rows // CHUNK, CHUNK, steps, SSD_HEADS_PER_STEP),
            (0, 1, 4, 2, 5, 3)).reshape(bsz, n_seg, steps, rows // CHUNK, SSD_STEP_WIDTH)
        out.append((cols, jnp.concatenate([as_rows(qd[:, :, :, 1]), as_rows(qd[:, :, :, 0])], axis=-1)))
    return out


def _ssd_expand_matrix():
    k = SSD_QUANTITIES * SSD_HEADS_PER_STEP
    src = jnp.arange(k)
    dst = jnp.arange(SSD_STEP_WIDTH, k * SSM_HEAD_DIM) // SSM_HEAD_DIM
    one = (src[:, None] == dst[None, :]).astype(BF16)
    return jnp.concatenate([one] * BF16_PIECES, axis=0)


def _ssd_chunks(items, incl, block_diag):
    n = len(items)
    cb = [_mm_nt(it[4], jnp.concatenate([it[3].astype(BF16)] * SSM_HPG, axis=0)) for it in items]
    inter = [_mm(it[4], it[9]) for it in items]
    lhs = []
    for i, it in enumerate(items):
        m = incl[it[10]]
        lhs.append(jnp.where(m, cb[i] * jnp.exp(jnp.where(m, it[5] - it[6], 0.0)) * it[1], 0.0))
    rhs = [jnp.where(block_diag, jnp.concatenate([it[0].astype(BF16)] * SSM_HPG, axis=0), 0.0) for it in items]
    intra = [_mm(lhs[i], rhs[i]) for i in range(n)]
    upd = [_mm_tn(it[3], it[2]) for it in items]
    out = []
    for i, it in enumerate(items):
        y = intra[i] + inter[i] * it[7]
        if it[11] is not None:
            y = y + it[11] * it[0]
        out.append((y, it[8] * it[9] + upd[i]))
    return out


def _ssd_kernel(xf_ref, bf_ref, cf_ref, xb_ref, bb_ref, cb_ref, colf_ref, colb_ref, growf_ref, growb_ref,
                e_ref, d_ref, s0_ref, yf_ref, yb_ref, sout_ref, state_ref, *, rows, n_seg, gs):
    step = pl.program_id(2)

    @pl.when(step == 0)
    def _():
        state_ref[...] = s0_ref[...]

    gw = SSM_HPG * SSM_HEAD_DIM
    width = gs * gw

    def expand(col_ref):
        c = col_ref[...]
        hi = c.astype(BF16)
        r1 = c - hi.astype(F32)
        mid = r1.astype(BF16)
        lo = (r1 - mid.astype(F32)).astype(BF16)
        ex = jnp.dot(jnp.concatenate([hi, mid, lo], axis=1), e_ref[...], preferred_element_type=F32)
        return [ex[:, q * width:(q + 1) * width] for q in range(SSD_QUANTITIES - 1)]

    xf, xb = xf_ref[...], xb_ref[...]
    gcx_f, egx_f, q4x_f = expand(colf_ref)
    gcx_b, egx_b, q4x_b = expand(colb_ref)
    xdec_f, xdec_b = xf * q4x_f, xb * q4x_b
    bmf, cmf, bmb, cmb = bf_ref[...], cf_ref[...], bb_ref[...], cb_ref[...]
    d_all = d_ref[...]

    row = lax.broadcasted_iota(jnp.int32, (CHUNK, gw), 0)
    tok = lax.broadcasted_iota(jnp.int32, (CHUNK, gw), 1) & (CHUNK - 1)
    incl = {True: row >= tok, False: row <= tok}
    block_diag = (lax.shift_right_logical(lax.broadcasted_iota(jnp.int32, (gw, gw), 0), CHUNK.bit_length() - 1)
                  == lax.shift_right_logical(lax.broadcasted_iota(jnp.int32, (gw, gw), 1), SSM_HEAD_DIM.bit_length() - 1))

    n_chunks = rows // CHUNK
    states = [[state_ref[gg, 0], state_ref[gg, 1]] for gg in range(gs)]
    for c in range(n_chunks):
        cb = n_chunks - 1 - c
        sl_f = slice(c * CHUNK, (c + 1) * CHUNK)
        sl_b = slice(cb * CHUNK, (cb + 1) * CHUNK)
        last_f = slice(c * CHUNK + CHUNK - 1, (c + 1) * CHUNK)
        last_b = slice(cb * CHUNK, cb * CHUNK + 1)
        items = []
        for gg in range(gs):
            xl = slice(gg * gw, (gg + 1) * gw)
            nl = slice(gg * SSM_STATE, (gg + 1) * SSM_STATE)
            dl = slice(width + gg * gw, width + (gg + 1) * gw)
            items.append((xf[sl_f, xl], growf_ref[c:c + 1, dl], xdec_f[sl_f, xl], bmf[sl_f, nl], cmf[sl_f, nl],
                          gcx_f[sl_f, xl], growf_ref[c:c + 1, xl], egx_f[sl_f, xl], egx_f[last_f, xl], states[gg][0], True,
                          d_all[:, xl]))
            items.append((xb[sl_b, xl], growb_ref[cb:cb + 1, dl], xdec_b[sl_b, xl], bmb[sl_b, nl], cmb[sl_b, nl],
                          gcx_b[sl_b, xl], growb_ref[cb:cb + 1, xl], egx_b[sl_b, xl], egx_b[last_b, xl], states[gg][1], False,
                          None))
        results = _ssd_chunks(items, incl, block_diag)
        for gg in range(gs):
            xl = slice(gg * gw, (gg + 1) * gw)
            yf_ref[sl_f, xl] = results[2 * gg][0].astype(yf_ref.dtype)
            yb_ref[sl_b, xl] = results[2 * gg + 1][0].astype(yb_ref.dtype)
            states[gg] = [results[2 * gg][1], results[2 * gg + 1][1]]
    for gg in range(gs):
        state_ref[gg, 0] = states[gg][0]
        state_ref[gg, 1] = states[gg][1]

    @pl.when(step == n_seg - 1)
    def _():
        sout_ref[...] = state_ref[...]


def _ssd_scan(xbc, layouts, expand, d_exp, s0):
    bsz, n_seg, rows, _ = xbc.shape
    gs = SSD_GROUPS_PER_STEP
    xw, nw = SSD_STEP_WIDTH, gs * SSM_STATE
    boff, coff = SSM_INNER // nw, (SSM_INNER + SSM_GN) // nw
    (cols_f, grow_f), (cols_b, grow_b) = layouts
    n_cols = cols_f.shape[-1]
    seg_f = lambda s: s
    seg_b = lambda s: n_seg - 1 - s
    xspec = lambda seg: pl.BlockSpec((None, None, rows, xw), lambda b, g, s: (b, seg(s), 0, g))
    nspec = lambda seg, off: pl.BlockSpec((None, None, rows, nw), lambda b, g, s: (b, seg(s), 0, off + g))
    cspec = lambda seg: pl.BlockSpec((None, None, None, rows, n_cols), lambda b, g, s: (b, seg(s), g, 0, 0))
    rspec = lambda seg: pl.BlockSpec((None, None, None, rows // CHUNK, 2 * xw), lambda b, g, s: (b, seg(s), g, 0, 0))
    state_spec = pl.BlockSpec((None, gs, 2, SSM_STATE, SSM_HPG * SSM_HEAD_DIM), lambda b, g, s: (b, g, 0, 0, 0))
    y_shape = jax.ShapeDtypeStruct((bsz, n_seg, rows, SSM_INNER), SCAN_OUT_DTYPE)
    return pl.pallas_call(
        functools.partial(_ssd_kernel, rows=rows, n_seg=n_seg, gs=gs),
        out_shape=(y_shape, y_shape,
                   jax.ShapeDtypeStruct((bsz, SSM_GROUPS, 2, SSM_STATE, SSM_HPG * SSM_HEAD_DIM), F32)),
        grid=(bsz, SSM_GROUPS // gs, n_seg),
        in_specs=[xspec(seg_f), nspec(seg_f, boff), nspec(seg_f, coff),
                  xspec(seg_b), nspec(seg_b, boff), nspec(seg_b, coff),
                  cspec(seg_f), cspec(seg_b), rspec(seg_f), rspec(seg_b),
                  pl.BlockSpec(expand.shape, lambda b, g, s: (0, 0)),
                  pl.BlockSpec((1, xw), lambda b, g, s: (0, g)), state_spec],
        out_specs=(xspec(seg_f), xspec(seg_b), state_spec),
        scratch_shapes=[pltpu.VMEM((gs, 2, SSM_STATE, SSM_HPG * SSM_HEAD_DIM), F32)],
        compiler_params=_params(3),
        name="ssd_scan",
    )(xbc, xbc, xbc, xbc, xbc, xbc, cols_f, cols_b, grow_f, grow_b, expand, d_exp, s0)


def _ssd_branch(xbc, small, xbc_c, small_c, a_log, dt_bias, d_skip):
    bsz, n_seg, rows, _ = xbc.shape
    assert rows % CHUNK == 0 and xbc_c.shape[2] % CHUNK == 0
    d_exp = jnp.repeat(d_skip.astype(F32), SSM_HEAD_DIM).reshape(1, SSM_INNER)
    s0 = jnp.zeros((bsz, SSM_GROUPS, 2, SSM_STATE, SSM_HPG * SSM_HEAD_DIM), F32)
    expand = _ssd_expand_matrix()
    yc_f, yc_b, s_ctx = _ssd_scan(xbc_c, _ssd_layouts(_ssd_prep(small_c, a_log, dt_bias, 1)), expand, d_exp, s0)
    q = _ssd_prep(small.reshape(bsz, rows, n_seg * SMALL_N), a_log, dt_bias, n_seg)
    steps, step_rows = n_seg // SSD_COLUMNS_PER_STEP, SSD_COLUMNS_PER_STEP * rows
    merged = lambda t: t.reshape(bsz, steps, step_rows, t.shape[-1])
    y_f, y_b, _ = _ssd_scan(merged(xbc), _ssd_layouts(merged(q)), expand, d_exp, s_ctx)
    to_raster = lambda y: jnp.swapaxes(y.reshape(bsz, n_seg, rows, SSM_INNER), 1, 2).reshape(bsz, rows * n_seg, SSM_INNER)
    return (to_raster(y_f), to_raster(y_b)), (yc_f[:, 0], yc_b[:, 0])


def _layer_norm(r, g, b):
    mu = jnp.mean(r, axis=1, keepdims=True)
    var = jnp.mean(jnp.square(r - mu), axis=1, keepdims=True)
    return (r - mu) * lax.rsqrt(var + LN_EPS) * g + b


def _merge_kernel(x_ref, sc_ref, sh_ref, wg_ref, of_ref, ob_ref, yf_ref, yb_ref,
                  nwa_ref, nwb_ref, wpg_ref, wps_ref, wout_ref, g1_ref, lng_ref, lnb_ref, o_ref, *, alpha):
    x = x_ref[...]
    h = (x * (1.0 + sc_ref[...]) + sh_ref[...]).astype(BF16)
    gate = lambda off, width: jnp.dot(h, wg_ref[:, off:off + width], preferred_element_type=F32)
    o = of_ref[...].astype(F32) + ob_ref[...].astype(F32)
    normed = []
    for hh in range(GDN_HEADS):
        oh = o[:, hh * GDN_DV:(hh + 1) * GDN_DV]
        normed.append(oh * lax.rsqrt(jnp.mean(oh * oh, axis=1, keepdims=True) + NORM_EPS))
    y_a = jnp.concatenate(normed, axis=1) * nwa_ref[...] * _silu(gate(GOUT_OFF, GDN_V))
    t = (yf_ref[...].astype(F32) + yb_ref[...].astype(F32)) * _silu(gate(Z_OFF, SSM_INNER))
    gw = SSM_INNER // SSM_GROUPS
    normed = []
    for g in range(SSM_GROUPS):
        tg = t[:, g * gw:(g + 1) * gw]
        normed.append(tg * lax.rsqrt(jnp.mean(tg * tg, axis=1, keepdims=True) + NORM_EPS))
    y_b = jnp.concatenate(normed, axis=1) * nwb_ref[...]
    d = x.shape[1]
    mix = (jax.nn.sigmoid(gate(GA_OFF, d)) * jnp.dot(y_a.astype(BF16), wpg_ref[...], preferred_element_type=F32)
           + jax.nn.sigmoid(gate(GB_OFF, d)) * jnp.dot(y_b.astype(BF16), wps_ref[...], preferred_element_type=F32))
    out = jnp.dot(mix.astype(BF16), wout_ref[...], preferred_element_type=F32)
    o_ref[...] = _layer_norm(alpha * x + g1_ref[...] * out, lng_ref[...], lnb_ref[...])


def _merge(x, sc, sh, w_gates, o_pair, y_pair, nwa, nwb, wpg, wps, wout, g1, ln_g, ln_b, alpha, tm):
    bsz, length, d = x.shape
    row = lambda width: pl.BlockSpec((None, tm, width), lambda b, i: (b, i, 0))
    const = lambda shape: pl.BlockSpec(shape, lambda b, i: (0,) * len(shape), pipeline_mode=pl.Buffered(1))
    mod = pl.BlockSpec((None, 1, d), lambda b, i: (b, 0, 0))
    return pl.pallas_call(
        functools.partial(_merge_kernel, alpha=alpha),
        out_shape=jax.ShapeDtypeStruct((bsz, length, d), F32),
        grid=(bsz, length // tm),
        in_specs=[row(d), mod, mod, const((d, BIG_N)),
                  row(GDN_V), row(GDN_V), row(SSM_INNER), row(SSM_INNER),
                  const((1, GDN_V)), const((1, SSM_INNER)), const((GDN_V, d)), const((SSM_INNER, d)), const((d, d)),
                  mod, const((1, d)), const((1, d))],
        out_specs=row(d),
        compiler_params=_params(2),
        name="merge",
    )(x, sc, sh, w_gates, o_pair[0], o_pair[1], y_pair[0], y_pair[1], nwa, nwb, wpg, wps, wout, g1, ln_g, ln_b)


def _mlp_kernel(x_ref, sc_ref, sh_ref, g2_ref, w1_ref, b1_ref, w2_ref, b2_ref, lng_ref, lnb_ref, o_ref, *, alpha):
    x = x_ref[...]
    h = (x * (1.0 + sc_ref[...]) + sh_ref[...]).astype(BF16)
    acc = None
    tf = D_MODEL
    for c in range(D_FF // tf):
        u = jnp.dot(h, w1_ref[:, c * tf:(c + 1) * tf], preferred_element_type=F32) + b1_ref[:, c * tf:(c + 1) * tf]
        u = jnp.square(jnp.maximum(u, 0.0))
        part = jnp.dot(u.astype(BF16), w2_ref[c * tf:(c + 1) * tf, :], preferred_element_type=F32)
        acc = part if acc is None else acc + part
    f = acc + b2_ref[...]
    o_ref[...] = _layer_norm(alpha * x + g2_ref[...] * f, lng_ref[...], lnb_ref[...])


def _mlp(x, sc, sh, g2, w1, b1, w2, b2, ln_g, ln_b, alpha, tm):
    bsz, length, d = x.shape
    const = lambda shape: pl.BlockSpec(shape, lambda b, i: (0,) * len(shape), pipeline_mode=pl.Buffered(1))
    mod = pl.BlockSpec((None, 1, d), lambda b, i: (b, 0, 0))
    return pl.pallas_call(
        functools.partial(_mlp_kernel, alpha=alpha),
        out_shape=jax.ShapeDtypeStruct((bsz, length, d), F32),
        grid=(bsz, length // tm),
        in_specs=[pl.BlockSpec((None, tm, d), lambda b, i: (b, i, 0)), mod, mod, mod,
                  const((d, D_FF)), const((1, D_FF)), const((D_FF, d)), const((1, d)), const((1, d)), const((1, d))],
        out_specs=pl.BlockSpec((None, tm, d), lambda b, i: (b, i, 0)),
        compiler_params=_params(2),
        name="mlp",
    )(x, sc, sh, g2, w1, b1, w2, b2, ln_g, ln_b)


def _split_w_in(w_in):
    pts, acc = [], 0
    for s in IN_SPLITS[:-1]:
        acc += s
        pts.append(acc)
    qkv, gout, a_raw, b_raw, z, xbc, dt_raw, gate_a, gate_b = jnp.split(w_in, pts, axis=1)
    big = jnp.concatenate([z, gout, gate_a, gate_b], axis=1).astype(BF16)
    pad = jnp.zeros((w_in.shape[0], SMALL_N - SM_DT - 2 * SSM_HEADS), w_in.dtype)
    small = jnp.concatenate([a_raw, b_raw, dt_raw, pad], axis=1).astype(BF16)
    return big, qkv.astype(BF16), xbc.astype(BF16), small


def kernel(x, c, ctx, c_ctx, w_mod, b_mod, w_in, gdn_conv_w, gdn_A_log, gdn_dt_bias, gdn_norm_w,
           ssm_conv_w, ssm_conv_b, ssm_A_log, ssm_dt_bias, ssm_D, ssm_norm_w,
           w_proj_gdn, w_proj_ssm, w_out, ln1_g, ln1_b, w_ff1, b_ff1, w_ff2, b_ff2, ln2_g, ln2_b):
    bsz, length, d = x.shape
    ctx_len = ctx.shape[1]
    depth = w_mod.shape[0]
    alpha = float((2 * depth) ** 0.25)
    mod_rows = -(-(bsz + 1) // (2 * SUBLANES)) * (2 * SUBLANES)
    cc = jnp.concatenate([c, c_ctx[None, :], jnp.zeros((mod_rows - bsz - 1, d), c.dtype)], axis=0)
    row2 = lambda t: t.reshape(1, -1)
    tm_lat = 1024
    tm_ctx = ctx_len

    for l in range(depth):
        last = l == depth - 1
        mod = _modulation(cc, w_mod[l], b_mod[l])
        lat = [mod[:bsz, i * d:(i + 1) * d].reshape(bsz, 1, d) for i in range(6)]
        cxm = [jnp.broadcast_to(mod[bsz, i * d:(i + 1) * d].reshape(1, 1, d), (bsz, 1, d)) for i in range(6)]
        w_big, w_qkv, w_xbc, w_small = _split_w_in(w_in[l])
        rows = length // GRID_W

        small = _inproj(x, lat[1], lat[0], w_small, tm_lat, SMALL_N, "inproj_narrow")
        qkv, kt = _proj_conv(x, lat[1], lat[0], w_qkv, gdn_conv_w[l], None, GRID_W, False, 2, tm_lat, GDN_QK,
                             "inproj_qkv")
        xbc = _proj_conv(x, lat[1], lat[0], w_xbc, ssm_conv_w[l], ssm_conv_b[l], rows, True, 0, None, 1024,
                         "inproj_xbc")
        small_c = _inproj(ctx, cxm[1], cxm[0], w_small, tm_ctx, SMALL_N, "inproj_narrow_ctx")
        qkv_c, kt_c = _proj_conv(ctx, cxm[1], cxm[0], w_qkv, gdn_conv_w[l], None, ctx_len, False, 2, tm_ctx, GDN_QK,
                                 "inproj_qkv_ctx")
        xbc_c = _proj_conv(ctx, cxm[1], cxm[0], w_xbc, ssm_conv_w[l], ssm_conv_b[l], ctx_len, False, 0, tm_ctx, 1024,
                           "inproj_xbc_ctx")[:, None]

        o_pair, oc_pair = _gdn_branch(qkv, kt, small, qkv_c, kt_c, small_c, gdn_A_log[l], gdn_dt_bias[l])
        y_pair, yc_pair = _ssd_branch(xbc, small, xbc_c, small_c, ssm_A_log[l], ssm_dt_bias[l], ssm_D[l])

        nwa = jnp.tile(gdn_norm_w[l], GDN_HEADS).reshape(1, GDN_V)
        nwb = row2(ssm_norm_w[l])
        wpg, wps, wo = w_proj_gdn[l].astype(BF16), w_proj_ssm[l].astype(BF16), w_out[l].astype(BF16)
        w1, w2 = w_ff1[l].astype(BF16), w_ff2[l].astype(BF16)
        merge_args = (nwa, nwb, wpg, wps, wo)
        ln1 = (row2(ln1_g[l]), row2(ln1_b[l]))
        mlp_w = (w1, row2(b_ff1[l]), w2, row2(b_ff2[l]), row2(ln2_g[l]), row2(ln2_b[l]))

        x1 = _merge(x, lat[1], lat[0], w_big, o_pair, y_pair, *merge_args, lat[2], *ln1, alpha, MERGE_ROWS)
        x = _mlp(x1, lat[4], lat[3], lat[5], *mlp_w, alpha, MLP_ROWS)
        if not last:
            c1 = _merge(ctx, cxm[1], cxm[0], w_big, oc_pair, yc_pair, *merge_args, cxm[2], *ln1, alpha, ctx_len)
            ctx = _mlp(c1, cxm[4], cxm[3], cxm[5], *mlp_w, alpha, ctx_len)
    return x
```

```python
import functools

import jax
import jax.numpy as jnp
from jax import lax
from jax.experimental import pallas as pl
from jax.experimental.pallas import tpu as pltpu

F32 = jnp.float32
BF16 = jnp.bfloat16

D_MODEL = 1024
GRID_W = 64
GDN_HEADS = 8
GDN_DK = 128
GDN_DV = 128
GDN_QK = GDN_HEADS * GDN_DK
GDN_V = GDN_HEADS * GDN_DV
GDN_QKV = 2 * GDN_QK + GDN_V
SSM_INNER = 2 * D_MODEL
SSM_HEAD_DIM = 64
SSM_HEADS = SSM_INNER // SSM_HEAD_DIM
SSM_GROUPS = 8
SSM_HPG = SSM_HEADS // SSM_GROUPS
SSM_STATE = 128
SSM_GN = SSM_GROUPS * SSM_STATE
SSM_XBC = SSM_INNER + 2 * SSM_GN
CONV_K = 5
CHUNK = 64
D_FF = 4 * D_MODEL
LN_EPS = 1e-5
NORM_EPS = 1e-6
IN_SPLITS = (GDN_QKV, GDN_V, 2 * GDN_HEADS, 2 * GDN_HEADS, SSM_INNER, SSM_XBC, 2 * SSM_HEADS, D_MODEL, D_MODEL)

LANES = 128
SUBLANES = 8
VMEM_LIMIT_BYTES = 56 * 1024 * 1024

Z_OFF = 0
GOUT_OFF = Z_OFF + SSM_INNER
GA_OFF = GOUT_OFF + GDN_V
GB_OFF = GA_OFF + D_MODEL
BIG_N = GB_OFF + D_MODEL
SM_A = 0
SM_B = SM_A + 2 * GDN_HEADS
SM_DT = SM_B + 2 * GDN_HEADS
SM_DT_B = SM_DT + SSM_HEADS
SMALL_N = LANES

GDN_CHUNKS_PER_STEP = 4
GDN_PREP_ROWS = 512
SCAN_OUT_DTYPE = BF16
MERGE_ROWS = 512
MLP_ROWS = 1024
PROJ_ROWS = 1024
PROJ_TILE_N = 1024
MOD_TILE_N = 1536
SSD_GROUPS_PER_STEP = 2
SSD_COLUMNS_PER_STEP = 8


def _params(n_axes):
    return pltpu.CompilerParams(dimension_semantics=("arbitrary",) * n_axes, vmem_limit_bytes=VMEM_LIMIT_BYTES)


def _silu(t):
    return t * jax.nn.sigmoid(t)


def _softplus(t):
    return jnp.maximum(t, 0.0) + jnp.log(1.0 + jnp.exp(-jnp.abs(t)))


def _mm(a, b):
    return jnp.dot(a.astype(BF16), b.astype(BF16), preferred_element_type=F32)


def _mm_nt(a, b):
    return lax.dot_general(a.astype(BF16), b.astype(BF16), (((1,), (1,)), ((), ())), preferred_element_type=F32)


def _mm_tn(a, b):
    return lax.dot_general(a.astype(BF16), b.astype(BF16), (((0,), (0,)), ((), ())), preferred_element_type=F32)


def _tri_masks(lower):
    ii = lax.broadcasted_iota(jnp.int32, (CHUNK, CHUNK), 0)
    jj = lax.broadcasted_iota(jnp.int32, (CHUNK, CHUNK), 1)
    if lower:
        return ii >= jj, ii > jj
    return ii <= jj, ii < jj


def _conv_seg(x, w, seg):
    n = x.shape[0]
    assert seg & (seg - 1) == 0 and n % seg == 0
    pos = lax.broadcasted_iota(jnp.int32, (n, 1), 0) & (seg - 1)
    out = None
    for j in range(CONV_K):
        d = j - CONV_K // 2
        if d == 0:
            term = x * w[j:j + 1, :]
        else:
            shifted = pltpu.roll(x, shift=(-d) % n, axis=0)
            valid = (pos + d >= 0) & (pos + d < seg)
            term = jnp.where(valid, shifted, 0.0) * w[j:j + 1, :]
        out = term if out is None else out + term
    return out


def _mod_kernel(c_ref, w_ref, b_ref, o_ref):
    o_ref[...] = _mm(_silu(c_ref[...]), w_ref[...]) + b_ref[...]


def _modulation(cc, w, b):
    rows, d = cc.shape
    n = w.shape[1]
    tn = MOD_TILE_N
    return pl.pallas_call(
        _mod_kernel,
        out_shape=jax.ShapeDtypeStruct((rows, n), F32),
        grid=(n // tn,),
        in_specs=[pl.BlockSpec((rows, d), lambda j: (0, 0)),
                  pl.BlockSpec((d, tn), lambda j: (0, j)),
                  pl.BlockSpec((1, tn), lambda j: (0, j))],
        out_specs=pl.BlockSpec((rows, tn), lambda j: (0, j)),
        compiler_params=_params(1),
        name="modulation",
    )(cc, w, b.reshape(1, n))


def _inproj_kernel(x_ref, sc_ref, sh_ref, w_ref, o_ref):
    h = x_ref[...] * (1.0 + sc_ref[...]) + sh_ref[...]
    o_ref[...] = jnp.dot(h.astype(BF16), w_ref[...], preferred_element_type=F32)


def _inproj(x, sc, sh, w, tm, tn, name):
    bsz, length, d = x.shape
    n = w.shape[1]
    return pl.pallas_call(
        _inproj_kernel,
        out_shape=jax.ShapeDtypeStruct((bsz, length, n), F32),
        grid=(bsz, length // tm, n // tn),
        in_specs=[pl.BlockSpec((None, tm, d), lambda b, i, j: (b, i, 0)),
                  pl.BlockSpec((None, 1, d), lambda b, i, j: (b, 0, 0)),
                  pl.BlockSpec((None, 1, d), lambda b, i, j: (b, 0, 0)),
                  pl.BlockSpec((d, tn), lambda b, i, j: (0, j))],
        out_specs=pl.BlockSpec((None, tm, tn), lambda b, i, j: (b, i, j)),
        compiler_params=_params(3),
        name=name,
    )(x, sc, sh, w)


def _proj_conv_kernel(*refs, seg, cols, has_bias, norm_tiles):
    x_ref, sc_ref, sh_ref, w_ref, cw_ref = refs[:5]
    k = 5
    cb_ref = refs[k] if has_bias else None
    k += has_bias
    perm_ref = refs[k] if cols else None
    k += cols
    o_ref = refs[k]
    kt_ref = refs[k + 1] if norm_tiles else None
    h_ref = refs[-1]
    j = pl.program_id(2)
    n_tok = h_ref.shape[0]

    @pl.when(j == 0)
    def _():
        x = x_ref[...]
        if cols:
            x = x.reshape(n_tok, x.shape[2])
        h = (x * (1.0 + sc_ref[...]) + sh_ref[...]).astype(BF16)
        if cols:
            h = jnp.dot(perm_ref[...], h, preferred_element_type=F32).astype(BF16)
        h_ref[...] = h

    y = jnp.dot(h_ref[...], w_ref[...], preferred_element_type=F32)
    y = _conv_seg(y, cw_ref[...], seg)
    if has_bias:
        y = y + cb_ref[...]
    y = _silu(y)

    def store(t):
        o_ref[...] = t.reshape(o_ref.shape)

    if norm_tiles:
        @pl.when(j < norm_tiles)
        def _():
            scale = jnp.where(j == 0, GDN_DK ** -0.5, 1.0)
            heads = []
            for hh in range(y.shape[1] // GDN_DK):
                yh = y[:, hh * GDN_DK:(hh + 1) * GDN_DK]
                heads.append(yh * (lax.rsqrt(jnp.sum(yh * yh, axis=1, keepdims=True) + NORM_EPS) * scale))
            store(jnp.concatenate(heads, axis=1))

            @pl.when(j == 1)
            def _():
                for hh, yh in enumerate(heads):
                    kt_ref[hh] = yh.T

        @pl.when(j >= norm_tiles)
        def _():
            store(y)
    else:
        store(y)


def _proj_conv(x, sc, sh, w, conv_w, conv_b, seg, cols, norm_tiles, tm, tn, name):
    bsz, length, d = x.shape
    n = w.shape[1]
    has_bias = conv_b is not None
    if cols:
        rows, cps = length // GRID_W, SUBLANES
        assert seg == rows and seg & (seg - 1) == 0
        n_tok = rows * cps
        x_in = x.reshape(bsz, rows, GRID_W, d)
        x_spec = pl.BlockSpec((None, rows, cps, d), lambda b, i, j: (b, 0, i, 0))
        out_shape = jax.ShapeDtypeStruct((bsz, GRID_W, rows, n), F32)
        out_spec = pl.BlockSpec((None, cps, rows, tn), lambda b, i, j: (b, i, 0, j))
        grid = (bsz, GRID_W // cps, n // tn)
    else:
        n_tok = tm
        x_in = x
        x_spec = pl.BlockSpec((None, tm, d), lambda b, i, j: (b, i, 0))
        out_shape = jax.ShapeDtypeStruct((bsz, length, n), F32)
        out_spec = pl.BlockSpec((None, tm, tn), lambda b, i, j: (b, i, j))
        grid = (bsz, length // tm, n // tn)
    mod = pl.BlockSpec((None, 1, d), lambda b, i, j: (b, 0, 0))
    in_specs = [x_spec, mod, mod, pl.BlockSpec((d, tn), lambda b, i, j: (0, j)),
                pl.BlockSpec((CONV_K, tn), lambda b, i, j: (0, j))]
    args = [x_in, sc, sh, w, conv_w]
    if has_bias:
        in_specs.append(pl.BlockSpec((1, tn), lambda b, i, j: (0, j)))
        args.append(conv_b.reshape(1, n))
    if cols:
        dst = jnp.arange(n_tok)
        src = (dst % rows) * cps + dst // rows
        in_specs.append(pl.BlockSpec((n_tok, n_tok), lambda b, i, j: (0, 0), pipeline_mode=pl.Buffered(1)))
        args.append((src[:, None] == jnp.arange(n_tok)[None, :]).astype(BF16))
    if norm_tiles:
        assert norm_tiles == 2 and tn == GDN_QK and not cols
        out_shape = (out_shape, jax.ShapeDtypeStruct((bsz, GDN_HEADS, GDN_DK, length), F32))
        out_spec = (out_spec, pl.BlockSpec((None, GDN_HEADS, GDN_DK, tm), lambda b, i, j: (b, 0, 0, i)))
    return pl.pallas_call(
        functools.partial(_proj_conv_kernel, seg=seg, cols=cols, has_bias=has_bias, norm_tiles=norm_tiles),
        out_shape=out_shape,
        grid=grid,
        in_specs=in_specs,
        out_specs=out_spec,
        scratch_shapes=[pltpu.VMEM((n_tok, d), BF16)],
        compiler_params=_params(3),
        name=name,
    )(*args)


def _chunk_cumsums(t):
    ii = lax.broadcasted_iota(jnp.int32, (2 * CHUNK, CHUNK), 0)
    jj = lax.broadcasted_iota(jnp.int32, (2 * CHUNK, CHUNK), 1)
    ones = ((ii < CHUNK) & (ii >= jj)) | ((ii >= CHUNK) & (ii - CHUNK <= jj))
    tri = jnp.where(ones, 1.0, 0.0).astype(BF16)
    hi = t.astype(BF16)
    r1 = t - hi.astype(F32)
    mid = r1.astype(BF16)
    lo = (r1 - mid.astype(F32)).astype(BF16)
    w = t.shape[1]
    sums = jnp.dot(tri, jnp.concatenate([hi, mid, lo], axis=1), preferred_element_type=F32)
    sums = sums[:, :w] + sums[:, w:2 * w] + sums[:, 2 * w:]
    return sums[:CHUNK], sums[CHUNK:]


def _gdn_prep_kernel(s_ref, alog_ref, dtb_ref, o_ref, *, rows):
    s = s_ref[...]
    lane = lax.broadcasted_iota(jnp.int32, (1, LANES), 1)
    g = -jnp.exp(alog_ref[...]) * _softplus(s + dtb_ref[...])
    beta = jax.nn.sigmoid(s)
    for c in range(rows // CHUNK):
        sl = slice(c * CHUNK, (c + 1) * CHUNK)
        fwd, bwd = _chunk_cumsums(g[sl, :])
        gc = jnp.where(lane < SM_A + GDN_HEADS, fwd, bwd)
        o_ref[sl, :] = jnp.where(lane < SM_B, gc, beta[sl, :])


def _gdn_prep(small, a_log, dt_bias, tb):
    bsz, length, _ = small.shape
    pad = lambda t: jnp.pad(t.reshape(1, -1).astype(F32), ((0, 0), (SM_A, LANES - SM_A - 2 * GDN_HEADS)))
    return pl.pallas_call(
        functools.partial(_gdn_prep_kernel, rows=tb),
        out_shape=jax.ShapeDtypeStruct((bsz, length, LANES), F32),
        grid=(bsz, length // tb),
        in_specs=[pl.BlockSpec((None, tb, LANES), lambda b, i: (b, i, 0)),
                  pl.BlockSpec((1, LANES), lambda b, i: (0, 0)),
                  pl.BlockSpec((1, LANES), lambda b, i: (0, 0))],
        out_specs=pl.BlockSpec((None, tb, LANES), lambda b, i: (b, i, 0)),
        compiler_params=_params(2),
        name="gdn_prep",
    )(small, pad(a_log), pad(dt_bias))


def _tri_inverse(mats):
    ii = lax.broadcasted_iota(jnp.int32, (CHUNK, CHUNK), 0)
    jj = lax.broadcasted_iota(jnp.int32, (CHUNK, CHUNK), 1)
    eye = jnp.where(ii == jj, 1.0, 0.0)
    zero = jnp.zeros((CHUNK, CHUNK), F32)
    right = lax.broadcasted_iota(jnp.int32, (CHUNK, 2 * CHUNK), 1) >= CHUNK
    zs = [jnp.concatenate([zero, eye], axis=1) + _mm(a, jnp.concatenate([a, -eye], axis=1)) for a in mats]
    power = 2
    while power < CHUNK:
        ps = [_mm(z[:, :CHUNK], z) for z in zs]
        zs = [p + jnp.where(right, z, 0.0) for p, z in zip(ps, zs)]
        power *= 2
    return zs


def _gdn_local(items, masks):
    qs, ks, vs, kts, gcols, grows, bcols, lowers = zip(*items)
    n = len(items)
    incl = [masks[lo][0] for lo in lowers]
    strict = [masks[lo][1] for lo in lowers]
    glast = [grows[i][:, CHUNK - 1:CHUNK] if lowers[i] else grows[i][:, 0:1] for i in range(n)]
    kq = [_mm(jnp.concatenate([ks[i].astype(BF16), qs[i].astype(BF16)], axis=0), kts[i]) for i in range(n)]
    dec = [jnp.exp(jnp.where(incl[i], gcols[i] - grows[i], 0.0)) for i in range(n)]
    a_mats = [jnp.where(strict[i], bcols[i] * kq[i][:CHUNK] * dec[i], 0.0) for i in range(n)]
    t_inv = _tri_inverse(a_mats)
    eg = [jnp.exp(g) for g in gcols]
    rhs = [jnp.concatenate([(bcols[i] * eg[i]) * ks[i], bcols[i] * vs[i]], axis=1) for i in range(n)]
    pad = jnp.zeros((CHUNK, GDN_DK + GDN_DV), F32)
    wu = [_mm(t, jnp.concatenate([pad, r], axis=0)) for t, r in zip(t_inv, rhs)]
    lhs = [jnp.concatenate([wu[i][:, :GDN_DK].astype(BF16), (qs[i] * eg[i]).astype(BF16)], axis=0) for i in range(n)]
    qk_masked = [jnp.where(incl[i], kq[i][CHUNK:] * dec[i], 0.0).astype(BF16) for i in range(n)]
    kt_dec = [(kts[i] * jnp.exp(glast[i] - grows[i])).astype(BF16) for i in range(n)]
    return [(lhs[i], wu[i][:, GDN_DK:], qk_masked[i], kt_dec[i], jnp.exp(glast[i])) for i in range(n)]


def _gdn_on_state(local, states):
    n = len(local)
    on_state = [_mm(local[i][0], states[i]) for i in range(n)]
    v_new = [local[i][1] - on_state[i][:CHUNK] for i in range(n)]
    o_intra = [_mm(local[i][2], v_new[i]) for i in range(n)]
    kv = [_mm(local[i][3], v_new[i]) for i in range(n)]
    return [(on_state[i][CHUNK:] + o_intra[i], local[i][4] * states[i] + kv[i]) for i in range(n)]


def _gdn_kernel(qf_ref, kf_ref, vf_ref, ktf_ref, qb_ref, kb_ref, vb_ref, ktb_ref, pf_ref, pb_ref, ptf_ref, ptb_ref,
                s0_ref, of_ref, ob_ref, sout_ref, state_ref, *, n_steps, nc):
    step = pl.program_id(1)
    hs = GDN_HEADS

    @pl.when(step == 0)
    def _():
        state_ref[...] = s0_ref[...]

    def heads_of(q_ref, k_ref, v_ref):
        q, k, v = q_ref[...], k_ref[...], v_ref[...]
        return [(q[:, hh * GDN_DK:(hh + 1) * GDN_DK], k[:, hh * GDN_DK:(hh + 1) * GDN_DK],
                 v[:, hh * GDN_DV:(hh + 1) * GDN_DV]) for hh in range(hs)]

    heads_f = heads_of(qf_ref, kf_ref, vf_ref)
    heads_b = heads_of(qb_ref, kb_ref, vb_ref)
    pf = pf_ref[...]
    pb = pb_ref[...]
    masks = {True: _tri_masks(True), False: _tri_masks(False)}
    col = lambda t, idx: t[:, idx:idx + 1]
    cols = [(col(pf, SM_A + hh), col(pf, SM_B + hh), col(pb, SM_A + hs + hh), col(pb, SM_B + hs + hh))
            for hh in range(hs)]
    items = []
    for c in range(nc):
        cb = nc - 1 - c
        sl_f = slice(c * CHUNK, (c + 1) * CHUNK)
        sl_b = slice(cb * CHUNK, (cb + 1) * CHUNK)
        for hh in range(hs):
            gcol_f, bcol_f, gcol_b, bcol_b = cols[hh]
            qf, kf, vf = heads_f[hh]
            qb, kb, vb = heads_b[hh]
            grow_f = ptf_ref[c, SM_A + hh:SM_A + hh + 1, :]
            grow_b = ptb_ref[cb, SM_A + hs + hh:SM_A + hs + hh + 1, :]
            items.append((qf[sl_f], kf[sl_f], vf[sl_f], ktf_ref[hh, :, sl_f], gcol_f[sl_f], grow_f, bcol_f[sl_f], True))
            items.append((qb[sl_b], kb[sl_b], vb[sl_b], ktb_ref[hh, :, sl_b], gcol_b[sl_b], grow_b, bcol_b[sl_b], False))
    local = _gdn_local(items, masks)
    states = [state_ref[hh, d] for hh in range(hs) for d in range(2)]
    for c in range(nc):
        cb = nc - 1 - c
        results = _gdn_on_state(local[c * 2 * hs:(c + 1) * 2 * hs], states)
        states = [r[1] for r in results]
        for hh in range(hs):
            hl = slice(hh * GDN_DV, (hh + 1) * GDN_DV)
            of_ref[c * CHUNK:(c + 1) * CHUNK, hl] = results[2 * hh][0].astype(of_ref.dtype)
            ob_ref[cb * CHUNK:(cb + 1) * CHUNK, hl] = results[2 * hh + 1][0].astype(ob_ref.dtype)
    for hh in range(hs):
        state_ref[hh, 0] = states[2 * hh]
        state_ref[hh, 1] = states[2 * hh + 1]

    @pl.when(step == n_steps - 1)
    def _():
        sout_ref[...] = state_ref[...]


def _gdn_scan(qkv, kt, p, pt, s0, nc):
    bsz, length, _ = qkv.shape
    tb = nc * CHUNK
    n_steps = length // tb
    qoff, koff, voff = 0, 1, 2
    fwd = lambda off: pl.BlockSpec((None, tb, GDN_QK), lambda b, s: (b, s, off))
    bwd = lambda off: pl.BlockSpec((None, tb, GDN_QK), lambda b, s: (b, n_steps - 1 - s, off))
    state_spec = pl.BlockSpec((None, GDN_HEADS, 2, GDN_DK, GDN_DV), lambda b, s: (b, 0, 0, 0, 0))
    return pl.pallas_call(
        functools.partial(_gdn_kernel, n_steps=n_steps, nc=nc),
        out_shape=(jax.ShapeDtypeStruct((bsz, length, GDN_V), SCAN_OUT_DTYPE),
                   jax.ShapeDtypeStruct((bsz, length, GDN_V), SCAN_OUT_DTYPE),
                   jax.ShapeDtypeStruct((bsz, GDN_HEADS, 2, GDN_DK, GDN_DV), F32)),
        grid=(bsz, n_steps),
        in_specs=[fwd(qoff), fwd(koff), fwd(voff), pl.BlockSpec((None, GDN_HEADS, GDN_DK, tb), lambda b, s: (b, 0, 0, s)),
                  bwd(qoff), bwd(koff), bwd(voff),
                  pl.BlockSpec((None, GDN_HEADS, GDN_DK, tb), lambda b, s: (b, 0, 0, n_steps - 1 - s)),
                  pl.BlockSpec((None, tb, LANES), lambda b, s: (b, s, 0)),
                  pl.BlockSpec((None, tb, LANES), lambda b, s: (b, n_steps - 1 - s, 0)),
                  pl.BlockSpec((None, nc, 4 * SUBLANES, CHUNK), lambda b, s: (b, s, 0, 0)),
                  pl.BlockSpec((None, nc, 4 * SUBLANES, CHUNK), lambda b, s: (b, n_steps - 1 - s, 0, 0)),
                  state_spec],
        out_specs=(pl.BlockSpec((None, tb, GDN_V), lambda b, s: (b, s, 0)),
                   pl.BlockSpec((None, tb, GDN_V), lambda b, s: (b, n_steps - 1 - s, 0)),
                   state_spec),
        scratch_shapes=[pltpu.VMEM((GDN_HEADS, 2, GDN_DK, GDN_DV), F32)],
        compiler_params=_params(2),
        name="gdn_scan",
    )(qkv, qkv, qkv, kt, qkv, qkv, qkv, kt, p, p, pt, pt, s0)


def _gdn_branch(qkv, kt, small, qkv_c, kt_c, small_c, a_log, dt_bias):
    bsz = qkv.shape[0]
    s0 = jnp.zeros((bsz, GDN_HEADS, 2, GDN_DK, GDN_DV), F32)

    def rows_t(p):
        chunks = p[:, :, :4 * SUBLANES].reshape(bsz, p.shape[1] // CHUNK, CHUNK, 4 * SUBLANES)
        return jnp.swapaxes(chunks, 2, 3)

    ctx_len = qkv_c.shape[1]
    p_c = _gdn_prep(small_c, a_log, dt_bias, ctx_len)
    oc_f, oc_b, s_ctx = _gdn_scan(qkv_c, kt_c, p_c, rows_t(p_c), s0, min(GDN_CHUNKS_PER_STEP, ctx_len // CHUNK))
    p = _gdn_prep(small, a_log, dt_bias, GDN_PREP_ROWS)
    o_f, o_b, _ = _gdn_scan(qkv, kt, p, rows_t(p), s_ctx, GDN_CHUNKS_PER_STEP)
    return (o_f, o_b), (oc_f, oc_b)


SSD_QUANTITIES = 4
SSD_HEADS_PER_STEP = SSD_GROUPS_PER_STEP * SSM_HPG
SSD_STEP_WIDTH = SSD_HEADS_PER_STEP * SSM_HEAD_DIM
BF16_PIECES = 3


def _ssd_prep_kernel(s_ref, alog_ref, dtb_ref, o_ref, *, rows):
    lane = lax.broadcasted_iota(jnp.int32, (1, LANES), 1)
    fwd_lane = lane < SM_DT_B
    neg_a = -jnp.exp(alog_ref[...])
    for seg in range(o_ref.shape[0]):
        dt = _softplus(s_ref[:, seg * LANES:(seg + 1) * LANES] + dtb_ref[...])
        la = dt * neg_a
        o_ref[seg, :, 0:LANES] = dt
        for c in range(rows // CHUNK):
            sl = slice(c * CHUNK, (c + 1) * CHUNK)
            fwd, bwd = _chunk_cumsums(la[sl, :])
            gc = jnp.where(fwd_lane, fwd, bwd)
            g_last = jnp.where(fwd_lane, gc[CHUNK - 1:CHUNK, :], gc[0:1, :])
            o_ref[seg, sl, LANES:2 * LANES] = gc
            o_ref[seg, sl, 2 * LANES:3 * LANES] = jnp.exp(gc)
            o_ref[seg, sl, 3 * LANES:4 * LANES] = dt[sl, :] * jnp.exp(g_last - gc)


def _ssd_prep(small_cols, a_log, dt_bias, n_seg):
    bsz, rows, _ = small_cols.shape
    sps = min(n_seg, SUBLANES)
    pad = lambda t: jnp.pad(t.reshape(1, -1).astype(F32), ((0, 0), (SM_DT, LANES - SM_DT - 2 * SSM_HEADS)))
    return pl.pallas_call(
        functools.partial(_ssd_prep_kernel, rows=rows),
        out_shape=jax.ShapeDtypeStruct((bsz, n_seg, rows, SSD_QUANTITIES * LANES), F32),
        grid=(bsz, n_seg // sps),
        in_specs=[pl.BlockSpec((None, rows, sps * LANES), lambda b, c: (b, 0, c)),
                  pl.BlockSpec((1, LANES), lambda b, c: (0, 0)),
                  pl.BlockSpec((1, LANES), lambda b, c: (0, 0))],
        out_specs=pl.BlockSpec((None, sps, rows, SSD_QUANTITIES * LANES), lambda b, c: (b, c, 0, 0)),
        compiler_params=_params(2),
        name="ssd_prep",
    )(small_cols, pad(a_log), pad(dt_bias))


def _ssd_layouts(q):
    bsz, n_seg, rows, _ = q.shape
    steps = SSM_GROUPS // SSD_GROUPS_PER_STEP
    q = q.reshape(bsz, n_seg, rows, SSD_QUANTITIES, LANES)[..., SM_DT:SM_DT + 2 * SSM_HEADS]
    q = q.reshape(bsz, n_seg, rows, SSD_QUANTITIES, 2, steps, SSD_HEADS_PER_STEP)
    out = []
    for d in range(2):
        qd = q[:, :, :, :, d]
        cols = jnp.transpose(qd, (0, 1, 4, 2, 3, 5)).reshape(bsz, n_seg, steps, rows, SSD_QUANTITIES * SSD_HEADS_PER_STEP)
        as_rows = lambda t: jnp.transpose(
            t.reshape(bsz, n_seg, rows // CHUNK, CHUNK, steps, SSD_HEADS_PER_STEP),
            (0, 1, 4, 2, 5, 3)).reshape(bsz, n_seg, steps, rows // CHUNK, SSD_STEP_WIDTH)
        out.append((cols, jnp.concatenate([as_rows(qd[:, :, :, 1]), as_rows(qd[:, :, :, 0])], axis=-1)))
    return out


def _ssd_expand_matrix():
    k = SSD_QUANTITIES * SSD_HEADS_PER_STEP
    src = jnp.arange(k)
    dst = jnp.arange(SSD_STEP_WIDTH, k * SSM_HEAD_DIM) // SSM_HEAD_DIM
    one = (src[:, None] == dst[None, :]).astype(BF16)
    return jnp.concatenate([one] * BF16_PIECES, axis=0)


def _ssd_chunks(items, incl, block_diag):
    n = len(items)
    cb = [_mm_nt(it[4], jnp.concatenate([it[3].astype(BF16)] * SSM_HPG, axis=0)) for it in items]
    inter = [_mm(it[4], it[9]) for it in items]
    lhs = []
    for i, it in enumerate(items):
        m = incl[it[10]]
        lhs.append(jnp.where(m, cb[i] * jnp.exp(jnp.where(m, it[5] - it[6], 0.0)) * it[1], 0.0))
    rhs = [jnp.where(block_diag, jnp.concatenate([it[0].astype(BF16)] * SSM_HPG, axis=0), 0.0) for it in items]
    intra = [_mm(lhs[i], rhs[i]) for i in range(n)]
    upd = [_mm_tn(it[3], it[2]) for it in items]
    out = []
    for i, it in enumerate(items):
        y = intra[i] + inter[i] * it[7]
        if it[11] is not None:
            y = y + it[11] * it[0]
        out.append((y, it[8] * it[9] + upd[i]))
    return out


def _ssd_kernel(xf_ref, bf_ref, cf_ref, xb_ref, bb_ref, cb_ref, colf_ref, colb_ref, growf_ref, growb_ref,
                e_ref, d_ref, s0_ref, yf_ref, yb_ref, sout_ref, state_ref, *, rows, n_seg, gs):
    step = pl.program_id(2)

    @pl.when(step == 0)
    def _():
        state_ref[...] = s0_ref[...]

    gw = SSM_HPG * SSM_HEAD_DIM
    width = gs * gw

    def expand(col_ref):
        c = col_ref[...]
        hi = c.astype(BF16)
        r1 = c - hi.astype(F32)
        mid = r1.astype(BF16)
        lo = (r1 - mid.astype(F32)).astype(BF16)
        ex = jnp.dot(jnp.concatenate([hi, mid, lo], axis=1), e_ref[...], preferred_element_type=F32)
        return [ex[:, q * width:(q + 1) * width] for q in range(SSD_QUANTITIES - 1)]

    xf, xb = xf_ref[...], xb_ref[...]
    gcx_f, egx_f, q4x_f = expand(colf_ref)
    gcx_b, egx_b, q4x_b = expand(colb_ref)
    xdec_f, xdec_b = xf * q4x_f, xb * q4x_b
    bmf, cmf, bmb, cmb = bf_ref[...], cf_ref[...], bb_ref[...], cb_ref[...]
    d_all = d_ref[...]

    row = lax.broadcasted_iota(jnp.int32, (CHUNK, gw), 0)
    tok = lax.broadcasted_iota(jnp.int32, (CHUNK, gw), 1) & (CHUNK - 1)
    incl = {True: row >= tok, False: row <= tok}
    block_diag = (lax.shift_right_logical(lax.broadcasted_iota(jnp.int32, (gw, gw), 0), CHUNK.bit_length() - 1)
                  == lax.shift_right_logical(lax.broadcasted_iota(jnp.int32, (gw, gw), 1), SSM_HEAD_DIM.bit_length() - 1))

    n_chunks = rows // CHUNK
    states = [[state_ref[gg, 0], state_ref[gg, 1]] for gg in range(gs)]
    for c in range(n_chunks):
        cb = n_chunks - 1 - c
        sl_f = slice(c * CHUNK, (c + 1) * CHUNK)
        sl_b = slice(cb * CHUNK, (cb + 1) * CHUNK)
        last_f = slice(c * CHUNK + CHUNK - 1, (c + 1) * CHUNK)
        last_b = slice(cb * CHUNK, cb * CHUNK + 1)
        items = []
        for gg in range(gs):
            xl = slice(gg * gw, (gg + 1) * gw)
            nl = slice(gg * SSM_STATE, (gg + 1) * SSM_STATE)
            dl = slice(width + gg * gw, width + (gg + 1) * gw)
            items.append((xf[sl_f, xl], growf_ref[c:c + 1, dl], xdec_f[sl_f, xl], bmf[sl_f, nl], cmf[sl_f, nl],
                          gcx_f[sl_f, xl], growf_ref[c:c + 1, xl], egx_f[sl_f, xl], egx_f[last_f, xl], states[gg][0], True,
                          d_all[:, xl]))
            items.append((xb[sl_b, xl], growb_ref[cb:cb + 1, dl], xdec_b[sl_b, xl], bmb[sl_b, nl], cmb[sl_b, nl],
                          gcx_b[sl_b, xl], growb_ref[cb:cb + 1, xl], egx_b[sl_b, xl], egx_b[last_b, xl], states[gg][1], False,
                          None))
        results = _ssd_chunks(items, incl, block_diag)
        for gg in range(gs):
            xl = slice(gg * gw, (gg + 1) * gw)
            yf_ref[sl_f, xl] = results[2 * gg][0].astype(yf_ref.dtype)
            yb_ref[sl_b, xl] = results[2 * gg + 1][0].astype(yb_ref.dtype)
            states[gg] = [results[2 * gg][1], results[2 * gg + 1][1]]
    for gg in range(gs):
        state_ref[gg, 0] = states[gg][0]
        state_ref[gg, 1] = states[gg][1]

    @pl.when(step == n_seg - 1)
    def _():
        sout_ref[...] = state_ref[...]


def _ssd_scan(xbc, layouts, expand, d_exp, s0):
    bsz, n_seg, rows, _ = xbc.shape
    gs = SSD_GROUPS_PER_STEP
    xw, nw = SSD_STEP_WIDTH, gs * SSM_STATE
    boff, coff = SSM_INNER // nw, (SSM_INNER + SSM_GN) // nw
    (cols_f, grow_f), (cols_b, grow_b) = layouts
    n_cols = cols_f.shape[-1]
    seg_f = lambda s: s
    seg_b = lambda s: n_seg - 1 - s
    xspec = lambda seg: pl.BlockSpec((None, None, rows, xw), lambda b, g, s: (b, seg(s), 0, g))
    nspec = lambda seg, off: pl.BlockSpec((None, None, rows, nw), lambda b, g, s: (b, seg(s), 0, off + g))
    cspec = lambda seg: pl.BlockSpec((None, None, None, rows, n_cols), lambda b, g, s: (b, seg(s), g, 0, 0))
    rspec = lambda seg: pl.BlockSpec((None, None, None, rows // CHUNK, 2 * xw), lambda b, g, s: (b, seg(s), g, 0, 0))
    state_spec = pl.BlockSpec((None, gs, 2, SSM_STATE, SSM_HPG * SSM_HEAD_DIM), lambda b, g, s: (b, g, 0, 0, 0))
    y_shape = jax.ShapeDtypeStruct((bsz, n_seg, rows, SSM_INNER), SCAN_OUT_DTYPE)
    return pl.pallas_call(
        functools.partial(_ssd_kernel, rows=rows, n_seg=n_seg, gs=gs),
        out_shape=(y_shape, y_shape,
                   jax.ShapeDtypeStruct((bsz, SSM_GROUPS, 2, SSM_STATE, SSM_HPG * SSM_HEAD_DIM), F32)),
        grid=(bsz, SSM_GROUPS // gs, n_seg),
        in_specs=[xspec(seg_f), nspec(seg_f, boff), nspec(seg_f, coff),
                  xspec(seg_b), nspec(seg_b, boff), nspec(seg_b, coff),
                  cspec(seg_f), cspec(seg_b), rspec(seg_f), rspec(seg_b),
                  pl.BlockSpec(expand.shape, lambda b, g, s: (0, 0)),
                  pl.BlockSpec((1, xw), lambda b, g, s: (0, g)), state_spec],
        out_specs=(xspec(seg_f), xspec(seg_b), state_spec),
        scratch_shapes=[pltpu.VMEM((gs, 2, SSM_STATE, SSM_HPG * SSM_HEAD_DIM), F32)],
        compiler_params=_params(3),
        name="ssd_scan",
    )(xbc, xbc, xbc, xbc, xbc, xbc, cols_f, cols_b, grow_f, grow_b, expand, d_exp, s0)


def _ssd_branch(xbc, small, xbc_c, small_c, a_log, dt_bias, d_skip):
    bsz, n_seg, rows, _ = xbc.shape
    assert rows % CHUNK == 0 and xbc_c.shape[2] % CHUNK == 0
    d_exp = jnp.repeat(d_skip.astype(F32), SSM_HEAD_DIM).reshape(1, SSM_INNER)
    s0 = jnp.zeros((bsz, SSM_GROUPS, 2, SSM_STATE, SSM_HPG * SSM_HEAD_DIM), F32)
    expand = _ssd_expand_matrix()
    yc_f, yc_b, s_ctx = _ssd_scan(xbc_c, _ssd_layouts(_ssd_prep(small_c, a_log, dt_bias, 1)), expand, d_exp, s0)
    q = _ssd_prep(small.reshape(bsz, rows, n_seg * SMALL_N), a_log, dt_bias, n_seg)
    steps, step_rows = n_seg // SSD_COLUMNS_PER_STEP, SSD_COLUMNS_PER_STEP * rows
    merged = lambda t: t.reshape(bsz, steps, step_rows, t.shape[-1])
    y_f, y_b, _ = _ssd_scan(merged(xbc), _ssd_layouts(merged(q)), expand, d_exp, s_ctx)
    to_raster = lambda y: jnp.swapaxes(y.reshape(bsz, n_seg, rows, SSM_INNER), 1, 2).reshape(bsz, rows * n_seg, SSM_INNER)
    return (to_raster(y_f), to_raster(y_b)), (yc_f[:, 0], yc_b[:, 0])


def _layer_norm(r, g, b):
    mu = jnp.mean(r, axis=1, keepdims=True)
    var = jnp.mean(jnp.square(r - mu), axis=1, keepdims=True)
    return (r - mu) * lax.rsqrt(var + LN_EPS) * g + b


def _merge_kernel(x_ref, sc_ref, sh_ref, wg_ref, of_ref, ob_ref, yf_ref, yb_ref,
                  nwa_ref, nwb_ref, wpg_ref, wps_ref, wout_ref, g1_ref, lng_ref, lnb_ref, o_ref, *, alpha):
    x = x_ref[...]
    h = (x * (1.0 + sc_ref[...]) + sh_ref[...]).astype(BF16)
    gate = lambda off, width: jnp.dot(h, wg_ref[:, off:off + width], preferred_element_type=F32)
    o = of_ref[...].astype(F32) + ob_ref[...].astype(F32)
    normed = []
    for hh in range(GDN_HEADS):
        oh = o[:, hh * GDN_DV:(hh + 1) * GDN_DV]
        normed.append(oh * lax.rsqrt(jnp.mean(oh * oh, axis=1, keepdims=True) + NORM_EPS))
    y_a = jnp.concatenate(normed, axis=1) * nwa_ref[...] * _silu(gate(GOUT_OFF, GDN_V))
    t = (yf_ref[...].astype(F32) + yb_ref[...].astype(F32)) * _silu(gate(Z_OFF, SSM_INNER))
    gw = SSM_INNER // SSM_GROUPS
    normed = []
    for g in range(SSM_GROUPS):
        tg = t[:, g * gw:(g + 1) * gw]
        normed.append(tg * lax.rsqrt(jnp.mean(tg * tg, axis=1, keepdims=True) + NORM_EPS))
    y_b = jnp.concatenate(normed, axis=1) * nwb_ref[...]
    d = x.shape[1]
    mix = (jax.nn.sigmoid(gate(GA_OFF, d)) * jnp.dot(y_a.astype(BF16), wpg_ref[...], preferred_element_type=F32)
           + jax.nn.sigmoid(gate(GB_OFF, d)) * jnp.dot(y_b.astype(BF16), wps_ref[...], preferred_element_type=F32))
    out = jnp.dot(mix.astype(BF16), wout_ref[...], preferred_element_type=F32)
    o_ref[...] = _layer_norm(alpha * x + g1_ref[...] * out, lng_ref[...], lnb_ref[...])


def _merge(x, sc, sh, w_gates, o_pair, y_pair, nwa, nwb, wpg, wps, wout, g1, ln_g, ln_b, alpha, tm):
    bsz, length, d = x.shape
    row = lambda width: pl.BlockSpec((None, tm, width), lambda b, i: (b, i, 0))
    const = lambda shape: pl.BlockSpec(shape, lambda b, i: (0,) * len(shape), pipeline_mode=pl.Buffered(1))
    mod = pl.BlockSpec((None, 1, d), lambda b, i: (b, 0, 0))
    return pl.pallas_call(
        functools.partial(_merge_kernel, alpha=alpha),
        out_shape=jax.ShapeDtypeStruct((bsz, length, d), F32),
        grid=(bsz, length // tm),
        in_specs=[row(d), mod, mod, const((d, BIG_N)),
                  row(GDN_V), row(GDN_V), row(SSM_INNER), row(SSM_INNER),
                  const((1, GDN_V)), const((1, SSM_INNER)), const((GDN_V, d)), const((SSM_INNER, d)), const((d, d)),
                  mod, const((1, d)), const((1, d))],
        out_specs=row(d),
        compiler_params=_params(2),
        name="merge",
    )(x, sc, sh, w_gates, o_pair[0], o_pair[1], y_pair[0], y_pair[1], nwa, nwb, wpg, wps, wout, g1, ln_g, ln_b)


def _mlp_kernel(x_ref, sc_ref, sh_ref, g2_ref, w1_ref, b1_ref, w2_ref, b2_ref, lng_ref, lnb_ref, o_ref, *, alpha):
    x = x_ref[...]
    h = (x * (1.0 + sc_ref[...]) + sh_ref[...]).astype(BF16)
    acc = None
    tf = D_MODEL
    for c in range(D_FF // tf):
        u = jnp.dot(h, w1_ref[:, c * tf:(c + 1) * tf], preferred_element_type=F32) + b1_ref[:, c * tf:(c + 1) * tf]
        u = jnp.square(jnp.maximum(u, 0.0))
        part = jnp.dot(u.astype(BF16), w2_ref[c * tf:(c + 1) * tf, :], preferred_element_type=F32)
        acc = part if acc is None else acc + part
    f = acc + b2_ref[...]
    o_ref[...] = _layer_norm(alpha * x + g2_ref[...] * f, lng_ref[...], lnb_ref[...])


def _mlp(x, sc, sh, g2, w1, b1, w2, b2, ln_g, ln_b, alpha, tm):
    bsz, length, d = x.shape
    const = lambda shape: pl.BlockSpec(shape, lambda b, i: (0,) * len(shape), pipeline_mode=pl.Buffered(1))
    mod = pl.BlockSpec((None, 1, d), lambda b, i: (b, 0, 0))
    return pl.pallas_call(
        functools.partial(_mlp_kernel, alpha=alpha),
        out_shape=jax.ShapeDtypeStruct((bsz, length, d), F32),
        grid=(bsz, length // tm),
        in_specs=[pl.BlockSpec((None, tm, d), lambda b, i: (b, i, 0)), mod, mod, mod,
                  const((d, D_FF)), const((1, D_FF)), const((D_FF, d)), const((1, d)), const((1, d)), const((1, d))],
        out_specs=pl.BlockSpec((None, tm, d), lambda b, i: (b, i, 0)),
        compiler_params=_params(2),
        name="mlp",
    )(x, sc, sh, g2, w1, b1, w2, b2, ln_g, ln_b)


def _split_w_in(w_in):
    pts, acc = [], 0
    for s in IN_SPLITS[:-1]:
        acc += s
        pts.append(acc)
    qkv, gout, a_raw, b_raw, z, xbc, dt_raw, gate_a, gate_b = jnp.split(w_in, pts, axis=1)
    big = jnp.concatenate([z, gout, gate_a, gate_b], axis=1).astype(BF16)
    pad = jnp.zeros((w_in.shape[0], SMALL_N - SM_DT - 2 * SSM_HEADS), w_in.dtype)
    small = jnp.concatenate([a_raw, b_raw, dt_raw, pad], axis=1).astype(BF16)
    return big, qkv.astype(BF16), xbc.astype(BF16), small


def kernel(x, c, ctx, c_ctx, w_mod, b_mod, w_in, gdn_conv_w, gdn_A_log, gdn_dt_bias, gdn_norm_w,
           ssm_conv_w, ssm_conv_b, ssm_A_log, ssm_dt_bias, ssm_D, ssm_norm_w,
           w_proj_gdn, w_proj_ssm, w_out, ln1_g, ln1_b, w_ff1, b_ff1, w_ff2, b_ff2, ln2_g, ln2_b):
    bsz, length, d = x.shape
    ctx_len = ctx.shape[1]
    depth = w_mod.shape[0]
    alpha = float((2 * depth) ** 0.25)
    mod_rows = -(-(bsz + 1) // (2 * SUBLANES)) * (2 * SUBLANES)
    cc = jnp.concatenate([c, c_ctx[None, :], jnp.zeros((mod_rows - bsz - 1, d), c.dtype)], axis=0)
    row2 = lambda t: t.reshape(1, -1)
    tm_lat = PROJ_ROWS
    tm_ctx = ctx_len

    for l in range(depth):
        last = l == depth - 1
        mod = _modulation(cc, w_mod[l], b_mod[l])
        lat = [mod[:bsz, i * d:(i + 1) * d].reshape(bsz, 1, d) for i in range(6)]
        cxm = [jnp.broadcast_to(mod[bsz, i * d:(i + 1) * d].reshape(1, 1, d), (bsz, 1, d)) for i in range(6)]
        w_big, w_qkv, w_xbc, w_small = _split_w_in(w_in[l])
        rows = length // GRID_W

        small = _inproj(x, lat[1], lat[0], w_small, tm_lat, SMALL_N, "inproj_narrow")
        qkv, kt = _proj_conv(x, lat[1], lat[0], w_qkv, gdn_conv_w[l], None, GRID_W, False, 2, tm_lat, GDN_QK,
                             "inproj_qkv")
        xbc = _proj_conv(x, lat[1], lat[0], w_xbc, ssm_conv_w[l], ssm_conv_b[l], rows, True, 0, None, PROJ_TILE_N,
                         "inproj_xbc")
        small_c = _inproj(ctx, cxm[1], cxm[0], w_small, tm_ctx, SMALL_N, "inproj_narrow_ctx")
        qkv_c, kt_c = _proj_conv(ctx, cxm[1], cxm[0], w_qkv, gdn_conv_w[l], None, ctx_len, False, 2, tm_ctx, GDN_QK,
                                 "inproj_qkv_ctx")
        xbc_c = _proj_conv(ctx, cxm[1], cxm[0], w_xbc, ssm_conv_w[l], ssm_conv_b[l], ctx_len, False, 0, tm_ctx, PROJ_TILE_N,
                           "inproj_xbc_ctx")[:, None]

        o_pair, oc_pair = _gdn_branch(qkv, kt, small, qkv_c, kt_c, small_c, gdn_A_log[l], gdn_dt_bias[l])
        y_pair, yc_pair = _ssd_branch(xbc, small, xbc_c, small_c, ssm_A_log[l], ssm_dt_bias[l], ssm_D[l])

        nwa = jnp.tile(gdn_norm_w[l], GDN_HEADS).reshape(1, GDN_V)
        nwb = row2(ssm_norm_w[l])
        wpg, wps, wo = w_proj_gdn[l].astype(BF16), w_proj_ssm[l].astype(BF16), w_out[l].astype(BF16)
        w1, w2 = w_ff1[l].astype(BF16), w_ff2[l].astype(BF16)
        merge_args = (nwa, nwb, wpg, wps, wo)
        ln1 = (row2(ln1_g[l]), row2(ln1_b[l]))
        mlp_w = (w1, row2(b_ff1[l]), w2, row2(b_ff2[l]), row2(ln2_g[l]), row2(ln2_b[l]))

        x1 = _merge(x, lat[1], lat[0], w_big, o_pair, y_pair, *merge_args, lat[2], *ln1, alpha, MERGE_ROWS)
        x = _mlp(x1, lat[4], lat[3], lat[5], *mlp_w, alpha, MLP_ROWS)
        if not last:
            c1 = _merge(ctx, cxm[1], cxm[0], w_big, oc_pair, yc_pair, *merge_args, cxm[2], *ln1, alpha, ctx_len)
            ctx = _mlp(c1, cxm[4], cxm[3], cxm[5], *mlp_w, alpha, ctx_len)
    return x
```

```python
import functools

import jax
import jax.numpy as jnp
from jax import lax
from jax.experimental import pallas as pl
from jax.experimental.pallas import tpu as pltpu

F32 = jnp.float32
BF16 = jnp.bfloat16

D_MODEL = 1024
GRID_W = 64
GDN_HEADS = 8
GDN_DK = 128
GDN_DV = 128
GDN_QK = GDN_HEADS * GDN_DK
GDN_V = GDN_HEADS * GDN_DV
GDN_QKV = 2 * GDN_QK + GDN_V
SSM_INNER = 2 * D_MODEL
SSM_HEAD_DIM = 64
SSM_HEADS = SSM_INNER // SSM_HEAD_DIM
SSM_GROUPS = 8
SSM_HPG = SSM_HEADS // SSM_GROUPS
SSM_STATE = 128
SSM_GN = SSM_GROUPS * SSM_STATE
SSM_XBC = SSM_INNER + 2 * SSM_GN
CONV_K = 5
CHUNK = 64
D_FF = 4 * D_MODEL
LN_EPS = 1e-5
NORM_EPS = 1e-6
IN_SPLITS = (GDN_QKV, GDN_V, 2 * GDN_HEADS, 2 * GDN_HEADS, SSM_INNER, SSM_XBC, 2 * SSM_HEADS, D_MODEL, D_MODEL)

LANES = 128
SUBLANES = 8
VMEM_LIMIT_BYTES = 56 * 1024 * 1024

Z_OFF = 0
GOUT_OFF = Z_OFF + SSM_INNER
GA_OFF = GOUT_OFF + GDN_V
GB_OFF = GA_OFF + D_MODEL
BIG_N = GB_OFF + D_MODEL
SM_A = 0
SM_B = SM_A + 2 * GDN_HEADS
SM_DT = SM_B + 2 * GDN_HEADS
SM_DT_B = SM_DT + SSM_HEADS
SMALL_N = LANES

GDN_CHUNKS_PER_STEP = 4
GDN_PREP_ROWS = 512
SCAN_OUT_DTYPE = BF16
MERGE_ROWS = 512
MLP_ROWS = 1024
PROJ_ROWS = 1024
PROJ_TILE_N = 1024
MOD_TILE_N = 1536
SSD_GROUPS_PER_STEP = 2
SSD_COLUMNS_PER_STEP = 8


def _params(n_axes):
    return pltpu.CompilerParams(dimension_semantics=("arbitrary",) * n_axes, vmem_limit_bytes=VMEM_LIMIT_BYTES)


def _silu(t):
    return t * jax.nn.sigmoid(t)


def _softplus(t):
    return jnp.maximum(t, 0.0) + jnp.log(1.0 + jnp.exp(-jnp.abs(t)))


def _mm(a, b):
    return jnp.dot(a.astype(BF16), b.astype(BF16), preferred_element_type=F32)


def _mm_nt(a, b):
    return lax.dot_general(a.astype(BF16), b.astype(BF16), (((1,), (1,)), ((), ())), preferred_element_type=F32)


def _mm_tn(a, b):
    return lax.dot_general(a.astype(BF16), b.astype(BF16), (((0,), (0,)), ((), ())), preferred_element_type=F32)


def _tri_masks(lower):
    ii = lax.broadcasted_iota(jnp.int32, (CHUNK, CHUNK), 0)
    jj = lax.broadcasted_iota(jnp.int32, (CHUNK, CHUNK), 1)
    if lower:
        return ii >= jj, ii > jj
    return ii <= jj, ii < jj


def _conv_seg(x, w, seg):
    n = x.shape[0]
    assert seg & (seg - 1) == 0 and n % seg == 0
    pos = lax.broadcasted_iota(jnp.int32, (n, 1), 0) & (seg - 1)
    out = None
    for j in range(CONV_K):
        d = j - CONV_K // 2
        if d == 0:
            term = x * w[j:j + 1, :]
        else:
            shifted = pltpu.roll(x, shift=(-d) % n, axis=0)
            valid = (pos + d >= 0) & (pos + d < seg)
            term = jnp.where(valid, shifted, 0.0) * w[j:j + 1, :]
        out = term if out is None else out + term
    return out


def _mod_kernel(c_ref, w_ref, b_ref, o_ref):
    o_ref[...] = _mm(_silu(c_ref[...]), w_ref[...]) + b_ref[...]


def _modulation(cc, w, b):
    rows, d = cc.shape
    n = w.shape[1]
    tn = MOD_TILE_N
    return pl.pallas_call(
        _mod_kernel,
        out_shape=jax.ShapeDtypeStruct((rows, n), F32),
        grid=(n // tn,),
        in_specs=[pl.BlockSpec((rows, d), lambda j: (0, 0)),
                  pl.BlockSpec((d, tn), lambda j: (0, j)),
                  pl.BlockSpec((1, tn), lambda j: (0, j))],
        out_specs=pl.BlockSpec((rows, tn), lambda j: (0, j)),
        compiler_params=_params(1),
        name="modulation",
    )(cc, w, b.reshape(1, n))


def _proj_conv_kernel(*refs, seg, cols, has_bias, norm_tiles):
    x_ref, sc_ref, sh_ref, w_ref, cw_ref = refs[:5]
    k = 5
    cb_ref = refs[k] if has_bias else None
    k += has_bias
    perm_ref = refs[k] if cols else None
    k += cols
    wn_ref = refs[k] if norm_tiles else None
    k += bool(norm_tiles)
    o_ref = refs[k]
    kt_ref = refs[k + 1] if norm_tiles else None
    narrow_ref = refs[k + 2] if norm_tiles else None
    h_ref = refs[-1]
    j = pl.program_id(2)
    n_tok = h_ref.shape[0]

    @pl.when(j == 0)
    def _():
        x = x_ref[...]
        if cols:
            x = x.reshape(n_tok, x.shape[2])
        h = (x * (1.0 + sc_ref[...]) + sh_ref[...]).astype(BF16)
        if cols:
            h = jnp.dot(perm_ref[...], h, preferred_element_type=F32).astype(BF16)
        h_ref[...] = h
        if norm_tiles:
            narrow_ref[...] = jnp.dot(h, wn_ref[...], preferred_element_type=F32)

    y = jnp.dot(h_ref[...], w_ref[...], preferred_element_type=F32)
    y = _conv_seg(y, cw_ref[...], seg)
    if has_bias:
        y = y + cb_ref[...]
    y = _silu(y)

    def store(t):
        o_ref[...] = t.reshape(o_ref.shape)

    if norm_tiles:
        @pl.when(j < norm_tiles)
        def _():
            scale = jnp.where(j == 0, GDN_DK ** -0.5, 1.0)
            heads = []
            for hh in range(y.shape[1] // GDN_DK):
                yh = y[:, hh * GDN_DK:(hh + 1) * GDN_DK]
                heads.append(yh * (lax.rsqrt(jnp.sum(yh * yh, axis=1, keepdims=True) + NORM_EPS) * scale))
            store(jnp.concatenate(heads, axis=1))

            @pl.when(j == 1)
            def _():
                for hh, yh in enumerate(heads):
                    kt_ref[hh] = yh.T

        @pl.when(j >= norm_tiles)
        def _():
            store(y)
    else:
        store(y)


def _proj_conv(x, sc, sh, w, conv_w, conv_b, seg, cols, norm_tiles, tm, tn, name, w_narrow=None):
    bsz, length, d = x.shape
    n = w.shape[1]
    has_bias = conv_b is not None
    if cols:
        rows, cps = length // GRID_W, SUBLANES
        assert seg == rows and seg & (seg - 1) == 0
        n_tok = rows * cps
        x_in = x.reshape(bsz, rows, GRID_W, d)
        x_spec = pl.BlockSpec((None, rows, cps, d), lambda b, i, j: (b, 0, i, 0))
        out_shape = jax.ShapeDtypeStruct((bsz, GRID_W, rows, n), F32)
        out_spec = pl.BlockSpec((None, cps, rows, tn), lambda b, i, j: (b, i, 0, j))
        grid = (bsz, GRID_W // cps, n // tn)
    else:
        n_tok = tm
        x_in = x
        x_spec = pl.BlockSpec((None, tm, d), lambda b, i, j: (b, i, 0))
        out_shape = jax.ShapeDtypeStruct((bsz, length, n), F32)
        out_spec = pl.BlockSpec((None, tm, tn), lambda b, i, j: (b, i, j))
        grid = (bsz, length // tm, n // tn)
    mod = pl.BlockSpec((None, 1, d), lambda b, i, j: (b, 0, 0))
    in_specs = [x_spec, mod, mod, pl.BlockSpec((d, tn), lambda b, i, j: (0, j)),
                pl.BlockSpec((CONV_K, tn), lambda b, i, j: (0, j))]
    args = [x_in, sc, sh, w, conv_w]
    if has_bias:
        in_specs.append(pl.BlockSpec((1, tn), lambda b, i, j: (0, j)))
        args.append(conv_b.reshape(1, n))
    if cols:
        dst = jnp.arange(n_tok)
        src = (dst % rows) * cps + dst // rows
        in_specs.append(pl.BlockSpec((n_tok, n_tok), lambda b, i, j: (0, 0), pipeline_mode=pl.Buffered(1)))
        args.append((src[:, None] == jnp.arange(n_tok)[None, :]).astype(BF16))
    if norm_tiles:
        assert norm_tiles == 2 and tn == GDN_QK and not cols
        in_specs.append(pl.BlockSpec((d, SMALL_N), lambda b, i, j: (0, 0), pipeline_mode=pl.Buffered(1)))
        args.append(w_narrow)
        out_shape = (out_shape, jax.ShapeDtypeStruct((bsz, GDN_HEADS, GDN_DK, length), F32),
                     jax.ShapeDtypeStruct((bsz, length, SMALL_N), F32))
        out_spec = (out_spec, pl.BlockSpec((None, GDN_HEADS, GDN_DK, tm), lambda b, i, j: (b, 0, 0, i)),
                    pl.BlockSpec((None, tm, SMALL_N), lambda b, i, j: (b, i, 0)))
    return pl.pallas_call(
        functools.partial(_proj_conv_kernel, seg=seg, cols=cols, has_bias=has_bias, norm_tiles=norm_tiles),
        out_shape=out_shape,
        grid=grid,
        in_specs=in_specs,
        out_specs=out_spec,
        scratch_shapes=[pltpu.VMEM((n_tok, d), BF16)],
        compiler_params=_params(3),
        name=name,
    )(*args)


def _chunk_cumsums(t):
    ii = lax.broadcasted_iota(jnp.int32, (2 * CHUNK, CHUNK), 0)
    jj = lax.broadcasted_iota(jnp.int32, (2 * CHUNK, CHUNK), 1)
    ones = ((ii < CHUNK) & (ii >= jj)) | ((ii >= CHUNK) & (ii - CHUNK <= jj))
    tri = jnp.where(ones, 1.0, 0.0).astype(BF16)
    hi = t.astype(BF16)
    r1 = t - hi.astype(F32)
    mid = r1.astype(BF16)
    lo = (r1 - mid.astype(F32)).astype(BF16)
    w = t.shape[1]
    sums = jnp.dot(tri, jnp.concatenate([hi, mid, lo], axis=1), preferred_element_type=F32)
    sums = sums[:, :w] + sums[:, w:2 * w] + sums[:, 2 * w:]
    return sums[:CHUNK], sums[CHUNK:]


def _gdn_prep_kernel(s_ref, alog_ref, dtb_ref, o_ref, *, rows):
    s = s_ref[...]
    lane = lax.broadcasted_iota(jnp.int32, (1, LANES), 1)
    g = -jnp.exp(alog_ref[...]) * _softplus(s + dtb_ref[...])
    beta = jax.nn.sigmoid(s)
    for c in range(rows // CHUNK):
        sl = slice(c * CHUNK, (c + 1) * CHUNK)
        fwd, bwd = _chunk_cumsums(g[sl, :])
        gc = jnp.where(lane < SM_A + GDN_HEADS, fwd, bwd)
        o_ref[sl, :] = jnp.where(lane < SM_B, gc, beta[sl, :])


def _gdn_prep(small, a_log, dt_bias, tb):
    bsz, length, _ = small.shape
    pad = lambda t: jnp.pad(t.reshape(1, -1).astype(F32), ((0, 0), (SM_A, LANES - SM_A - 2 * GDN_HEADS)))
    return pl.pallas_call(
        functools.partial(_gdn_prep_kernel, rows=tb),
        out_shape=jax.ShapeDtypeStruct((bsz, length, LANES), F32),
        grid=(bsz, length // tb),
        in_specs=[pl.BlockSpec((None, tb, LANES), lambda b, i: (b, i, 0)),
                  pl.BlockSpec((1, LANES), lambda b, i: (0, 0)),
                  pl.BlockSpec((1, LANES), lambda b, i: (0, 0))],
        out_specs=pl.BlockSpec((None, tb, LANES), lambda b, i: (b, i, 0)),
        compiler_params=_params(2),
        name="gdn_prep",
    )(small, pad(a_log), pad(dt_bias))


def _tri_inverse(mats):
    ii = lax.broadcasted_iota(jnp.int32, (CHUNK, CHUNK), 0)
    jj = lax.broadcasted_iota(jnp.int32, (CHUNK, CHUNK), 1)
    eye = jnp.where(ii == jj, 1.0, 0.0)
    zero = jnp.zeros((CHUNK, CHUNK), F32)
    right = lax.broadcasted_iota(jnp.int32, (CHUNK, 2 * CHUNK), 1) >= CHUNK
    zs = [jnp.concatenate([zero, eye], axis=1) + _mm(a, jnp.concatenate([a, -eye], axis=1)) for a in mats]
    power = 2
    while power < CHUNK:
        ps = [_mm(z[:, :CHUNK], z) for z in zs]
        zs = [p + jnp.where(right, z, 0.0) for p, z in zip(ps, zs)]
        power *= 2
    return zs


def _gdn_local(items, masks):
    qs, ks, vs, kts, gcols, grows, bcols, lowers = zip(*items)
    n = len(items)
    incl = [masks[lo][0] for lo in lowers]
    strict = [masks[lo][1] for lo in lowers]
    glast = [grows[i][:, CHUNK - 1:CHUNK] if lowers[i] else grows[i][:, 0:1] for i in range(n)]
    kq = [_mm(jnp.concatenate([ks[i].astype(BF16), qs[i].astype(BF16)], axis=0), kts[i]) for i in range(n)]
    dec = [jnp.exp(jnp.where(incl[i], gcols[i] - grows[i], 0.0)) for i in range(n)]
    a_mats = [jnp.where(strict[i], bcols[i] * kq[i][:CHUNK] * dec[i], 0.0) for i in range(n)]
    t_inv = _tri_inverse(a_mats)
    eg = [jnp.exp(g) for g in gcols]
    rhs = [jnp.concatenate([(bcols[i] * eg[i]) * ks[i], bcols[i] * vs[i]], axis=1) for i in range(n)]
    pad = jnp.zeros((CHUNK, GDN_DK + GDN_DV), F32)
    wu = [_mm(t, jnp.concatenate([pad, r], axis=0)) for t, r in zip(t_inv, rhs)]
    lhs = [jnp.concatenate([wu[i][:, :GDN_DK].astype(BF16), (qs[i] * eg[i]).astype(BF16)], axis=0) for i in range(n)]
    qk_masked = [jnp.where(incl[i], kq[i][CHUNK:] * dec[i], 0.0).astype(BF16) for i in range(n)]
    kt_dec = [(kts[i] * jnp.exp(glast[i] - grows[i])).astype(BF16) for i in range(n)]
    return [(lhs[i], wu[i][:, GDN_DK:], qk_masked[i], kt_dec[i], jnp.exp(glast[i])) for i in range(n)]


def _gdn_on_state(local, states):
    n = len(local)
    on_state = [_mm(local[i][0], states[i]) for i in range(n)]
    v_new = [local[i][1] - on_state[i][:CHUNK] for i in range(n)]
    o_intra = [_mm(local[i][2], v_new[i]) for i in range(n)]
    kv = [_mm(local[i][3], v_new[i]) for i in range(n)]
    return [(on_state[i][CHUNK:] + o_intra[i], local[i][4] * states[i] + kv[i]) for i in range(n)]


def _gdn_kernel(qf_ref, kf_ref, vf_ref, ktf_ref, qb_ref, kb_ref, vb_ref, ktb_ref, pf_ref, pb_ref, ptf_ref, ptb_ref,
                s0_ref, of_ref, ob_ref, sout_ref, state_ref, *, n_steps, nc):
    step = pl.program_id(1)
    hs = GDN_HEADS

    @pl.when(step == 0)
    def _():
        state_ref[...] = s0_ref[...]

    def heads_of(q_ref, k_ref, v_ref):
        q, k, v = q_ref[...], k_ref[...], v_ref[...]
        return [(q[:, hh * GDN_DK:(hh + 1) * GDN_DK], k[:, hh * GDN_DK:(hh + 1) * GDN_DK],
                 v[:, hh * GDN_DV:(hh + 1) * GDN_DV]) for hh in range(hs)]

    heads_f = heads_of(qf_ref, kf_ref, vf_ref)
    heads_b = heads_of(qb_ref, kb_ref, vb_ref)
    pf = pf_ref[...]
    pb = pb_ref[...]
    masks = {True: _tri_masks(True), False: _tri_masks(False)}
    col = lambda t, idx: t[:, idx:idx + 1]
    cols = [(col(pf, SM_A + hh), col(pf, SM_B + hh), col(pb, SM_A + hs + hh), col(pb, SM_B + hs + hh))
            for hh in range(hs)]
    items = []
    for c in range(nc):
        cb = nc - 1 - c
        sl_f = slice(c * CHUNK, (c + 1) * CHUNK)
        sl_b = slice(cb * CHUNK, (cb + 1) * CHUNK)
        for hh in range(hs):
            gcol_f, bcol_f, gcol_b, bcol_b = cols[hh]
            qf, kf, vf = heads_f[hh]
            qb, kb, vb = heads_b[hh]
            grow_f = ptf_ref[c, SM_A + hh:SM_A + hh + 1, :]
            grow_b = ptb_ref[cb, SM_A + hs + hh:SM_A + hs + hh + 1, :]
            items.append((qf[sl_f], kf[sl_f], vf[sl_f], ktf_ref[hh, :, sl_f], gcol_f[sl_f], grow_f, bcol_f[sl_f], True))
            items.append((qb[sl_b], kb[sl_b], vb[sl_b], ktb_ref[hh, :, sl_b], gcol_b[sl_b], grow_b, bcol_b[sl_b], False))
    local = _gdn_local(items, masks)
    states = [state_ref[hh, d] for hh in range(hs) for d in range(2)]
    for c in range(nc):
        cb = nc - 1 - c
        results = _gdn_on_state(local[c * 2 * hs:(c + 1) * 2 * hs], states)
        states = [r[1] for r in results]
        for hh in range(hs):
            hl = slice(hh * GDN_DV, (hh + 1) * GDN_DV)
            of_ref[c * CHUNK:(c + 1) * CHUNK, hl] = results[2 * hh][0].astype(of_ref.dtype)
            ob_ref[cb * CHUNK:(cb + 1) * CHUNK, hl] = results[2 * hh + 1][0].astype(ob_ref.dtype)
    for hh in range(hs):
        state_ref[hh, 0] = states[2 * hh]
        state_ref[hh, 1] = states[2 * hh + 1]

    @pl.when(step == n_steps - 1)
    def _():
        sout_ref[...] = state_ref[...]


def _gdn_scan(qkv, kt, p, pt, s0, nc):
    bsz, length, _ = qkv.shape
    tb = nc * CHUNK
    n_steps = length // tb
    qoff, koff, voff = 0, 1, 2
    fwd = lambda off: pl.BlockSpec((None, tb, GDN_QK), lambda b, s: (b, s, off))
    bwd = lambda off: pl.BlockSpec((None, tb, GDN_QK), lambda b, s: (b, n_steps - 1 - s, off))
    state_spec = pl.BlockSpec((None, GDN_HEADS, 2, GDN_DK, GDN_DV), lambda b, s: (b, 0, 0, 0, 0))
    return pl.pallas_call(
        functools.partial(_gdn_kernel, n_steps=n_steps, nc=nc),
        out_shape=(jax.ShapeDtypeStruct((bsz, length, GDN_V), SCAN_OUT_DTYPE),
                   jax.ShapeDtypeStruct((bsz, length, GDN_V), SCAN_OUT_DTYPE),
                   jax.ShapeDtypeStruct((bsz, GDN_HEADS, 2, GDN_DK, GDN_DV), F32)),
        grid=(bsz, n_steps),
        in_specs=[fwd(qoff), fwd(koff), fwd(voff), pl.BlockSpec((None, GDN_HEADS, GDN_DK, tb), lambda b, s: (b, 0, 0, s)),
                  bwd(qoff), bwd(koff), bwd(voff),
                  pl.BlockSpec((None, GDN_HEADS, GDN_DK, tb), lambda b, s: (b, 0, 0, n_steps - 1 - s)),
                  pl.BlockSpec((None, tb, LANES), lambda b, s: (b, s, 0)),
                  pl.BlockSpec((None, tb, LANES), lambda b, s: (b, n_steps - 1 - s, 0)),
                  pl.BlockSpec((None, nc, 4 * SUBLANES, CHUNK), lambda b, s: (b, s, 0, 0)),
                  pl.BlockSpec((None, nc, 4 * SUBLANES, CHUNK), lambda b, s: (b, n_steps - 1 - s, 0, 0)),
                  state_spec],
        out_specs=(pl.BlockSpec((None, tb, GDN_V), lambda b, s: (b, s, 0)),
                   pl.BlockSpec((None, tb, GDN_V), lambda b, s: (b, n_steps - 1 - s, 0)),
                   state_spec),
        scratch_shapes=[pltpu.VMEM((GDN_HEADS, 2, GDN_DK, GDN_DV), F32)],
        compiler_params=_params(2),
        name="gdn_scan",
    )(qkv, qkv, qkv, kt, qkv, qkv, qkv, kt, p, p, pt, pt, s0)


def _gdn_branch(qkv, kt, small, qkv_c, kt_c, small_c, a_log, dt_bias):
    bsz = qkv.shape[0]
    s0 = jnp.zeros((bsz, GDN_HEADS, 2, GDN_DK, GDN_DV), F32)

    def rows_t(p):
        chunks = p[:, :, :4 * SUBLANES].reshape(bsz, p.shape[1] // CHUNK, CHUNK, 4 * SUBLANES)
        return jnp.swapaxes(chunks, 2, 3)

    ctx_len = qkv_c.shape[1]
    p_c = _gdn_prep(small_c, a_log, dt_bias, ctx_len)
    oc_f, oc_b, s_ctx = _gdn_scan(qkv_c, kt_c, p_c, rows_t(p_c), s0, min(GDN_CHUNKS_PER_STEP, ctx_len // CHUNK))
    p = _gdn_prep(small, a_log, dt_bias, GDN_PREP_ROWS)
    o_f, o_b, _ = _gdn_scan(qkv, kt, p, rows_t(p), s_ctx, GDN_CHUNKS_PER_STEP)
    return (o_f, o_b), (oc_f, oc_b)


SSD_QUANTITIES = 4
SSD_HEADS_PER_STEP = SSD_GROUPS_PER_STEP * SSM_HPG
SSD_STEP_WIDTH = SSD_HEADS_PER_STEP * SSM_HEAD_DIM
BF16_PIECES = 3


def _ssd_prep_kernel(s_ref, alog_ref, dtb_ref, o_ref, *, rows):
    lane = lax.broadcasted_iota(jnp.int32, (1, LANES), 1)
    fwd_lane = lane < SM_DT_B
    neg_a = -jnp.exp(alog_ref[...])
    for seg in range(o_ref.shape[0]):
        dt = _softplus(s_ref[:, seg * LANES:(seg + 1) * LANES] + dtb_ref[...])
        la = dt * neg_a
        o_ref[seg, :, 0:LANES] = dt
        for c in range(rows // CHUNK):
            sl = slice(c * CHUNK, (c + 1) * CHUNK)
            fwd, bwd = _chunk_cumsums(la[sl, :])
            gc = jnp.where(fwd_lane, fwd, bwd)
            g_last = jnp.where(fwd_lane, gc[CHUNK - 1:CHUNK, :], gc[0:1, :])
            o_ref[seg, sl, LANES:2 * LANES] = gc
            o_ref[seg, sl, 2 * LANES:3 * LANES] = jnp.exp(gc)
            o_ref[seg, sl, 3 * LANES:4 * LANES] = dt[sl, :] * jnp.exp(g_last - gc)


def _ssd_prep(small_cols, a_log, dt_bias, n_seg):
    bsz, rows, _ = small_cols.shape
    sps = min(n_seg, SUBLANES)
    pad = lambda t: jnp.pad(t.reshape(1, -1).astype(F32), ((0, 0), (SM_DT, LANES - SM_DT - 2 * SSM_HEADS)))
    return pl.pallas_call(
        functools.partial(_ssd_prep_kernel, rows=rows),
        out_shape=jax.ShapeDtypeStruct((bsz, n_seg, rows, SSD_QUANTITIES * LANES), F32),
        grid=(bsz, n_seg // sps),
        in_specs=[pl.BlockSpec((None, rows, sps * LANES), lambda b, c: (b, 0, c)),
                  pl.BlockSpec((1, LANES), lambda b, c: (0, 0)),
                  pl.BlockSpec((1, LANES), lambda b, c: (0, 0))],
        out_specs=pl.BlockSpec((None, sps, rows, SSD_QUANTITIES * LANES), lambda b, c: (b, c, 0, 0)),
        compiler_params=_params(2),
        name="ssd_prep",
    )(small_cols, pad(a_log), pad(dt_bias))


def _ssd_layouts(q):
    bsz, n_seg, rows, _ = q.shape
    steps = SSM_GROUPS // SSD_GROUPS_PER_STEP
    q = q.reshape(bsz, n_seg, rows, SSD_QUANTITIES, LANES)[..., SM_DT:SM_DT + 2 * SSM_HEADS]
    q = q.reshape(bsz, n_seg, rows, SSD_QUANTITIES, 2, steps, SSD_HEADS_PER_STEP)
    out = []
    for d in range(2):
        qd = q[:, :, :, :, d]
        cols = jnp.transpose(qd, (0, 1, 4, 2, 3, 5)).reshape(bsz, n_seg, steps, rows, SSD_QUANTITIES * SSD_HEADS_PER_STEP)
        as_rows = lambda t: jnp.transpose(
            t.reshape(bsz, n_seg, rows // CHUNK, CHUNK, steps, SSD_HEADS_PER_STEP),
            (0, 1, 4, 2, 5, 3)).reshape(bsz, n_seg, steps, rows // CHUNK, SSD_STEP_WIDTH)
        out.append((cols, jnp.concatenate([as_rows(qd[:, :, :, 1]), as_rows(qd[:, :, :, 0])], axis=-1)))
    return out


def _ssd_expand_matrix():
    k = SSD_QUANTITIES * SSD_HEADS_PER_STEP
    src = jnp.arange(k)
    dst = jnp.arange(SSD_STEP_WIDTH, k * SSM_HEAD_DIM) // SSM_HEAD_DIM
    one = (src[:, None] == dst[None, :]).astype(BF16)
    return jnp.concatenate([one] * BF16_PIECES, axis=0)


def _ssd_chunks(items, incl, block_diag):
    n = len(items)
    cb = [_mm_nt(it[4], jnp.concatenate([it[3].astype(BF16)] * SSM_HPG, axis=0)) for it in items]
    inter = [_mm(it[4], it[9]) for it in items]
    lhs = []
    for i, it in enumerate(items):
        m = incl[it[10]]
        lhs.append(jnp.where(m, cb[i] * jnp.exp(jnp.where(m, it[5] - it[6], 0.0)) * it[1], 0.0))
    rhs = [jnp.where(block_diag, jnp.concatenate([it[0].astype(BF16)] * SSM_HPG, axis=0), 0.0) for it in items]
    intra = [_mm(lhs[i], rhs[i]) for i in range(n)]
    upd = [_mm_tn(it[3], it[2]) for it in items]
    out = []
    for i, it in enumerate(items):
        y = intra[i] + inter[i] * it[7]
        if it[11] is not None:
            y = y + it[11] * it[0]
        out.append((y, it[8] * it[9] + upd[i]))
    return out


def _ssd_kernel(xf_ref, bf_ref, cf_ref, xb_ref, bb_ref, cb_ref, colf_ref, colb_ref, growf_ref, growb_ref,
                e_ref, d_ref, s0_ref, yf_ref, yb_ref, sout_ref, state_ref, *, rows, n_seg, gs):
    step = pl.program_id(2)

    @pl.when(step == 0)
    def _():
        state_ref[...] = s0_ref[...]

    gw = SSM_HPG * SSM_HEAD_DIM
    width = gs * gw

    def expand(col_ref):
        c = col_ref[...]
        hi = c.astype(BF16)
        r1 = c - hi.astype(F32)
        mid = r1.astype(BF16)
        lo = (r1 - mid.astype(F32)).astype(BF16)
        ex = jnp.dot(jnp.concatenate([hi, mid, lo], axis=1), e_ref[...], preferred_element_type=F32)
        return [ex[:, q * width:(q + 1) * width] for q in range(SSD_QUANTITIES - 1)]

    xf, xb = xf_ref[...], xb_ref[...]
    gcx_f, egx_f, q4x_f = expand(colf_ref)
    gcx_b, egx_b, q4x_b = expand(colb_ref)
    xdec_f, xdec_b = xf * q4x_f, xb * q4x_b
    bmf, cmf, bmb, cmb = bf_ref[...], cf_ref[...], bb_ref[...], cb_ref[...]
    d_all = d_ref[...]

    row = lax.broadcasted_iota(jnp.int32, (CHUNK, gw), 0)
    tok = lax.broadcasted_iota(jnp.int32, (CHUNK, gw), 1) & (CHUNK - 1)
    incl = {True: row >= tok, False: row <= tok}
    block_diag = (lax.shift_right_logical(lax.broadcasted_iota(jnp.int32, (gw, gw), 0), CHUNK.bit_length() - 1)
                  == lax.shift_right_logical(lax.broadcasted_iota(jnp.int32, (gw, gw), 1), SSM_HEAD_DIM.bit_length() - 1))

    n_chunks = rows // CHUNK
    states = [[state_ref[gg, 0], state_ref[gg, 1]] for gg in range(gs)]
    for c in range(n_chunks):
        cb = n_chunks - 1 - c
        sl_f = slice(c * CHUNK, (c + 1) * CHUNK)
        sl_b = slice(cb * CHUNK, (cb + 1) * CHUNK)
        last_f = slice(c * CHUNK + CHUNK - 1, (c + 1) * CHUNK)
        last_b = slice(cb * CHUNK, cb * CHUNK + 1)
        items = []
        for gg in range(gs):
            xl = slice(gg * gw, (gg + 1) * gw)
            nl = slice(gg * SSM_STATE, (gg + 1) * SSM_STATE)
            dl = slice(width + gg * gw, width + (gg + 1) * gw)
            items.append((xf[sl_f, xl], growf_ref[c:c + 1, dl], xdec_f[sl_f, xl], bmf[sl_f, nl], cmf[sl_f, nl],
                          gcx_f[sl_f, xl], growf_ref[c:c + 1, xl], egx_f[sl_f, xl], egx_f[last_f, xl], states[gg][0], True,
                          d_all[:, xl]))
            items.append((xb[sl_b, xl], growb_ref[cb:cb + 1, dl], xdec_b[sl_b, xl], bmb[sl_b, nl], cmb[sl_b, nl],
                          gcx_b[sl_b, xl], growb_ref[cb:cb + 1, xl], egx_b[sl_b, xl], egx_b[last_b, xl], states[gg][1], False,
                          None))
        results = _ssd_chunks(items, incl, block_diag)
        for gg in range(gs):
            xl = slice(gg * gw, (gg + 1) * gw)
            yf_ref[sl_f, xl] = results[2 * gg][0].astype(yf_ref.dtype)
            yb_ref[sl_b, xl] = results[2 * gg + 1][0].astype(yb_ref.dtype)
            states[gg] = [results[2 * gg][1], results[2 * gg + 1][1]]
    for gg in range(gs):
        state_ref[gg, 0] = states[gg][0]
        state_ref[gg, 1] = states[gg][1]

    @pl.when(step == n_seg - 1)
    def _():
        sout_ref[...] = state_ref[...]


def _ssd_scan(xbc, layouts, expand, d_exp, s0):
    bsz, n_seg, rows, _ = xbc.shape
    gs = SSD_GROUPS_PER_STEP
    xw, nw = SSD_STEP_WIDTH, gs * SSM_STATE
    boff, coff = SSM_INNER // nw, (SSM_INNER + SSM_GN) // nw
    (cols_f, grow_f), (cols_b, grow_b) = layouts
    n_cols = cols_f.shape[-1]
    seg_f = lambda s: s
    seg_b = lambda s: n_seg - 1 - s
    xspec = lambda seg: pl.BlockSpec((None, None, rows, xw), lambda b, g, s: (b, seg(s), 0, g))
    nspec = lambda seg, off: pl.BlockSpec((None, None, rows, nw), lambda b, g, s: (b, seg(s), 0, off + g))
    cspec = lambda seg: pl.BlockSpec((None, None, None, rows, n_cols), lambda b, g, s: (b, seg(s), g, 0, 0))
    rspec = lambda seg: pl.BlockSpec((None, None, None, rows // CHUNK, 2 * xw), lambda b, g, s: (b, seg(s), g, 0, 0))
    state_spec = pl.BlockSpec((None, gs, 2, SSM_STATE, SSM_HPG * SSM_HEAD_DIM), lambda b, g, s: (b, g, 0, 0, 0))
    y_shape = jax.ShapeDtypeStruct((bsz, n_seg, rows, SSM_INNER), SCAN_OUT_DTYPE)
    return pl.pallas_call(
        functools.partial(_ssd_kernel, rows=rows, n_seg=n_seg, gs=gs),
        out_shape=(y_shape, y_shape,
                   jax.ShapeDtypeStruct((bsz, SSM_GROUPS, 2, SSM_STATE, SSM_HPG * SSM_HEAD_DIM), F32)),
        grid=(bsz, SSM_GROUPS // gs, n_seg),
        in_specs=[xspec(seg_f), nspec(seg_f, boff), nspec(seg_f, coff),
                  xspec(seg_b), nspec(seg_b, boff), nspec(seg_b, coff),
                  cspec(seg_f), cspec(seg_b), rspec(seg_f), rspec(seg_b),
                  pl.BlockSpec(expand.shape, lambda b, g, s: (0, 0)),
                  pl.BlockSpec((1, xw), lambda b, g, s: (0, g)), state_spec],
        out_specs=(xspec(seg_f), xspec(seg_b), state_spec),
        scratch_shapes=[pltpu.VMEM((gs, 2, SSM_STATE, SSM_HPG * SSM_HEAD_DIM), F32)],
        compiler_params=_params(3),
        name="ssd_scan",
    )(xbc, xbc, xbc, xbc, xbc, xbc, cols_f, cols_b, grow_f, grow_b, expand, d_exp, s0)


def _ssd_branch(xbc, small, xbc_c, small_c, a_log, dt_bias, d_skip):
    bsz, n_seg, rows, _ = xbc.shape
    assert rows % CHUNK == 0 and xbc_c.shape[2] % CHUNK == 0
    d_exp = jnp.repeat(d_skip.astype(F32), SSM_HEAD_DIM).reshape(1, SSM_INNER)
    s0 = jnp.zeros((bsz, SSM_GROUPS, 2, SSM_STATE, SSM_HPG * SSM_HEAD_DIM), F32)
    expand = _ssd_expand_matrix()
    yc_f, yc_b, s_ctx = _ssd_scan(xbc_c, _ssd_layouts(_ssd_prep(small_c, a_log, dt_bias, 1)), expand, d_exp, s0)
    q = _ssd_prep(small.reshape(bsz, rows, n_seg * SMALL_N), a_log, dt_bias, n_seg)
    steps, step_rows = n_seg // SSD_COLUMNS_PER_STEP, SSD_COLUMNS_PER_STEP * rows
    merged = lambda t: t.reshape(bsz, steps, step_rows, t.shape[-1])
    y_f, y_b, _ = _ssd_scan(merged(xbc), _ssd_layouts(merged(q)), expand, d_exp, s_ctx)
    to_raster = lambda y: jnp.swapaxes(y.reshape(bsz, n_seg, rows, SSM_INNER), 1, 2).reshape(bsz, rows * n_seg, SSM_INNER)
    return (to_raster(y_f), to_raster(y_b)), (yc_f[:, 0], yc_b[:, 0])


def _layer_norm(r, g, b):
    mu = jnp.mean(r, axis=1, keepdims=True)
    var = jnp.mean(jnp.square(r - mu), axis=1, keepdims=True)
    return (r - mu) * lax.rsqrt(var + LN_EPS) * g + b


def _merge_kernel(x_ref, sc_ref, sh_ref, wg_ref, of_ref, ob_ref, yf_ref, yb_ref,
                  nwa_ref, nwb_ref, wpg_ref, wps_ref, wout_ref, g1_ref, lng_ref, lnb_ref, o_ref, *, alpha):
    x = x_ref[...]
    h = (x * (1.0 + sc_ref[...]) + sh_ref[...]).astype(BF16)
    gate = lambda off, width: jnp.dot(h, wg_ref[:, off:off + width], preferred_element_type=F32)
    o = of_ref[...].astype(F32) + ob_ref[...].astype(F32)
    normed = []
    for hh in range(GDN_HEADS):
        oh = o[:, hh * GDN_DV:(hh + 1) * GDN_DV]
        normed.append(oh * lax.rsqrt(jnp.mean(oh * oh, axis=1, keepdims=True) + NORM_EPS))
    y_a = jnp.concatenate(normed, axis=1) * nwa_ref[...] * _silu(gate(GOUT_OFF, GDN_V))
    t = (yf_ref[...].astype(F32) + yb_ref[...].astype(F32)) * _silu(gate(Z_OFF, SSM_INNER))
    gw = SSM_INNER // SSM_GROUPS
    normed = []
    for g in range(SSM_GROUPS):
        tg = t[:, g * gw:(g + 1) * gw]
        normed.append(tg * lax.rsqrt(jnp.mean(tg * tg, axis=1, keepdims=True) + NORM_EPS))
    y_b = jnp.concatenate(normed, axis=1) * nwb_ref[...]
    d = x.shape[1]
    mix = (jax.nn.sigmoid(gate(GA_OFF, d)) * jnp.dot(y_a.astype(BF16), wpg_ref[...], preferred_element_type=F32)
           + jax.nn.sigmoid(gate(GB_OFF, d)) * jnp.dot(y_b.astype(BF16), wps_ref[...], preferred_element_type=F32))
    out = jnp.dot(mix.astype(BF16), wout_ref[...], preferred_element_type=F32)
    o_ref[...] = _layer_norm(alpha * x + g1_ref[...] * out, lng_ref[...], lnb_ref[...])


def _merge(x, sc, sh, w_gates, o_pair, y_pair, nwa, nwb, wpg, wps, wout, g1, ln_g, ln_b, alpha, tm):
    bsz, length, d = x.shape
    row = lambda width: pl.BlockSpec((None, tm, width), lambda b, i: (b, i, 0))
    const = lambda shape: pl.BlockSpec(shape, lambda b, i: (0,) * len(shape), pipeline_mode=pl.Buffered(1))
    mod = pl.BlockSpec((None, 1, d), lambda b, i: (b, 0, 0))
    return pl.pallas_call(
        functools.partial(_merge_kernel, alpha=alpha),
        out_shape=jax.ShapeDtypeStruct((bsz, length, d), F32),
        grid=(bsz, length // tm),
        in_specs=[row(d), mod, mod, const((d, BIG_N)),
                  row(GDN_V), row(GDN_V), row(SSM_INNER), row(SSM_INNER),
                  const((1, GDN_V)), const((1, SSM_INNER)), const((GDN_V, d)), const((SSM_INNER, d)), const((d, d)),
                  mod, const((1, d)), const((1, d))],
        out_specs=row(d),
        compiler_params=_params(2),
        name="merge",
    )(x, sc, sh, w_gates, o_pair[0], o_pair[1], y_pair[0], y_pair[1], nwa, nwb, wpg, wps, wout, g1, ln_g, ln_b)


def _mlp_kernel(x_ref, sc_ref, sh_ref, g2_ref, w1_ref, b1_ref, w2_ref, b2_ref, lng_ref, lnb_ref, o_ref, *, alpha):
    x = x_ref[...]
    h = (x * (1.0 + sc_ref[...]) + sh_ref[...]).astype(BF16)
    acc = None
    tf = D_MODEL
    for c in range(D_FF // tf):
        u = jnp.dot(h, w1_ref[:, c * tf:(c + 1) * tf], preferred_element_type=F32) + b1_ref[:, c * tf:(c + 1) * tf]
        u = jnp.square(jnp.maximum(u, 0.0))
        part = jnp.dot(u.astype(BF16), w2_ref[c * tf:(c + 1) * tf, :], preferred_element_type=F32)
        acc = part if acc is None else acc + part
    f = acc + b2_ref[...]
    o_ref[...] = _layer_norm(alpha * x + g2_ref[...] * f, lng_ref[...], lnb_ref[...])


def _mlp(x, sc, sh, g2, w1, b1, w2, b2, ln_g, ln_b, alpha, tm):
    bsz, length, d = x.shape
    const = lambda shape: pl.BlockSpec(shape, lambda b, i: (0,) * len(shape), pipeline_mode=pl.Buffered(1))
    mod = pl.BlockSpec((None, 1, d), lambda b, i: (b, 0, 0))
    return pl.pallas_call(
        functools.partial(_mlp_kernel, alpha=alpha),
        out_shape=jax.ShapeDtypeStruct((bsz, length, d), F32),
        grid=(bsz, length // tm),
        in_specs=[pl.BlockSpec((None, tm, d), lambda b, i: (b, i, 0)), mod, mod, mod,
                  const((d, D_FF)), const((1, D_FF)), const((D_FF, d)), const((1, d)), const((1, d)), const((1, d))],
        out_specs=pl.BlockSpec((None, tm, d), lambda b, i: (b, i, 0)),
        compiler_params=_params(2),
        name="mlp",
    )(x, sc, sh, g2, w1, b1, w2, b2, ln_g, ln_b)


def _split_w_in(w_in):
    pts, acc = [], 0
    for s in IN_SPLITS[:-1]:
        acc += s
        pts.append(acc)
    qkv, gout, a_raw, b_raw, z, xbc, dt_raw, gate_a, gate_b = jnp.split(w_in, pts, axis=1)
    big = jnp.concatenate([z, gout, gate_a, gate_b], axis=1).astype(BF16)
    pad = jnp.zeros((w_in.shape[0], SMALL_N - SM_DT - 2 * SSM_HEADS), w_in.dtype)
    small = jnp.concatenate([a_raw, b_raw, dt_raw, pad], axis=1).astype(BF16)
    return big, qkv.astype(BF16), xbc.astype(BF16), small


def kernel(x, c, ctx, c_ctx, w_mod, b_mod, w_in, gdn_conv_w, gdn_A_log, gdn_dt_bias, gdn_norm_w,
           ssm_conv_w, ssm_conv_b, ssm_A_log, ssm_dt_bias, ssm_D, ssm_norm_w,
           w_proj_gdn, w_proj_ssm, w_out, ln1_g, ln1_b, w_ff1, b_ff1, w_ff2, b_ff2, ln2_g, ln2_b):
    bsz, length, d = x.shape
    ctx_len = ctx.shape[1]
    depth = w_mod.shape[0]
    alpha = float((2 * depth) ** 0.25)
    mod_rows = -(-(bsz + 1) // (2 * SUBLANES)) * (2 * SUBLANES)
    cc = jnp.concatenate([c, c_ctx[None, :], jnp.zeros((mod_rows - bsz - 1, d), c.dtype)], axis=0)
    row2 = lambda t: t.reshape(1, -1)
    tm_lat = PROJ_ROWS
    tm_ctx = ctx_len

    for l in range(depth):
        last = l == depth - 1
        mod = _modulation(cc, w_mod[l], b_mod[l])
        lat = [mod[:bsz, i * d:(i + 1) * d].reshape(bsz, 1, d) for i in range(6)]
        cxm = [jnp.broadcast_to(mod[bsz, i * d:(i + 1) * d].reshape(1, 1, d), (bsz, 1, d)) for i in range(6)]
        w_big, w_qkv, w_xbc, w_small = _split_w_in(w_in[l])
        rows = length // GRID_W

        qkv, kt, small = _proj_conv(x, lat[1], lat[0], w_qkv, gdn_conv_w[l], None, GRID_W, False, 2, tm_lat, GDN_QK,
                                    "inproj_qkv", w_small)
        xbc = _proj_conv(x, lat[1], lat[0], w_xbc, ssm_conv_w[l], ssm_conv_b[l], rows, True, 0, None, PROJ_TILE_N,
                         "inproj_xbc")
        qkv_c, kt_c, small_c = _proj_conv(ctx, cxm[1], cxm[0], w_qkv, gdn_conv_w[l], None, ctx_len, False, 2, tm_ctx,
                                          GDN_QK, "inproj_qkv_ctx", w_small)
        xbc_c = _proj_conv(ctx, cxm[1], cxm[0], w_xbc, ssm_conv_w[l], ssm_conv_b[l], ctx_len, False, 0, tm_ctx, PROJ_TILE_N,
                           "inproj_xbc_ctx")[:, None]

        o_pair, oc_pair = _gdn_branch(qkv, kt, small, qkv_c, kt_c, small_c, gdn_A_log[l], gdn_dt_bias[l])
        y_pair, yc_pair = _ssd_branch(xbc, small, xbc_c, small_c, ssm_A_log[l], ssm_dt_bias[l], ssm_D[l])

        nwa = jnp.tile(gdn_norm_w[l], GDN_HEADS).reshape(1, GDN_V)
        nwb = row2(ssm_norm_w[l])
        wpg, wps, wo = w_proj_gdn[l].astype(BF16), w_proj_ssm[l].astype(BF16), w_out[l].astype(BF16)
        w1, w2 = w_ff1[l].astype(BF16), w_ff2[l].astype(BF16)
        merge_args = (nwa, nwb, wpg, wps, wo)
        ln1 = (row2(ln1_g[l]), row2(ln1_b[l]))
        mlp_w = (w1, row2(b_ff1[l]), w2, row2(b_ff2[l]), row2(ln2_g[l]), row2(ln2_b[l]))

        x1 = _merge(x, lat[1], lat[0], w_big, o_pair, y_pair, *merge_args, lat[2], *ln1, alpha, MERGE_ROWS)
        x = _mlp(x1, lat[4], lat[3], lat[5], *mlp_w, alpha, MLP_ROWS)
        if not last:
            c1 = _merge(ctx, cxm[1], cxm[0], w_big, oc_pair, yc_pair, *merge_args, cxm[2], *ln1, alpha, ctx_len)
            ctx = _mlp(c1, cxm[4], cxm[3], cxm[5], *mlp_w, alpha, ctx_len)
    return x
```

```python
import functools

import jax
import jax.numpy as jnp
from jax import lax
from jax.experimental import pallas as pl
from jax.experimental.pallas import tpu as pltpu

F32 = jnp.float32
BF16 = jnp.bfloat16

D_MODEL = 1024
GRID_W = 64
GDN_HEADS = 8
GDN_DK = 128
GDN_DV = 128
GDN_QK = GDN_HEADS * GDN_DK
GDN_V = GDN_HEADS * GDN_DV
GDN_QKV = 2 * GDN_QK + GDN_V
SSM_INNER = 2 * D_MODEL
SSM_HEAD_DIM = 64
SSM_HEADS = SSM_INNER // SSM_HEAD_DIM
SSM_GROUPS = 8
SSM_HPG = SSM_HEADS // SSM_GROUPS
SSM_STATE = 128
SSM_GN = SSM_GROUPS * SSM_STATE
SSM_XBC = SSM_INNER + 2 * SSM_GN
CONV_K = 5
CHUNK = 64
D_FF = 4 * D_MODEL
LN_EPS = 1e-5
NORM_EPS = 1e-6
IN_SPLITS = (GDN_QKV, GDN_V, 2 * GDN_HEADS, 2 * GDN_HEADS, SSM_INNER, SSM_XBC, 2 * SSM_HEADS, D_MODEL, D_MODEL)

LANES = 128
SUBLANES = 8
VMEM_LIMIT_BYTES = 56 * 1024 * 1024

Z_OFF = 0
GOUT_OFF = Z_OFF + SSM_INNER
GA_OFF = GOUT_OFF + GDN_V
GB_OFF = GA_OFF + D_MODEL
BIG_N = GB_OFF + D_MODEL
SM_A = 0
SM_B = SM_A + 2 * GDN_HEADS
SM_DT = SM_B + 2 * GDN_HEADS
SM_DT_B = SM_DT + SSM_HEADS
SMALL_N = LANES

GDN_CHUNKS_PER_STEP = 4
GDN_PREP_ROWS = 512
SCAN_OUT_DTYPE = BF16
MERGE_ROWS = 512
MLP_ROWS = 1024
PROJ_ROWS = 1024
PROJ_TILE_N = 1024
MOD_TILE_N = 1536
SSD_GROUPS_PER_STEP = 2
SSD_COLUMNS_PER_STEP = 8


def _params(n_axes):
    return pltpu.CompilerParams(dimension_semantics=("arbitrary",) * n_axes, vmem_limit_bytes=VMEM_LIMIT_BYTES)


def _silu(t):
    return t * jax.nn.sigmoid(t)


def _softplus(t):
    return jnp.maximum(t, 0.0) + jnp.log(1.0 + jnp.exp(-jnp.abs(t)))


def _mm(a, b):
    return jnp.dot(a.astype(BF16), b.astype(BF16), preferred_element_type=F32)


def _mm_nt(a, b):
    return lax.dot_general(a.astype(BF16), b.astype(BF16), (((1,), (1,)), ((), ())), preferred_element_type=F32)


def _mm_tn(a, b):
    return lax.dot_general(a.astype(BF16), b.astype(BF16), (((0,), (0,)), ((), ())), preferred_element_type=F32)


def _tri_masks(lower):
    ii = lax.broadcasted_iota(jnp.int32, (CHUNK, CHUNK), 0)
    jj = lax.broadcasted_iota(jnp.int32, (CHUNK, CHUNK), 1)
    if lower:
        return ii >= jj, ii > jj
    return ii <= jj, ii < jj


def _conv_seg(x, w, seg):
    n = x.shape[0]
    assert seg & (seg - 1) == 0 and n % seg == 0
    pos = lax.broadcasted_iota(jnp.int32, (n, 1), 0) & (seg - 1)
    out = None
    for j in range(CONV_K):
        d = j - CONV_K // 2
        if d == 0:
            term = x * w[j:j + 1, :]
        else:
            shifted = pltpu.roll(x, shift=(-d) % n, axis=0)
            valid = (pos + d >= 0) & (pos + d < seg)
            term = jnp.where(valid, shifted, 0.0) * w[j:j + 1, :]
        out = term if out is None else out + term
    return out


def _mod_kernel(c_ref, w_ref, b_ref, o_ref):
    o_ref[...] = _mm(_silu(c_ref[...]), w_ref[...]) + b_ref[...]


def _modulation(cc, w, b):
    rows, d = cc.shape
    n = w.shape[1]
    tn = MOD_TILE_N
    return pl.pallas_call(
        _mod_kernel,
        out_shape=jax.ShapeDtypeStruct((rows, n), F32),
        grid=(n // tn,),
        in_specs=[pl.BlockSpec((rows, d), lambda j: (0, 0)),
                  pl.BlockSpec((d, tn), lambda j: (0, j)),
                  pl.BlockSpec((1, tn), lambda j: (0, j))],
        out_specs=pl.BlockSpec((rows, tn), lambda j: (0, j)),
        compiler_params=_params(1),
        name="modulation",
    )(cc, w, b.reshape(1, n))


def _proj_conv_kernel(*refs, seg, cols, has_bias, norm_tiles):
    x_ref, sc_ref, sh_ref, w_ref, cw_ref = refs[:5]
    k = 5
    cb_ref = refs[k] if has_bias else None
    k += has_bias
    perm_ref = refs[k] if cols else None
    k += cols
    wn_ref = refs[k] if norm_tiles else None
    k += bool(norm_tiles)
    o_ref = refs[k]
    kt_ref = refs[k + 1] if norm_tiles else None
    narrow_ref = refs[k + 2] if norm_tiles else None
    h_ref = refs[-1]
    j = pl.program_id(2)
    n_tok = h_ref.shape[0]

    @pl.when(j == 0)
    def _():
        x = x_ref[...]
        if cols:
            x = x.reshape(n_tok, x.shape[2])
        h = (x * (1.0 + sc_ref[...]) + sh_ref[...]).astype(BF16)
        if cols:
            h = jnp.dot(perm_ref[...], h, preferred_element_type=F32).astype(BF16)
        h_ref[...] = h
        if norm_tiles:
            narrow_ref[...] = jnp.dot(h, wn_ref[...], preferred_element_type=F32)

    y = jnp.dot(h_ref[...], w_ref[...], preferred_element_type=F32)
    y = _conv_seg(y, cw_ref[...], seg)
    if has_bias:
        y = y + cb_ref[...]
    y = _silu(y)

    def store(t):
        o_ref[...] = t.reshape(o_ref.shape)

    if norm_tiles:
        @pl.when(j < norm_tiles)
        def _():
            scale = jnp.where(j == 0, GDN_DK ** -0.5, 1.0)
            heads = []
            for hh in range(y.shape[1] // GDN_DK):
                yh = y[:, hh * GDN_DK:(hh + 1) * GDN_DK]
                heads.append(yh * (lax.rsqrt(jnp.sum(yh * yh, axis=1, keepdims=True) + NORM_EPS) * scale))
            store(jnp.concatenate(heads, axis=1))

            @pl.when(j == 1)
            def _():
                for hh, yh in enumerate(heads):
                    kt_ref[hh] = yh.T

        @pl.when(j >= norm_tiles)
        def _():
            store(y)
    else:
        store(y)


def _proj_conv(x, sc, sh, w, conv_w, conv_b, seg, cols, norm_tiles, tm, tn, name, w_narrow=None):
    bsz, length, d = x.shape
    n = w.shape[1]
    has_bias = conv_b is not None
    if cols:
        rows, cps = length // GRID_W, SUBLANES
        assert seg == rows and seg & (seg - 1) == 0
        n_tok = rows * cps
        x_in = x.reshape(bsz, rows, GRID_W, d)
        x_spec = pl.BlockSpec((None, rows, cps, d), lambda b, i, j: (b, 0, i, 0))
        out_shape = jax.ShapeDtypeStruct((bsz, GRID_W, rows, n), F32)
        out_spec = pl.BlockSpec((None, cps, rows, tn), lambda b, i, j: (b, i, 0, j))
        grid = (bsz, GRID_W // cps, n // tn)
    else:
        n_tok = tm
        x_in = x
        x_spec = pl.BlockSpec((None, tm, d), lambda b, i, j: (b, i, 0))
        out_shape = jax.ShapeDtypeStruct((bsz, length, n), F32)
        out_spec = pl.BlockSpec((None, tm, tn), lambda b, i, j: (b, i, j))
        grid = (bsz, length // tm, n // tn)
    mod = pl.BlockSpec((None, 1, d), lambda b, i, j: (b, 0, 0))
    in_specs = [x_spec, mod, mod, pl.BlockSpec((d, tn), lambda b, i, j: (0, j)),
                pl.BlockSpec((CONV_K, tn), lambda b, i, j: (0, j))]
    args = [x_in, sc, sh, w, conv_w]
    if has_bias:
        in_specs.append(pl.BlockSpec((1, tn), lambda b, i, j: (0, j)))
        args.append(conv_b.reshape(1, n))
    if cols:
        dst = jnp.arange(n_tok)
        src = (dst % rows) * cps + dst // rows
        in_specs.append(pl.BlockSpec((n_tok, n_tok), lambda b, i, j: (0, 0), pipeline_mode=pl.Buffered(1)))
        args.append((src[:, None] == jnp.arange(n_tok)[None, :]).astype(BF16))
    if norm_tiles:
        assert norm_tiles == 2 and tn == GDN_QK and not cols
        in_specs.append(pl.BlockSpec((d, SMALL_N), lambda b, i, j: (0, 0), pipeline_mode=pl.Buffered(1)))
        args.append(w_narrow)
        out_shape = (out_shape, jax.ShapeDtypeStruct((bsz, GDN_HEADS, GDN_DK, length), F32),
                     jax.ShapeDtypeStruct((bsz, length, SMALL_N), F32))
        out_spec = (out_spec, pl.BlockSpec((None, GDN_HEADS, GDN_DK, tm), lambda b, i, j: (b, 0, 0, i)),
                    pl.BlockSpec((None, tm, SMALL_N), lambda b, i, j: (b, i, 0)))
    return pl.pallas_call(
        functools.partial(_proj_conv_kernel, seg=seg, cols=cols, has_bias=has_bias, norm_tiles=norm_tiles),
        out_shape=out_shape,
        grid=grid,
        in_specs=in_specs,
        out_specs=out_spec,
        scratch_shapes=[pltpu.VMEM((n_tok, d), BF16)],
        compiler_params=_params(3),
        name=name,
    )(*args)


def _chunk_cumsums(t):
    ii = lax.broadcasted_iota(jnp.int32, (2 * CHUNK, CHUNK), 0)
    jj = lax.broadcasted_iota(jnp.int32, (2 * CHUNK, CHUNK), 1)
    ones = ((ii < CHUNK) & (ii >= jj)) | ((ii >= CHUNK) & (ii - CHUNK <= jj))
    tri = jnp.where(ones, 1.0, 0.0).astype(BF16)
    hi = t.astype(BF16)
    r1 = t - hi.astype(F32)
    mid = r1.astype(BF16)
    lo = (r1 - mid.astype(F32)).astype(BF16)
    w = t.shape[1]
    sums = jnp.dot(tri, jnp.concatenate([hi, mid, lo], axis=1), preferred_element_type=F32)
    sums = sums[:, :w] + sums[:, w:2 * w] + sums[:, 2 * w:]
    return sums[:CHUNK], sums[CHUNK:]


def _gdn_prep_kernel(s_ref, alog_ref, dtb_ref, o_ref, *, rows):
    s = s_ref[...]
    lane = lax.broadcasted_iota(jnp.int32, (1, LANES), 1)
    g = -jnp.exp(alog_ref[...]) * _softplus(s + dtb_ref[...])
    beta = jax.nn.sigmoid(s)
    for c in range(rows // CHUNK):
        sl = slice(c * CHUNK, (c + 1) * CHUNK)
        fwd, bwd = _chunk_cumsums(g[sl, :])
        gc = jnp.where(lane < SM_A + GDN_HEADS, fwd, bwd)
        o_ref[sl, :] = jnp.where(lane < SM_B, gc, beta[sl, :])


def _gdn_prep(small, a_log, dt_bias, tb):
    bsz, length, _ = small.shape
    pad = lambda t: jnp.pad(t.reshape(1, -1).astype(F32), ((0, 0), (SM_A, LANES - SM_A - 2 * GDN_HEADS)))
    return pl.pallas_call(
        functools.partial(_gdn_prep_kernel, rows=tb),
        out_shape=jax.ShapeDtypeStruct((bsz, length, LANES), F32),
        grid=(bsz, length // tb),
        in_specs=[pl.BlockSpec((None, tb, LANES), lambda b, i: (b, i, 0)),
                  pl.BlockSpec((1, LANES), lambda b, i: (0, 0)),
                  pl.BlockSpec((1, LANES), lambda b, i: (0, 0))],
        out_specs=pl.BlockSpec((None, tb, LANES), lambda b, i: (b, i, 0)),
        compiler_params=_params(2),
        name="gdn_prep",
    )(small, pad(a_log), pad(dt_bias))


def _tri_inverse(mats):
    ii = lax.broadcasted_iota(jnp.int32, (CHUNK, CHUNK), 0)
    jj = lax.broadcasted_iota(jnp.int32, (CHUNK, CHUNK), 1)
    eye = jnp.where(ii == jj, 1.0, 0.0)
    zero = jnp.zeros((CHUNK, CHUNK), F32)
    right = lax.broadcasted_iota(jnp.int32, (CHUNK, 2 * CHUNK), 1) >= CHUNK
    zs = [jnp.concatenate([zero, eye], axis=1) + _mm(a, jnp.concatenate([a, -eye], axis=1)) for a in mats]
    power = 2
    while power < CHUNK:
        ps = [_mm(z[:, :CHUNK], z) for z in zs]
        zs = [p + jnp.where(right, z, 0.0) for p, z in zip(ps, zs)]
        power *= 2
    return zs


def _gdn_local(items, masks):
    qs, ks, vs, kts, gcols, grows, bcols, lowers = zip(*items)
    n = len(items)
    incl = [masks[lo][0] for lo in lowers]
    strict = [masks[lo][1] for lo in lowers]
    glast = [grows[i][:, CHUNK - 1:CHUNK] if lowers[i] else grows[i][:, 0:1] for i in range(n)]
    kq = [_mm(jnp.concatenate([ks[i].astype(BF16), qs[i].astype(BF16)], axis=0), kts[i]) for i in range(n)]
    dec = [jnp.exp(jnp.where(incl[i], gcols[i] - grows[i], 0.0)) for i in range(n)]
    a_mats = [jnp.where(strict[i], bcols[i] * kq[i][:CHUNK] * dec[i], 0.0) for i in range(n)]
    t_inv = _tri_inverse(a_mats)
    eg = [jnp.exp(g) for g in gcols]
    rhs = [jnp.concatenate([(bcols[i] * eg[i]) * ks[i], bcols[i] * vs[i]], axis=1) for i in range(n)]
    pad = jnp.zeros((CHUNK, GDN_DK + GDN_DV), F32)
    wu = [_mm(t, jnp.concatenate([pad, r], axis=0)) for t, r in zip(t_inv, rhs)]
    lhs = [jnp.concatenate([wu[i][:, :GDN_DK].astype(BF16), (qs[i] * eg[i]).astype(BF16)], axis=0) for i in range(n)]
    qk_masked = [jnp.where(incl[i], kq[i][CHUNK:] * dec[i], 0.0).astype(BF16) for i in range(n)]
    kt_dec = [(kts[i] * jnp.exp(glast[i] - grows[i])).astype(BF16) for i in range(n)]
    return [(lhs[i], wu[i][:, GDN_DK:], qk_masked[i], kt_dec[i], jnp.exp(glast[i])) for i in range(n)]


def _gdn_on_state(local, states):
    n = len(local)
    on_state = [_mm(local[i][0], states[i]) for i in range(n)]
    v_new = [local[i][1] - on_state[i][:CHUNK] for i in range(n)]
    o_intra = [_mm(local[i][2], v_new[i]) for i in range(n)]
    kv = [_mm(local[i][3], v_new[i]) for i in range(n)]
    return [(on_state[i][CHUNK:] + o_intra[i], local[i][4] * states[i] + kv[i]) for i in range(n)]


def _gdn_kernel(qf_ref, kf_ref, vf_ref, ktf_ref, qb_ref, kb_ref, vb_ref, ktb_ref, pf_ref, pb_ref, ptf_ref, ptb_ref,
                s0_ref, of_ref, ob_ref, sout_ref, state_ref, *, n_steps, nc):
    step = pl.program_id(1)
    hs = GDN_HEADS

    @pl.when(step == 0)
    def _():
        state_ref[...] = s0_ref[...]

    def heads_of(q_ref, k_ref, v_ref):
        q, k, v = q_ref[...], k_ref[...], v_ref[...]
        return [(q[:, hh * GDN_DK:(hh + 1) * GDN_DK], k[:, hh * GDN_DK:(hh + 1) * GDN_DK],
                 v[:, hh * GDN_DV:(hh + 1) * GDN_DV]) for hh in range(hs)]

    heads_f = heads_of(qf_ref, kf_ref, vf_ref)
    heads_b = heads_of(qb_ref, kb_ref, vb_ref)
    pf = pf_ref[...]
    pb = pb_ref[...]
    masks = {True: _tri_masks(True), False: _tri_masks(False)}
    col = lambda t, idx: t[:, idx:idx + 1]
    cols = [(col(pf, SM_A + hh), col(pf, SM_B + hh), col(pb, SM_A + hs + hh), col(pb, SM_B + hs + hh))
            for hh in range(hs)]
    items = []
    for c in range(nc):
        cb = nc - 1 - c
        sl_f = slice(c * CHUNK, (c + 1) * CHUNK)
        sl_b = slice(cb * CHUNK, (cb + 1) * CHUNK)
        for hh in range(hs):
            gcol_f, bcol_f, gcol_b, bcol_b = cols[hh]
            qf, kf, vf = heads_f[hh]
            qb, kb, vb = heads_b[hh]
            grow_f = ptf_ref[c, SM_A + hh:SM_A + hh + 1, :]
            grow_b = ptb_ref[cb, SM_A + hs + hh:SM_A + hs + hh + 1, :]
            items.append((qf[sl_f], kf[sl_f], vf[sl_f], ktf_ref[hh, :, sl_f], gcol_f[sl_f], grow_f, bcol_f[sl_f], True))
            items.append((qb[sl_b], kb[sl_b], vb[sl_b], ktb_ref[hh, :, sl_b], gcol_b[sl_b], grow_b, bcol_b[sl_b], False))
    local = _gdn_local(items, masks)
    states = [state_ref[hh, d] for hh in range(hs) for d in range(2)]
    for c in range(nc):
        cb = nc - 1 - c
        results = _gdn_on_state(local[c * 2 * hs:(c + 1) * 2 * hs], states)
        states = [r[1] for r in results]
        for hh in range(hs):
            hl = slice(hh * GDN_DV, (hh + 1) * GDN_DV)
            of_ref[c * CHUNK:(c + 1) * CHUNK, hl] = results[2 * hh][0].astype(of_ref.dtype)
            ob_ref[cb * CHUNK:(cb + 1) * CHUNK, hl] = results[2 * hh + 1][0].astype(ob_ref.dtype)
    for hh in range(hs):
        state_ref[hh, 0] = states[2 * hh]
        state_ref[hh, 1] = states[2 * hh + 1]

    @pl.when(step == n_steps - 1)
    def _():
        sout_ref[...] = state_ref[...]


def _gdn_scan(qkv, kt, p, pt, s0, nc):
    bsz, length, _ = qkv.shape
    tb = nc * CHUNK
    n_steps = length // tb
    qoff, koff, voff = 0, 1, 2
    fwd = lambda off: pl.BlockSpec((None, tb, GDN_QK), lambda b, s: (b, s, off))
    bwd = lambda off: pl.BlockSpec((None, tb, GDN_QK), lambda b, s: (b, n_steps - 1 - s, off))
    state_spec = pl.BlockSpec((None, GDN_HEADS, 2, GDN_DK, GDN_DV), lambda b, s: (b, 0, 0, 0, 0))
    return pl.pallas_call(
        functools.partial(_gdn_kernel, n_steps=n_steps, nc=nc),
        out_shape=(jax.ShapeDtypeStruct((bsz, length, GDN_V), SCAN_OUT_DTYPE),
                   jax.ShapeDtypeStruct((bsz, length, GDN_V), SCAN_OUT_DTYPE),
                   jax.ShapeDtypeStruct((bsz, GDN_HEADS, 2, GDN_DK, GDN_DV), F32)),
        grid=(bsz, n_steps),
        in_specs=[fwd(qoff), fwd(koff), fwd(voff), pl.BlockSpec((None, GDN_HEADS, GDN_DK, tb), lambda b, s: (b, 0, 0, s)),
                  bwd(qoff), bwd(koff), bwd(voff),
                  pl.BlockSpec((None, GDN_HEADS, GDN_DK, tb), lambda b, s: (b, 0, 0, n_steps - 1 - s)),
                  pl.BlockSpec((None, tb, LANES), lambda b, s: (b, s, 0)),
                  pl.BlockSpec((None, tb, LANES), lambda b, s: (b, n_steps - 1 - s, 0)),
                  pl.BlockSpec((None, nc, 4 * SUBLANES, CHUNK), lambda b, s: (b, s, 0, 0)),
                  pl.BlockSpec((None, nc, 4 * SUBLANES, CHUNK), lambda b, s: (b, n_steps - 1 - s, 0, 0)),
                  state_spec],
        out_specs=(pl.BlockSpec((None, tb, GDN_V), lambda b, s: (b, s, 0)),
                   pl.BlockSpec((None, tb, GDN_V), lambda b, s: (b, n_steps - 1 - s, 0)),
                   state_spec),
        scratch_shapes=[pltpu.VMEM((GDN_HEADS, 2, GDN_DK, GDN_DV), F32)],
        compiler_params=_params(2),
        name="gdn_scan",
    )(qkv, qkv, qkv, kt, qkv, qkv, qkv, kt, p, p, pt, pt, s0)


def _gdn_branch(qkv, kt, small, qkv_c, kt_c, small_c, a_log, dt_bias):
    bsz = qkv.shape[0]
    s0 = jnp.zeros((bsz, GDN_HEADS, 2, GDN_DK, GDN_DV), F32)

    def rows_t(p):
        chunks = p[:, :, :4 * SUBLANES].reshape(bsz, p.shape[1] // CHUNK, CHUNK, 4 * SUBLANES)
        return jnp.swapaxes(chunks, 2, 3)

    ctx_len = qkv_c.shape[1]
    p_c = _gdn_prep(small_c, a_log, dt_bias, ctx_len)
    oc_f, oc_b, s_ctx = _gdn_scan(qkv_c, kt_c, p_c, rows_t(p_c), s0, min(GDN_CHUNKS_PER_STEP, ctx_len // CHUNK))
    p = _gdn_prep(small, a_log, dt_bias, GDN_PREP_ROWS)
    o_f, o_b, _ = _gdn_scan(qkv, kt, p, rows_t(p), s_ctx, GDN_CHUNKS_PER_STEP)
    return (o_f, o_b), (oc_f, oc_b)


SSD_QUANTITIES = 4
SSD_HEADS_PER_STEP = SSD_GROUPS_PER_STEP * SSM_HPG
SSD_STEP_WIDTH = SSD_HEADS_PER_STEP * SSM_HEAD_DIM
BF16_PIECES = 3


def _ssd_prep_kernel(s_ref, alog_ref, dtb_ref, o_ref, *, rows):
    lane = lax.broadcasted_iota(jnp.int32, (1, LANES), 1)
    fwd_lane = lane < SM_DT_B
    neg_a = -jnp.exp(alog_ref[...])
    for seg in range(o_ref.shape[0]):
        dt = _softplus(s_ref[:, seg * LANES:(seg + 1) * LANES] + dtb_ref[...])
        la = dt * neg_a
        o_ref[seg, :, 0:LANES] = dt
        for c in range(rows // CHUNK):
            sl = slice(c * CHUNK, (c + 1) * CHUNK)
            fwd, bwd = _chunk_cumsums(la[sl, :])
            gc = jnp.where(fwd_lane, fwd, bwd)
            g_last = jnp.where(fwd_lane, gc[CHUNK - 1:CHUNK, :], gc[0:1, :])
            o_ref[seg, sl, LANES:2 * LANES] = gc
            o_ref[seg, sl, 2 * LANES:3 * LANES] = jnp.exp(gc)
            o_ref[seg, sl, 3 * LANES:4 * LANES] = dt[sl, :] * jnp.exp(g_last - gc)


def _ssd_prep(small_cols, a_log, dt_bias, n_seg):
    bsz, rows, _ = small_cols.shape
    sps = min(n_seg, SUBLANES)
    pad = lambda t: jnp.pad(t.reshape(1, -1).astype(F32), ((0, 0), (SM_DT, LANES - SM_DT - 2 * SSM_HEADS)))
    return pl.pallas_call(
        functools.partial(_ssd_prep_kernel, rows=rows),
        out_shape=jax.ShapeDtypeStruct((bsz, n_seg, rows, SSD_QUANTITIES * LANES), F32),
        grid=(bsz, n_seg // sps),
        in_specs=[pl.BlockSpec((None, rows, sps * LANES), lambda b, c: (b, 0, c)),
                  pl.BlockSpec((1, LANES), lambda b, c: (0, 0)),
                  pl.BlockSpec((1, LANES), lambda b, c: (0, 0))],
        out_specs=pl.BlockSpec((None, sps, rows, SSD_QUANTITIES * LANES), lambda b, c: (b, c, 0, 0)),
        compiler_params=_params(2),
        name="ssd_prep",
    )(small_cols, pad(a_log), pad(dt_bias))


def _ssd_layouts(q):
    bsz, n_seg, rows, _ = q.shape
    steps = SSM_GROUPS // SSD_GROUPS_PER_STEP
    q = q.reshape(bsz, n_seg, rows, SSD_QUANTITIES, LANES)[..., SM_DT:SM_DT + 2 * SSM_HEADS]
    q = q.reshape(bsz, n_seg, rows, SSD_QUANTITIES, 2, steps, SSD_HEADS_PER_STEP)
    out = []
    for d in range(2):
        qd = q[:, :, :, :, d]
        cols = jnp.transpose(qd, (0, 1, 4, 2, 3, 5)).reshape(bsz, n_seg, steps, rows, SSD_QUANTITIES * SSD_HEADS_PER_STEP)
        as_rows = lambda t: jnp.transpose(
            t.reshape(bsz, n_seg, rows // CHUNK, CHUNK, steps, SSD_HEADS_PER_STEP),
            (0, 1, 4, 2, 5, 3)).reshape(bsz, n_seg, steps, rows // CHUNK, SSD_STEP_WIDTH)
        out.append((cols, jnp.concatenate([as_rows(qd[:, :, :, 1]), as_rows(qd[:, :, :, 0])], axis=-1)))
    return out


def _ssd_expand_matrix():
    k = SSD_QUANTITIES * SSD_HEADS_PER_STEP
    src = jnp.arange(k)
    w = SSD_STEP_WIDTH
    dst = jnp.concatenate([jnp.arange(w, 2 * w), jnp.arange(3 * w, 4 * w)]) // SSM_HEAD_DIM
    one = (src[:, None] == dst[None, :]).astype(BF16)
    return jnp.concatenate([one] * BF16_PIECES, axis=0)


def _ssd_chunks(items, incl, block_diag):
    n = len(items)
    cb = [_mm_nt(it[4], jnp.concatenate([it[3].astype(BF16)] * SSM_HPG, axis=0)) for it in items]
    inter = [_mm(it[4], it[9]) for it in items]
    lhs = []
    for i, it in enumerate(items):
        m = incl[it[10]]
        lhs.append(jnp.where(m, cb[i] * jnp.exp(jnp.where(m, it[5] - it[6], 0.0)) * it[1], 0.0))
    rhs = [jnp.where(block_diag, jnp.concatenate([it[0].astype(BF16)] * SSM_HPG, axis=0), 0.0) for it in items]
    intra = [_mm(lhs[i], rhs[i]) for i in range(n)]
    upd = [_mm_tn(it[3], it[2]) for it in items]
    out = []
    for i, it in enumerate(items):
        y = intra[i] + inter[i] * it[7]
        if it[11] is not None:
            y = y + it[11] * it[0]
        out.append((y, it[8] * it[9] + upd[i]))
    return out


def _ssd_kernel(xf_ref, bf_ref, cf_ref, xb_ref, bb_ref, cb_ref, colf_ref, colb_ref, growf_ref, growb_ref,
                e_ref, d_ref, s0_ref, yf_ref, yb_ref, sout_ref, state_ref, *, rows, n_seg, gs):
    step = pl.program_id(2)

    @pl.when(step == 0)
    def _():
        state_ref[...] = s0_ref[...]

    gw = SSM_HPG * SSM_HEAD_DIM
    width = gs * gw

    def expand(col_ref):
        c = col_ref[...]
        hi = c.astype(BF16)
        r1 = c - hi.astype(F32)
        mid = r1.astype(BF16)
        lo = (r1 - mid.astype(F32)).astype(BF16)
        ex = jnp.dot(jnp.concatenate([hi, mid, lo], axis=1), e_ref[...], preferred_element_type=F32)
        return ex[:, :width], ex[:, width:]

    xf, xb = xf_ref[...], xb_ref[...]
    gcx_f, q4x_f = expand(colf_ref)
    gcx_b, q4x_b = expand(colb_ref)
    egx_f, egx_b = jnp.exp(gcx_f), jnp.exp(gcx_b)
    xdec_f, xdec_b = xf * q4x_f, xb * q4x_b
    bmf, cmf, bmb, cmb = bf_ref[...], cf_ref[...], bb_ref[...], cb_ref[...]
    d_all = d_ref[...]

    row = lax.broadcasted_iota(jnp.int32, (CHUNK, gw), 0)
    tok = lax.broadcasted_iota(jnp.int32, (CHUNK, gw), 1) & (CHUNK - 1)
    incl = {True: row >= tok, False: row <= tok}
    block_diag = (lax.shift_right_logical(lax.broadcasted_iota(jnp.int32, (gw, gw), 0), CHUNK.bit_length() - 1)
                  == lax.shift_right_logical(lax.broadcasted_iota(jnp.int32, (gw, gw), 1), SSM_HEAD_DIM.bit_length() - 1))

    n_chunks = rows // CHUNK
    states = [[state_ref[gg, 0], state_ref[gg, 1]] for gg in range(gs)]
    for c in range(n_chunks):
        cb = n_chunks - 1 - c
        sl_f = slice(c * CHUNK, (c + 1) * CHUNK)
        sl_b = slice(cb * CHUNK, (cb + 1) * CHUNK)
        last_f = slice(c * CHUNK + CHUNK - 1, (c + 1) * CHUNK)
        last_b = slice(cb * CHUNK, cb * CHUNK + 1)
        items = []
        for gg in range(gs):
            xl = slice(gg * gw, (gg + 1) * gw)
            nl = slice(gg * SSM_STATE, (gg + 1) * SSM_STATE)
            dl = slice(width + gg * gw, width + (gg + 1) * gw)
            items.append((xf[sl_f, xl], growf_ref[c:c + 1, dl], xdec_f[sl_f, xl], bmf[sl_f, nl], cmf[sl_f, nl],
                          gcx_f[sl_f, xl], growf_ref[c:c + 1, xl], egx_f[sl_f, xl], egx_f[last_f, xl], states[gg][0], True,
                          d_all[:, xl]))
            items.append((xb[sl_b, xl], growb_ref[cb:cb + 1, dl], xdec_b[sl_b, xl], bmb[sl_b, nl], cmb[sl_b, nl],
                          gcx_b[sl_b, xl], growb_ref[cb:cb + 1, xl], egx_b[sl_b, xl], egx_b[last_b, xl], states[gg][1], False,
                          None))
        results = _ssd_chunks(items, incl, block_diag)
        for gg in range(gs):
            xl = slice(gg * gw, (gg + 1) * gw)
            yf_ref[sl_f, xl] = results[2 * gg][0].astype(yf_ref.dtype)
            yb_ref[sl_b, xl] = results[2 * gg + 1][0].astype(yb_ref.dtype)
            states[gg] = [results[2 * gg][1], results[2 * gg + 1][1]]
    for gg in range(gs):
        state_ref[gg, 0] = states[gg][0]
        state_ref[gg, 1] = states[gg][1]

    @pl.when(step == n_seg - 1)
    def _():
        sout_ref[...] = state_ref[...]


def _ssd_scan(xbc, layouts, expand, d_exp, s0):
    bsz, n_seg, rows, _ = xbc.shape
    gs = SSD_GROUPS_PER_STEP
    xw, nw = SSD_STEP_WIDTH, gs * SSM_STATE
    boff, coff = SSM_INNER // nw, (SSM_INNER + SSM_GN) // nw
    (cols_f, grow_f), (cols_b, grow_b) = layouts
    n_cols = cols_f.shape[-1]
    seg_f = lambda s: s
    seg_b = lambda s: n_seg - 1 - s
    xspec = lambda seg: pl.BlockSpec((None, None, rows, xw), lambda b, g, s: (b, seg(s), 0, g))
    nspec = lambda seg, off: pl.BlockSpec((None, None, rows, nw), lambda b, g, s: (b, seg(s), 0, off + g))
    cspec = lambda seg: pl.BlockSpec((None, None, None, rows, n_cols), lambda b, g, s: (b, seg(s), g, 0, 0))
    rspec = lambda seg: pl.BlockSpec((None, None, None, rows // CHUNK, 2 * xw), lambda b, g, s: (b, seg(s), g, 0, 0))
    state_spec = pl.BlockSpec((None, gs, 2, SSM_STATE, SSM_HPG * SSM_HEAD_DIM), lambda b, g, s: (b, g, 0, 0, 0))
    y_shape = jax.ShapeDtypeStruct((bsz, n_seg, rows, SSM_INNER), SCAN_OUT_DTYPE)
    return pl.pallas_call(
        functools.partial(_ssd_kernel, rows=rows, n_seg=n_seg, gs=gs),
        out_shape=(y_shape, y_shape,
                   jax.ShapeDtypeStruct((bsz, SSM_GROUPS, 2, SSM_STATE, SSM_HPG * SSM_HEAD_DIM), F32)),
        grid=(bsz, SSM_GROUPS // gs, n_seg),
        in_specs=[xspec(seg_f), nspec(seg_f, boff), nspec(seg_f, coff),
                  xspec(seg_b), nspec(seg_b, boff), nspec(seg_b, coff),
                  cspec(seg_f), cspec(seg_b), rspec(seg_f), rspec(seg_b),
                  pl.BlockSpec(expand.shape, lambda b, g, s: (0, 0)),
                  pl.BlockSpec((1, xw), lambda b, g, s: (0, g)), state_spec],
        out_specs=(xspec(seg_f), xspec(seg_b), state_spec),
        scratch_shapes=[pltpu.VMEM((gs, 2, SSM_STATE, SSM_HPG * SSM_HEAD_DIM), F32)],
        compiler_params=_params(3),
        name="ssd_scan",
    )(xbc, xbc, xbc, xbc, xbc, xbc, cols_f, cols_b, grow_f, grow_b, expand, d_exp, s0)


def _ssd_branch(xbc, small, xbc_c, small_c, a_log, dt_bias, d_skip):
    bsz, n_seg, rows, _ = xbc.shape
    assert rows % CHUNK == 0 and xbc_c.shape[2] % CHUNK == 0
    d_exp = jnp.repeat(d_skip.astype(F32), SSM_HEAD_DIM).reshape(1, SSM_INNER)
    s0 = jnp.zeros((bsz, SSM_GROUPS, 2, SSM_STATE, SSM_HPG * SSM_HEAD_DIM), F32)
    expand = _ssd_expand_matrix()
    yc_f, yc_b, s_ctx = _ssd_scan(xbc_c, _ssd_layouts(_ssd_prep(small_c, a_log, dt_bias, 1)), expand, d_exp, s0)
    q = _ssd_prep(small.reshape(bsz, rows, n_seg * SMALL_N), a_log, dt_bias, n_seg)
    steps, step_rows = n_seg // SSD_COLUMNS_PER_STEP, SSD_COLUMNS_PER_STEP * rows
    merged = lambda t: t.reshape(bsz, steps, step_rows, t.shape[-1])
    y_f, y_b, _ = _ssd_scan(merged(xbc), _ssd_layouts(merged(q)), expand, d_exp, s_ctx)
    to_raster = lambda y: jnp.swapaxes(y.reshape(bsz, n_seg, rows, SSM_INNER), 1, 2).reshape(bsz, rows * n_seg, SSM_INNER)
    return (to_raster(y_f), to_raster(y_b)), (yc_f[:, 0], yc_b[:, 0])


def _layer_norm(r, g, b):
    mu = jnp.mean(r, axis=1, keepdims=True)
    var = jnp.mean(jnp.square(r - mu), axis=1, keepdims=True)
    return (r - mu) * lax.rsqrt(var + LN_EPS) * g + b


def _merge_kernel(x_ref, sc_ref, sh_ref, wg_ref, of_ref, ob_ref, yf_ref, yb_ref,
                  nwa_ref, nwb_ref, wpg_ref, wps_ref, wout_ref, g1_ref, lng_ref, lnb_ref, o_ref, *, alpha):
    x = x_ref[...]
    h = (x * (1.0 + sc_ref[...]) + sh_ref[...]).astype(BF16)
    gate = lambda off, width: jnp.dot(h, wg_ref[:, off:off + width], preferred_element_type=F32)
    o = of_ref[...].astype(F32) + ob_ref[...].astype(F32)
    normed = []
    for hh in range(GDN_HEADS):
        oh = o[:, hh * GDN_DV:(hh + 1) * GDN_DV]
        normed.append(oh * lax.rsqrt(jnp.mean(oh * oh, axis=1, keepdims=True) + NORM_EPS))
    y_a = jnp.concatenate(normed, axis=1) * nwa_ref[...] * _silu(gate(GOUT_OFF, GDN_V))
    t = (yf_ref[...].astype(F32) + yb_ref[...].astype(F32)) * _silu(gate(Z_OFF, SSM_INNER))
    gw = SSM_INNER // SSM_GROUPS
    normed = []
    for g in range(SSM_GROUPS):
        tg = t[:, g * gw:(g + 1) * gw]
        normed.append(tg * lax.rsqrt(jnp.mean(tg * tg, axis=1, keepdims=True) + NORM_EPS))
    y_b = jnp.concatenate(normed, axis=1) * nwb_ref[...]
    d = x.shape[1]
    mix = (jax.nn.sigmoid(gate(GA_OFF, d)) * jnp.dot(y_a.astype(BF16), wpg_ref[...], preferred_element_type=F32)
           + jax.nn.sigmoid(gate(GB_OFF, d)) * jnp.dot(y_b.astype(BF16), wps_ref[...], preferred_element_type=F32))
    out = jnp.dot(mix.astype(BF16), wout_ref[...], preferred_element_type=F32)
    o_ref[...] = _layer_norm(alpha * x + g1_ref[...] * out, lng_ref[...], lnb_ref[...])


def _merge(x, sc, sh, w_gates, o_pair, y_pair, nwa, nwb, wpg, wps, wout, g1, ln_g, ln_b, alpha, tm):
    bsz, length, d = x.shape
    row = lambda width: pl.BlockSpec((None, tm, width), lambda b, i: (b, i, 0))
    const = lambda shape: pl.BlockSpec(shape, lambda b, i: (0,) * len(shape), pipeline_mode=pl.Buffered(1))
    mod = pl.BlockSpec((None, 1, d), lambda b, i: (b, 0, 0))
    return pl.pallas_call(
        functools.partial(_merge_kernel, alpha=alpha),
        out_shape=jax.ShapeDtypeStruct((bsz, length, d), F32),
        grid=(bsz, length // tm),
        in_specs=[row(d), mod, mod, const((d, BIG_N)),
                  row(GDN_V), row(GDN_V), row(SSM_INNER), row(SSM_INNER),
                  const((1, GDN_V)), const((1, SSM_INNER)), const((GDN_V, d)), const((SSM_INNER, d)), const((d, d)),
                  mod, const((1, d)), const((1, d))],
        out_specs=row(d),
        compiler_params=_params(2),
        name="merge",
    )(x, sc, sh, w_gates, o_pair[0], o_pair[1], y_pair[0], y_pair[1], nwa, nwb, wpg, wps, wout, g1, ln_g, ln_b)


def _mlp_kernel(x_ref, sc_ref, sh_ref, g2_ref, w1_ref, b1_ref, w2_ref, b2_ref, lng_ref, lnb_ref, o_ref, *, alpha):
    x = x_ref[...]
    h = (x * (1.0 + sc_ref[...]) + sh_ref[...]).astype(BF16)
    acc = None
    tf = D_MODEL
    for c in range(D_FF // tf):
        u = jnp.dot(h, w1_ref[:, c * tf:(c + 1) * tf], preferred_element_type=F32) + b1_ref[:, c * tf:(c + 1) * tf]
        u = jnp.square(jnp.maximum(u, 0.0))
        part = jnp.dot(u.astype(BF16), w2_ref[c * tf:(c + 1) * tf, :], preferred_element_type=F32)
        acc = part if acc is None else acc + part
    f = acc + b2_ref[...]
    o_ref[...] = _layer_norm(alpha * x + g2_ref[...] * f, lng_ref[...], lnb_ref[...])


def _mlp(x, sc, sh, g2, w1, b1, w2, b2, ln_g, ln_b, alpha, tm):
    bsz, length, d = x.shape
    const = lambda shape: pl.BlockSpec(shape, lambda b, i: (0,) * len(shape), pipeline_mode=pl.Buffered(1))
    mod = pl.BlockSpec((None, 1, d), lambda b, i: (b, 0, 0))
    return pl.pallas_call(
        functools.partial(_mlp_kernel, alpha=alpha),
        out_shape=jax.ShapeDtypeStruct((bsz, length, d), F32),
        grid=(bsz, length // tm),
        in_specs=[pl.BlockSpec((None, tm, d), lambda b, i: (b, i, 0)), mod, mod, mod,
                  const((d, D_FF)), const((1, D_FF)), const((D_FF, d)), const((1, d)), const((1, d)), const((1, d))],
        out_specs=pl.BlockSpec((None, tm, d), lambda b, i: (b, i, 0)),
        compiler_params=_params(2),
        name="mlp",
    )(x, sc, sh, g2, w1, b1, w2, b2, ln_g, ln_b)


def _split_w_in(w_in):
    pts, acc = [], 0
    for s in IN_SPLITS[:-1]:
        acc += s
        pts.append(acc)
    qkv, gout, a_raw, b_raw, z, xbc, dt_raw, gate_a, gate_b = jnp.split(w_in, pts, axis=1)
    big = jnp.concatenate([z, gout, gate_a, gate_b], axis=1).astype(BF16)
    pad = jnp.zeros((w_in.shape[0], SMALL_N - SM_DT - 2 * SSM_HEADS), w_in.dtype)
    small = jnp.concatenate([a_raw, b_raw, dt_raw, pad], axis=1).astype(BF16)
    return big, qkv.astype(BF16), xbc.astype(BF16), small


def kernel(x, c, ctx, c_ctx, w_mod, b_mod, w_in, gdn_conv_w, gdn_A_log, gdn_dt_bias, gdn_norm_w,
           ssm_conv_w, ssm_conv_b, ssm_A_log, ssm_dt_bias, ssm_D, ssm_norm_w,
           w_proj_gdn, w_proj_ssm, w_out, ln1_g, ln1_b, w_ff1, b_ff1, w_ff2, b_ff2, ln2_g, ln2_b):
    bsz, length, d = x.shape
    ctx_len = ctx.shape[1]
    depth = w_mod.shape[0]
    alpha = float((2 * depth) ** 0.25)
    mod_rows = -(-(bsz + 1) // (2 * SUBLANES)) * (2 * SUBLANES)
    cc = jnp.concatenate([c, c_ctx[None, :], jnp.zeros((mod_rows - bsz - 1, d), c.dtype)], axis=0)
    row2 = lambda t: t.reshape(1, -1)
    tm_lat = PROJ_ROWS
    tm_ctx = ctx_len

    for l in range(depth):
        last = l == depth - 1
        mod = _modulation(cc, w_mod[l], b_mod[l])
        lat = [mod[:bsz, i * d:(i + 1) * d].reshape(bsz, 1, d) for i in range(6)]
        cxm = [jnp.broadcast_to(mod[bsz, i * d:(i + 1) * d].reshape(1, 1, d), (bsz, 1, d)) for i in range(6)]
        w_big, w_qkv, w_xbc, w_small = _split_w_in(w_in[l])
        rows = length // GRID_W

        qkv, kt, small = _proj_conv(x, lat[1], lat[0], w_qkv, gdn_conv_w[l], None, GRID_W, False, 2, tm_lat, GDN_QK,
                                    "inproj_qkv", w_small)
        xbc = _proj_conv(x, lat[1], lat[0], w_xbc, ssm_conv_w[l], ssm_conv_b[l], rows, True, 0, None, PROJ_TILE_N,
                         "inproj_xbc")
        qkv_c, kt_c, small_c = _proj_conv(ctx, cxm[1], cxm[0], w_qkv, gdn_conv_w[l], None, ctx_len, False, 2, tm_ctx,
                                          GDN_QK, "inproj_qkv_ctx", w_small)
        xbc_c = _proj_conv(ctx, cxm[1], cxm[0], w_xbc, ssm_conv_w[l], ssm_conv_b[l], ctx_len, False, 0, tm_ctx, PROJ_TILE_N,
                           "inproj_xbc_ctx")[:, None]

        o_pair, oc_pair = _gdn_branch(qkv, kt, small, qkv_c, kt_c, small_c, gdn_A_log[l], gdn_dt_bias[l])
        y_pair, yc_pair = _ssd_branch(xbc, small, xbc_c, small_c, ssm_A_log[l], ssm_dt_bias[l], ssm_D[l])

        nwa = jnp.tile(gdn_norm_w[l], GDN_HEADS).reshape(1, GDN_V)
        nwb = row2(ssm_norm_w[l])
        wpg, wps, wo = w_proj_gdn[l].astype(BF16), w_proj_ssm[l].astype(BF16), w_out[l].astype(BF16)
        w1, w2 = w_ff1[l].astype(BF16), w_ff2[l].astype(BF16)
        merge_args = (nwa, nwb, wpg, wps, wo)
        ln1 = (row2(ln1_g[l]), row2(ln1_b[l]))
        mlp_w = (w1, row2(b_ff1[l]), w2, row2(b_ff2[l]), row2(ln2_g[l]), row2(ln2_b[l]))

        x1 = _merge(x, lat[1], lat[0], w_big, o_pair, y_pair, *merge_args, lat[2], *ln1, alpha, MERGE_ROWS)
        x = _mlp(x1, lat[4], lat[3], lat[5], *mlp_w, alpha, MLP_ROWS)
        if not last:
            c1 = _merge(ctx, cxm[1], cxm[0], w_big, oc_pair, yc_pair, *merge_args, cxm[2], *ln1, alpha, ctx_len)
            ctx = _mlp(c1, cxm[4], cxm[3], cxm[5], *mlp_w, alpha, ctx_len)
    return x
```

```python
import functools

import jax
import jax.numpy as jnp
from jax import lax
from jax.experimental import pallas as pl
from jax.experimental.pallas import tpu as pltpu

F32 = jnp.float32
BF16 = jnp.bfloat16

D_MODEL = 1024
GRID_W = 64
GDN_HEADS = 8
GDN_DK = 128
GDN_DV = 128
GDN_QK = GDN_HEADS * GDN_DK
GDN_V = GDN_HEADS * GDN_DV
GDN_QKV = 2 * GDN_QK + GDN_V
SSM_INNER = 2 * D_MODEL
SSM_HEAD_DIM = 64
SSM_HEADS = SSM_INNER // SSM_HEAD_DIM
SSM_GROUPS = 8
SSM_HPG = SSM_HEADS // SSM_GROUPS
SSM_STATE = 128
SSM_GN = SSM_GROUPS * SSM_STATE
SSM_XBC = SSM_INNER + 2 * SSM_GN
CONV_K = 5
CHUNK = 64
D_FF = 4 * D_MODEL
LN_EPS = 1e-5
NORM_EPS = 1e-6
IN_SPLITS = (GDN_QKV, GDN_V, 2 * GDN_HEADS, 2 * GDN_HEADS, SSM_INNER, SSM_XBC, 2 * SSM_HEADS, D_MODEL, D_MODEL)

LANES = 128
SUBLANES = 8
VMEM_LIMIT_BYTES = 56 * 1024 * 1024

Z_OFF = 0
GOUT_OFF = Z_OFF + SSM_INNER
GA_OFF = GOUT_OFF + GDN_V
GB_OFF = GA_OFF + D_MODEL
BIG_N = GB_OFF + D_MODEL
SM_A = 0
SM_B = SM_A + 2 * GDN_HEADS
SM_DT = SM_B + 2 * GDN_HEADS
SM_DT_B = SM_DT + SSM_HEADS
SMALL_N = LANES

GDN_CHUNKS_PER_STEP = 4
GDN_PREP_ROWS = 512
SCAN_OUT_DTYPE = BF16
MERGE_ROWS = 512
MLP_ROWS = 1024
PROJ_ROWS = 1024
PROJ_TILE_N = 1024
MOD_TILE_N = 1536
SSD_GROUPS_PER_STEP = 2
SSD_COLUMNS_PER_STEP = 8


def _params(n_axes):
    return pltpu.CompilerParams(dimension_semantics=("arbitrary",) * n_axes, vmem_limit_bytes=VMEM_LIMIT_BYTES)


def _silu(t):
    return t * jax.nn.sigmoid(t)


def _softplus(t):
    return jnp.maximum(t, 0.0) + jnp.log(1.0 + jnp.exp(-jnp.abs(t)))


def _mm(a, b):
    return jnp.dot(a.astype(BF16), b.astype(BF16), preferred_element_type=F32)


def _mm_nt(a, b):
    return lax.dot_general(a.astype(BF16), b.astype(BF16), (((1,), (1,)), ((), ())), preferred_element_type=F32)


def _mm_tn(a, b):
    return lax.dot_general(a.astype(BF16), b.astype(BF16), (((0,), (0,)), ((), ())), preferred_element_type=F32)


def _tri_masks(lower):
    ii = lax.broadcasted_iota(jnp.int32, (CHUNK, CHUNK), 0)
    jj = lax.broadcasted_iota(jnp.int32, (CHUNK, CHUNK), 1)
    if lower:
        return ii >= jj, ii > jj
    return ii <= jj, ii < jj


def _conv_seg(x, w, seg):
    n = x.shape[0]
    assert seg & (seg - 1) == 0 and n % seg == 0
    pos = lax.broadcasted_iota(jnp.int32, (n, 1), 0) & (seg - 1)
    out = None
    for j in range(CONV_K):
        d = j - CONV_K // 2
        if d == 0:
            term = x * w[j:j + 1, :]
        else:
            shifted = pltpu.roll(x, shift=(-d) % n, axis=0)
            valid = (pos + d >= 0) & (pos + d < seg)
            term = jnp.where(valid, shifted, 0.0) * w[j:j + 1, :]
        out = term if out is None else out + term
    return out


def _mod_kernel(c_ref, w_ref, b_ref, o_ref):
    o_ref[...] = _mm(_silu(c_ref[...]), w_ref[...]) + b_ref[...]


def _modulation(cc, w, b):
    rows, d = cc.shape
    n = w.shape[1]
    tn = MOD_TILE_N
    return pl.pallas_call(
        _mod_kernel,
        out_shape=jax.ShapeDtypeStruct((rows, n), F32),
        grid=(n // tn,),
        in_specs=[pl.BlockSpec((rows, d), lambda j: (0, 0)),
                  pl.BlockSpec((d, tn), lambda j: (0, j)),
                  pl.BlockSpec((1, tn), lambda j: (0, j))],
        out_specs=pl.BlockSpec((rows, tn), lambda j: (0, j)),
        compiler_params=_params(1),
        name="modulation",
    )(cc, w, b.reshape(1, n))


def _proj_conv_kernel(*refs, seg, cols, has_bias, norm_tiles):
    x_ref, sc_ref, sh_ref, w_ref, cw_ref = refs[:5]
    k = 5
    cb_ref = refs[k] if has_bias else None
    k += has_bias
    perm_ref = refs[k] if cols else None
    k += cols
    wn_ref = refs[k] if norm_tiles else None
    k += bool(norm_tiles)
    o_ref = refs[k]
    kt_ref = refs[k + 1] if norm_tiles else None
    narrow_ref = refs[k + 2] if norm_tiles else None
    h_ref = refs[-1]
    j = pl.program_id(2)
    n_tok = h_ref.shape[0]

    @pl.when(j == 0)
    def _():
        x = x_ref[...]
        if cols:
            x = x.reshape(n_tok, x.shape[2])
        h = (x * (1.0 + sc_ref[...]) + sh_ref[...]).astype(BF16)
        if cols:
            h = jnp.dot(perm_ref[...], h, preferred_element_type=F32).astype(BF16)
        h_ref[...] = h
        if norm_tiles:
            narrow_ref[...] = jnp.dot(h, wn_ref[...], preferred_element_type=F32)

    y = jnp.dot(h_ref[...], w_ref[...], preferred_element_type=F32)
    y = _conv_seg(y, cw_ref[...], seg)
    if has_bias:
        y = y + cb_ref[...]
    y = _silu(y)

    def store(t):
        o_ref[...] = t.reshape(o_ref.shape)

    if norm_tiles:
        @pl.when(j < norm_tiles)
        def _():
            scale = jnp.where(j == 0, GDN_DK ** -0.5, 1.0)
            heads = []
            for hh in range(y.shape[1] // GDN_DK):
                yh = y[:, hh * GDN_DK:(hh + 1) * GDN_DK]
                heads.append(yh * (lax.rsqrt(jnp.sum(yh * yh, axis=1, keepdims=True) + NORM_EPS) * scale))
            store(jnp.concatenate(heads, axis=1))

            @pl.when(j == 1)
            def _():
                for hh, yh in enumerate(heads):
                    kt_ref[hh] = yh.T

        @pl.when(j >= norm_tiles)
        def _():
            store(y)
    else:
        store(y)


def _proj_conv(x, sc, sh, w, conv_w, conv_b, seg, cols, norm_tiles, tm, tn, name, w_narrow=None):
    bsz, length, d = x.shape
    n = w.shape[1]
    has_bias = conv_b is not None
    if cols:
        rows, cps = length // GRID_W, SUBLANES
        assert seg == rows and seg & (seg - 1) == 0
        n_tok = rows * cps
        x_in = x.reshape(bsz, rows, GRID_W, d)
        x_spec = pl.BlockSpec((None, rows, cps, d), lambda b, i, j: (b, 0, i, 0))
        out_shape = jax.ShapeDtypeStruct((bsz, GRID_W, rows, n), F32)
        out_spec = pl.BlockSpec((None, cps, rows, tn), lambda b, i, j: (b, i, 0, j))
        grid = (bsz, GRID_W // cps, n // tn)
    else:
        n_tok = tm
        x_in = x
        x_spec = pl.BlockSpec((None, tm, d), lambda b, i, j: (b, i, 0))
        out_shape = jax.ShapeDtypeStruct((bsz, length, n), F32)
        out_spec = pl.BlockSpec((None, tm, tn), lambda b, i, j: (b, i, j))
        grid = (bsz, length // tm, n // tn)
    mod = pl.BlockSpec((None, 1, d), lambda b, i, j: (b, 0, 0))
    in_specs = [x_spec, mod, mod, pl.BlockSpec((d, tn), lambda b, i, j: (0, j)),
                pl.BlockSpec((CONV_K, tn), lambda b, i, j: (0, j))]
    args = [x_in, sc, sh, w, conv_w]
    if has_bias:
        in_specs.append(pl.BlockSpec((1, tn), lambda b, i, j: (0, j)))
        args.append(conv_b.reshape(1, n))
    if cols:
        dst = jnp.arange(n_tok)
        src = (dst % rows) * cps + dst // rows
        in_specs.append(pl.BlockSpec((n_tok, n_tok), lambda b, i, j: (0, 0), pipeline_mode=pl.Buffered(1)))
        args.append((src[:, None] == jnp.arange(n_tok)[None, :]).astype(BF16))
    if norm_tiles:
        assert norm_tiles == 2 and tn == GDN_QK and not cols
        in_specs.append(pl.BlockSpec((d, SMALL_N), lambda b, i, j: (0, 0), pipeline_mode=pl.Buffered(1)))
        args.append(w_narrow)
        out_shape = (out_shape, jax.ShapeDtypeStruct((bsz, GDN_HEADS, GDN_DK, length), F32),
                     jax.ShapeDtypeStruct((bsz, length, SMALL_N), F32))
        out_spec = (out_spec, pl.BlockSpec((None, GDN_HEADS, GDN_DK, tm), lambda b, i, j: (b, 0, 0, i)),
                    pl.BlockSpec((None, tm, SMALL_N), lambda b, i, j: (b, i, 0)))
    return pl.pallas_call(
        functools.partial(_proj_conv_kernel, seg=seg, cols=cols, has_bias=has_bias, norm_tiles=norm_tiles),
        out_shape=out_shape,
        grid=grid,
        in_specs=in_specs,
        out_specs=out_spec,
        scratch_shapes=[pltpu.VMEM((n_tok, d), BF16)],
        compiler_params=_params(3),
        name=name,
    )(*args)


def _chunk_cumsums(t):
    ii = lax.broadcasted_iota(jnp.int32, (2 * CHUNK, CHUNK), 0)
    jj = lax.broadcasted_iota(jnp.int32, (2 * CHUNK, CHUNK), 1)
    ones = ((ii < CHUNK) & (ii >= jj)) | ((ii >= CHUNK) & (ii - CHUNK <= jj))
    tri = jnp.where(ones, 1.0, 0.0).astype(BF16)
    hi = t.astype(BF16)
    r1 = t - hi.astype(F32)
    mid = r1.astype(BF16)
    lo = (r1 - mid.astype(F32)).astype(BF16)
    w = t.shape[1]
    sums = jnp.dot(tri, jnp.concatenate([hi, mid, lo], axis=1), preferred_element_type=F32)
    sums = sums[:, :w] + sums[:, w:2 * w] + sums[:, 2 * w:]
    return sums[:CHUNK], sums[CHUNK:]


def _gdn_prep_kernel(s_ref, alog_ref, dtb_ref, o_ref, *, rows):
    s = s_ref[...]
    lane = lax.broadcasted_iota(jnp.int32, (1, LANES), 1)
    g = -jnp.exp(alog_ref[...]) * _softplus(s + dtb_ref[...])
    beta = jax.nn.sigmoid(s)
    for c in range(rows // CHUNK):
        sl = slice(c * CHUNK, (c + 1) * CHUNK)
        fwd, bwd = _chunk_cumsums(g[sl, :])
        gc = jnp.where(lane < SM_A + GDN_HEADS, fwd, bwd)
        o_ref[sl, :] = jnp.where(lane < SM_B, gc, beta[sl, :])


def _gdn_prep(small, a_log, dt_bias, tb):
    bsz, length, _ = small.shape
    pad = lambda t: jnp.pad(t.reshape(1, -1).astype(F32), ((0, 0), (SM_A, LANES - SM_A - 2 * GDN_HEADS)))
    return pl.pallas_call(
        functools.partial(_gdn_prep_kernel, rows=tb),
        out_shape=jax.ShapeDtypeStruct((bsz, length, LANES), F32),
        grid=(bsz, length // tb),
        in_specs=[pl.BlockSpec((None, tb, LANES), lambda b, i: (b, i, 0)),
                  pl.BlockSpec((1, LANES), lambda b, i: (0, 0)),
                  pl.BlockSpec((1, LANES), lambda b, i: (0, 0))],
        out_specs=pl.BlockSpec((None, tb, LANES), lambda b, i: (b, i, 0)),
        compiler_params=_params(2),
        name="gdn_prep",
    )(small, pad(a_log), pad(dt_bias))


def _tri_inverse(mats):
    ii = lax.broadcasted_iota(jnp.int32, (CHUNK, CHUNK), 0)
    jj = lax.broadcasted_iota(jnp.int32, (CHUNK, CHUNK), 1)
    eye = jnp.where(ii == jj, 1.0, 0.0)
    zero = jnp.zeros((CHUNK, CHUNK), F32)
    right = lax.broadcasted_iota(jnp.int32, (CHUNK, 2 * CHUNK), 1) >= CHUNK
    zs = [jnp.concatenate([zero, eye], axis=1) + _mm(a, jnp.concatenate([a, -eye], axis=1)) for a in mats]
    power = 2
    while power < CHUNK:
        ps = [_mm(z[:, :CHUNK], z) for z in zs]
        zs = [p + jnp.where(right, z, 0.0) for p, z in zip(ps, zs)]
        power *= 2
    return zs


def _gdn_local(items, masks):
    qs, ks, vs, kts, gcols, grows, bcols, lowers = zip(*items)
    n = len(items)
    incl = [masks[lo][0] for lo in lowers]
    strict = [masks[lo][1] for lo in lowers]
    glast = [grows[i][:, CHUNK - 1:CHUNK] if lowers[i] else grows[i][:, 0:1] for i in range(n)]
    kq = [_mm(jnp.concatenate([ks[i].astype(BF16), qs[i].astype(BF16)], axis=0), kts[i]) for i in range(n)]
    dec = [jnp.exp(jnp.where(incl[i], gcols[i] - grows[i], 0.0)) for i in range(n)]
    a_mats = [jnp.where(strict[i], bcols[i] * kq[i][:CHUNK] * dec[i], 0.0) for i in range(n)]
    t_inv = _tri_inverse(a_mats)
    eg = [jnp.exp(g) for g in gcols]
    rhs = [jnp.concatenate([(bcols[i] * eg[i]) * ks[i], bcols[i] * vs[i]], axis=1) for i in range(n)]
    pad = jnp.zeros((CHUNK, GDN_DK + GDN_DV), F32)
    wu = [_mm(t, jnp.concatenate([pad, r], axis=0)) for t, r in zip(t_inv, rhs)]
    lhs = [jnp.concatenate([wu[i][:, :GDN_DK].astype(BF16), (qs[i] * eg[i]).astype(BF16)], axis=0) for i in range(n)]
    qk_masked = [jnp.where(incl[i], kq[i][CHUNK:] * dec[i], 0.0).astype(BF16) for i in range(n)]
    kt_dec = [(kts[i] * jnp.exp(glast[i] - grows[i])).astype(BF16) for i in range(n)]
    return [(lhs[i], wu[i][:, GDN_DK:], qk_masked[i], kt_dec[i], jnp.exp(glast[i])) for i in range(n)]


def _gdn_on_state(local, states):
    n = len(local)
    on_state = [_mm(local[i][0], states[i]) for i in range(n)]
    v_new = [local[i][1] - on_state[i][:CHUNK] for i in range(n)]
    on_v = [_mm(jnp.concatenate([local[i][2], local[i][3]], axis=0), v_new[i]) for i in range(n)]
    return [(on_state[i][CHUNK:] + on_v[i][:CHUNK], local[i][4] * states[i] + on_v[i][CHUNK:]) for i in range(n)]


def _gdn_kernel(qf_ref, kf_ref, vf_ref, ktf_ref, qb_ref, kb_ref, vb_ref, ktb_ref, pf_ref, pb_ref, ptf_ref, ptb_ref,
                s0_ref, of_ref, ob_ref, sout_ref, state_ref, *, n_steps, nc):
    step = pl.program_id(1)
    hs = GDN_HEADS

    @pl.when(step == 0)
    def _():
        state_ref[...] = s0_ref[...]

    def heads_of(q_ref, k_ref, v_ref):
        q, k, v = q_ref[...], k_ref[...], v_ref[...]
        return [(q[:, hh * GDN_DK:(hh + 1) * GDN_DK], k[:, hh * GDN_DK:(hh + 1) * GDN_DK],
                 v[:, hh * GDN_DV:(hh + 1) * GDN_DV]) for hh in range(hs)]

    heads_f = heads_of(qf_ref, kf_ref, vf_ref)
    heads_b = heads_of(qb_ref, kb_ref, vb_ref)
    pf = pf_ref[...]
    pb = pb_ref[...]
    masks = {True: _tri_masks(True), False: _tri_masks(False)}
    col = lambda t, idx: t[:, idx:idx + 1]
    cols = [(col(pf, SM_A + hh), col(pf, SM_B + hh), col(pb, SM_A + hs + hh), col(pb, SM_B + hs + hh))
            for hh in range(hs)]
    items = []
    for c in range(nc):
        cb = nc - 1 - c
        sl_f = slice(c * CHUNK, (c + 1) * CHUNK)
        sl_b = slice(cb * CHUNK, (cb + 1) * CHUNK)
        for hh in range(hs):
            gcol_f, bcol_f, gcol_b, bcol_b = cols[hh]
            qf, kf, vf = heads_f[hh]
            qb, kb, vb = heads_b[hh]
            grow_f = ptf_ref[c, SM_A + hh:SM_A + hh + 1, :]
            grow_b = ptb_ref[cb, SM_A + hs + hh:SM_A + hs + hh + 1, :]
            items.append((qf[sl_f], kf[sl_f], vf[sl_f], ktf_ref[hh, :, sl_f], gcol_f[sl_f], grow_f, bcol_f[sl_f], True))
            items.append((qb[sl_b], kb[sl_b], vb[sl_b], ktb_ref[hh, :, sl_b], gcol_b[sl_b], grow_b, bcol_b[sl_b], False))
    local = _gdn_local(items, masks)
    states = [state_ref[hh, d] for hh in range(hs) for d in range(2)]
    for c in range(nc):
        cb = nc - 1 - c
        results = _gdn_on_state(local[c * 2 * hs:(c + 1) * 2 * hs], states)
        states = [r[1] for r in results]
        for hh in range(hs):
            hl = slice(hh * GDN_DV, (hh + 1) * GDN_DV)
            of_ref[c * CHUNK:(c + 1) * CHUNK, hl] = results[2 * hh][0].astype(of_ref.dtype)
            ob_ref[cb * CHUNK:(cb + 1) * CHUNK, hl] = results[2 * hh + 1][0].astype(ob_ref.dtype)
    for hh in range(hs):
        state_ref[hh, 0] = states[2 * hh]
        state_ref[hh, 1] = states[2 * hh + 1]

    @pl.when(step == n_steps - 1)
    def _():
        sout_ref[...] = state_ref[...]


def _gdn_scan(qkv, kt, p, pt, s0, nc):
    bsz, length, _ = qkv.shape
    tb = nc * CHUNK
    n_steps = length // tb
    qoff, koff, voff = 0, 1, 2
    fwd = lambda off: pl.BlockSpec((None, tb, GDN_QK), lambda b, s: (b, s, off))
    bwd = lambda off: pl.BlockSpec((None, tb, GDN_QK), lambda b, s: (b, n_steps - 1 - s, off))
    state_spec = pl.BlockSpec((None, GDN_HEADS, 2, GDN_DK, GDN_DV), lambda b, s: (b, 0, 0, 0, 0))
    return pl.pallas_call(
        functools.partial(_gdn_kernel, n_steps=n_steps, nc=nc),
        out_shape=(jax.ShapeDtypeStruct((bsz, length, GDN_V), SCAN_OUT_DTYPE),
                   jax.ShapeDtypeStruct((bsz, length, GDN_V), SCAN_OUT_DTYPE),
                   jax.ShapeDtypeStruct((bsz, GDN_HEADS, 2, GDN_DK, GDN_DV), F32)),
        grid=(bsz, n_steps),
        in_specs=[fwd(qoff), fwd(koff), fwd(voff), pl.BlockSpec((None, GDN_HEADS, GDN_DK, tb), lambda b, s: (b, 0, 0, s)),
                  bwd(qoff), bwd(koff), bwd(voff),
                  pl.BlockSpec((None, GDN_HEADS, GDN_DK, tb), lambda b, s: (b, 0, 0, n_steps - 1 - s)),
                  pl.BlockSpec((None, tb, LANES), lambda b, s: (b, s, 0)),
                  pl.BlockSpec((None, tb, LANES), lambda b, s: (b, n_steps - 1 - s, 0)),
                  pl.BlockSpec((None, nc, 4 * SUBLANES, CHUNK), lambda b, s: (b, s, 0, 0)),
                  pl.BlockSpec((None, nc, 4 * SUBLANES, CHUNK), lambda b, s: (b, n_steps - 1 - s, 0, 0)),
                  state_spec],
        out_specs=(pl.BlockSpec((None, tb, GDN_V), lambda b, s: (b, s, 0)),
                   pl.BlockSpec((None, tb, GDN_V), lambda b, s: (b, n_steps - 1 - s, 0)),
                   state_spec),
        scratch_shapes=[pltpu.VMEM((GDN_HEADS, 2, GDN_DK, GDN_DV), F32)],
        compiler_params=_params(2),
        name="gdn_scan",
    )(qkv, qkv, qkv, kt, qkv, qkv, qkv, kt, p, p, pt, pt, s0)


def _gdn_branch(qkv, kt, small, qkv_c, kt_c, small_c, a_log, dt_bias):
    bsz = qkv.shape[0]
    s0 = jnp.zeros((bsz, GDN_HEADS, 2, GDN_DK, GDN_DV), F32)

    def rows_t(p):
        chunks = p[:, :, :4 * SUBLANES].reshape(bsz, p.shape[1] // CHUNK, CHUNK, 4 * SUBLANES)
        return jnp.swapaxes(chunks, 2, 3)

    ctx_len = qkv_c.shape[1]
    p_c = _gdn_prep(small_c, a_log, dt_bias, ctx_len)
    oc_f, oc_b, s_ctx = _gdn_scan(qkv_c, kt_c, p_c, rows_t(p_c), s0, min(GDN_CHUNKS_PER_STEP, ctx_len // CHUNK))
    p = _gdn_prep(small, a_log, dt_bias, GDN_PREP_ROWS)
    o_f, o_b, _ = _gdn_scan(qkv, kt, p, rows_t(p), s_ctx, GDN_CHUNKS_PER_STEP)
    return (o_f, o_b), (oc_f, oc_b)


SSD_QUANTITIES = 4
SSD_HEADS_PER_STEP = SSD_GROUPS_PER_STEP * SSM_HPG
SSD_STEP_WIDTH = SSD_HEADS_PER_STEP * SSM_HEAD_DIM
BF16_PIECES = 3


def _ssd_prep_kernel(s_ref, alog_ref, dtb_ref, o_ref, *, rows):
    lane = lax.broadcasted_iota(jnp.int32, (1, LANES), 1)
    fwd_lane = lane < SM_DT_B
    neg_a = -jnp.exp(alog_ref[...])
    for seg in range(o_ref.shape[0]):
        dt = _softplus(s_ref[:, seg * LANES:(seg + 1) * LANES] + dtb_ref[...])
        la = dt * neg_a
        o_ref[seg, :, 0:LANES] = dt
        for c in range(rows // CHUNK):
            sl = slice(c * CHUNK, (c + 1) * CHUNK)
            fwd, bwd = _chunk_cumsums(la[sl, :])
            gc = jnp.where(fwd_lane, fwd, bwd)
            g_last = jnp.where(fwd_lane, gc[CHUNK - 1:CHUNK, :], gc[0:1, :])
            o_ref[seg, sl, LANES:2 * LANES] = gc
            o_ref[seg, sl, 2 * LANES:3 * LANES] = jnp.exp(gc)
            o_ref[seg, sl, 3 * LANES:4 * LANES] = dt[sl, :] * jnp.exp(g_last - gc)


def _ssd_prep(small_cols, a_log, dt_bias, n_seg):
    bsz, rows, _ = small_cols.shape
    sps = min(n_seg, SUBLANES)
    pad = lambda t: jnp.pad(t.reshape(1, -1).astype(F32), ((0, 0), (SM_DT, LANES - SM_DT - 2 * SSM_HEADS)))
    return pl.pallas_call(
        functools.partial(_ssd_prep_kernel, rows=rows),
        out_shape=jax.ShapeDtypeStruct((bsz, n_seg, rows, SSD_QUANTITIES * LANES), F32),
        grid=(bsz, n_seg // sps),
        in_specs=[pl.BlockSpec((None, rows, sps * LANES), lambda b, c: (b, 0, c)),
                  pl.BlockSpec((1, LANES), lambda b, c: (0, 0)),
                  pl.BlockSpec((1, LANES), lambda b, c: (0, 0))],
        out_specs=pl.BlockSpec((None, sps, rows, SSD_QUANTITIES * LANES), lambda b, c: (b, c, 0, 0)),
        compiler_params=_params(2),
        name="ssd_prep",
    )(small_cols, pad(a_log), pad(dt_bias))


def _ssd_layouts(q):
    bsz, n_seg, rows, _ = q.shape
    steps = SSM_GROUPS // SSD_GROUPS_PER_STEP
    q = q.reshape(bsz, n_seg, rows, SSD_QUANTITIES, LANES)[..., SM_DT:SM_DT + 2 * SSM_HEADS]
    q = q.reshape(bsz, n_seg, rows, SSD_QUANTITIES, 2, steps, SSD_HEADS_PER_STEP)
    out = []
    for d in range(2):
        qd = q[:, :, :, :, d]
        cols = jnp.transpose(qd, (0, 1, 4, 2, 3, 5)).reshape(bsz, n_seg, steps, rows, SSD_QUANTITIES * SSD_HEADS_PER_STEP)
        as_rows = lambda t: jnp.transpose(
            t.reshape(bsz, n_seg, rows // CHUNK, CHUNK, steps, SSD_HEADS_PER_STEP),
            (0, 1, 4, 2, 5, 3)).reshape(bsz, n_seg, steps, rows // CHUNK, SSD_STEP_WIDTH)
        out.append((cols, jnp.concatenate([as_rows(qd[:, :, :, 1]), as_rows(qd[:, :, :, 0])], axis=-1)))
    return out


def _ssd_expand_matrix():
    k = SSD_QUANTITIES * SSD_HEADS_PER_STEP
    src = jnp.arange(k)
    w = SSD_STEP_WIDTH
    dst = jnp.concatenate([jnp.arange(w, 2 * w), jnp.arange(3 * w, 4 * w)]) // SSM_HEAD_DIM
    one = (src[:, None] == dst[None, :]).astype(BF16)
    return jnp.concatenate([one] * BF16_PIECES, axis=0)


def _ssd_chunks(items, incl, block_diag):
    n = len(items)
    cb = [_mm_nt(it[4], jnp.concatenate([it[3].astype(BF16)] * SSM_HPG, axis=0)) for it in items]
    inter = [_mm(it[4], it[9]) for it in items]
    lhs = []
    for i, it in enumerate(items):
        m = incl[it[10]]
        lhs.append(jnp.where(m, cb[i] * jnp.exp(jnp.where(m, it[5] - it[6], 0.0)) * it[1], 0.0))
    rhs = [jnp.where(block_diag, jnp.concatenate([it[0].astype(BF16)] * SSM_HPG, axis=0), 0.0) for it in items]
    intra = [_mm(lhs[i], rhs[i]) for i in range(n)]
    upd = [_mm_tn(it[3], it[2]) for it in items]
    out = []
    for i, it in enumerate(items):
        y = intra[i] + inter[i] * it[7]
        if it[11] is not None:
            y = y + it[11] * it[0]
        out.append((y, it[8] * it[9] + upd[i]))
    return out


def _ssd_kernel(xf_ref, bf_ref, cf_ref, xb_ref, bb_ref, cb_ref, colf_ref, colb_ref, growf_ref, growb_ref,
                e_ref, d_ref, s0_ref, yf_ref, yb_ref, sout_ref, state_ref, *, rows, n_seg, gs):
    step = pl.program_id(2)

    @pl.when(step == 0)
    def _():
        state_ref[...] = s0_ref[...]

    gw = SSM_HPG * SSM_HEAD_DIM
    width = gs * gw

    def expand(col_ref):
        c = col_ref[...]
        hi = c.astype(BF16)
        r1 = c - hi.astype(F32)
        mid = r1.astype(BF16)
        lo = (r1 - mid.astype(F32)).astype(BF16)
        ex = jnp.dot(jnp.concatenate([hi, mid, lo], axis=1), e_ref[...], preferred_element_type=F32)
        return ex[:, :width], ex[:, width:]

    xf, xb = xf_ref[...], xb_ref[...]
    gcx_f, q4x_f = expand(colf_ref)
    gcx_b, q4x_b = expand(colb_ref)
    egx_f, egx_b = jnp.exp(gcx_f), jnp.exp(gcx_b)
    xdec_f, xdec_b = xf * q4x_f, xb * q4x_b
    bmf, cmf, bmb, cmb = bf_ref[...], cf_ref[...], bb_ref[...], cb_ref[...]
    d_all = d_ref[...]

    row = lax.broadcasted_iota(jnp.int32, (CHUNK, gw), 0)
    tok = lax.broadcasted_iota(jnp.int32, (CHUNK, gw), 1) & (CHUNK - 1)
    incl = {True: row >= tok, False: row <= tok}
    block_diag = (lax.shift_right_logical(lax.broadcasted_iota(jnp.int32, (gw, gw), 0), CHUNK.bit_length() - 1)
                  == lax.shift_right_logical(lax.broadcasted_iota(jnp.int32, (gw, gw), 1), SSM_HEAD_DIM.bit_length() - 1))

    n_chunks = rows // CHUNK
    states = [[state_ref[gg, 0], state_ref[gg, 1]] for gg in range(gs)]
    for c in range(n_chunks):
        cb = n_chunks - 1 - c
        sl_f = slice(c * CHUNK, (c + 1) * CHUNK)
        sl_b = slice(cb * CHUNK, (cb + 1) * CHUNK)
        last_f = slice(c * CHUNK + CHUNK - 1, (c + 1) * CHUNK)
        last_b = slice(cb * CHUNK, cb * CHUNK + 1)
        items = []
        for gg in range(gs):
            xl = slice(gg * gw, (gg + 1) * gw)
            nl = slice(gg * SSM_STATE, (gg + 1) * SSM_STATE)
            dl = slice(width + gg * gw, width + (gg + 1) * gw)
            items.append((xf[sl_f, xl], growf_ref[c:c + 1, dl], xdec_f[sl_f, xl], bmf[sl_f, nl], cmf[sl_f, nl],
                          gcx_f[sl_f, xl], growf_ref[c:c + 1, xl], egx_f[sl_f, xl], egx_f[last_f, xl], states[gg][0], True,
                          d_all[:, xl]))
            items.append((xb[sl_b, xl], growb_ref[cb:cb + 1, dl], xdec_b[sl_b, xl], bmb[sl_b, nl], cmb[sl_b, nl],
                          gcx_b[sl_b, xl], growb_ref[cb:cb + 1, xl], egx_b[sl_b, xl], egx_b[last_b, xl], states[gg][1], False,
                          None))
        results = _ssd_chunks(items, incl, block_diag)
        for gg in range(gs):
            xl = slice(gg * gw, (gg + 1) * gw)
            yf_ref[sl_f, xl] = results[2 * gg][0].astype(yf_ref.dtype)
            yb_ref[sl_b, xl] = results[2 * gg + 1][0].astype(yb_ref.dtype)
            states[gg] = [results[2 * gg][1], results[2 * gg + 1][1]]
    for gg in range(gs):
        state_ref[gg, 0] = states[gg][0]
        state_ref[gg, 1] = states[gg][1]

    @pl.when(step == n_seg - 1)
    def _():
        sout_ref[...] = state_ref[...]


def _ssd_scan(xbc, layouts, expand, d_exp, s0):
    bsz, n_seg, rows, _ = xbc.shape
    gs = SSD_GROUPS_PER_STEP
    xw, nw = SSD_STEP_WIDTH, gs * SSM_STATE
    boff, coff = SSM_INNER // nw, (SSM_INNER + SSM_GN) // nw
    (cols_f, grow_f), (cols_b, grow_b) = layouts
    n_cols = cols_f.shape[-1]
    seg_f = lambda s: s
    seg_b = lambda s: n_seg - 1 - s
    xspec = lambda seg: pl.BlockSpec((None, None, rows, xw), lambda b, g, s: (b, seg(s), 0, g))
    nspec = lambda seg, off: pl.BlockSpec((None, None, rows, nw), lambda b, g, s: (b, seg(s), 0, off + g))
    cspec = lambda seg: pl.BlockSpec((None, None, None, rows, n_cols), lambda b, g, s: (b, seg(s), g, 0, 0))
    rspec = lambda seg: pl.BlockSpec((None, None, None, rows // CHUNK, 2 * xw), lambda b, g, s: (b, seg(s), g, 0, 0))
    state_spec = pl.BlockSpec((None, gs, 2, SSM_STATE, SSM_HPG * SSM_HEAD_DIM), lambda b, g, s: (b, g, 0, 0, 0))
    y_shape = jax.ShapeDtypeStruct((bsz, n_seg, rows, SSM_INNER), SCAN_OUT_DTYPE)
    return pl.pallas_call(
        functools.partial(_ssd_kernel, rows=rows, n_seg=n_seg, gs=gs),
        out_shape=(y_shape, y_shape,
                   jax.ShapeDtypeStruct((bsz, SSM_GROUPS, 2, SSM_STATE, SSM_HPG * SSM_HEAD_DIM), F32)),
        grid=(bsz, SSM_GROUPS // gs, n_seg),
        in_specs=[xspec(seg_f), nspec(seg_f, boff), nspec(seg_f, coff),
                  xspec(seg_b), nspec(seg_b, boff), nspec(seg_b, coff),
                  cspec(seg_f), cspec(seg_b), rspec(seg_f), rspec(seg_b),
                  pl.BlockSpec(expand.shape, lambda b, g, s: (0, 0)),
                  pl.BlockSpec((1, xw), lambda b, g, s: (0, g)), state_spec],
        out_specs=(xspec(seg_f), xspec(seg_b), state_spec),
        scratch_shapes=[pltpu.VMEM((gs, 2, SSM_STATE, SSM_HPG * SSM_HEAD_DIM), F32)],
        compiler_params=_params(3),
        name="ssd_scan",
    )(xbc, xbc, xbc, xbc, xbc, xbc, cols_f, cols_b, grow_f, grow_b, expand, d_exp, s0)


def _ssd_branch(xbc, small, xbc_c, small_c, a_log, dt_bias, d_skip):
    bsz, n_seg, rows, _ = xbc.shape
    assert rows % CHUNK == 0 and xbc_c.shape[2] % CHUNK == 0
    d_exp = jnp.repeat(d_skip.astype(F32), SSM_HEAD_DIM).reshape(1, SSM_INNER)
    s0 = jnp.zeros((bsz, SSM_GROUPS, 2, SSM_STATE, SSM_HPG * SSM_HEAD_DIM), F32)
    expand = _ssd_expand_matrix()
    yc_f, yc_b, s_ctx = _ssd_scan(xbc_c, _ssd_layouts(_ssd_prep(small_c, a_log, dt_bias, 1)), expand, d_exp, s0)
    q = _ssd_prep(small.reshape(bsz, rows, n_seg * SMALL_N), a_log, dt_bias, n_seg)
    steps, step_rows = n_seg // SSD_COLUMNS_PER_STEP, SSD_COLUMNS_PER_STEP * rows
    merged = lambda t: t.reshape(bsz, steps, step_rows, t.shape[-1])
    y_f, y_b, _ = _ssd_scan(merged(xbc), _ssd_layouts(merged(q)), expand, d_exp, s_ctx)
    to_raster = lambda y: jnp.swapaxes(y.reshape(bsz, n_seg, rows, SSM_INNER), 1, 2).reshape(bsz, rows * n_seg, SSM_INNER)
    return (to_raster(y_f), to_raster(y_b)), (yc_f[:, 0], yc_b[:, 0])


def _layer_norm(r, g, b):
    mu = jnp.mean(r, axis=1, keepdims=True)
    var = jnp.mean(jnp.square(r - mu), axis=1, keepdims=True)
    return (r - mu) * lax.rsqrt(var + LN_EPS) * g + b


def _merge_kernel(x_ref, sc_ref, sh_ref, wg_ref, of_ref, ob_ref, yf_ref, yb_ref,
                  nwa_ref, nwb_ref, wpg_ref, wps_ref, wout_ref, g1_ref, lng_ref, lnb_ref, o_ref, *, alpha):
    x = x_ref[...]
    h = (x * (1.0 + sc_ref[...]) + sh_ref[...]).astype(BF16)
    gate = lambda off, width: jnp.dot(h, wg_ref[:, off:off + width], preferred_element_type=F32)
    o = of_ref[...].astype(F32) + ob_ref[...].astype(F32)
    normed = []
    for hh in range(GDN_HEADS):
        oh = o[:, hh * GDN_DV:(hh + 1) * GDN_DV]
        normed.append(oh * lax.rsqrt(jnp.mean(oh * oh, axis=1, keepdims=True) + NORM_EPS))
    y_a = jnp.concatenate(normed, axis=1) * nwa_ref[...] * _silu(gate(GOUT_OFF, GDN_V))
    t = (yf_ref[...].astype(F32) + yb_ref[...].astype(F32)) * _silu(gate(Z_OFF, SSM_INNER))
    gw = SSM_INNER // SSM_GROUPS
    normed = []
    for g in range(SSM_GROUPS):
        tg = t[:, g * gw:(g + 1) * gw]
        normed.append(tg * lax.rsqrt(jnp.mean(tg * tg, axis=1, keepdims=True) + NORM_EPS))
    y_b = jnp.concatenate(normed, axis=1) * nwb_ref[...]
    d = x.shape[1]
    mix = (jax.nn.sigmoid(gate(GA_OFF, d)) * jnp.dot(y_a.astype(BF16), wpg_ref[...], preferred_element_type=F32)
           + jax.nn.sigmoid(gate(GB_OFF, d)) * jnp.dot(y_b.astype(BF16), wps_ref[...], preferred_element_type=F32))
    out = jnp.dot(mix.astype(BF16), wout_ref[...], preferred_element_type=F32)
    o_ref[...] = _layer_norm(alpha * x + g1_ref[...] * out, lng_ref[...], lnb_ref[...])


def _merge(x, sc, sh, w_gates, o_pair, y_pair, nwa, nwb, wpg, wps, wout, g1, ln_g, ln_b, alpha, tm):
    bsz, length, d = x.shape
    row = lambda width: pl.BlockSpec((None, tm, width), lambda b, i: (b, i, 0))
    const = lambda shape: pl.BlockSpec(shape, lambda b, i: (0,) * len(shape), pipeline_mode=pl.Buffered(1))
    mod = pl.BlockSpec((None, 1, d), lambda b, i: (b, 0, 0))
    return pl.pallas_call(
        functools.partial(_merge_kernel, alpha=alpha),
        out_shape=jax.ShapeDtypeStruct((bsz, length, d), F32),
        grid=(bsz, length // tm),
        in_specs=[row(d), mod, mod, const((d, BIG_N)),
                  row(GDN_V), row(GDN_V), row(SSM_INNER), row(SSM_INNER),
                  const((1, GDN_V)), const((1, SSM_INNER)), const((GDN_V, d)), const((SSM_INNER, d)), const((d, d)),
                  mod, const((1, d)), const((1, d))],
        out_specs=row(d),
        compiler_params=_params(2),
        name="merge",
    )(x, sc, sh, w_gates, o_pair[0], o_pair[1], y_pair[0], y_pair[1], nwa, nwb, wpg, wps, wout, g1, ln_g, ln_b)


def _mlp_kernel(x_ref, sc_ref, sh_ref, g2_ref, w1_ref, b1_ref, w2_ref, b2_ref, lng_ref, lnb_ref, o_ref, *, alpha):
    x = x_ref[...]
    h = (x * (1.0 + sc_ref[...]) + sh_ref[...]).astype(BF16)
    acc = None
    tf = D_MODEL
    for c in range(D_FF // tf):
        u = jnp.dot(h, w1_ref[:, c * tf:(c + 1) * tf], preferred_element_type=F32) + b1_ref[:, c * tf:(c + 1) * tf]
        u = jnp.square(jnp.maximum(u, 0.0))
        part = jnp.dot(u.astype(BF16), w2_ref[c * tf:(c + 1) * tf, :], preferred_element_type=F32)
        acc = part if acc is None else acc + part
    f = acc + b2_ref[...]
    o_ref[...] = _layer_norm(alpha * x + g2_ref[...] * f, lng_ref[...], lnb_ref[...])


def _mlp(x, sc, sh, g2, w1, b1, w2, b2, ln_g, ln_b, alpha, tm):
    bsz, length, d = x.shape
    const = lambda shape: pl.BlockSpec(shape, lambda b, i: (0,) * len(shape), pipeline_mode=pl.Buffered(1))
    mod = pl.BlockSpec((None, 1, d), lambda b, i: (b, 0, 0))
    return pl.pallas_call(
        functools.partial(_mlp_kernel, alpha=alpha),
        out_shape=jax.ShapeDtypeStruct((bsz, length, d), F32),
        grid=(bsz, length // tm),
        in_specs=[pl.BlockSpec((None, tm, d), lambda b, i: (b, i, 0)), mod, mod, mod,
                  const((d, D_FF)), const((1, D_FF)), const((D_FF, d)), const((1, d)), const((1, d)), const((1, d))],
        out_specs=pl.BlockSpec((None, tm, d), lambda b, i: (b, i, 0)),
        compiler_params=_params(2),
        name="mlp",
    )(x, sc, sh, g2, w1, b1, w2, b2, ln_g, ln_b)


def _split_w_in(w_in):
    pts, acc = [], 0
    for s in IN_SPLITS[:-1]:
        acc += s
        pts.append(acc)
    qkv, gout, a_raw, b_raw, z, xbc, dt_raw, gate_a, gate_b = jnp.split(w_in, pts, axis=1)
    big = jnp.concatenate([z, gout, gate_a, gate_b], axis=1).astype(BF16)
    pad = jnp.zeros((w_in.shape[0], SMALL_N - SM_DT - 2 * SSM_HEADS), w_in.dtype)
    small = jnp.concatenate([a_raw, b_raw, dt_raw, pad], axis=1).astype(BF16)
    return big, qkv.astype(BF16), xbc.astype(BF16), small


def kernel(x, c, ctx, c_ctx, w_mod, b_mod, w_in, gdn_conv_w, gdn_A_log, gdn_dt_bias, gdn_norm_w,
           ssm_conv_w, ssm_conv_b, ssm_A_log, ssm_dt_bias, ssm_D, ssm_norm_w,
           w_proj_gdn, w_proj_ssm, w_out, ln1_g, ln1_b, w_ff1, b_ff1, w_ff2, b_ff2, ln2_g, ln2_b):
    bsz, length, d = x.shape
    ctx_len = ctx.shape[1]
    depth = w_mod.shape[0]
    alpha = float((2 * depth) ** 0.25)
    mod_rows = -(-(bsz + 1) // (2 * SUBLANES)) * (2 * SUBLANES)
    cc = jnp.concatenate([c, c_ctx[None, :], jnp.zeros((mod_rows - bsz - 1, d), c.dtype)], axis=0)
    row2 = lambda t: t.reshape(1, -1)
    tm_lat = PROJ_ROWS
    tm_ctx = ctx_len

    for l in range(depth):
        last = l == depth - 1
        mod = _modulation(cc, w_mod[l], b_mod[l])
        lat = [mod[:bsz, i * d:(i + 1) * d].reshape(bsz, 1, d) for i in range(6)]
        cxm = [jnp.broadcast_to(mod[bsz, i * d:(i + 1) * d].reshape(1, 1, d), (bsz, 1, d)) for i in range(6)]
        w_big, w_qkv, w_xbc, w_small = _split_w_in(w_in[l])
        rows = length // GRID_W

        qkv, kt, small = _proj_conv(x, lat[1], lat[0], w_qkv, gdn_conv_w[l], None, GRID_W, False, 2, tm_lat, GDN_QK,
                                    "inproj_qkv", w_small)
        xbc = _proj_conv(x, lat[1], lat[0], w_xbc, ssm_conv_w[l], ssm_conv_b[l], rows, True, 0, None, PROJ_TILE_N,
                         "inproj_xbc")
        qkv_c, kt_c, small_c = _proj_conv(ctx, cxm[1], cxm[0], w_qkv, gdn_conv_w[l], None, ctx_len, False, 2, tm_ctx,
                                          GDN_QK, "inproj_qkv_ctx", w_small)
        xbc_c = _proj_conv(ctx, cxm[1], cxm[0], w_xbc, ssm_conv_w[l], ssm_conv_b[l], ctx_len, False, 0, tm_ctx, PROJ_TILE_N,
                           "inproj_xbc_ctx")[:, None]

        o_pair, oc_pair = _gdn_branch(qkv, kt, small, qkv_c, kt_c, small_c, gdn_A_log[l], gdn_dt_bias[l])
        y_pair, yc_pair = _ssd_branch(xbc, small, xbc_c, small_c, ssm_A_log[l], ssm_dt_bias[l], ssm_D[l])

        nwa = jnp.tile(gdn_norm_w[l], GDN_HEADS).reshape(1, GDN_V)
        nwb = row2(ssm_norm_w[l])
        wpg, wps, wo = w_proj_gdn[l].astype(BF16), w_proj_ssm[l].astype(BF16), w_out[l].astype(BF16)
        w1, w2 = w_ff1[l].astype(BF16), w_ff2[l].astype(BF16)
        merge_args = (nwa, nwb, wpg, wps, wo)
        ln1 = (row2(ln1_g[l]), row2(ln1_b[l]))
        mlp_w = (w1, row2(b_ff1[l]), w2, row2(b_ff2[l]), row2(ln2_g[l]), row2(ln2_b[l]))

        x1 = _merge(x, lat[1], lat[0], w_big, o_pair, y_pair, *merge_args, lat[2], *ln1, alpha, MERGE_ROWS)
        x = _mlp(x1, lat[4], lat[3], lat[5], *mlp_w, alpha, MLP_ROWS)
        if not last:
            c1 = _merge(ctx, cxm[1], cxm[0], w_big, oc_pair, yc_pair, *merge_args, cxm[2], *ln1, alpha, ctx_len)
            ctx = _mlp(c1, cxm[4], cxm[3], cxm[5], *mlp_w, alpha, ctx_len)
    return x
```

```python
import functools

import jax
import jax.numpy as jnp
from jax import lax
from jax.experimental import pallas as pl
from jax.experimental.pallas import tpu as pltpu

F32 = jnp.float32
BF16 = jnp.bfloat16

D_MODEL = 1024
GRID_W = 64
GDN_HEADS = 8
GDN_DK = 128
GDN_DV = 128
GDN_QK = GDN_HEADS * GDN_DK
GDN_V = GDN_HEADS * GDN_DV
GDN_QKV = 2 * GDN_QK + GDN_V
SSM_INNER = 2 * D_MODEL
SSM_HEAD_DIM = 64
SSM_HEADS = SSM_INNER // SSM_HEAD_DIM
SSM_GROUPS = 8
SSM_HPG = SSM_HEADS // SSM_GROUPS
SSM_STATE = 128
SSM_GN = SSM_GROUPS * SSM_STATE
SSM_XBC = SSM_INNER + 2 * SSM_GN
CONV_K = 5
CHUNK = 64
D_FF = 4 * D_MODEL
LN_EPS = 1e-5
NORM_EPS = 1e-6
IN_SPLITS = (GDN_QKV, GDN_V, 2 * GDN_HEADS, 2 * GDN_HEADS, SSM_INNER, SSM_XBC, 2 * SSM_HEADS, D_MODEL, D_MODEL)

LANES = 128
SUBLANES = 8
VMEM_LIMIT_BYTES = 56 * 1024 * 1024

Z_OFF = 0
GOUT_OFF = Z_OFF + SSM_INNER
GA_OFF = GOUT_OFF + GDN_V
GB_OFF = GA_OFF + D_MODEL
BIG_N = GB_OFF + D_MODEL
SM_A = 0
SM_B = SM_A + 2 * GDN_HEADS
SM_DT = SM_B + 2 * GDN_HEADS
SM_DT_B = SM_DT + SSM_HEADS
SMALL_N = LANES

GDN_CHUNKS_PER_STEP = 4
SCAN_OUT_DTYPE = BF16
MERGE_ROWS = 512
MLP_ROWS = 1024
PROJ_ROWS = 1024
PROJ_TILE_N = 1024
MOD_TILE_N = 1536
SSD_GROUPS_PER_STEP = 2
SSD_COLUMNS_PER_STEP = 8


def _params(n_axes):
    return pltpu.CompilerParams(dimension_semantics=("arbitrary",) * n_axes, vmem_limit_bytes=VMEM_LIMIT_BYTES)


def _silu(t):
    return t * jax.nn.sigmoid(t)


def _softplus(t):
    return jnp.maximum(t, 0.0) + jnp.log(1.0 + jnp.exp(-jnp.abs(t)))


def _mm(a, b):
    return jnp.dot(a.astype(BF16), b.astype(BF16), preferred_element_type=F32)


def _mm_nt(a, b):
    return lax.dot_general(a.astype(BF16), b.astype(BF16), (((1,), (1,)), ((), ())), preferred_element_type=F32)


def _mm_tn(a, b):
    return lax.dot_general(a.astype(BF16), b.astype(BF16), (((0,), (0,)), ((), ())), preferred_element_type=F32)


def _tri_masks(lower):
    ii = lax.broadcasted_iota(jnp.int32, (CHUNK, CHUNK), 0)
    jj = lax.broadcasted_iota(jnp.int32, (CHUNK, CHUNK), 1)
    if lower:
        return ii >= jj, ii > jj
    return ii <= jj, ii < jj


def _conv_seg(x, w, seg):
    n = x.shape[0]
    assert seg & (seg - 1) == 0 and n % seg == 0
    pos = lax.broadcasted_iota(jnp.int32, (n, 1), 0) & (seg - 1)
    out = None
    for j in range(CONV_K):
        d = j - CONV_K // 2
        if d == 0:
            term = x * w[j:j + 1, :]
        else:
            shifted = pltpu.roll(x, shift=(-d) % n, axis=0)
            valid = (pos + d >= 0) & (pos + d < seg)
            term = jnp.where(valid, shifted, 0.0) * w[j:j + 1, :]
        out = term if out is None else out + term
    return out


def _mod_kernel(c_ref, w_ref, b_ref, o_ref):
    o_ref[...] = _mm(_silu(c_ref[...]), w_ref[...]) + b_ref[...]


def _modulation(cc, w, b):
    rows, d = cc.shape
    n = w.shape[1]
    tn = MOD_TILE_N
    return pl.pallas_call(
        _mod_kernel,
        out_shape=jax.ShapeDtypeStruct((rows, n), F32),
        grid=(n // tn,),
        in_specs=[pl.BlockSpec((rows, d), lambda j: (0, 0)),
                  pl.BlockSpec((d, tn), lambda j: (0, j)),
                  pl.BlockSpec((1, tn), lambda j: (0, j))],
        out_specs=pl.BlockSpec((rows, tn), lambda j: (0, j)),
        compiler_params=_params(1),
        name="modulation",
    )(cc, w, b.reshape(1, n))


def _proj_conv_kernel(*refs, seg, cols, has_bias, norm_tiles):
    x_ref, sc_ref, sh_ref, w_ref, cw_ref = refs[:5]
    k = 5
    cb_ref = refs[k] if has_bias else None
    k += has_bias
    perm_ref = refs[k] if cols else None
    k += cols
    wn_ref, alog_ref, dtb_ref = refs[k:k + 3] if norm_tiles else (None, None, None)
    k += 3 * bool(norm_tiles)
    o_ref = refs[k]
    kt_ref = refs[k + 1] if norm_tiles else None
    narrow_ref, p_ref = refs[k + 2:k + 4] if norm_tiles else (None, None)
    h_ref = refs[-1]
    j = pl.program_id(2)
    n_tok = h_ref.shape[0]

    @pl.when(j == 0)
    def _():
        x = x_ref[...]
        if cols:
            x = x.reshape(n_tok, x.shape[2])
        h = (x * (1.0 + sc_ref[...]) + sh_ref[...]).astype(BF16)
        if cols:
            h = jnp.dot(perm_ref[...], h, preferred_element_type=F32).astype(BF16)
        h_ref[...] = h
        if norm_tiles:
            narrow = jnp.dot(h, wn_ref[...], preferred_element_type=F32)
            narrow_ref[...] = narrow
            _gdn_prep_rows(narrow, alog_ref[...], dtb_ref[...], p_ref)

    y = jnp.dot(h_ref[...], w_ref[...], preferred_element_type=F32)
    y = _conv_seg(y, cw_ref[...], seg)
    if has_bias:
        y = y + cb_ref[...]
    y = _silu(y)

    def store(t):
        o_ref[...] = t.reshape(o_ref.shape)

    if norm_tiles:
        @pl.when(j < norm_tiles)
        def _():
            scale = jnp.where(j == 0, GDN_DK ** -0.5, 1.0)
            heads = []
            for hh in range(y.shape[1] // GDN_DK):
                yh = y[:, hh * GDN_DK:(hh + 1) * GDN_DK]
                heads.append(yh * (lax.rsqrt(jnp.sum(yh * yh, axis=1, keepdims=True) + NORM_EPS) * scale))
            store(jnp.concatenate(heads, axis=1))

            @pl.when(j == 1)
            def _():
                for hh, yh in enumerate(heads):
                    kt_ref[hh] = yh.T

        @pl.when(j >= norm_tiles)
        def _():
            store(y)
    else:
        store(y)


def _proj_conv(x, sc, sh, w, conv_w, conv_b, seg, cols, norm_tiles, tm, tn, name, w_narrow=None, gdn_decay=None):
    bsz, length, d = x.shape
    n = w.shape[1]
    has_bias = conv_b is not None
    if cols:
        rows, cps = length // GRID_W, SUBLANES
        assert seg == rows and seg & (seg - 1) == 0
        n_tok = rows * cps
        x_in = x.reshape(bsz, rows, GRID_W, d)
        x_spec = pl.BlockSpec((None, rows, cps, d), lambda b, i, j: (b, 0, i, 0))
        out_shape = jax.ShapeDtypeStruct((bsz, GRID_W, rows, n), F32)
        out_spec = pl.BlockSpec((None, cps, rows, tn), lambda b, i, j: (b, i, 0, j))
        grid = (bsz, GRID_W // cps, n // tn)
    else:
        n_tok = tm
        x_in = x
        x_spec = pl.BlockSpec((None, tm, d), lambda b, i, j: (b, i, 0))
        out_shape = jax.ShapeDtypeStruct((bsz, length, n), F32)
        out_spec = pl.BlockSpec((None, tm, tn), lambda b, i, j: (b, i, j))
        grid = (bsz, length // tm, n // tn)
    mod = pl.BlockSpec((None, 1, d), lambda b, i, j: (b, 0, 0))
    in_specs = [x_spec, mod, mod, pl.BlockSpec((d, tn), lambda b, i, j: (0, j)),
                pl.BlockSpec((CONV_K, tn), lambda b, i, j: (0, j))]
    args = [x_in, sc, sh, w, conv_w]
    if has_bias:
        in_specs.append(pl.BlockSpec((1, tn), lambda b, i, j: (0, j)))
        args.append(conv_b.reshape(1, n))
    if cols:
        dst = jnp.arange(n_tok)
        src = (dst % rows) * cps + dst // rows
        in_specs.append(pl.BlockSpec((n_tok, n_tok), lambda b, i, j: (0, 0), pipeline_mode=pl.Buffered(1)))
        args.append((src[:, None] == jnp.arange(n_tok)[None, :]).astype(BF16))
    if norm_tiles:
        assert norm_tiles == 2 and tn == GDN_QK and not cols
        in_specs.append(pl.BlockSpec((d, SMALL_N), lambda b, i, j: (0, 0), pipeline_mode=pl.Buffered(1)))
        args.append(w_narrow)
        lanes_of = lambda t: jnp.pad(t.reshape(1, -1).astype(F32), ((0, 0), (SM_A, LANES - SM_A - 2 * GDN_HEADS)))
        for t in gdn_decay:
            in_specs.append(pl.BlockSpec((1, LANES), lambda b, i, j: (0, 0)))
            args.append(lanes_of(t))
        out_shape = (out_shape, jax.ShapeDtypeStruct((bsz, GDN_HEADS, GDN_DK, length), F32),
                     jax.ShapeDtypeStruct((bsz, length, SMALL_N), F32), jax.ShapeDtypeStruct((bsz, length, LANES), F32))
        out_spec = (out_spec, pl.BlockSpec((None, GDN_HEADS, GDN_DK, tm), lambda b, i, j: (b, 0, 0, i)),
                    pl.BlockSpec((None, tm, SMALL_N), lambda b, i, j: (b, i, 0)),
                    pl.BlockSpec((None, tm, LANES), lambda b, i, j: (b, i, 0)))
    return pl.pallas_call(
        functools.partial(_proj_conv_kernel, seg=seg, cols=cols, has_bias=has_bias, norm_tiles=norm_tiles),
        out_shape=out_shape,
        grid=grid,
        in_specs=in_specs,
        out_specs=out_spec,
        scratch_shapes=[pltpu.VMEM((n_tok, d), BF16)],
        compiler_params=_params(3),
        name=name,
    )(*args)


def _chunk_cumsums(t):
    ii = lax.broadcasted_iota(jnp.int32, (2 * CHUNK, CHUNK), 0)
    jj = lax.broadcasted_iota(jnp.int32, (2 * CHUNK, CHUNK), 1)
    ones = ((ii < CHUNK) & (ii >= jj)) | ((ii >= CHUNK) & (ii - CHUNK <= jj))
    tri = jnp.where(ones, 1.0, 0.0).astype(BF16)
    hi = t.astype(BF16)
    r1 = t - hi.astype(F32)
    mid = r1.astype(BF16)
    lo = (r1 - mid.astype(F32)).astype(BF16)
    w = t.shape[1]
    sums = jnp.dot(tri, jnp.concatenate([hi, mid, lo], axis=1), preferred_element_type=F32)
    sums = sums[:, :w] + sums[:, w:2 * w] + sums[:, 2 * w:]
    return sums[:CHUNK], sums[CHUNK:]


def _gdn_prep_rows(s, alog, dtb, o_ref):
    lane = lax.broadcasted_iota(jnp.int32, (1, LANES), 1)
    g = -jnp.exp(alog) * _softplus(s + dtb)
    beta = jax.nn.sigmoid(s)
    for c in range(s.shape[0] // CHUNK):
        sl = slice(c * CHUNK, (c + 1) * CHUNK)
        fwd, bwd = _chunk_cumsums(g[sl, :])
        gc = jnp.where(lane < SM_A + GDN_HEADS, fwd, bwd)
        o_ref[sl, :] = jnp.where(lane < SM_B, gc, beta[sl, :])


def _tri_inverse(mats):
    ii = lax.broadcasted_iota(jnp.int32, (CHUNK, CHUNK), 0)
    jj = lax.broadcasted_iota(jnp.int32, (CHUNK, CHUNK), 1)
    eye = jnp.where(ii == jj, 1.0, 0.0)
    zero = jnp.zeros((CHUNK, CHUNK), F32)
    right = lax.broadcasted_iota(jnp.int32, (CHUNK, 2 * CHUNK), 1) >= CHUNK
    zs = [jnp.concatenate([zero, eye], axis=1) + _mm(a, jnp.concatenate([a, -eye], axis=1)) for a in mats]
    power = 2
    while power < CHUNK:
        ps = [_mm(z[:, :CHUNK], z) for z in zs]
        zs = [p + jnp.where(right, z, 0.0) for p, z in zip(ps, zs)]
        power *= 2
    return zs


def _gdn_local(items, masks):
    qs, ks, vs, kts, gcols, grows, bcols, lowers = zip(*items)
    n = len(items)
    incl = [masks[lo][0] for lo in lowers]
    strict = [masks[lo][1] for lo in lowers]
    glast = [grows[i][:, CHUNK - 1:CHUNK] if lowers[i] else grows[i][:, 0:1] for i in range(n)]
    kq = [_mm(jnp.concatenate([ks[i].astype(BF16), qs[i].astype(BF16)], axis=0), kts[i]) for i in range(n)]
    dec = [jnp.exp(jnp.where(incl[i], gcols[i] - grows[i], 0.0)) for i in range(n)]
    a_mats = [jnp.where(strict[i], bcols[i] * kq[i][:CHUNK] * dec[i], 0.0) for i in range(n)]
    t_inv = _tri_inverse(a_mats)
    eg = [jnp.exp(g) for g in gcols]
    rhs = [jnp.concatenate([(bcols[i] * eg[i]) * ks[i], bcols[i] * vs[i]], axis=1) for i in range(n)]
    pad = jnp.zeros((CHUNK, GDN_DK + GDN_DV), F32)
    wu = [_mm(t, jnp.concatenate([pad, r], axis=0)) for t, r in zip(t_inv, rhs)]
    lhs = [jnp.concatenate([wu[i][:, :GDN_DK].astype(BF16), (qs[i] * eg[i]).astype(BF16)], axis=0) for i in range(n)]
    qk_masked = [jnp.where(incl[i], kq[i][CHUNK:] * dec[i], 0.0).astype(BF16) for i in range(n)]
    kt_dec = [(kts[i] * jnp.exp(glast[i] - grows[i])).astype(BF16) for i in range(n)]
    return [(lhs[i], wu[i][:, GDN_DK:], qk_masked[i], kt_dec[i], jnp.exp(glast[i])) for i in range(n)]


def _gdn_on_state(local, states):
    n = len(local)
    on_state = [_mm(local[i][0], states[i]) for i in range(n)]
    v_new = [local[i][1] - on_state[i][:CHUNK] for i in range(n)]
    on_v = [_mm(jnp.concatenate([local[i][2], local[i][3]], axis=0), v_new[i]) for i in range(n)]
    return [(on_state[i][CHUNK:] + on_v[i][:CHUNK], local[i][4] * states[i] + on_v[i][CHUNK:]) for i in range(n)]


def _gdn_kernel(qf_ref, kf_ref, vf_ref, ktf_ref, qb_ref, kb_ref, vb_ref, ktb_ref, pf_ref, pb_ref, ptf_ref, ptb_ref,
                s0_ref, of_ref, ob_ref, sout_ref, state_ref, *, n_steps, nc):
    step = pl.program_id(1)
    hs = GDN_HEADS

    @pl.when(step == 0)
    def _():
        state_ref[...] = s0_ref[...]

    def heads_of(q_ref, k_ref, v_ref):
        q, k, v = q_ref[...], k_ref[...], v_ref[...]
        return [(q[:, hh * GDN_DK:(hh + 1) * GDN_DK], k[:, hh * GDN_DK:(hh + 1) * GDN_DK],
                 v[:, hh * GDN_DV:(hh + 1) * GDN_DV]) for hh in range(hs)]

    heads_f = heads_of(qf_ref, kf_ref, vf_ref)
    heads_b = heads_of(qb_ref, kb_ref, vb_ref)
    pf = pf_ref[...]
    pb = pb_ref[...]
    masks = {True: _tri_masks(True), False: _tri_masks(False)}
    col = lambda t, idx: t[:, idx:idx + 1]
    cols = [(col(pf, SM_A + hh), col(pf, SM_B + hh), col(pb, SM_A + hs + hh), col(pb, SM_B + hs + hh))
            for hh in range(hs)]
    items = []
    for c in range(nc):
        cb = nc - 1 - c
        sl_f = slice(c * CHUNK, (c + 1) * CHUNK)
        sl_b = slice(cb * CHUNK, (cb + 1) * CHUNK)
        for hh in range(hs):
            gcol_f, bcol_f, gcol_b, bcol_b = cols[hh]
            qf, kf, vf = heads_f[hh]
            qb, kb, vb = heads_b[hh]
            grow_f = ptf_ref[c, SM_A + hh:SM_A + hh + 1, :]
            grow_b = ptb_ref[cb, SM_A + hs + hh:SM_A + hs + hh + 1, :]
            items.append((qf[sl_f], kf[sl_f], vf[sl_f], ktf_ref[hh, :, sl_f], gcol_f[sl_f], grow_f, bcol_f[sl_f], True))
            items.append((qb[sl_b], kb[sl_b], vb[sl_b], ktb_ref[hh, :, sl_b], gcol_b[sl_b], grow_b, bcol_b[sl_b], False))
    local = _gdn_local(items, masks)
    states = [state_ref[hh, d] for hh in range(hs) for d in range(2)]
    for c in range(nc):
        cb = nc - 1 - c
        results = _gdn_on_state(local[c * 2 * hs:(c + 1) * 2 * hs], states)
        states = [r[1] for r in results]
        for hh in range(hs):
            hl = slice(hh * GDN_DV, (hh + 1) * GDN_DV)
            of_ref[c * CHUNK:(c + 1) * CHUNK, hl] = results[2 * hh][0].astype(of_ref.dtype)
            ob_ref[cb * CHUNK:(cb + 1) * CHUNK, hl] = results[2 * hh + 1][0].astype(ob_ref.dtype)
    for hh in range(hs):
        state_ref[hh, 0] = states[2 * hh]
        state_ref[hh, 1] = states[2 * hh + 1]

    @pl.when(step == n_steps - 1)
    def _():
        sout_ref[...] = state_ref[...]


def _gdn_scan(qkv, kt, p, pt, s0, nc):
    bsz, length, _ = qkv.shape
    tb = nc * CHUNK
    n_steps = length // tb
    qoff, koff, voff = 0, 1, 2
    fwd = lambda off: pl.BlockSpec((None, tb, GDN_QK), lambda b, s: (b, s, off))
    bwd = lambda off: pl.BlockSpec((None, tb, GDN_QK), lambda b, s: (b, n_steps - 1 - s, off))
    state_spec = pl.BlockSpec((None, GDN_HEADS, 2, GDN_DK, GDN_DV), lambda b, s: (b, 0, 0, 0, 0))
    return pl.pallas_call(
        functools.partial(_gdn_kernel, n_steps=n_steps, nc=nc),
        out_shape=(jax.ShapeDtypeStruct((bsz, length, GDN_V), SCAN_OUT_DTYPE),
                   jax.ShapeDtypeStruct((bsz, length, GDN_V), SCAN_OUT_DTYPE),
                   jax.ShapeDtypeStruct((bsz, GDN_HEADS, 2, GDN_DK, GDN_DV), F32)),
        grid=(bsz, n_steps),
        in_specs=[fwd(qoff), fwd(koff), fwd(voff), pl.BlockSpec((None, GDN_HEADS, GDN_DK, tb), lambda b, s: (b, 0, 0, s)),
                  bwd(qoff), bwd(koff), bwd(voff),
                  pl.BlockSpec((None, GDN_HEADS, GDN_DK, tb), lambda b, s: (b, 0, 0, n_steps - 1 - s)),
                  pl.BlockSpec((None, tb, LANES), lambda b, s: (b, s, 0)),
                  pl.BlockSpec((None, tb, LANES), lambda b, s: (b, n_steps - 1 - s, 0)),
                  pl.BlockSpec((None, nc, 4 * SUBLANES, CHUNK), lambda b, s: (b, s, 0, 0)),
                  pl.BlockSpec((None, nc, 4 * SUBLANES, CHUNK), lambda b, s: (b, n_steps - 1 - s, 0, 0)),
                  state_spec],
        out_specs=(pl.BlockSpec((None, tb, GDN_V), lambda b, s: (b, s, 0)),
                   pl.BlockSpec((None, tb, GDN_V), lambda b, s: (b, n_steps - 1 - s, 0)),
                   state_spec),
        scratch_shapes=[pltpu.VMEM((GDN_HEADS, 2, GDN_DK, GDN_DV), F32)],
        compiler_params=_params(2),
        name="gdn_scan",
    )(qkv, qkv, qkv, kt, qkv, qkv, qkv, kt, p, p, pt, pt, s0)


def _gdn_branch(qkv, kt, p, qkv_c, kt_c, p_c):
    bsz = qkv.shape[0]
    s0 = jnp.zeros((bsz, GDN_HEADS, 2, GDN_DK, GDN_DV), F32)

    def rows_t(p):
        chunks = p[:, :, :4 * SUBLANES].reshape(bsz, p.shape[1] // CHUNK, CHUNK, 4 * SUBLANES)
        return jnp.swapaxes(chunks, 2, 3)

    ctx_len = qkv_c.shape[1]
    oc_f, oc_b, s_ctx = _gdn_scan(qkv_c, kt_c, p_c, rows_t(p_c), s0, min(GDN_CHUNKS_PER_STEP, ctx_len // CHUNK))
    o_f, o_b, _ = _gdn_scan(qkv, kt, p, rows_t(p), s_ctx, GDN_CHUNKS_PER_STEP)
    return (o_f, o_b), (oc_f, oc_b)


SSD_QUANTITIES = 4
SSD_HEADS_PER_STEP = SSD_GROUPS_PER_STEP * SSM_HPG
SSD_STEP_WIDTH = SSD_HEADS_PER_STEP * SSM_HEAD_DIM
BF16_PIECES = 3


def _ssd_prep_kernel(s_ref, alog_ref, dtb_ref, o_ref, *, rows):
    lane = lax.broadcasted_iota(jnp.int32, (1, LANES), 1)
    fwd_lane = lane < SM_DT_B
    neg_a = -jnp.exp(alog_ref[...])
    for seg in range(o_ref.shape[0]):
        dt = _softplus(s_ref[:, seg * LANES:(seg + 1) * LANES] + dtb_ref[...])
        la = dt * neg_a
        o_ref[seg, :, 0:LANES] = dt
        for c in range(rows // CHUNK):
            sl = slice(c * CHUNK, (c + 1) * CHUNK)
            fwd, bwd = _chunk_cumsums(la[sl, :])
            gc = jnp.where(fwd_lane, fwd, bwd)
            g_last = jnp.where(fwd_lane, gc[CHUNK - 1:CHUNK, :], gc[0:1, :])
            o_ref[seg, sl, LANES:2 * LANES] = gc
            o_ref[seg, sl, 2 * LANES:3 * LANES] = jnp.exp(gc)
            o_ref[seg, sl, 3 * LANES:4 * LANES] = dt[sl, :] * jnp.exp(g_last - gc)


def _ssd_prep(small_cols, a_log, dt_bias, n_seg):
    bsz, rows, _ = small_cols.shape
    sps = min(n_seg, SUBLANES)
    pad = lambda t: jnp.pad(t.reshape(1, -1).astype(F32), ((0, 0), (SM_DT, LANES - SM_DT - 2 * SSM_HEADS)))
    return pl.pallas_call(
        functools.partial(_ssd_prep_kernel, rows=rows),
        out_shape=jax.ShapeDtypeStruct((bsz, n_seg, rows, SSD_QUANTITIES * LANES), F32),
        grid=(bsz, n_seg // sps),
        in_specs=[pl.BlockSpec((None, rows, sps * LANES), lambda b, c: (b, 0, c)),
                  pl.BlockSpec((1, LANES), lambda b, c: (0, 0)),
                  pl.BlockSpec((1, LANES), lambda b, c: (0, 0))],
        out_specs=pl.BlockSpec((None, sps, rows, SSD_QUANTITIES * LANES), lambda b, c: (b, c, 0, 0)),
        compiler_params=_params(2),
        name="ssd_prep",
    )(small_cols, pad(a_log), pad(dt_bias))


def _ssd_layouts(q):
    bsz, n_seg, rows, _ = q.shape
    steps = SSM_GROUPS // SSD_GROUPS_PER_STEP
    q = q.reshape(bsz, n_seg, rows, SSD_QUANTITIES, LANES)[..., SM_DT:SM_DT + 2 * SSM_HEADS]
    q = q.reshape(bsz, n_seg, rows, SSD_QUANTITIES, 2, steps, SSD_HEADS_PER_STEP)
    out = []
    for d in range(2):
        qd = q[:, :, :, :, d]
        cols = jnp.transpose(qd, (0, 1, 4, 2, 3, 5)).reshape(bsz, n_seg, steps, rows, SSD_QUANTITIES * SSD_HEADS_PER_STEP)
        as_rows = lambda t: jnp.transpose(
            t.reshape(bsz, n_seg, rows // CHUNK, CHUNK, steps, SSD_HEADS_PER_STEP),
            (0, 1, 4, 2, 5, 3)).reshape(bsz, n_seg, steps, rows // CHUNK, SSD_STEP_WIDTH)
        out.append((cols, jnp.concatenate([as_rows(qd[:, :, :, 1]), as_rows(qd[:, :, :, 0])], axis=-1)))
    return out


def _ssd_expand_matrix():
    k = SSD_QUANTITIES * SSD_HEADS_PER_STEP
    src = jnp.arange(k)
    w = SSD_STEP_WIDTH
    dst = jnp.concatenate([jnp.arange(w, 2 * w), jnp.arange(3 * w, 4 * w)]) // SSM_HEAD_DIM
    one = (src[:, None] == dst[None, :]).astype(BF16)
    return jnp.concatenate([one] * BF16_PIECES, axis=0)


def _ssd_chunks(items, incl, block_diag):
    n = len(items)
    cb = [_mm_nt(it[4], jnp.concatenate([it[3].astype(BF16)] * SSM_HPG, axis=0)) for it in items]
    inter = [_mm(it[4], it[9]) for it in items]
    lhs = []
    for i, it in enumerate(items):
        m = incl[it[10]]
        lhs.append(jnp.where(m, cb[i] * jnp.exp(jnp.where(m, it[5] - it[6], 0.0)) * it[1], 0.0))
    rhs = [jnp.where(block_diag, jnp.concatenate([it[0].astype(BF16)] * SSM_HPG, axis=0), 0.0) for it in items]
    intra = [_mm(lhs[i], rhs[i]) for i in range(n)]
    upd = [_mm_tn(it[3], it[2]) for it in items]
    out = []
    for i, it in enumerate(items):
        y = intra[i] + inter[i] * it[7]
        if it[11] is not None:
            y = y + it[11] * it[0]
        out.append((y, it[8] * it[9] + upd[i]))
    return out


def _ssd_kernel(xf_ref, bf_ref, cf_ref, xb_ref, bb_ref, cb_ref, colf_ref, colb_ref, growf_ref, growb_ref,
                e_ref, d_ref, s0_ref, yf_ref, yb_ref, sout_ref, state_ref, *, rows, n_seg, gs):
    step = pl.program_id(2)

    @pl.when(step == 0)
    def _():
        state_ref[...] = s0_ref[...]

    gw = SSM_HPG * SSM_HEAD_DIM
    width = gs * gw

    def expand(col_ref):
        c = col_ref[...]
        hi = c.astype(BF16)
        r1 = c - hi.astype(F32)
        mid = r1.astype(BF16)
        lo = (r1 - mid.astype(F32)).astype(BF16)
        ex = jnp.dot(jnp.concatenate([hi, mid, lo], axis=1), e_ref[...], preferred_element_type=F32)
        return ex[:, :width], ex[:, width:]

    xf, xb = xf_ref[...], xb_ref[...]
    gcx_f, q4x_f = expand(colf_ref)
    gcx_b, q4x_b = expand(colb_ref)
    egx_f, egx_b = jnp.exp(gcx_f), jnp.exp(gcx_b)
    xdec_f, xdec_b = xf * q4x_f, xb * q4x_b
    bmf, cmf, bmb, cmb = bf_ref[...], cf_ref[...], bb_ref[...], cb_ref[...]
    d_all = d_ref[...]

    row = lax.broadcasted_iota(jnp.int32, (CHUNK, gw), 0)
    tok = lax.broadcasted_iota(jnp.int32, (CHUNK, gw), 1) & (CHUNK - 1)
    incl = {True: row >= tok, False: row <= tok}
    block_diag = (lax.shift_right_logical(lax.broadcasted_iota(jnp.int32, (gw, gw), 0), CHUNK.bit_length() - 1)
                  == lax.shift_right_logical(lax.broadcasted_iota(jnp.int32, (gw, gw), 1), SSM_HEAD_DIM.bit_length() - 1))

    n_chunks = rows // CHUNK
    states = [[state_ref[gg, 0], state_ref[gg, 1]] for gg in range(gs)]
    for c in range(n_chunks):
        cb = n_chunks - 1 - c
        sl_f = slice(c * CHUNK, (c + 1) * CHUNK)
        sl_b = slice(cb * CHUNK, (cb + 1) * CHUNK)
        last_f = slice(c * CHUNK + CHUNK - 1, (c + 1) * CHUNK)
        last_b = slice(cb * CHUNK, cb * CHUNK + 1)
        items = []
        for gg in range(gs):
            xl = slice(gg * gw, (gg + 1) * gw)
            nl = slice(gg * SSM_STATE, (gg + 1) * SSM_STATE)
            dl = slice(width + gg * gw, width + (gg + 1) * gw)
            items.append((xf[sl_f, xl], growf_ref[c:c + 1, dl], xdec_f[sl_f, xl], bmf[sl_f, nl], cmf[sl_f, nl],
                          gcx_f[sl_f, xl], growf_ref[c:c + 1, xl], egx_f[sl_f, xl], egx_f[last_f, xl], states[gg][0], True,
                          d_all[:, xl]))
            items.append((xb[sl_b, xl], growb_ref[cb:cb + 1, dl], xdec_b[sl_b, xl], bmb[sl_b, nl], cmb[sl_b, nl],
                          gcx_b[sl_b, xl], growb_ref[cb:cb + 1, xl], egx_b[sl_b, xl], egx_b[last_b, xl], states[gg][1], False,
                          None))
        results = _ssd_chunks(items, incl, block_diag)
        for gg in range(gs):
            xl = slice(gg * gw, (gg + 1) * gw)
            yf_ref[sl_f, xl] = results[2 * gg][0].astype(yf_ref.dtype)
            yb_ref[sl_b, xl] = results[2 * gg + 1][0].astype(yb_ref.dtype)
            states[gg] = [results[2 * gg][1], results[2 * gg + 1][1]]
    for gg in range(gs):
        state_ref[gg, 0] = states[gg][0]
        state_ref[gg, 1] = states[gg][1]

    @pl.when(step == n_seg - 1)
    def _():
        sout_ref[...] = state_ref[...]


def _ssd_scan(xbc, layouts, expand, d_exp, s0):
    bsz, n_seg, rows, _ = xbc.shape
    gs = SSD_GROUPS_PER_STEP
    xw, nw = SSD_STEP_WIDTH, gs * SSM_STATE
    boff, coff = SSM_INNER // nw, (SSM_INNER + SSM_GN) // nw
    (cols_f, grow_f), (cols_b, grow_b) = layouts
    n_cols = cols_f.shape[-1]
    seg_f = lambda s: s
    seg_b = lambda s: n_seg - 1 - s
    xspec = lambda seg: pl.BlockSpec((None, None, rows, xw), lambda b, g, s: (b, seg(s), 0, g))
    nspec = lambda seg, off: pl.BlockSpec((None, None, rows, nw), lambda b, g, s: (b, seg(s), 0, off + g))
    cspec = lambda seg: pl.BlockSpec((None, None, None, rows, n_cols), lambda b, g, s: (b, seg(s), g, 0, 0))
    rspec = lambda seg: pl.BlockSpec((None, None, None, rows // CHUNK, 2 * xw), lambda b, g, s: (b, seg(s), g, 0, 0))
    state_spec = pl.BlockSpec((None, gs, 2, SSM_STATE, SSM_HPG * SSM_HEAD_DIM), lambda b, g, s: (b, g, 0, 0, 0))
    y_shape = jax.ShapeDtypeStruct((bsz, n_seg, rows, SSM_INNER), SCAN_OUT_DTYPE)
    return pl.pallas_call(
        functools.partial(_ssd_kernel, rows=rows, n_seg=n_seg, gs=gs),
        out_shape=(y_shape, y_shape,
                   jax.ShapeDtypeStruct((bsz, SSM_GROUPS, 2, SSM_STATE, SSM_HPG * SSM_HEAD_DIM), F32)),
        grid=(bsz, SSM_GROUPS // gs, n_seg),
        in_specs=[xspec(seg_f), nspec(seg_f, boff), nspec(seg_f, coff),
                  xspec(seg_b), nspec(seg_b, boff), nspec(seg_b, coff),
                  cspec(seg_f), cspec(seg_b), rspec(seg_f), rspec(seg_b),
                  pl.BlockSpec(expand.shape, lambda b, g, s: (0, 0)),
                  pl.BlockSpec((1, xw), lambda b, g, s: (0, g)), state_spec],
        out_specs=(xspec(seg_f), xspec(seg_b), state_spec),
        scratch_shapes=[pltpu.VMEM((gs, 2, SSM_STATE, SSM_HPG * SSM_HEAD_DIM), F32)],
        compiler_params=_params(3),
        name="ssd_scan",
    )(xbc, xbc, xbc, xbc, xbc, xbc, cols_f, cols_b, grow_f, grow_b, expand, d_exp, s0)


def _ssd_branch(xbc, small, xbc_c, small_c, a_log, dt_bias, d_skip):
    bsz, n_seg, rows, _ = xbc.shape
    assert rows % CHUNK == 0 and xbc_c.shape[2] % CHUNK == 0
    d_exp = jnp.repeat(d_skip.astype(F32), SSM_HEAD_DIM).reshape(1, SSM_INNER)
    s0 = jnp.zeros((bsz, SSM_GROUPS, 2, SSM_STATE, SSM_HPG * SSM_HEAD_DIM), F32)
    expand = _ssd_expand_matrix()
    yc_f, yc_b, s_ctx = _ssd_scan(xbc_c, _ssd_layouts(_ssd_prep(small_c, a_log, dt_bias, 1)), expand, d_exp, s0)
    q = _ssd_prep(small.reshape(bsz, rows, n_seg * SMALL_N), a_log, dt_bias, n_seg)
    steps, step_rows = n_seg // SSD_COLUMNS_PER_STEP, SSD_COLUMNS_PER_STEP * rows
    merged = lambda t: t.reshape(bsz, steps, step_rows, t.shape[-1])
    y_f, y_b, _ = _ssd_scan(merged(xbc), _ssd_layouts(merged(q)), expand, d_exp, s_ctx)
    to_raster = lambda y: jnp.swapaxes(y.reshape(bsz, n_seg, rows, SSM_INNER), 1, 2).reshape(bsz, rows * n_seg, SSM_INNER)
    return (to_raster(y_f), to_raster(y_b)), (yc_f[:, 0], yc_b[:, 0])


def _layer_norm(r, g, b):
    mu = jnp.mean(r, axis=1, keepdims=True)
    var = jnp.mean(jnp.square(r - mu), axis=1, keepdims=True)
    return (r - mu) * lax.rsqrt(var + LN_EPS) * g + b


def _merge_kernel(x_ref, sc_ref, sh_ref, wg_ref, of_ref, ob_ref, yf_ref, yb_ref,
                  nwa_ref, nwb_ref, wpg_ref, wps_ref, wout_ref, g1_ref, lng_ref, lnb_ref, o_ref, *, alpha):
    x = x_ref[...]
    h = (x * (1.0 + sc_ref[...]) + sh_ref[...]).astype(BF16)
    gate = lambda off, width: jnp.dot(h, wg_ref[:, off:off + width], preferred_element_type=F32)
    o = of_ref[...].astype(F32) + ob_ref[...].astype(F32)
    normed = []
    for hh in range(GDN_HEADS):
        oh = o[:, hh * GDN_DV:(hh + 1) * GDN_DV]
        normed.append(oh * lax.rsqrt(jnp.mean(oh * oh, axis=1, keepdims=True) + NORM_EPS))
    y_a = jnp.concatenate(normed, axis=1) * nwa_ref[...] * _silu(gate(GOUT_OFF, GDN_V))
    t = (yf_ref[...].astype(F32) + yb_ref[...].astype(F32)) * _silu(gate(Z_OFF, SSM_INNER))
    gw = SSM_INNER // SSM_GROUPS
    normed = []
    for g in range(SSM_GROUPS):
        tg = t[:, g * gw:(g + 1) * gw]
        normed.append(tg * lax.rsqrt(jnp.mean(tg * tg, axis=1, keepdims=True) + NORM_EPS))
    y_b = jnp.concatenate(normed, axis=1) * nwb_ref[...]
    d = x.shape[1]
    mix = (jax.nn.sigmoid(gate(GA_OFF, d)) * jnp.dot(y_a.astype(BF16), wpg_ref[...], preferred_element_type=F32)
           + jax.nn.sigmoid(gate(GB_OFF, d)) * jnp.dot(y_b.astype(BF16), wps_ref[...], preferred_element_type=F32))
    out = jnp.dot(mix.astype(BF16), wout_ref[...], preferred_element_type=F32)
    o_ref[...] = _layer_norm(alpha * x + g1_ref[...] * out, lng_ref[...], lnb_ref[...])


def _merge(x, sc, sh, w_gates, o_pair, y_pair, nwa, nwb, wpg, wps, wout, g1, ln_g, ln_b, alpha, tm):
    bsz, length, d = x.shape
    row = lambda width: pl.BlockSpec((None, tm, width), lambda b, i: (b, i, 0))
    const = lambda shape: pl.BlockSpec(shape, lambda b, i: (0,) * len(shape), pipeline_mode=pl.Buffered(1))
    mod = pl.BlockSpec((None, 1, d), lambda b, i: (b, 0, 0))
    return pl.pallas_call(
        functools.partial(_merge_kernel, alpha=alpha),
        out_shape=jax.ShapeDtypeStruct((bsz, length, d), F32),
        grid=(bsz, length // tm),
        in_specs=[row(d), mod, mod, const((d, BIG_N)),
                  row(GDN_V), row(GDN_V), row(SSM_INNER), row(SSM_INNER),
                  const((1, GDN_V)), const((1, SSM_INNER)), const((GDN_V, d)), const((SSM_INNER, d)), const((d, d)),
                  mod, const((1, d)), const((1, d))],
        out_specs=row(d),
        compiler_params=_params(2),
        name="merge",
    )(x, sc, sh, w_gates, o_pair[0], o_pair[1], y_pair[0], y_pair[1], nwa, nwb, wpg, wps, wout, g1, ln_g, ln_b)


def _mlp_kernel(x_ref, sc_ref, sh_ref, g2_ref, w1_ref, b1_ref, w2_ref, b2_ref, lng_ref, lnb_ref, o_ref, *, alpha):
    x = x_ref[...]
    h = (x * (1.0 + sc_ref[...]) + sh_ref[...]).astype(BF16)
    acc = None
    tf = D_MODEL
    for c in range(D_FF // tf):
        u = jnp.dot(h, w1_ref[:, c * tf:(c + 1) * tf], preferred_element_type=F32) + b1_ref[:, c * tf:(c + 1) * tf]
        u = jnp.square(jnp.maximum(u, 0.0))
        part = jnp.dot(u.astype(BF16), w2_ref[c * tf:(c + 1) * tf, :], preferred_element_type=F32)
        acc = part if acc is None else acc + part
    f = acc + b2_ref[...]
    o_ref[...] = _layer_norm(alpha * x + g2_ref[...] * f, lng_ref[...], lnb_ref[...])


def _mlp(x, sc, sh, g2, w1, b1, w2, b2, ln_g, ln_b, alpha, tm):
    bsz, length, d = x.shape
    const = lambda shape: pl.BlockSpec(shape, lambda b, i: (0,) * len(shape), pipeline_mode=pl.Buffered(1))
    mod = pl.BlockSpec((None, 1, d), lambda b, i: (b, 0, 0))
    return pl.pallas_call(
        functools.partial(_mlp_kernel, alpha=alpha),
        out_shape=jax.ShapeDtypeStruct((bsz, length, d), F32),
        grid=(bsz, length // tm),
        in_specs=[pl.BlockSpec((None, tm, d), lambda b, i: (b, i, 0)), mod, mod, mod,
                  const((d, D_FF)), const((1, D_FF)), const((D_FF, d)), const((1, d)), const((1, d)), const((1, d))],
        out_specs=pl.BlockSpec((None, tm, d), lambda b, i: (b, i, 0)),
        compiler_params=_params(2),
        name="mlp",
    )(x, sc, sh, g2, w1, b1, w2, b2, ln_g, ln_b)


def _split_w_in(w_in):
    pts, acc = [], 0
    for s in IN_SPLITS[:-1]:
        acc += s
        pts.append(acc)
    qkv, gout, a_raw, b_raw, z, xbc, dt_raw, gate_a, gate_b = jnp.split(w_in, pts, axis=1)
    big = jnp.concatenate([z, gout, gate_a, gate_b], axis=1).astype(BF16)
    pad = jnp.zeros((w_in.shape[0], SMALL_N - SM_DT - 2 * SSM_HEADS), w_in.dtype)
    small = jnp.concatenate([a_raw, b_raw, dt_raw, pad], axis=1).astype(BF16)
    return big, qkv.astype(BF16), xbc.astype(BF16), small


def kernel(x, c, ctx, c_ctx, w_mod, b_mod, w_in, gdn_conv_w, gdn_A_log, gdn_dt_bias, gdn_norm_w,
           ssm_conv_w, ssm_conv_b, ssm_A_log, ssm_dt_bias, ssm_D, ssm_norm_w,
           w_proj_gdn, w_proj_ssm, w_out, ln1_g, ln1_b, w_ff1, b_ff1, w_ff2, b_ff2, ln2_g, ln2_b):
    bsz, length, d = x.shape
    ctx_len = ctx.shape[1]
    depth = w_mod.shape[0]
    alpha = float((2 * depth) ** 0.25)
    mod_rows = -(-(bsz + 1) // (2 * SUBLANES)) * (2 * SUBLANES)
    cc = jnp.concatenate([c, c_ctx[None, :], jnp.zeros((mod_rows - bsz - 1, d), c.dtype)], axis=0)
    row2 = lambda t: t.reshape(1, -1)
    tm_lat = PROJ_ROWS
    tm_ctx = ctx_len

    for l in range(depth):
        last = l == depth - 1
        mod = _modulation(cc, w_mod[l], b_mod[l])
        lat = [mod[:bsz, i * d:(i + 1) * d].reshape(bsz, 1, d) for i in range(6)]
        cxm = [jnp.broadcast_to(mod[bsz, i * d:(i + 1) * d].reshape(1, 1, d), (bsz, 1, d)) for i in range(6)]
        w_big, w_qkv, w_xbc, w_small = _split_w_in(w_in[l])
        rows = length // GRID_W

        qkv, kt, small, p = _proj_conv(x, lat[1], lat[0], w_qkv, gdn_conv_w[l], None, GRID_W, False, 2, tm_lat, GDN_QK,
                                    "inproj_qkv", w_small, (gdn_A_log[l], gdn_dt_bias[l]))
        xbc = _proj_conv(x, lat[1], lat[0], w_xbc, ssm_conv_w[l], ssm_conv_b[l], rows, True, 0, None, PROJ_TILE_N,
                         "inproj_xbc")
        qkv_c, kt_c, small_c, p_c = _proj_conv(ctx, cxm[1], cxm[0], w_qkv, gdn_conv_w[l], None, ctx_len, False, 2, tm_ctx,
                                          GDN_QK, "inproj_qkv_ctx", w_small, (gdn_A_log[l], gdn_dt_bias[l]))
        xbc_c = _proj_conv(ctx, cxm[1], cxm[0], w_xbc, ssm_conv_w[l], ssm_conv_b[l], ctx_len, False, 0, tm_ctx, PROJ_TILE_N,
                           "inproj_xbc_ctx")[:, None]

        o_pair, oc_pair = _gdn_branch(qkv, kt, p, qkv_c, kt_c, p_c)
        y_pair, yc_pair = _ssd_branch(xbc, small, xbc_c, small_c, ssm_A_log[l], ssm_dt_bias[l], ssm_D[l])

        nwa = jnp.tile(gdn_norm_w[l], GDN_HEADS).reshape(1, GDN_V)
        nwb = row2(ssm_norm_w[l])
        wpg, wps, wo = w_proj_gdn[l].astype(BF16), w_proj_ssm[l].astype(BF16), w_out[l].astype(BF16)
        w1, w2 = w_ff1[l].astype(BF16), w_ff2[l].astype(BF16)
        merge_args = (nwa, nwb, wpg, wps, wo)
        ln1 = (row2(ln1_g[l]), row2(ln1_b[l]))
        mlp_w = (w1, row2(b_ff1[l]), w2, row2(b_ff2[l]), row2(ln2_g[l]), row2(ln2_b[l]))

        x1 = _merge(x, lat[1], lat[0], w_big, o_pair, y_pair, *merge_args, lat[2], *ln1, alpha, MERGE_ROWS)
        x = _mlp(x1, lat[4], lat[3], lat[5], *mlp_w, alpha, MLP_ROWS)
        if not last:
            c1 = _merge(ctx, cxm[1], cxm[0], w_big, oc_pair, yc_pair, *merge_args, cxm[2], *ln1, alpha, ctx_len)
            ctx = _mlp(c1, cxm[4], cxm[3], cxm[5], *mlp_w, alpha, ctx_len)
    return x
```
